```python
import math
import jax
import jax.numpy as jnp
from jax import lax
import numpy as np

D_MODEL = 1024
BATCH = 16
SEQ = 2048
DEPTH = 2

N_EVEN = (DEPTH + 1) // 2
N_ODD = DEPTH // 2
EPS = 1e-5

S5_WIDTH = D_MODEL // 2
S5_GROUP = 16
S5_GROUPS = S5_WIDTH // S5_GROUP
S5_STATE = 64
DT_MIN = 1e-3
DT_MAX = 1e-1
MLSTM_HEADS = 4
MLSTM_HEAD_DIM = D_MODEL // 8
MLSTM_WIDTH = MLSTM_HEADS * MLSTM_HEAD_DIM
MLSTM_CHUNK = 128
CONV_WIDTH = 4
MIX_AB = S5_WIDTH + MLSTM_WIDTH
AB_IN = S5_WIDTH + 4 * MLSTM_WIDTH + 2 * MLSTM_HEADS
GMLP_GROUPS = 4
GMLP_GROUP_DIM = D_MODEL // 8
GMLP_WIDTH = GMLP_GROUPS * GMLP_GROUP_DIM
GMLP_CHUNK = 128
MOBA_HEADS = 4
MOBA_HEAD_DIM = D_MODEL // 8
MOBA_WIDTH = MOBA_HEADS * MOBA_HEAD_DIM
MOBA_BLOCK = 256
MOBA_TOPK = 3
MOBA_QBLOCK = 16
MIX_CD = GMLP_WIDTH + MOBA_WIDTH
CD_IN = 2 * GMLP_WIDTH + 3 * MOBA_WIDTH
REL_BUCKETS = 32
REL_MAX_DIST = 128
D_FF = 2816
N_EXPERTS = 8
TOP_K = 2
D_FF_EXPERT = 3584

kernel_name = 'hybrid_s5_mlstm_gmlp_moba_moe'


def rmsnorm(x, g):
    xf = x.astype(jnp.float32)
    var = jnp.mean(xf * xf, axis=-1, keepdims=True)
    return (xf * lax.rsqrt(var + EPS)).astype(x.dtype) * g


def layernorm(x, g, b):
    xf = x.astype(jnp.float32)
    mu = jnp.mean(xf, axis=-1, keepdims=True)
    var = jnp.mean((xf - mu) ** 2, axis=-1, keepdims=True)
    return ((xf - mu) * lax.rsqrt(var + EPS)).astype(x.dtype) * g + b


def causal_conv(u, w, b):
    width, seq = w.shape[0], u.shape[1]
    up = jnp.pad(u, ((0, 0), (width - 1, 0), (0, 0)))
    out = b
    for j in range(width):
        out = out + w[j] * up[:, width - 1 - j: width - 1 - j + seq]
    return out


def swiglu(h, w_gate, w_up, w_down):
    return (jax.nn.silu(h @ w_gate) * (h @ w_up)) @ w_down


def _linear_recurrence_combine(left, right):
    a_l, b_l = left
    a_r, b_r = right
    return a_r * a_l, a_r * b_l + b_r


def s5_mixer(u, lam_re, lam_im, log_dt, b_re, b_im, c_re, c_im, d_skip, w_glu):
    bsz, seq, _ = u.shape
    f32 = jnp.float32
    uf = u.astype(f32)
    lam = lax.complex(lam_re.astype(f32), lam_im.astype(f32))
    dt = jnp.exp(log_dt.astype(f32))[:, None]
    lam_bar = jnp.exp(lam * dt)
    b_in = lax.complex(b_re.astype(f32), b_im.astype(f32))
    b_bar = ((lam_bar - 1.0) / lam)[..., None] * b_in
    ug = uf.reshape(bsz, seq, S5_GROUPS, S5_GROUP).astype(jnp.complex64)
    bu = jnp.einsum('gph,bsgh->bsgp', b_bar, ug)
    a = jnp.broadcast_to(lam_bar, (1, seq) + lam_bar.shape)
    _, states = lax.associative_scan(_linear_recurrence_combine, (a, bu), axis=1)
    c_out = lax.complex(c_re.astype(f32), c_im.astype(f32))
    y = jnp.real(jnp.einsum('ghp,bsgp->bsgh', c_out, states)).reshape(bsz, seq, S5_WIDTH)
    y = jax.nn.gelu(y + d_skip.astype(f32) * uf)
    y = y * jax.nn.sigmoid(y @ w_glu.astype(f32))
    return y.astype(u.dtype)


def mlstm_mixer(q, k, v, o_pre, i_pre, f_pre, norm_g):
    bsz, seq, _ = q.shape
    H, Dh, L = MLSTM_HEADS, MLSTM_HEAD_DIM, MLSTM_CHUNK
    nc = seq // L
    f32 = jnp.float32

    def heads(t):
        return t.astype(f32).reshape(bsz, nc, L, H, Dh).transpose(0, 3, 1, 2, 4)

    def gates(t):
        return t.astype(f32).reshape(bsz, nc, L, H).transpose(0, 3, 1, 2)

    qh, kh, vh = heads(q), heads(k) / math.sqrt(Dh), heads(v)
    log_i = gates(i_pre)
    log_f = jax.nn.log_sigmoid(gates(f_pre))
    bcum = jnp.cumsum(log_f, axis=-1)
    b_last = bcum[..., -1]
    a_end = b_last[..., None] - bcum + log_i
    m_loc = jnp.max(a_end, axis=-1)
    w_end = jnp.exp(a_end - m_loc[..., None])
    c_loc = jnp.einsum('bhcs,bhcsd,bhcse->bhcde', w_end, vh, kh)
    n_loc = jnp.einsum('bhcs,bhcse->bhce', w_end, kh)

    def step(carry, xs):
        c_prev, n_prev, m_prev = carry
        c_l, n_l, m_l, bl = xs
        m_new = jnp.maximum(bl + m_prev, m_l)
        s_prev = jnp.exp(bl + m_prev - m_new)
        s_loc = jnp.exp(m_l - m_new)
        c_new = s_prev[..., None, None] * c_prev + s_loc[..., None, None] * c_l
        n_new = s_prev[..., None] * n_prev + s_loc[..., None] * n_l
        return (c_new, n_new, m_new), (c_prev, n_prev, m_prev)

    init = (jnp.zeros((bsz, H, Dh, Dh), f32), jnp.zeros((bsz, H, Dh), f32), jnp.zeros((bsz, H), f32))
    xs = (jnp.moveaxis(c_loc, 2, 0), jnp.moveaxis(n_loc, 2, 0), jnp.moveaxis(m_loc, 2, 0), jnp.moveaxis(b_last, 2, 0))
    _, (c_st, n_st, m_st) = lax.scan(step, init, xs)
    c_st = jnp.moveaxis(c_st, 0, 2)
    n_st = jnp.moveaxis(n_st, 0, 2)
    m_st = jnp.moveaxis(m_st, 0, 2)
    causal = jnp.tril(jnp.ones((L, L), dtype=bool))
    d_mat = jnp.where(causal, bcum[..., :, None] - bcum[..., None, :] + log_i[..., None, :], -jnp.inf)
    m_inter = bcum + m_st[..., None]
    m_t = jnp.maximum(jnp.max(d_mat, axis=-1), m_inter)
    s = jnp.einsum('bhctd,bhcsd->bhcts', qh, kh) * jnp.exp(d_mat - m_t[..., None])
    w_inter = jnp.exp(m_inter - m_t)
    num = jnp.einsum('bhcts,bhcsd->bhctd', s, vh) + w_inter[..., None] * jnp.einsum('bhcte,bhcde->bhctd', qh, c_st)
    den = jnp.sum(s, axis=-1) + w_inter * jnp.einsum('bhcte,bhce->bhct', qh, n_st)
    h = num / jnp.maximum(jnp.abs(den), jnp.exp(-m_t))[..., None]
    h = h * lax.rsqrt(jnp.mean(h * h, axis=-1, keepdims=True) + EPS)
    h = h.transpose(0, 2, 3, 1, 4).reshape(bsz, seq, MLSTM_WIDTH) * norm_g.astype(f32)
    return (jax.nn.sigmoid(o_pre.astype(f32)) * h).astype(q.dtype)


def mixer_ab(hn, w_in, lam_re, lam_im, log_dt, b_re, b_im, c_re, c_im, d_skip, w_glu,
             conv_w, conv_b, gate_b, mlstm_norm_g, w_out):
    z = hn @ w_in
    splits = [S5_WIDTH, S5_WIDTH + 2 * MLSTM_WIDTH, S5_WIDTH + 3 * MLSTM_WIDTH, S5_WIDTH + 4 * MLSTM_WIDTH]
    u_s5, qk_pre, v, o_pre, if_pre = jnp.split(z, splits, axis=-1)
    qk = jax.nn.silu(causal_conv(qk_pre, conv_w, conv_b))
    q, k = jnp.split(qk, 2, axis=-1)
    if_pre = if_pre + gate_b
    i_pre, f_pre = jnp.split(if_pre, 2, axis=-1)
    y_a = s5_mixer(u_s5, lam_re, lam_im, log_dt, b_re, b_im, c_re, c_im, d_skip, w_glu)
    y_b = mlstm_mixer(q, k, v, o_pre, i_pre, f_pre, mlstm_norm_g)
    return jnp.concatenate([y_a, y_b], axis=-1) @ w_out


def gmlp_mixer(u_pre, v_pre, norm_g, norm_b, w_s, b_s):
    bsz, seq, _ = u_pre.shape
    u = jax.nn.gelu(u_pre)
    v = layernorm(jax.nn.gelu(v_pre), norm_g, norm_b)
    vc = v.reshape(bsz, seq // GMLP_CHUNK, GMLP_CHUNK, GMLP_GROUPS, GMLP_GROUP_DIM)
    ws = w_s * jnp.tril(jnp.ones((GMLP_CHUNK, GMLP_CHUNK), w_s.dtype))
    s = jnp.einsum('gts,bcsgd->bctgd', ws, vc) + b_s.T[:, :, None]
    return u * s.reshape(bsz, seq, GMLP_WIDTH)


def rel_bucket(rel):
    n = jnp.maximum(rel, 0)
    max_exact = REL_BUCKETS // 2
    nf = jnp.maximum(n, 1).astype(jnp.float32)
    large = max_exact + (jnp.log(nf / max_exact) / math.log(REL_MAX_DIST / max_exact)
                         * (REL_BUCKETS - max_exact)).astype(jnp.int32)
    large = jnp.minimum(large, REL_BUCKETS - 1)
    return jnp.where(n < max_exact, n, large)


def moba_mixer(q, k, v, rel_bias):
    bsz, seq, _ = q.shape
    H, Dh, BL, QB = MOBA_HEADS, MOBA_HEAD_DIM, MOBA_BLOCK, MOBA_QBLOCK
    nb = -(-seq // BL)
    pad = nb * BL - seq
    n_sel = min(MOBA_TOPK, nb)
    scale = 1.0 / math.sqrt(Dh)

    def heads(t):
        return t.reshape(bsz, seq, H, Dh).transpose(0, 2, 1, 3)

    qh = heads(q)
    kb = jnp.pad(heads(k), ((0, 0), (0, 0), (0, pad), (0, 0))).reshape(bsz, H, nb, BL, Dh)
    vb = jnp.pad(heads(v), ((0, 0), (0, 0), (0, pad), (0, 0))).reshape(bsz, H, nb, BL, Dh)
    k_mean = jnp.mean(kb.astype(jnp.float32), axis=3)
    b_idx = jnp.arange(bsz)[:, None, None, None]
    h_idx = jnp.arange(H)[None, :, None, None]
    bias_t = rel_bias.T

    def attend_block(qi):
        t0 = qi * QB
        qblk = lax.dynamic_slice_in_dim(qh, t0, QB, axis=2)
        own = t0 // BL
        qpos = t0 + jnp.arange(QB, dtype=jnp.int32)
        gate = jnp.einsum('bhqd,bhnd->bhqn', qblk.astype(jnp.float32), k_mean)
        gate = jnp.where(jnp.arange(nb) < own, gate, -jnp.inf)
        gval, sel = lax.top_k(gate, n_sel)
        valid = jnp.isfinite(gval)
        ks = kb[b_idx, h_idx, sel]
        vs = vb[b_idx, h_idx, sel]
        kpos_sel = sel[..., None] * BL + jnp.arange(BL, dtype=jnp.int32)
        bias_sel = bias_t[h_idx[..., None], rel_bucket(qpos[:, None, None] - kpos_sel)]
        logit_sel = jnp.einsum('bhqd,bhqnkd->bhqnk', qblk, ks).astype(jnp.float32) * scale + bias_sel
        logit_sel = jnp.where(valid[..., None], logit_sel, -jnp.inf).reshape(bsz, H, QB, n_sel * BL)
        k_own = lax.dynamic_index_in_dim(kb, own, axis=2, keepdims=False)
        v_own = lax.dynamic_index_in_dim(vb, own, axis=2, keepdims=False)
        rel_own = qpos[:, None] - (own * BL + jnp.arange(BL, dtype=jnp.int32))[None, :]
        logit_own = jnp.einsum('bhqd,bhkd->bhqk', qblk, k_own).astype(jnp.float32) * scale + bias_t[:, rel_bucket(rel_own)]
        logit_own = jnp.where(rel_own >= 0, logit_own, -jnp.inf)
        p = jax.nn.softmax(jnp.concatenate([logit_sel, logit_own], axis=-1), axis=-1)
        p_sel = p[..., :n_sel * BL].reshape(bsz, H, QB, n_sel, BL).astype(vs.dtype)
        p_own = p[..., n_sel * BL:].astype(v_own.dtype)
        return jnp.einsum('bhqnk,bhqnkd->bhqd', p_sel, vs) + jnp.einsum('bhqk,bhkd->bhqd', p_own, v_own)

    outs = lax.map(attend_block, jnp.arange(seq // QB, dtype=jnp.int32))
    return outs.transpose(1, 0, 3, 2, 4).reshape(bsz, seq, MOBA_WIDTH)


def mixer_cd(hn, w_in, gmlp_norm_g, gmlp_norm_b, gmlp_w_s, gmlp_b_s, rel_bias, w_out):
    z = hn @ w_in
    splits = [GMLP_WIDTH, 2 * GMLP_WIDTH, 2 * GMLP_WIDTH + MOBA_WIDTH, 2 * GMLP_WIDTH + 2 * MOBA_WIDTH]
    u_pre, v_pre, q, k, v = jnp.split(z, splits, axis=-1)
    y_c = gmlp_mixer(u_pre, v_pre, gmlp_norm_g, gmlp_norm_b, gmlp_w_s, gmlp_b_s)
    y_d = moba_mixer(q, k, v, rel_bias)
    return jnp.concatenate([y_c, y_d], axis=-1) @ w_out


def moe_swiglu(h, router_w, router_b, w_gate, w_up, w_down):
    bsz, seq, d = h.shape
    xt = h.reshape(-1, d)
    logits = (xt @ router_w + router_b).astype(jnp.float32)
    top_v, top_i = lax.top_k(logits, TOP_K)
    gates = jax.nn.softmax(top_v, axis=-1)
    combine = jnp.sum(jax.nn.one_hot(top_i, N_EXPERTS, dtype=jnp.float32) * gates[..., None], axis=1)
    y = jnp.zeros_like(xt)
    for e in range(N_EXPERTS):
        y_e = (jax.nn.silu(xt @ w_gate[e]) * (xt @ w_up[e])) @ w_down[e]
        y = y + combine[:, e:e + 1].astype(xt.dtype) * y_e
    return y.reshape(bsz, seq, d)


def setup_inputs(seed: int = 0) -> dict:
    key = jax.random.key(seed)
    ks = jax.random.split(key, 36)
    f32 = jnp.float32
    E, O = N_EVEN, N_ODD

    def nrm(i, shape, scale):
        return scale * jax.random.normal(ks[i], shape, f32)

    x = nrm(0, (BATCH, SEQ, D_MODEL), 1.0)
    norm_mix_g = 1.0 + nrm(1, (DEPTH, D_MODEL), 0.01)
    norm_ffn_g = 1.0 + nrm(2, (DEPTH, D_MODEL), 0.01)
    norm_final_g = 1.0 + nrm(3, (D_MODEL,), 0.01)
    ab_w_in = nrm(4, (E, D_MODEL, AB_IN), D_MODEL ** -0.5)
    s5_lambda_re = -0.5 + nrm(5, (E, S5_GROUPS, S5_STATE), 0.01)
    s5_lambda_im = jnp.pi * jnp.arange(S5_STATE, dtype=f32) + nrm(6, (E, S5_GROUPS, S5_STATE), 0.01)
    s5_log_dt = jax.random.uniform(ks[7], (E, S5_GROUPS), f32, math.log(DT_MIN), math.log(DT_MAX))
    s5_b_re = nrm(8, (E, S5_GROUPS, S5_STATE, S5_GROUP), (2 * S5_GROUP) ** -0.5)
    s5_b_im = nrm(9, (E, S5_GROUPS, S5_STATE, S5_GROUP), (2 * S5_GROUP) ** -0.5)
    s5_c_re = nrm(10, (E, S5_GROUPS, S5_GROUP, S5_STATE), S5_STATE ** -0.5)
    s5_c_im = nrm(11, (E, S5_GROUPS, S5_GROUP, S5_STATE), S5_STATE ** -0.5)
    s5_d = nrm(12, (E, S5_WIDTH), 1.0)
    s5_w_glu = nrm(13, (E, S5_WIDTH, S5_WIDTH), S5_WIDTH ** -0.5)
    mlstm_conv_w = nrm(14, (E, CONV_WIDTH, 2 * MLSTM_WIDTH), CONV_WIDTH ** -0.5)
    mlstm_conv_b = nrm(15, (E, 2 * MLSTM_WIDTH), 0.02)
    mlstm_gate_b = jnp.concatenate([
        nrm(16, (E, MLSTM_HEADS), 0.1),
        jnp.linspace(3.0, 6.0, MLSTM_HEADS, dtype=f32) + nrm(17, (E, MLSTM_HEADS), 0.1)], axis=-1)
    mlstm_norm_g = 1.0 + nrm(18, (E, MLSTM_WIDTH), 0.01)
    ab_w_out = nrm(19, (E, MIX_AB, D_MODEL), MIX_AB ** -0.5)
    ffn_w_gate = nrm(20, (E, D_MODEL, D_FF), D_MODEL ** -0.5)
    ffn_w_up = nrm(21, (E, D_MODEL, D_FF), D_MODEL ** -0.5)
    ffn_w_down = nrm(22, (E, D_FF, D_MODEL), D_FF ** -0.5)
    cd_w_in = nrm(23, (O, D_MODEL, CD_IN), D_MODEL ** -0.5)
    gmlp_norm_g = 1.0 + nrm(24, (O, GMLP_WIDTH), 0.01)
    gmlp_norm_b = nrm(25, (O, GMLP_WIDTH), 0.01)
    gmlp_w_s = nrm(26, (O, GMLP_GROUPS, GMLP_CHUNK, GMLP_CHUNK), GMLP_CHUNK ** -0.5)
    gmlp_b_s = 1.0 + nrm(27, (O, GMLP_GROUPS, GMLP_CHUNK), 0.1)
    rel_bias = nrm(28, (REL_BUCKETS, MOBA_HEADS), 0.5)
    cd_w_out = nrm(29, (O, MIX_CD, D_MODEL), MIX_CD ** -0.5)
    moe_router_w = nrm(30, (O, D_MODEL, N_EXPERTS), D_MODEL ** -0.5)
    moe_router_b = nrm(31, (O, N_EXPERTS), 0.01)
    moe_w_gate = nrm(32, (O, N_EXPERTS, D_MODEL, D_FF_EXPERT), D_MODEL ** -0.5)
    moe_w_up = nrm(33, (O, N_EXPERTS, D_MODEL, D_FF_EXPERT), D_MODEL ** -0.5)
    moe_w_down = nrm(34, (O, N_EXPERTS, D_FF_EXPERT, D_MODEL), D_FF_EXPERT ** -0.5)
    return {
        'x': x, 'norm_mix_g': norm_mix_g, 'norm_ffn_g': norm_ffn_g, 'norm_final_g': norm_final_g,
        'ab_w_in': ab_w_in, 's5_lambda_re': s5_lambda_re, 's5_lambda_im': s5_lambda_im,
        's5_log_dt': s5_log_dt, 's5_b_re': s5_b_re, 's5_b_im': s5_b_im, 's5_c_re': s5_c_re,
        's5_c_im': s5_c_im, 's5_d': s5_d, 's5_w_glu': s5_w_glu, 'mlstm_conv_w': mlstm_conv_w,
        'mlstm_conv_b': mlstm_conv_b, 'mlstm_gate_b': mlstm_gate_b, 'mlstm_norm_g': mlstm_norm_g,
        'ab_w_out': ab_w_out, 'ffn_w_gate': ffn_w_gate, 'ffn_w_up': ffn_w_up, 'ffn_w_down': ffn_w_down,
        'cd_w_in': cd_w_in, 'gmlp_norm_g': gmlp_norm_g, 'gmlp_norm_b': gmlp_norm_b,
        'gmlp_w_s': gmlp_w_s, 'gmlp_b_s': gmlp_b_s, 'rel_bias': rel_bias, 'cd_w_out': cd_w_out,
        'moe_router_w': moe_router_w, 'moe_router_b': moe_router_b, 'moe_w_gate': moe_w_gate,
        'moe_w_up': moe_w_up, 'moe_w_down': moe_w_down,
    }


def reference(x, norm_mix_g, norm_ffn_g, norm_final_g, ab_w_in, s5_lambda_re, s5_lambda_im,
              s5_log_dt, s5_b_re, s5_b_im, s5_c_re, s5_c_im, s5_d, s5_w_glu, mlstm_conv_w,
              mlstm_conv_b, mlstm_gate_b, mlstm_norm_g, ab_w_out, ffn_w_gate, ffn_w_up, ffn_w_down,
              cd_w_in, gmlp_norm_g, gmlp_norm_b, gmlp_w_s, gmlp_b_s, rel_bias, cd_w_out,
              moe_router_w, moe_router_b, moe_w_gate, moe_w_up, moe_w_down):
    h = x
    for layer in range(DEPTH):
        p = layer // 2
        hn = rmsnorm(h, norm_mix_g[layer])
        if layer % 2 == 0:
            h = h + mixer_ab(hn, ab_w_in[p], s5_lambda_re[p], s5_lambda_im[p], s5_log_dt[p],
                             s5_b_re[p], s5_b_im[p], s5_c_re[p], s5_c_im[p], s5_d[p], s5_w_glu[p],
                             mlstm_conv_w[p], mlstm_conv_b[p], mlstm_gate_b[p], mlstm_norm_g[p],
                             ab_w_out[p])
            hn = rmsnorm(h, norm_ffn_g[layer])
            h = h + swiglu(hn, ffn_w_gate[p], ffn_w_up[p], ffn_w_down[p])
        else:
            h = h + mixer_cd(hn, cd_w_in[p], gmlp_norm_g[p], gmlp_norm_b[p], gmlp_w_s[p],
                             gmlp_b_s[p], rel_bias, cd_w_out[p])
            hn = rmsnorm(h, norm_ffn_g[layer])
            h = h + moe_swiglu(hn, moe_router_w[p], moe_router_b[p], moe_w_gate[p],
                               moe_w_up[p], moe_w_down[p])
    return rmsnorm(h, norm_final_g)
```

```python
import functools
import math

import jax
import jax.numpy as jnp
from jax import lax
from jax.experimental import pallas as pl
from jax.experimental.pallas import tpu as pltpu

F32 = jnp.float32
BF16 = jnp.bfloat16
EPS = 1e-5

LANES = 128
SUBLANES = 8
VMEM_LIMIT_BYTES = 56 * 1024 * 1024

S5_GROUP = 16
S5_STATE = 64
S5_SLAB_GROUPS = 8
MLSTM_HEADS = 4
MLSTM_CHUNK = 128
CONV_WIDTH = 4
GMLP_GROUPS = 4
GMLP_CHUNK = 128
MOBA_HEADS = 4
MOBA_BLOCK = 256
MOBA_TOPK = 3
REL_BUCKETS = 32
REL_MAX_DIST = 128
N_EXPERTS = 8
TOP_K = 2


def _cparams(sem):
    return pltpu.CompilerParams(dimension_semantics=sem, vmem_limit_bytes=VMEM_LIMIT_BYTES)


def _rms_bf16(x, g):
    var = jnp.mean(x * x, axis=-1, keepdims=True)
    return (x * lax.rsqrt(var + EPS) * g).astype(BF16)


def _gelu(x):
    return jax.nn.gelu(x, approximate=True)


def _sigmoid(x):
    return 1.0 / (1.0 + jnp.exp(-x))


def _norm_proj_kernel(x_ref, g_ref, w_ref, *out_refs, splits):
    hn = _rms_bf16(x_ref[0], g_ref[...])
    for o_ref, (c0, c1) in zip(out_refs, splits):
        r = jnp.dot(hn, w_ref[:, c0:c1], preferred_element_type=F32)
        o_ref[...] = r.reshape(o_ref.shape).astype(o_ref.dtype)


def _norm_proj(x, g, w, outs, tm):
    B, S, D = x.shape
    splits = tuple((c0, c1) for c0, c1, _, _ in outs)
    out_shape, out_specs = [], []
    for c0, c1, dt, time_major in outs:
        n = c1 - c0
        if time_major:
            out_shape.append(jax.ShapeDtypeStruct((S, B * n), dt))
            out_specs.append(pl.BlockSpec((tm, n), lambda b, i: (i, b)))
        else:
            out_shape.append(jax.ShapeDtypeStruct((B, S, n), dt))
            out_specs.append(pl.BlockSpec((1, tm, n), lambda b, i: (b, i, 0)))
    return pl.pallas_call(
        functools.partial(_norm_proj_kernel, splits=splits),
        out_shape=out_shape,
        grid=(B, S // tm),
        in_specs=[
            pl.BlockSpec((1, tm, D), lambda b, i: (b, i, 0)),
            pl.BlockSpec((1, D), lambda b, i: (0, 0)),
            pl.BlockSpec(w.shape, lambda b, i: (0, 0)),
        ],
        out_specs=out_specs,
        compiler_params=_cparams(("parallel", "parallel")),
        name="norm_proj",
    )(x, g.reshape(1, D), w)


def _s5_kernel(u_ref, bmat_ref, ar_ref, ai_ref, cmat_ref, d_ref, wglu_ref, y_ref,
               buf_ref, xr_ref, xi_ref, *, lc, nb, nslab, sw):
    @pl.when(pl.program_id(0) == 0)
    def _():
        xr_ref[...] = jnp.zeros_like(xr_ref)
        xi_ref[...] = jnp.zeros_like(xi_ref)

    u = u_ref[...]
    for k in range(nslab):
        buf_ref[:, 2 * sw * k:2 * sw * (k + 1)] = jnp.dot(
            u[:, LANES * k:LANES * (k + 1)], bmat_ref[k], preferred_element_type=F32)

    for k in range(nslab):
        re_cols = slice(2 * sw * k, 2 * sw * k + sw)
        im_cols = slice(2 * sw * k + sw, 2 * sw * (k + 1))
        st_cols = slice(sw * k, sw * (k + 1))
        ar = jnp.broadcast_to(ar_ref[:, st_cols], (nb, sw))
        ai = jnp.broadcast_to(ai_ref[:, st_cols], (nb, sw))

        def step(t, carry, re_cols=re_cols, im_cols=im_cols, ar=ar, ai=ai):
            xr, xi = carry
            rows = pl.ds(pl.multiple_of(t * nb, nb), nb)
            nxr = ar * xr - ai * xi + buf_ref[rows, re_cols]
            nxi = ar * xi + ai * xr + buf_ref[rows, im_cols]
            buf_ref[rows, re_cols] = nxr
            buf_ref[rows, im_cols] = nxi
            return nxr, nxi

        xr, xi = lax.fori_loop(0, lc, step, (xr_ref[:, st_cols], xi_ref[:, st_cols]), unroll=4)
        xr_ref[:, st_cols] = xr
        xi_ref[:, st_cols] = xi

    ys = []
    for k in range(nslab):
        st = buf_ref[:, 2 * sw * k:2 * sw * (k + 1)].astype(BF16)
        ys.append(jnp.dot(st, cmat_ref[k], preferred_element_type=F32))
    y = jnp.concatenate(ys, axis=-1)
    y = _gelu(y + d_ref[...] * u.astype(F32))
    gl = jnp.dot(y.astype(BF16), wglu_ref[...], preferred_element_type=F32)
    y_ref[...] = (y * _sigmoid(gl)).astype(y_ref.dtype)


def _s5_params(lam_re, lam_im, log_dt, b_re, b_im, c_re, c_im):
    G, P = lam_re.shape
    Hc = b_re.shape[-1]
    dt = jnp.exp(log_dt.astype(F32))[:, None]
    mag = jnp.exp(lam_re * dt)
    ar = mag * jnp.cos(lam_im * dt)
    ai = mag * jnp.sin(lam_im * dt)
    den = lam_re * lam_re + lam_im * lam_im
    cr = ((ar - 1.0) * lam_re + ai * lam_im) / den
    ci = (ai * lam_re - (ar - 1.0) * lam_im) / den
    bb_re = cr[..., None] * b_re - ci[..., None] * b_im
    bb_im = cr[..., None] * b_im + ci[..., None] * b_re
    gs = S5_SLAB_GROUPS
    nslab = G // gs
    eye = jnp.eye(gs, dtype=F32)

    def bd_in(b):
        b = b.reshape(nslab, gs, P, Hc)
        return jnp.einsum('kgph,gj->kghjp', b, eye).reshape(nslab, gs * Hc, gs * P)

    def bd_out(c):
        c = c.reshape(nslab, gs, Hc, P)
        return jnp.einsum('kghp,gj->kgpjh', c, eye).reshape(nslab, gs * P, gs * Hc)

    bmat = jnp.concatenate([bd_in(bb_re), bd_in(bb_im)], axis=-1).astype(BF16)
    cmat = jnp.concatenate([bd_out(c_re), -bd_out(c_im)], axis=1).astype(BF16)
    return bmat, ar.reshape(1, G * P), ai.reshape(1, G * P), cmat


def _s5(u_tm, nb, bmat, ar, ai, cmat, d_skip, w_glu, lc):
    R, W = u_tm.shape
    S = R // nb
    nslab = bmat.shape[0]
    sw = bmat.shape[2] // 2
    rows = lc * nb
    return pl.pallas_call(
        functools.partial(_s5_kernel, lc=lc, nb=nb, nslab=nslab, sw=sw),
        out_shape=jax.ShapeDtypeStruct((R, W), BF16),
        grid=(S // lc,),
        in_specs=[
            pl.BlockSpec((rows, W), lambda i: (i, 0)),
            pl.BlockSpec(bmat.shape, lambda i: (0, 0, 0)),
            pl.BlockSpec(ar.shape, lambda i: (0, 0)),
            pl.BlockSpec(ai.shape, lambda i: (0, 0)),
            pl.BlockSpec(cmat.shape, lambda i: (0, 0, 0)),
            pl.BlockSpec((1, W), lambda i: (0, 0)),
            pl.BlockSpec(w_glu.shape, lambda i: (0, 0)),
        ],
        out_specs=pl.BlockSpec((rows, W), lambda i: (i, 0)),
        scratch_shapes=[
            pltpu.VMEM((rows, 2 * sw * nslab), F32),
            pltpu.VMEM((nb, sw * nslab), F32),
            pltpu.VMEM((nb, sw * nslab), F32),
        ],
        compiler_params=_cparams(("arbitrary",)),
        name="s5",
    )(u_tm, bmat, ar, ai, cmat, d_skip.reshape(1, W), w_glu)


def _mlstm_kernel(qk_ref, v_ref, o_ref, gt_ref, cw_ref, cb_ref, gb_ref, ng_ref, y_ref,
                  xcat_ref, c_ref, n_ref, m_ref, *, L, H, Dh):
    W = H * Dh
    halo = SUBLANES

    @pl.when(pl.program_id(1) == 0)
    def _():
        xcat_ref[0:halo, :] = jnp.zeros((halo, 2 * W), F32)
        c_ref[...] = jnp.zeros_like(c_ref)
        n_ref[...] = jnp.zeros_like(n_ref)
        m_ref[...] = jnp.zeros_like(m_ref)

    xcat_ref[halo:halo + L, :] = qk_ref[0].astype(F32)
    conv = cb_ref[...] + cw_ref[0:1, :] * xcat_ref[halo:halo + L, :]
    for j in range(1, CONV_WIDTH):
        conv = conv + cw_ref[j:j + 1, :] * xcat_ref[halo - j:halo - j + L, :]
    xcat_ref[0:halo, :] = xcat_ref[L:L + halo, :]
    qk = conv * _sigmoid(conv)
    q = qk[:, :W].astype(BF16)
    kf = qk[:, W:] * (1.0 / math.sqrt(Dh))
    v = v_ref[0]
    vb = v.astype(BF16)

    g = gt_ref[0] + gb_ref[...]
    logf = jnp.minimum(g, 0.0) - jnp.log(1.0 + jnp.exp(-jnp.abs(g)))
    row = lax.broadcasted_iota(jnp.int32, (L, L), 0)
    col = lax.broadcasted_iota(jnp.int32, (L, L), 1)
    causal = row >= col
    tril = causal.astype(F32)
    bcum = jnp.dot(tril, logf, preferred_element_type=F32, precision=lax.Precision.HIGHEST)
    lane = lax.broadcasted_iota(jnp.int32, g.shape, 1)
    a_cols = jnp.where(lane < H, g, bcum)
    a_rows = a_cols.T

    for h in range(H):
        hs = slice(h * Dh, (h + 1) * Dh)
        li_col = a_cols[:, h:h + 1]
        bc_col = a_cols[:, H + h:H + h + 1]
        li_row = a_rows[h:h + 1, :]
        bc_row = a_rows[H + h:H + h + 1, :]
        b_last = bc_row[:, L - 1:L]
        m_st = m_ref[h:h + 1, 0:1]
        n_st = n_ref[h:h + 1, :]
        ct_st = c_ref[h]
        qh = q[:, hs]
        kh = kf[:, hs]
        vh = vb[:, hs]

        d_mat = jnp.where(causal, bc_col - bc_row + li_row, -jnp.inf)
        m_inter = bc_col + m_st
        m_t = jnp.maximum(jnp.max(d_mat, axis=-1, keepdims=True), m_inter)
        s = lax.dot_general(qh, kh.astype(BF16), (((1,), (1,)), ((), ())),
                            preferred_element_type=F32) * jnp.exp(d_mat - m_t)
        w_inter = jnp.exp(m_inter - m_t)
        num = jnp.dot(s.astype(BF16), vh, preferred_element_type=F32) + w_inter * jnp.dot(
            qh, ct_st.astype(BF16), preferred_element_type=F32)
        den = jnp.sum(s, axis=-1, keepdims=True) + w_inter * jnp.sum(
            qh.astype(F32) * n_st, axis=-1, keepdims=True)
        hh = num / jnp.maximum(jnp.abs(den), jnp.exp(-m_t))
        hh = hh * lax.rsqrt(jnp.mean(hh * hh, axis=-1, keepdims=True) + EPS) * ng_ref[:, hs]
        y_ref[0, :, hs] = (_sigmoid(o_ref[0, :, hs].astype(F32)) * hh).astype(y_ref.dtype)

        a_end = b_last - bc_col + li_col
        m_loc = jnp.max(a_end, axis=0, keepdims=True)
        kw = kh * jnp.exp(a_end - m_loc)
        ct_loc = jnp.dot(kw.T.astype(BF16), vh, preferred_element_type=F32)
        n_loc = jnp.sum(kw, axis=0, keepdims=True)
        m_new = jnp.maximum(b_last + m_st, m_loc)
        s_prev = jnp.exp(b_last + m_st - m_new)
        s_loc = jnp.exp(m_loc - m_new)
        c_ref[h] = s_prev * ct_st + s_loc * ct_loc
        n_ref[h:h + 1, :] = s_prev * n_st + s_loc * n_loc
        m_ref[h:h + 1, :] = jnp.broadcast_to(m_new, (1, LANES))


def _mlstm(qk_pre, v, o_pre, gates, conv_w, conv_b, gate_b_pad, norm_g):
    B, S, W2 = qk_pre.shape
    W = W2 // 2
    H, L = MLSTM_HEADS, MLSTM_CHUNK
    Dh = W // H
    return pl.pallas_call(
        functools.partial(_mlstm_kernel, L=L, H=H, Dh=Dh),
        out_shape=jax.ShapeDtypeStruct((B, S, W), BF16),
        grid=(B, S // L),
        in_specs=[
            pl.BlockSpec((1, L, W2), lambda b, c: (b, c, 0)),
            pl.BlockSpec((1, L, W), lambda b, c: (b, c, 0)),
            pl.BlockSpec((1, L, W), lambda b, c: (b, c, 0)),
            pl.BlockSpec((1, L, LANES), lambda b, c: (b, c, 0)),
            pl.BlockSpec((CONV_WIDTH, W2), lambda b, c: (0, 0)),
            pl.BlockSpec((1, W2), lambda b, c: (0, 0)),
            pl.BlockSpec((1, LANES), lambda b, c: (0, 0)),
            pl.BlockSpec((1, W), lambda b, c: (0, 0)),
        ],
        out_specs=pl.BlockSpec((1, L, W), lambda b, c: (b, c, 0)),
        scratch_shapes=[
            pltpu.VMEM((L + SUBLANES, W2), F32),
            pltpu.VMEM((H, Dh, Dh), F32),
            pltpu.VMEM((SUBLANES, Dh), F32),
            pltpu.VMEM((SUBLANES, LANES), F32),
        ],
        compiler_params=_cparams(("parallel", "arbitrary")),
        name="mlstm",
    )(qk_pre, v, o_pre, gates, conv_w, conv_b.reshape(1, W2), gate_b_pad, norm_g.reshape(1, W))


def _out_proj_kernel(res_ref, a_ref, b_ref, wa_ref, wb_ref, o_ref):
    a = a_ref[...].reshape(a_ref.shape[-2:])
    b = b_ref[...].reshape(b_ref.shape[-2:])
    acc = jnp.dot(a, wa_ref[...], preferred_element_type=F32)
    acc = acc + jnp.dot(b, wb_ref[...], preferred_element_type=F32)
    o_ref[0] = res_ref[0] + acc


def _out_proj(res, ya, yb, w, tm, a_time_major):
    B, S, D = res.shape
    wa_n = w.shape[0] // 2
    wa, wb = w[:wa_n], w[wa_n:]
    if a_time_major:
        a_spec = pl.BlockSpec((tm, wa_n), lambda b, i: (i, b))
    else:
        a_spec = pl.BlockSpec((1, tm, wa_n), lambda b, i: (b, i, 0))
    return pl.pallas_call(
        _out_proj_kernel,
        out_shape=jax.ShapeDtypeStruct((B, S, D), F32),
        grid=(B, S // tm),
        in_specs=[
            pl.BlockSpec((1, tm, D), lambda b, i: (b, i, 0)),
            a_spec,
            pl.BlockSpec((1, tm, w.shape[0] - wa_n), lambda b, i: (b, i, 0)),
            pl.BlockSpec(wa.shape, lambda b, i: (0, 0)),
            pl.BlockSpec(wb.shape, lambda b, i: (0, 0)),
        ],
        out_specs=pl.BlockSpec((1, tm, D), lambda b, i: (b, i, 0)),
        compiler_params=_cparams(("parallel", "parallel")),
        name="out_proj",
    )(res, ya, yb, wa, wb)


def _ffn_kernel(x_ref, g_ref, wg_ref, wu_ref, wd_ref, o_ref, hn_ref, acc_ref):
    f = pl.program_id(1)

    @pl.when(f == 0)
    def _():
        hn_ref[...] = _rms_bf16(x_ref[...], g_ref[...])
        acc_ref[...] = jnp.zeros_like(acc_ref)

    hn = hn_ref[...]
    a = jnp.dot(hn, wg_ref[...], preferred_element_type=F32)
    u = jnp.dot(hn, wu_ref[...], preferred_element_type=F32)
    hmid = (a * _sigmoid(a) * u).astype(BF16)
    acc_ref[...] += jnp.dot(hmid, wd_ref[...], preferred_element_type=F32)

    @pl.when(f == pl.num_programs(1) - 1)
    def _():
        o_ref[...] = x_ref[...] + acc_ref[...]


def _ffn(x2d, g, wg, wu, wd, tm, fc):
    T, D = x2d.shape
    F = wg.shape[1]
    return pl.pallas_call(
        _ffn_kernel,
        out_shape=jax.ShapeDtypeStruct((T, D), F32),
        grid=(T // tm, F // fc),
        in_specs=[
            pl.BlockSpec((tm, D), lambda i, f: (i, 0)),
            pl.BlockSpec((1, D), lambda i, f: (0, 0)),
            pl.BlockSpec((D, fc), lambda i, f: (0, f)),
            pl.BlockSpec((D, fc), lambda i, f: (0, f)),
            pl.BlockSpec((fc, D), lambda i, f: (f, 0)),
        ],
        out_specs=pl.BlockSpec((tm, D), lambda i, f: (i, 0)),
        scratch_shapes=[pltpu.VMEM((tm, D), BF16), pltpu.VMEM((tm, D), F32)],
        compiler_params=_cparams(("parallel", "arbitrary")),
        name="ffn_swiglu",
    )(x2d, g.reshape(1, D), wg, wu, wd)


def _pick_tile(n, target):
    t = min(n, target)
    while n % t:
        t //= 2
    return t


def _layer_ab(h, norm_g, w_in, lam_re, lam_im, log_dt, b_re, b_im, c_re, c_im, d_skip, w_glu,
              conv_w, conv_b, gate_b, mlstm_norm_g, w_out):
    B, S, D = h.shape
    s5w = lam_re.shape[0] * S5_GROUP
    mw = mlstm_norm_g.shape[0]
    c_qk, c_v, c_o, c_if = s5w, s5w + 2 * mw, s5w + 3 * mw, s5w + 4 * mw
    n_if = w_in.shape[1] - c_if
    w_pad = jnp.pad(w_in, ((0, 0), (0, LANES - n_if))).astype(BF16)
    tm = _pick_tile(S, 512)
    u_tm, qk_pre, v, o_pre, gates = _norm_proj(
        h, norm_g, w_pad,
        [(0, c_qk, BF16, True), (c_qk, c_v, BF16, False), (c_v, c_o, BF16, False),
         (c_o, c_if, BF16, False), (c_if, c_if + LANES, F32, False)], tm)
    bmat, ar, ai, cmat = _s5_params(lam_re, lam_im, log_dt, b_re, b_im, c_re, c_im)
    y_a = _s5(u_tm.reshape(S * B, s5w), B, bmat, ar, ai, cmat, d_skip, w_glu.astype(BF16),
              lc=_pick_tile(S, 32))
    gate_b_pad = jnp.pad(gate_b, (0, LANES - n_if)).reshape(1, LANES)
    y_b = _mlstm(qk_pre, v, o_pre, gates, conv_w, conv_b, gate_b_pad, mlstm_norm_g)
    return _out_proj(h, y_a.reshape(S, B * s5w), y_b, w_out.astype(BF16), tm, True)


def _gmlp_kernel(u_ref, v_ref, ng_ref, nb_ref, ws_ref, bs_ref, y_ref, *, L, G, Dg, nchunk):
    u = _gelu(u_ref[0].astype(F32))
    v = _gelu(v_ref[0].astype(F32))
    mu = jnp.mean(v, axis=-1, keepdims=True)
    vc = v - mu
    var = jnp.mean(vc * vc, axis=-1, keepdims=True)
    vn = (vc * lax.rsqrt(var + EPS) * ng_ref[...] + nb_ref[...]).astype(BF16)
    for c in range(nchunk):
        rows = slice(c * L, (c + 1) * L)
        for g in range(G):
            cols = slice(g * Dg, (g + 1) * Dg)
            s = jnp.dot(ws_ref[g], vn[rows, cols], preferred_element_type=F32) + bs_ref[:, g:g + 1]
            y_ref[0, rows, cols] = (u[rows, cols] * s).astype(y_ref.dtype)


def _gmlp(u_pre, v_pre, norm_g, norm_b, w_s, b_s, tm):
    B, S, W = u_pre.shape
    G, L = GMLP_GROUPS, GMLP_CHUNK
    ws = (w_s * jnp.tril(jnp.ones((L, L), w_s.dtype))).astype(BF16)
    bs = jnp.pad(b_s.T, ((0, 0), (0, LANES - G)))
    return pl.pallas_call(
        functools.partial(_gmlp_kernel, L=L, G=G, Dg=W // G, nchunk=tm // L),
        out_shape=jax.ShapeDtypeStruct((B, S, W), BF16),
        grid=(B, S // tm),
        in_specs=[
            pl.BlockSpec((1, tm, W), lambda b, i: (b, i, 0)),
            pl.BlockSpec((1, tm, W), lambda b, i: (b, i, 0)),
            pl.BlockSpec((1, W), lambda b, i: (0, 0)),
            pl.BlockSpec((1, W), lambda b, i: (0, 0)),
            pl.BlockSpec((G, L, L), lambda b, i: (0, 0, 0)),
            pl.BlockSpec((L, LANES), lambda b, i: (0, 0)),
        ],
        out_specs=pl.BlockSpec((1, tm, W), lambda b, i: (b, i, 0)),
        compiler_params=_cparams(("parallel", "parallel")),
        name="gmlp",
    )(u_pre, v_pre, norm_g.reshape(1, W), norm_b.reshape(1, W), ws, bs)


NEG_BIG = -1e30


def _moba_kernel(q_ref, k_ref, v_ref, t0_ref, t1_ref, bfar_ref, y_ref, kmean_ref, *, BL, NB, Dh, topk):
    qi = pl.program_id(2)
    nt = (((1,), (1,)), ((), ()))

    @pl.when(qi == 0)
    def _():
        for n in range(NB):
            kmean_ref[n:n + 1, :] = jnp.mean(k_ref[0, n * BL:(n + 1) * BL, :].astype(F32), axis=0, keepdims=True)

    qf = q_ref[0].astype(F32)
    qs = (qf * (1.0 / math.sqrt(Dh))).astype(BF16)

    gate = lax.dot_general(kmean_ref[...], qf, nt, preferred_element_type=F32,
                           precision=lax.Precision.HIGHEST)
    nrow = lax.broadcasted_iota(jnp.int32, (NB, BL), 0)
    rank = jnp.zeros((NB, BL), F32)
    for m in range(NB):
        gm = gate[m:m + 1, :]
        beats = jnp.where(gm > gate, 1.0, jnp.where(gm == gate, jnp.where(nrow > m, 1.0, 0.0), 0.0))
        rank = rank + jnp.where(m < qi, beats, 0.0)
    sel_rows = jnp.where(nrow < qi, jnp.where(rank < topk, 1.0, 0.0), 0.0)
    sel_pad = jnp.concatenate([sel_rows, jnp.zeros((LANES - NB, BL), F32)], axis=0).astype(BF16)
    eye = (lax.broadcasted_iota(jnp.int32, (BL, BL), 0) ==
           lax.broadcasted_iota(jnp.int32, (BL, BL), 1)).astype(BF16)
    sel_cols = lax.dot_general(eye, sel_pad, nt, preferred_element_type=F32)
    lane = lax.broadcasted_iota(jnp.int32, (BL, LANES), 1)

    own = pl.ds(pl.multiple_of(qi * BL, BL), BL)
    s = lax.dot_general(qs, k_ref[0, own, :], nt, preferred_element_type=F32) + t0_ref[0]
    m0 = jnp.max(s, axis=-1, keepdims=True)
    p = jnp.exp(s - m0)
    l0 = jnp.sum(p, axis=-1, keepdims=True)
    acc0 = jnp.dot(p.astype(BF16), v_ref[0, own, :], preferred_element_type=F32)
    bfar = bfar_ref[0][:, 0:1]

    def body(kb, carry):
        m, l, acc = carry
        blk = pl.ds(pl.multiple_of(kb * BL, BL), BL)
        s = lax.dot_general(qs, k_ref[0, blk, :], nt, preferred_element_type=F32)
        bias = jnp.where(kb == qi - 1, t1_ref[0], bfar)
        selc = jnp.sum(jnp.where(lane == kb, sel_cols, 0.0), axis=-1, keepdims=True)
        s = jnp.where(selc > 0.5, s + bias, NEG_BIG)
        m_new = jnp.maximum(m, jnp.max(s, axis=-1, keepdims=True))
        alpha = jnp.exp(m - m_new)
        p = jnp.exp(s - m_new)
        l = alpha * l + jnp.sum(p, axis=-1, keepdims=True)
        acc = alpha * acc + jnp.dot(p.astype(BF16), v_ref[0, blk, :], preferred_element_type=F32)
        return m_new, l, acc

    m, l, acc = lax.fori_loop(0, qi, body, (m0, l0, acc0))
    y_ref[0] = (acc / l).astype(y_ref.dtype)


def _rel_bucket(n):
    max_exact = REL_BUCKETS // 2
    nf = jnp.maximum(n, 1).astype(F32)
    large = max_exact + (jnp.log(nf / max_exact) / math.log(REL_MAX_DIST / max_exact)
                         * (REL_BUCKETS - max_exact)).astype(jnp.int32)
    large = jnp.minimum(large, REL_BUCKETS - 1)
    return jnp.where(n < max_exact, n, large)


def _moba(qkv, rel_bias):
    B, S, W3 = qkv.shape
    H, BL = MOBA_HEADS, MOBA_BLOCK
    W = W3 // 3
    Dh = W // H
    NB = S // BL
    assert BL + 1 >= REL_MAX_DIST and Dh == LANES and S % BL == 0
    biasd = rel_bias.T[:, _rel_bucket(jnp.arange(2 * BL, dtype=jnp.int32))]
    i = jnp.arange(BL, dtype=jnp.int32)[:, None]
    j = jnp.arange(BL, dtype=jnp.int32)[None, :]
    t0 = jnp.where(i >= j, biasd[:, jnp.maximum(i - j, 0)], NEG_BIG)
    t1 = biasd[:, BL + i - j]
    bfar = jnp.broadcast_to(rel_bias.T[:, REL_BUCKETS - 1][:, None, None], (H, 1, LANES))
    return pl.pallas_call(
        functools.partial(_moba_kernel, BL=BL, NB=NB, Dh=Dh, topk=min(MOBA_TOPK, NB)),
        out_shape=jax.ShapeDtypeStruct((B, S, W), BF16),
        grid=(B, H, NB),
        in_specs=[
            pl.BlockSpec((1, BL, Dh), lambda b, h, i: (b, i, h)),
            pl.BlockSpec((1, S, Dh), lambda b, h, i: (b, 0, H + h)),
            pl.BlockSpec((1, S, Dh), lambda b, h, i: (b, 0, 2 * H + h)),
            pl.BlockSpec((1, BL, BL), lambda b, h, i: (h, 0, 0)),
            pl.BlockSpec((1, BL, BL), lambda b, h, i: (h, 0, 0)),
            pl.BlockSpec((1, 1, LANES), lambda b, h, i: (h, 0, 0)),
        ],
        out_specs=pl.BlockSpec((1, BL, Dh), lambda b, h, i: (b, i, h)),
        scratch_shapes=[pltpu.VMEM((NB, Dh), F32)],
        compiler_params=_cparams(("parallel", "parallel", "arbitrary")),
        name="moba",
    )(qkv, qkv, qkv, t0, t1, bfar)


def _layer_cd(h, norm_g, w_in, gmlp_norm_g, gmlp_norm_b, gmlp_w_s, gmlp_b_s, rel_bias, w_out):
    B, S, D = h.shape
    gw = gmlp_norm_g.shape[0]
    tm = _pick_tile(S, 512)
    u_pre, v_pre, qkv = _norm_proj(
        h, norm_g, w_in.astype(BF16),
        [(0, gw, BF16, False), (gw, 2 * gw, BF16, False), (2 * gw, w_in.shape[1], BF16, False)], tm)
    y_c = _gmlp(u_pre, v_pre, gmlp_norm_g, gmlp_norm_b, gmlp_w_s, gmlp_b_s, tm)
    y_d = _moba(qkv, rel_bias)
    return _out_proj(h, y_c, y_d, w_out.astype(BF16), tm, False)


def _router_kernel(x_ref, g_ref, wr_ref, rb_ref, xn_ref, idx_ref, gts_ref, cnt_ref, carry_ref, *, E):
    @pl.when(pl.program_id(0) == 0)
    def _():
        carry_ref[...] = jnp.zeros_like(carry_ref)

    x = x_ref[...]
    tm = x.shape[0]
    var = jnp.mean(x * x, axis=-1, keepdims=True)
    hn = x * lax.rsqrt(var + EPS) * g_ref[...]
    xn_ref[...] = hn.astype(xn_ref.dtype)
    logits = lax.dot_general(wr_ref[...], hn, (((1,), (1,)), ((), ())), preferred_element_type=F32,
                             precision=lax.Precision.HIGHEST) + rb_ref[:, 0:1]
    rowi = lax.broadcasted_iota(jnp.int32, (E, tm), 0)
    v1 = jnp.max(logits, axis=0, keepdims=True)
    e1 = jnp.min(jnp.where(logits == v1, rowi, E), axis=0, keepdims=True)
    masked = jnp.where(rowi == e1, -jnp.inf, logits)
    v2 = jnp.max(masked, axis=0, keepdims=True)
    e2 = jnp.min(jnp.where(masked == v2, rowi, E), axis=0, keepdims=True)
    ex = jnp.exp(v2 - v1)
    g1 = 1.0 / (1.0 + ex)
    g2 = ex / (1.0 + ex)
    oh1 = rowi == e1
    oh2 = rowi == e2
    cnt = jnp.where(oh1, 1.0, jnp.where(oh2, 1.0, 0.0))
    before = (lax.broadcasted_iota(jnp.int32, (tm, tm), 0) <
              lax.broadcasted_iota(jnp.int32, (tm, tm), 1)).astype(BF16)
    excl = jnp.dot(cnt.astype(BF16), before, preferred_element_type=F32) + carry_ref[:, 0:1]
    r1 = jnp.sum(jnp.where(oh1, excl, 0.0), axis=0, keepdims=True).astype(jnp.int32)
    r2 = jnp.sum(jnp.where(oh2, excl, 0.0), axis=0, keepdims=True).astype(jnp.int32)
    idx_ref[...] = jnp.where(rowi == 0, e1, jnp.where(rowi == 1, e2, jnp.where(rowi == 2, r1, jnp.where(rowi == 3, r2, 0))))
    gts_ref[...] = jnp.where(rowi == 0, g1, jnp.where(rowi == 1, g2, 0.0))
    carry_ref[...] = carry_ref[...] + jnp.sum(cnt, axis=1, keepdims=True)
    cnt_ref[...] = carry_ref[...]


def _router(x2d, g, router_w, router_b, tm):
    T, D = x2d.shape
    E = router_w.shape[1]
    assert E == SUBLANES
    return pl.pallas_call(
        functools.partial(_router_kernel, E=E),
        out_shape=[
            jax.ShapeDtypeStruct((T, D), BF16),
            jax.ShapeDtypeStruct((E, T), jnp.int32),
            jax.ShapeDtypeStruct((E, T), F32),
            jax.ShapeDtypeStruct((E, LANES), F32),
        ],
        grid=(T // tm,),
        in_specs=[
            pl.BlockSpec((tm, D), lambda i: (i, 0)),
            pl.BlockSpec((1, D), lambda i: (0, 0)),
            pl.BlockSpec((E, D), lambda i: (0, 0)),
            pl.BlockSpec((E, LANES), lambda i: (0, 0)),
        ],
        out_specs=[
            pl.BlockSpec((tm, D), lambda i: (i, 0)),
            pl.BlockSpec((E, tm), lambda i: (0, i)),
            pl.BlockSpec((E, tm), lambda i: (0, i)),
            pl.BlockSpec((E, LANES), lambda i: (0, 0)),
        ],
        scratch_shapes=[pltpu.VMEM((E, LANES), F32)],
        compiler_params=_cparams(("arbitrary",)),
        name="moe_router",
    )(x2d, g.reshape(1, D), router_w.T, jnp.broadcast_to(router_b[:, None], (E, LANES)))


def _gmm_kernel(te_ref, na_ref, x_ref, wg_ref, wu_ref, wd_ref, o_ref, acc_ref):
    i = pl.program_id(0)
    f = pl.program_id(1)
    active = i < na_ref[0]

    @pl.when(f == 0)
    def _():
        acc_ref[...] = jnp.zeros_like(acc_ref)

    @pl.when(active)
    def _():
        x = x_ref[...]
        a = jnp.dot(x, wg_ref[0], preferred_element_type=F32)
        u = jnp.dot(x, wu_ref[0], preferred_element_type=F32)
        hmid = (a * _sigmoid(a) * u).astype(BF16)
        acc_ref[...] += jnp.dot(hmid, wd_ref[0], preferred_element_type=F32)

    @pl.when(f == pl.num_programs(1) - 1)
    def _():
        o_ref[...] = acc_ref[...]


def _gmm(tile_expert, n_active, xs, wg, wu, wd, tm, fc):
    R, D = xs.shape
    F = wg.shape[2]
    nf = F // fc

    def fsel(i, f, na):
        return jnp.where(i < na[0], f, nf - 1)

    return pl.pallas_call(
        _gmm_kernel,
        out_shape=jax.ShapeDtypeStruct((R, D), F32),
        grid_spec=pltpu.PrefetchScalarGridSpec(
            num_scalar_prefetch=2,
            grid=(R // tm, nf),
            in_specs=[
                pl.BlockSpec((tm, D), lambda i, f, te, na: (i, 0)),
                pl.BlockSpec((1, D, fc), lambda i, f, te, na: (te[i], 0, fsel(i, f, na))),
                pl.BlockSpec((1, D, fc), lambda i, f, te, na: (te[i], 0, fsel(i, f, na))),
                pl.BlockSpec((1, fc, D), lambda i, f, te, na: (te[i], fsel(i, f, na), 0)),
            ],
            out_specs=pl.BlockSpec((tm, D), lambda i, f, te, na: (i, 0)),
            scratch_shapes=[pltpu.VMEM((tm, D), F32)],
        ),
        compiler_params=_cparams(("parallel", "arbitrary")),
        name="moe_gmm",
    )(tile_expert, n_active, xs, wg, wu, wd)


def _final_kernel(h_ref, ya_ref, yb_ref, ga_ref, gb_ref, g_ref, o_ref):
    h = h_ref[...] + ga_ref[...] * ya_ref[...] + gb_ref[...] * yb_ref[...]
    var = jnp.mean(h * h, axis=-1, keepdims=True)
    o_ref[...] = h * lax.rsqrt(var + EPS) * g_ref[...]


def _final(h2d, ya, yb, ga, gb, g, tm):
    T, D = h2d.shape
    row = pl.BlockSpec((tm, D), lambda i: (i, 0))
    colv = pl.BlockSpec((tm, 1), lambda i: (i, 0))
    return pl.pallas_call(
        _final_kernel,
        out_shape=jax.ShapeDtypeStruct((T, D), F32),
        grid=(T // tm,),
        in_specs=[row, row, row, colv, colv, pl.BlockSpec((1, D), lambda i: (0, 0))],
        out_specs=row,
        compiler_params=_cparams(("parallel",)),
        name="moe_combine_norm",
    )(h2d, ya, yb, ga, gb, g.reshape(1, D))


def _moe_final(h, norm_g, final_g, router_w, router_b, w_gate, w_up, w_down):
    B, S, D = h.shape
    T = B * S
    E = router_w.shape[1]
    h2d = h.reshape(T, D)
    xn, idx, gts, cnt = _router(h2d, norm_g, router_w, router_b, _pick_tile(T, 512))
    e1, e2, r1, r2 = idx[0], idx[1], idx[2], idx[3]
    tm = _pick_tile(T, 1024)
    counts = cnt[:, 0].astype(jnp.int32)
    padded = ((counts + tm - 1) // tm) * tm
    ends = jnp.cumsum(padded)
    offs = ends - padded
    pos1 = offs[e1] + r1
    pos2 = offs[e2] + r2
    n_tiles = (TOP_K * T) // tm + E
    tile_start = jnp.arange(n_tiles, dtype=jnp.int32) * tm
    tile_expert = jnp.minimum(jnp.searchsorted(ends, tile_start, side='right'), E - 1).astype(jnp.int32)
    n_active = (ends[-1] // tm).astype(jnp.int32).reshape(1)
    xs = jnp.zeros((n_tiles * tm, D), BF16).at[pos1].set(xn).at[pos2].set(xn)
    ys = _gmm(tile_expert, n_active, xs, w_gate.astype(BF16), w_up.astype(BF16), w_down.astype(BF16),
              tm, _pick_tile(w_gate.shape[2], 512))
    out = _final(h2d, ys[pos1], ys[pos2], gts[0][:, None], gts[1][:, None], final_g, _pick_tile(T, 512))
    return out.reshape(B, S, D)


def kernel(x, norm_mix_g, norm_ffn_g, norm_final_g, ab_w_in, s5_lambda_re, s5_lambda_im, s5_log_dt, s5_b_re, s5_b_im, s5_c_re, s5_c_im, s5_d, s5_w_glu, mlstm_conv_w, mlstm_conv_b, mlstm_gate_b, mlstm_norm_g, ab_w_out, ffn_w_gate, ffn_w_up, ffn_w_down, cd_w_in, gmlp_norm_g, gmlp_norm_b, gmlp_w_s, gmlp_b_s, rel_bias, cd_w_out, moe_router_w, moe_router_b, moe_w_gate, moe_w_up, moe_w_down):
    B, S, D = x.shape
    h = _layer_ab(x, norm_mix_g[0], ab_w_in[0], s5_lambda_re[0], s5_lambda_im[0], s5_log_dt[0],
                  s5_b_re[0], s5_b_im[0], s5_c_re[0], s5_c_im[0], s5_d[0], s5_w_glu[0],
                  mlstm_conv_w[0], mlstm_conv_b[0], mlstm_gate_b[0], mlstm_norm_g[0], ab_w_out[0])
    h = _ffn(h.reshape(B * S, D), norm_ffn_g[0], ffn_w_gate[0].astype(BF16), ffn_w_up[0].astype(BF16),
             ffn_w_down[0].astype(BF16), _pick_tile(B * S, 512), _pick_tile(ffn_w_gate.shape[2], 1408))
    h = _layer_cd(h.reshape(B, S, D), norm_mix_g[1], cd_w_in[0], gmlp_norm_g[0], gmlp_norm_b[0],
                  gmlp_w_s[0], gmlp_b_s[0], rel_bias, cd_w_out[0])
    return _moe_final(h, norm_ffn_g[1], norm_final_g, moe_router_w[0], moe_router_b[0],
                      moe_w_gate[0], moe_w_up[0], moe_w_down[0])
```

```python
import functools
import math

import jax
import jax.numpy as jnp
from jax import lax
from jax.experimental import pallas as pl
from jax.experimental.pallas import tpu as pltpu

F32 = jnp.float32
BF16 = jnp.bfloat16
EPS = 1e-5

LANES = 128
SUBLANES = 8
VMEM_LIMIT_BYTES = 56 * 1024 * 1024

S5_GROUP = 16
S5_STATE = 64
S5_SLAB_GROUPS = 8
MLSTM_HEADS = 4
MLSTM_CHUNK = 128
CONV_WIDTH = 4
GMLP_GROUPS = 4
GMLP_CHUNK = 128
MOBA_HEADS = 4
MOBA_BLOCK = 256
MOBA_TOPK = 3
REL_BUCKETS = 32
REL_MAX_DIST = 128
N_EXPERTS = 8
TOP_K = 2


def _cparams(sem):
    return pltpu.CompilerParams(dimension_semantics=sem, vmem_limit_bytes=VMEM_LIMIT_BYTES)


def _rms_bf16(x, g):
    var = jnp.mean(x * x, axis=-1, keepdims=True)
    return (x * lax.rsqrt(var + EPS) * g).astype(BF16)


def _gelu(x):
    return jax.nn.gelu(x, approximate=True)


def _sigmoid(x):
    return 1.0 / (1.0 + jnp.exp(-x))


def _norm_proj_kernel(x_ref, g_ref, w_ref, *out_refs, splits):
    hn = _rms_bf16(x_ref[0], g_ref[...])
    for o_ref, (c0, c1) in zip(out_refs, splits):
        r = jnp.dot(hn, w_ref[:, c0:c1], preferred_element_type=F32)
        o_ref[...] = r.reshape(o_ref.shape).astype(o_ref.dtype)


def _norm_proj(x, g, w, outs, tm):
    B, S, D = x.shape
    splits = tuple((c0, c1) for c0, c1, _, _ in outs)
    out_shape, out_specs = [], []
    for c0, c1, dt, time_major in outs:
        n = c1 - c0
        if time_major:
            out_shape.append(jax.ShapeDtypeStruct((S, B * n), dt))
            out_specs.append(pl.BlockSpec((tm, n), lambda b, i: (i, b)))
        else:
            out_shape.append(jax.ShapeDtypeStruct((B, S, n), dt))
            out_specs.append(pl.BlockSpec((1, tm, n), lambda b, i: (b, i, 0)))
    return pl.pallas_call(
        functools.partial(_norm_proj_kernel, splits=splits),
        out_shape=out_shape,
        grid=(B, S // tm),
        in_specs=[
            pl.BlockSpec((1, tm, D), lambda b, i: (b, i, 0)),
            pl.BlockSpec((1, D), lambda b, i: (0, 0)),
            pl.BlockSpec(w.shape, lambda b, i: (0, 0)),
        ],
        out_specs=out_specs,
        compiler_params=_cparams(("parallel", "parallel")),
        name="norm_proj",
    )(x, g.reshape(1, D), w)


def _s5_kernel(u_ref, bmat_ref, ar_ref, ai_ref, cmat_ref, d_ref, wglu_ref, y_ref,
               buf_ref, xr_ref, xi_ref, *, lc, nb, nslab, sw):
    @pl.when(pl.program_id(0) == 0)
    def _():
        xr_ref[...] = jnp.zeros_like(xr_ref)
        xi_ref[...] = jnp.zeros_like(xi_ref)

    u = u_ref[...]
    for k in range(nslab):
        buf_ref[:, 2 * sw * k:2 * sw * (k + 1)] = jnp.dot(
            u[:, LANES * k:LANES * (k + 1)], bmat_ref[k], preferred_element_type=F32)

    for k in range(nslab):
        re_cols = slice(2 * sw * k, 2 * sw * k + sw)
        im_cols = slice(2 * sw * k + sw, 2 * sw * (k + 1))
        st_cols = slice(sw * k, sw * (k + 1))
        ar = jnp.broadcast_to(ar_ref[:, st_cols], (nb, sw))
        ai = jnp.broadcast_to(ai_ref[:, st_cols], (nb, sw))

        def step(t, carry, re_cols=re_cols, im_cols=im_cols, ar=ar, ai=ai):
            xr, xi = carry
            rows = pl.ds(pl.multiple_of(t * nb, nb), nb)
            nxr = ar * xr - ai * xi + buf_ref[rows, re_cols]
            nxi = ar * xi + ai * xr + buf_ref[rows, im_cols]
            buf_ref[rows, re_cols] = nxr
            buf_ref[rows, im_cols] = nxi
            return nxr, nxi

        xr, xi = lax.fori_loop(0, lc, step, (xr_ref[:, st_cols], xi_ref[:, st_cols]), unroll=4)
        xr_ref[:, st_cols] = xr
        xi_ref[:, st_cols] = xi

    ys = []
    for k in range(nslab):
        st = buf_ref[:, 2 * sw * k:2 * sw * (k + 1)].astype(BF16)
        ys.append(jnp.dot(st, cmat_ref[k], preferred_element_type=F32))
    y = jnp.concatenate(ys, axis=-1)
    y = _gelu(y + d_ref[...] * u.astype(F32))
    gl = jnp.dot(y.astype(BF16), wglu_ref[...], preferred_element_type=F32)
    y_ref[...] = (y * _sigmoid(gl)).astype(y_ref.dtype)


def _s5_params(lam_re, lam_im, log_dt, b_re, b_im, c_re, c_im):
    G, P = lam_re.shape
    Hc = b_re.shape[-1]
    dt = jnp.exp(log_dt.astype(F32))[:, None]
    mag = jnp.exp(lam_re * dt)
    ar = mag * jnp.cos(lam_im * dt)
    ai = mag * jnp.sin(lam_im * dt)
    den = lam_re * lam_re + lam_im * lam_im
    cr = ((ar - 1.0) * lam_re + ai * lam_im) / den
    ci = (ai * lam_re - (ar - 1.0) * lam_im) / den
    bb_re = cr[..., None] * b_re - ci[..., None] * b_im
    bb_im = cr[..., None] * b_im + ci[..., None] * b_re
    gs = S5_SLAB_GROUPS
    nslab = G // gs
    eye = jnp.eye(gs, dtype=F32)

    def bd_in(b):
        b = b.reshape(nslab, gs, P, Hc)
        return jnp.einsum('kgph,gj->kghjp', b, eye).reshape(nslab, gs * Hc, gs * P)

    def bd_out(c):
        c = c.reshape(nslab, gs, Hc, P)
        return jnp.einsum('kghp,gj->kgpjh', c, eye).reshape(nslab, gs * P, gs * Hc)

    bmat = jnp.concatenate([bd_in(bb_re), bd_in(bb_im)], axis=-1).astype(BF16)
    cmat = jnp.concatenate([bd_out(c_re), -bd_out(c_im)], axis=1).astype(BF16)
    return bmat, ar.reshape(1, G * P), ai.reshape(1, G * P), cmat


def _s5(u_tm, nb, bmat, ar, ai, cmat, d_skip, w_glu, lc):
    R, W = u_tm.shape
    S = R // nb
    nslab = bmat.shape[0]
    sw = bmat.shape[2] // 2
    rows = lc * nb
    return pl.pallas_call(
        functools.partial(_s5_kernel, lc=lc, nb=nb, nslab=nslab, sw=sw),
        out_shape=jax.ShapeDtypeStruct((R, W), BF16),
        grid=(S // lc,),
        in_specs=[
            pl.BlockSpec((rows, W), lambda i: (i, 0)),
            pl.BlockSpec(bmat.shape, lambda i: (0, 0, 0)),
            pl.BlockSpec(ar.shape, lambda i: (0, 0)),
            pl.BlockSpec(ai.shape, lambda i: (0, 0)),
            pl.BlockSpec(cmat.shape, lambda i: (0, 0, 0)),
            pl.BlockSpec((1, W), lambda i: (0, 0)),
            pl.BlockSpec(w_glu.shape, lambda i: (0, 0)),
        ],
        out_specs=pl.BlockSpec((rows, W), lambda i: (i, 0)),
        scratch_shapes=[
            pltpu.VMEM((rows, 2 * sw * nslab), F32),
            pltpu.VMEM((nb, sw * nslab), F32),
            pltpu.VMEM((nb, sw * nslab), F32),
        ],
        compiler_params=_cparams(("arbitrary",)),
        name="s5",
    )(u_tm, bmat, ar, ai, cmat, d_skip.reshape(1, W), w_glu)


def _mlstm_kernel(qk_ref, v_ref, o_ref, gt_ref, cw_ref, cb_ref, gb_ref, ng_ref, y_ref,
                  xcat_ref, c_ref, n_ref, m_ref, *, L, H, Dh):
    W = H * Dh
    halo = SUBLANES

    @pl.when(pl.program_id(1) == 0)
    def _():
        xcat_ref[0:halo, :] = jnp.zeros((halo, 2 * W), F32)
        c_ref[...] = jnp.zeros_like(c_ref)
        n_ref[...] = jnp.zeros_like(n_ref)
        m_ref[...] = jnp.zeros_like(m_ref)

    xcat_ref[halo:halo + L, :] = qk_ref[0].astype(F32)
    conv = cb_ref[...] + cw_ref[0:1, :] * xcat_ref[halo:halo + L, :]
    for j in range(1, CONV_WIDTH):
        conv = conv + cw_ref[j:j + 1, :] * xcat_ref[halo - j:halo - j + L, :]
    xcat_ref[0:halo, :] = xcat_ref[L:L + halo, :]
    qk = conv * _sigmoid(conv)
    q = qk[:, :W].astype(BF16)
    kf = qk[:, W:] * (1.0 / math.sqrt(Dh))
    v = v_ref[0]
    vb = v.astype(BF16)

    g = gt_ref[0] + gb_ref[...]
    logf = jnp.minimum(g, 0.0) - jnp.log(1.0 + jnp.exp(-jnp.abs(g)))
    row = lax.broadcasted_iota(jnp.int32, (L, L), 0)
    col = lax.broadcasted_iota(jnp.int32, (L, L), 1)
    causal = row >= col
    tril = causal.astype(F32)
    bcum = jnp.dot(tril, logf, preferred_element_type=F32, precision=lax.Precision.HIGHEST)
    lane = lax.broadcasted_iota(jnp.int32, g.shape, 1)
    a_cols = jnp.where(lane < H, g, bcum)
    a_rows = a_cols.T

    for h in range(H):
        hs = slice(h * Dh, (h + 1) * Dh)
        li_col = a_cols[:, h:h + 1]
        bc_col = a_cols[:, H + h:H + h + 1]
        li_row = a_rows[h:h + 1, :]
        bc_row = a_rows[H + h:H + h + 1, :]
        b_last = bc_row[:, L - 1:L]
        m_st = m_ref[h:h + 1, 0:1]
        n_st = n_ref[h:h + 1, :]
        ct_st = c_ref[h]
        qh = q[:, hs]
        kh = kf[:, hs]
        vh = vb[:, hs]

        d_mat = jnp.where(causal, bc_col - bc_row + li_row, -jnp.inf)
        m_inter = bc_col + m_st
        m_t = jnp.maximum(jnp.max(d_mat, axis=-1, keepdims=True), m_inter)
        s = lax.dot_general(qh, kh.astype(BF16), (((1,), (1,)), ((), ())),
                            preferred_element_type=F32) * jnp.exp(d_mat - m_t)
        w_inter = jnp.exp(m_inter - m_t)
        num = jnp.dot(s.astype(BF16), vh, preferred_element_type=F32) + w_inter * jnp.dot(
            qh, ct_st.astype(BF16), preferred_element_type=F32)
        den = jnp.sum(s, axis=-1, keepdims=True) + w_inter * jnp.sum(
            qh.astype(F32) * n_st, axis=-1, keepdims=True)
        hh = num / jnp.maximum(jnp.abs(den), jnp.exp(-m_t))
        hh = hh * lax.rsqrt(jnp.mean(hh * hh, axis=-1, keepdims=True) + EPS) * ng_ref[:, hs]
        y_ref[0, :, hs] = (_sigmoid(o_ref[0, :, hs].astype(F32)) * hh).astype(y_ref.dtype)

        a_end = b_last - bc_col + li_col
        m_loc = jnp.max(a_end, axis=0, keepdims=True)
        kw = kh * jnp.exp(a_end - m_loc)
        ct_loc = jnp.dot(kw.T.astype(BF16), vh, preferred_element_type=F32)
        n_loc = jnp.sum(kw, axis=0, keepdims=True)
        m_new = jnp.maximum(b_last + m_st, m_loc)
        s_prev = jnp.exp(b_last + m_st - m_new)
        s_loc = jnp.exp(m_loc - m_new)
        c_ref[h] = s_prev * ct_st + s_loc * ct_loc
        n_ref[h:h + 1, :] = s_prev * n_st + s_loc * n_loc
        m_ref[h:h + 1, :] = jnp.broadcast_to(m_new, (1, LANES))


def _mlstm(qk_pre, v, o_pre, gates, conv_w, conv_b, gate_b_pad, norm_g):
    B, S, W2 = qk_pre.shape
    W = W2 // 2
    H, L = MLSTM_HEADS, MLSTM_CHUNK
    Dh = W // H
    return pl.pallas_call(
        functools.partial(_mlstm_kernel, L=L, H=H, Dh=Dh),
        out_shape=jax.ShapeDtypeStruct((B, S, W), BF16),
        grid=(B, S // L),
        in_specs=[
            pl.BlockSpec((1, L, W2), lambda b, c: (b, c, 0)),
            pl.BlockSpec((1, L, W), lambda b, c: (b, c, 0)),
            pl.BlockSpec((1, L, W), lambda b, c: (b, c, 0)),
            pl.BlockSpec((1, L, LANES), lambda b, c: (b, c, 0)),
            pl.BlockSpec((CONV_WIDTH, W2), lambda b, c: (0, 0)),
            pl.BlockSpec((1, W2), lambda b, c: (0, 0)),
            pl.BlockSpec((1, LANES), lambda b, c: (0, 0)),
            pl.BlockSpec((1, W), lambda b, c: (0, 0)),
        ],
        out_specs=pl.BlockSpec((1, L, W), lambda b, c: (b, c, 0)),
        scratch_shapes=[
            pltpu.VMEM((L + SUBLANES, W2), F32),
            pltpu.VMEM((H, Dh, Dh), F32),
            pltpu.VMEM((SUBLANES, Dh), F32),
            pltpu.VMEM((SUBLANES, LANES), F32),
        ],
        compiler_params=_cparams(("parallel", "arbitrary")),
        name="mlstm",
    )(qk_pre, v, o_pre, gates, conv_w, conv_b.reshape(1, W2), gate_b_pad, norm_g.reshape(1, W))


def _out_proj_kernel(res_ref, a_ref, b_ref, wa_ref, wb_ref, o_ref):
    a = a_ref[...].reshape(a_ref.shape[-2:])
    b = b_ref[...].reshape(b_ref.shape[-2:])
    acc = jnp.dot(a, wa_ref[...], preferred_element_type=F32)
    acc = acc + jnp.dot(b, wb_ref[...], preferred_element_type=F32)
    o_ref[0] = res_ref[0] + acc


def _out_proj(res, ya, yb, w, tm, a_time_major):
    B, S, D = res.shape
    wa_n = w.shape[0] // 2
    wa, wb = w[:wa_n], w[wa_n:]
    if a_time_major:
        a_spec = pl.BlockSpec((tm, wa_n), lambda b, i: (i, b))
    else:
        a_spec = pl.BlockSpec((1, tm, wa_n), lambda b, i: (b, i, 0))
    return pl.pallas_call(
        _out_proj_kernel,
        out_shape=jax.ShapeDtypeStruct((B, S, D), F32),
        grid=(B, S // tm),
        in_specs=[
            pl.BlockSpec((1, tm, D), lambda b, i: (b, i, 0)),
            a_spec,
            pl.BlockSpec((1, tm, w.shape[0] - wa_n), lambda b, i: (b, i, 0)),
            pl.BlockSpec(wa.shape, lambda b, i: (0, 0)),
            pl.BlockSpec(wb.shape, lambda b, i: (0, 0)),
        ],
        out_specs=pl.BlockSpec((1, tm, D), lambda b, i: (b, i, 0)),
        compiler_params=_cparams(("parallel", "parallel")),
        name="out_proj",
    )(res, ya, yb, wa, wb)


def _ffn_kernel(x_ref, g_ref, wg_ref, wu_ref, wd_ref, o_ref, hn_ref, acc_ref):
    f = pl.program_id(1)

    @pl.when(f == 0)
    def _():
        hn_ref[...] = _rms_bf16(x_ref[...], g_ref[...])
        acc_ref[...] = jnp.zeros_like(acc_ref)

    hn = hn_ref[...]
    a = jnp.dot(hn, wg_ref[...], preferred_element_type=F32)
    u = jnp.dot(hn, wu_ref[...], preferred_element_type=F32)
    hmid = (a * _sigmoid(a) * u).astype(BF16)
    acc_ref[...] += jnp.dot(hmid, wd_ref[...], preferred_element_type=F32)

    @pl.when(f == pl.num_programs(1) - 1)
    def _():
        o_ref[...] = x_ref[...] + acc_ref[...]


def _ffn(x2d, g, wg, wu, wd, tm, fc):
    T, D = x2d.shape
    F = wg.shape[1]
    return pl.pallas_call(
        _ffn_kernel,
        out_shape=jax.ShapeDtypeStruct((T, D), F32),
        grid=(T // tm, F // fc),
        in_specs=[
            pl.BlockSpec((tm, D), lambda i, f: (i, 0)),
            pl.BlockSpec((1, D), lambda i, f: (0, 0)),
            pl.BlockSpec((D, fc), lambda i, f: (0, f)),
            pl.BlockSpec((D, fc), lambda i, f: (0, f)),
            pl.BlockSpec((fc, D), lambda i, f: (f, 0)),
        ],
        out_specs=pl.BlockSpec((tm, D), lambda i, f: (i, 0)),
        scratch_shapes=[pltpu.VMEM((tm, D), BF16), pltpu.VMEM((tm, D), F32)],
        compiler_params=_cparams(("parallel", "arbitrary")),
        name="ffn_swiglu",
    )(x2d, g.reshape(1, D), wg, wu, wd)


def _pick_tile(n, target):
    t = min(n, target)
    while n % t:
        t //= 2
    return t


def _layer_ab(h, norm_g, w_in, lam_re, lam_im, log_dt, b_re, b_im, c_re, c_im, d_skip, w_glu,
              conv_w, conv_b, gate_b, mlstm_norm_g, w_out):
    B, S, D = h.shape
    s5w = lam_re.shape[0] * S5_GROUP
    mw = mlstm_norm_g.shape[0]
    c_qk, c_v, c_o, c_if = s5w, s5w + 2 * mw, s5w + 3 * mw, s5w + 4 * mw
    n_if = w_in.shape[1] - c_if
    w_pad = jnp.pad(w_in, ((0, 0), (0, LANES - n_if))).astype(BF16)
    tm = _pick_tile(S, 512)
    u_tm, qk_pre, v, o_pre, gates = _norm_proj(
        h, norm_g, w_pad,
        [(0, c_qk, BF16, True), (c_qk, c_v, BF16, False), (c_v, c_o, BF16, False),
         (c_o, c_if, BF16, False), (c_if, c_if + LANES, F32, False)], tm)
    bmat, ar, ai, cmat = _s5_params(lam_re, lam_im, log_dt, b_re, b_im, c_re, c_im)
    y_a = _s5(u_tm.reshape(S * B, s5w), B, bmat, ar, ai, cmat, d_skip, w_glu.astype(BF16),
              lc=_pick_tile(S, 32))
    gate_b_pad = jnp.pad(gate_b, (0, LANES - n_if)).reshape(1, LANES)
    y_b = _mlstm(qk_pre, v, o_pre, gates, conv_w, conv_b, gate_b_pad, mlstm_norm_g)
    return _out_proj(h, y_a.reshape(S, B * s5w), y_b, w_out.astype(BF16), tm, True)


def _gmlp_kernel(u_ref, v_ref, ng_ref, nb_ref, ws_ref, bs_ref, y_ref, *, L, G, Dg, nchunk):
    u = _gelu(u_ref[0].astype(F32))
    v = _gelu(v_ref[0].astype(F32))
    mu = jnp.mean(v, axis=-1, keepdims=True)
    vc = v - mu
    var = jnp.mean(vc * vc, axis=-1, keepdims=True)
    vn = (vc * lax.rsqrt(var + EPS) * ng_ref[...] + nb_ref[...]).astype(BF16)
    for c in range(nchunk):
        rows = slice(c * L, (c + 1) * L)
        for g in range(G):
            cols = slice(g * Dg, (g + 1) * Dg)
            s = jnp.dot(ws_ref[g], vn[rows, cols], preferred_element_type=F32) + bs_ref[:, g:g + 1]
            y_ref[0, rows, cols] = (u[rows, cols] * s).astype(y_ref.dtype)


def _gmlp(u_pre, v_pre, norm_g, norm_b, w_s, b_s, tm):
    B, S, W = u_pre.shape
    G, L = GMLP_GROUPS, GMLP_CHUNK
    ws = (w_s * jnp.tril(jnp.ones((L, L), w_s.dtype))).astype(BF16)
    bs = jnp.pad(b_s.T, ((0, 0), (0, LANES - G)))
    return pl.pallas_call(
        functools.partial(_gmlp_kernel, L=L, G=G, Dg=W // G, nchunk=tm // L),
        out_shape=jax.ShapeDtypeStruct((B, S, W), BF16),
        grid=(B, S // tm),
        in_specs=[
            pl.BlockSpec((1, tm, W), lambda b, i: (b, i, 0)),
            pl.BlockSpec((1, tm, W), lambda b, i: (b, i, 0)),
            pl.BlockSpec((1, W), lambda b, i: (0, 0)),
            pl.BlockSpec((1, W), lambda b, i: (0, 0)),
            pl.BlockSpec((G, L, L), lambda b, i: (0, 0, 0)),
            pl.BlockSpec((L, LANES), lambda b, i: (0, 0)),
        ],
        out_specs=pl.BlockSpec((1, tm, W), lambda b, i: (b, i, 0)),
        compiler_params=_cparams(("parallel", "parallel")),
        name="gmlp",
    )(u_pre, v_pre, norm_g.reshape(1, W), norm_b.reshape(1, W), ws, bs)


NEG_BIG = -1e30


def _moba_kernel(q_ref, k_ref, v_ref, t0_ref, t1_ref, bfar_ref, y_ref, vext_ref, s_ref, p_ref,
                 *, BL, NB, Dh, topk):
    nt = (((1,), (1,)), ((), ()))
    scale = 1.0 / math.sqrt(Dh)
    vext_ref[:, :Dh] = v_ref[0]
    vext_ref[:, Dh:] = jnp.ones((NB * BL, Dh), BF16)
    bfar = bfar_ref[0][:, 0:1]
    kmean = jnp.concatenate(
        [jnp.mean(k_ref[0, n * BL:(n + 1) * BL, :].astype(F32), axis=0, keepdims=True) for n in range(NB)],
        axis=0)
    nrow = lax.broadcasted_iota(jnp.int32, (NB, BL), 0)
    eye = (lax.broadcasted_iota(jnp.int32, (BL, BL), 0) ==
           lax.broadcasted_iota(jnp.int32, (BL, BL), 1)).astype(BF16)

    for qi in range(NB):
        rows = slice(qi * BL, (qi + 1) * BL)
        qf = q_ref[0, rows, :].astype(F32)
        qs = (qf * scale).astype(BF16)
        neg_cols = None
        if qi > topk:
            gate = lax.dot_general(kmean, qf, nt, preferred_element_type=F32,
                                   precision=lax.Precision.HIGHEST)
            rank = jnp.zeros((NB, BL), F32)
            for m in range(qi):
                gm = gate[m:m + 1, :]
                rank = rank + jnp.where(gm > gate, 1.0, jnp.where(gm == gate, jnp.where(nrow > m, 1.0, 0.0), 0.0))
            sel_rows = jnp.where(nrow < qi, jnp.where(rank < topk, 1.0, 0.0), 0.0)
            sel_pad = jnp.concatenate([sel_rows, jnp.zeros((LANES - NB, BL), F32)], axis=0).astype(BF16)
            sel_cols = lax.dot_general(eye, sel_pad, nt, preferred_element_type=F32)
            neg_cols = (1.0 - sel_cols) * NEG_BIG

        m_tile = None
        for kb in range(qi + 1):
            s = lax.dot_general(qs, k_ref[0, kb * BL:(kb + 1) * BL, :], nt, preferred_element_type=F32)
            if kb == qi:
                s = s + t0_ref[0]
            elif kb == qi - 1:
                s = s + t1_ref[0]
                if neg_cols is not None:
                    s = s + neg_cols[:, kb:kb + 1]
            else:
                s = s + (bfar if neg_cols is None else bfar + neg_cols[:, kb:kb + 1])
            s_ref[:, kb * BL:(kb + 1) * BL] = s
            for c in range(BL // LANES):
                piece = s[:, c * LANES:(c + 1) * LANES]
                m_tile = piece if m_tile is None else jnp.maximum(m_tile, piece)
        m = jnp.max(m_tile, axis=-1, keepdims=True)
        nk = (qi + 1) * BL
        p_ref[:, :nk] = jnp.exp(s_ref[:, :nk] - m).astype(BF16)
        acc = jnp.dot(p_ref[:, :nk], vext_ref[:nk, :], preferred_element_type=F32)
        y_ref[0, rows, :] = (acc[:, :Dh] / acc[:, Dh:Dh + 1]).astype(y_ref.dtype)


def _rel_bucket(n):
    max_exact = REL_BUCKETS // 2
    nf = jnp.maximum(n, 1).astype(F32)
    large = max_exact + (jnp.log(nf / max_exact) / math.log(REL_MAX_DIST / max_exact)
                         * (REL_BUCKETS - max_exact)).astype(jnp.int32)
    large = jnp.minimum(large, REL_BUCKETS - 1)
    return jnp.where(n < max_exact, n, large)


def _moba(qkv, rel_bias):
    B, S, W3 = qkv.shape
    H, BL = MOBA_HEADS, MOBA_BLOCK
    W = W3 // 3
    Dh = W // H
    NB = S // BL
    assert BL + 1 >= REL_MAX_DIST and Dh == LANES and S % BL == 0
    biasd = rel_bias.T[:, _rel_bucket(jnp.arange(2 * BL, dtype=jnp.int32))]
    i = jnp.arange(BL, dtype=jnp.int32)[:, None]
    j = jnp.arange(BL, dtype=jnp.int32)[None, :]
    t0 = jnp.where(i >= j, biasd[:, jnp.maximum(i - j, 0)], NEG_BIG)
    t1 = biasd[:, BL + i - j]
    bfar = jnp.broadcast_to(rel_bias.T[:, REL_BUCKETS - 1][:, None, None], (H, 1, LANES))
    return pl.pallas_call(
        functools.partial(_moba_kernel, BL=BL, NB=NB, Dh=Dh, topk=min(MOBA_TOPK, NB)),
        out_shape=jax.ShapeDtypeStruct((B, S, W), BF16),
        grid=(B, H),
        in_specs=[
            pl.BlockSpec((1, S, Dh), lambda b, h: (b, 0, h)),
            pl.BlockSpec((1, S, Dh), lambda b, h: (b, 0, H + h)),
            pl.BlockSpec((1, S, Dh), lambda b, h: (b, 0, 2 * H + h)),
            pl.BlockSpec((1, BL, BL), lambda b, h: (h, 0, 0)),
            pl.BlockSpec((1, BL, BL), lambda b, h: (h, 0, 0)),
            pl.BlockSpec((1, 1, LANES), lambda b, h: (h, 0, 0)),
        ],
        out_specs=pl.BlockSpec((1, S, Dh), lambda b, h: (b, 0, h)),
        scratch_shapes=[
            pltpu.VMEM((S, 2 * Dh), BF16),
            pltpu.VMEM((BL, S), F32),
            pltpu.VMEM((BL, S), BF16),
        ],
        compiler_params=_cparams(("parallel", "parallel")),
        name="moba",
    )(qkv, qkv, qkv, t0, t1, bfar)


def _layer_cd(h, norm_g, w_in, gmlp_norm_g, gmlp_norm_b, gmlp_w_s, gmlp_b_s, rel_bias, w_out):
    B, S, D = h.shape
    gw = gmlp_norm_g.shape[0]
    tm = _pick_tile(S, 512)
    u_pre, v_pre, qkv = _norm_proj(
        h, norm_g, w_in.astype(BF16),
        [(0, gw, BF16, False), (gw, 2 * gw, BF16, False), (2 * gw, w_in.shape[1], BF16, False)], tm)
    y_c = _gmlp(u_pre, v_pre, gmlp_norm_g, gmlp_norm_b, gmlp_w_s, gmlp_b_s, tm)
    y_d = _moba(qkv, rel_bias)
    return _out_proj(h, y_c, y_d, w_out.astype(BF16), tm, False)


def _router_kernel(x_ref, g_ref, wr_ref, rb_ref, idx_ref, gts_ref, cnt_ref, carry_ref, *, E):
    @pl.when(pl.program_id(0) == 0)
    def _():
        carry_ref[...] = jnp.zeros_like(carry_ref)

    x = x_ref[...]
    tm = x.shape[0]
    var = jnp.mean(x * x, axis=-1, keepdims=True)
    hn = x * lax.rsqrt(var + EPS) * g_ref[...]
    logits = lax.dot_general(wr_ref[...], hn, (((1,), (1,)), ((), ())), preferred_element_type=F32,
                             precision=lax.Precision.HIGHEST) + rb_ref[:, 0:1]
    rowi = lax.broadcasted_iota(jnp.int32, (E, tm), 0)
    v1 = jnp.max(logits, axis=0, keepdims=True)
    e1 = jnp.min(jnp.where(logits == v1, rowi, E), axis=0, keepdims=True)
    masked = jnp.where(rowi == e1, -jnp.inf, logits)
    v2 = jnp.max(masked, axis=0, keepdims=True)
    e2 = jnp.min(jnp.where(masked == v2, rowi, E), axis=0, keepdims=True)
    ex = jnp.exp(v2 - v1)
    g1 = 1.0 / (1.0 + ex)
    g2 = ex / (1.0 + ex)
    oh1 = rowi == e1
    oh2 = rowi == e2
    cnt = jnp.where(oh1, 1.0, jnp.where(oh2, 1.0, 0.0))
    before = (lax.broadcasted_iota(jnp.int32, (tm, tm), 0) <
              lax.broadcasted_iota(jnp.int32, (tm, tm), 1)).astype(BF16)
    excl = jnp.dot(cnt.astype(BF16), before, preferred_element_type=F32) + carry_ref[:, 0:1]
    r1 = jnp.sum(jnp.where(oh1, excl, 0.0), axis=0, keepdims=True).astype(jnp.int32)
    r2 = jnp.sum(jnp.where(oh2, excl, 0.0), axis=0, keepdims=True).astype(jnp.int32)
    idx_ref[...] = jnp.where(rowi == 0, e1, jnp.where(rowi == 1, e2, jnp.where(rowi == 2, r1, jnp.where(rowi == 3, r2, 0))))
    gts_ref[...] = jnp.where(rowi == 0, g1, jnp.where(rowi == 1, g2, 0.0))
    carry_ref[...] = carry_ref[...] + jnp.sum(cnt, axis=1, keepdims=True)
    cnt_ref[...] = carry_ref[...]


def _router(x2d, g, router_w, router_b, tm):
    T, D = x2d.shape
    E = router_w.shape[1]
    assert E == SUBLANES
    return pl.pallas_call(
        functools.partial(_router_kernel, E=E),
        out_shape=[
            jax.ShapeDtypeStruct((E, T), jnp.int32),
            jax.ShapeDtypeStruct((E, T), F32),
            jax.ShapeDtypeStruct((E, LANES), F32),
        ],
        grid=(T // tm,),
        in_specs=[
            pl.BlockSpec((tm, D), lambda i: (i, 0)),
            pl.BlockSpec((1, D), lambda i: (0, 0)),
            pl.BlockSpec((E, D), lambda i: (0, 0)),
            pl.BlockSpec((E, LANES), lambda i: (0, 0)),
        ],
        out_specs=[
            pl.BlockSpec((E, tm), lambda i: (0, i)),
            pl.BlockSpec((E, tm), lambda i: (0, i)),
            pl.BlockSpec((E, LANES), lambda i: (0, 0)),
        ],
        scratch_shapes=[pltpu.VMEM((E, LANES), F32)],
        compiler_params=_cparams(("arbitrary",)),
        name="moe_router",
    )(x2d, g.reshape(1, D), router_w.T, jnp.broadcast_to(router_b[:, None], (E, LANES)))


def _dispatch_rows(pos1_ref, pos2_ref, src_hbm, dst_hbm, sem, c, chunk):
    def row_copies(r):
        t = c * chunk + r
        src = src_hbm.at[pl.ds(t, 1)]
        return (pltpu.make_async_copy(src, dst_hbm.at[pl.ds(pos1_ref[t], 1)], sem),
                pltpu.make_async_copy(src, dst_hbm.at[pl.ds(pos2_ref[t], 1)], sem))
    return row_copies


def _dispatch_kernel(pos1_ref, pos2_ref, h_hbm, xs_in_hbm, xs_hbm, sem, *, n_chunks, chunk):
    del xs_in_hbm

    def start_chunk(c):
        copies = _dispatch_rows(pos1_ref, pos2_ref, h_hbm, xs_hbm, sem, c, chunk)

        def body(r, carry):
            a, b = copies(r)
            a.start()
            b.start()
            return carry
        lax.fori_loop(0, chunk, body, 0)

    def wait_chunk(c):
        copies = _dispatch_rows(pos1_ref, pos2_ref, h_hbm, xs_hbm, sem, c, chunk)

        def body(r, carry):
            a, b = copies(r)
            a.wait()
            b.wait()
            return carry
        lax.fori_loop(0, chunk, body, 0)

    c = pl.program_id(0)
    start_chunk(c)

    @pl.when(c > 0)
    def _():
        wait_chunk(c - 1)

    @pl.when(c == n_chunks - 1)
    def _():
        wait_chunk(c)


def _dispatch(pos1, pos2, h2d, n_rows, chunk):
    T, D = h2d.shape
    xs0 = jnp.zeros((n_rows, D), h2d.dtype)
    return pl.pallas_call(
        functools.partial(_dispatch_kernel, n_chunks=T // chunk, chunk=chunk),
        out_shape=jax.ShapeDtypeStruct((n_rows, D), h2d.dtype),
        grid_spec=pltpu.PrefetchScalarGridSpec(
            num_scalar_prefetch=2,
            grid=(T // chunk,),
            in_specs=[pl.BlockSpec(memory_space=pl.ANY), pl.BlockSpec(memory_space=pl.ANY)],
            out_specs=pl.BlockSpec(memory_space=pl.ANY),
            scratch_shapes=[pltpu.SemaphoreType.DMA],
        ),
        input_output_aliases={3: 0},
        compiler_params=_cparams(("arbitrary",)),
        name="moe_dispatch",
    )(pos1, pos2, h2d, xs0)


def _gmm_kernel(te_ref, na_ref, x_ref, g_ref, wg_ref, wu_ref, wd_ref, o_ref, xn_ref, acc_ref):
    i = pl.program_id(0)
    f = pl.program_id(1)
    active = i < na_ref[0]

    @pl.when(f == 0)
    def _():
        xn_ref[...] = _rms_bf16(x_ref[...], g_ref[...])
        acc_ref[...] = jnp.zeros_like(acc_ref)

    @pl.when(active)
    def _():
        x = xn_ref[...]
        a = jnp.dot(x, wg_ref[0], preferred_element_type=F32)
        u = jnp.dot(x, wu_ref[0], preferred_element_type=F32)
        hmid = (a * _sigmoid(a) * u).astype(BF16)
        acc_ref[...] += jnp.dot(hmid, wd_ref[0], preferred_element_type=F32)

    @pl.when(f == pl.num_programs(1) - 1)
    def _():
        o_ref[...] = acc_ref[...]


def _gmm(tile_expert, n_active, xs, g, wg, wu, wd, tm, fc):
    R, D = xs.shape
    F = wg.shape[2]
    nf = F // fc

    def fsel(i, f, na):
        return jnp.where(i < na[0], f, nf - 1)

    return pl.pallas_call(
        _gmm_kernel,
        out_shape=jax.ShapeDtypeStruct((R, D), F32),
        grid_spec=pltpu.PrefetchScalarGridSpec(
            num_scalar_prefetch=2,
            grid=(R // tm, nf),
            in_specs=[
                pl.BlockSpec((tm, D), lambda i, f, te, na: (i, 0)),
                pl.BlockSpec((1, D), lambda i, f, te, na: (0, 0)),
                pl.BlockSpec((1, D, fc), lambda i, f, te, na: (te[i], 0, fsel(i, f, na))),
                pl.BlockSpec((1, D, fc), lambda i, f, te, na: (te[i], 0, fsel(i, f, na))),
                pl.BlockSpec((1, fc, D), lambda i, f, te, na: (te[i], fsel(i, f, na), 0)),
            ],
            out_specs=pl.BlockSpec((tm, D), lambda i, f, te, na: (i, 0)),
            scratch_shapes=[pltpu.VMEM((tm, D), BF16), pltpu.VMEM((tm, D), F32)],
        ),
        compiler_params=_cparams(("parallel", "arbitrary")),
        name="moe_gmm",
    )(tile_expert, n_active, xs, g.reshape(1, D), wg, wu, wd)


def _combine_copies(pos1_ref, pos2_ref, ys_hbm, ya_ref, yb_ref, sems, tile, slot, tm):
    def row_copies(r):
        t = tile * tm + r
        return (pltpu.make_async_copy(ys_hbm.at[pl.ds(pos1_ref[t], 1)], ya_ref.at[slot, pl.ds(r, 1)], sems.at[slot]),
                pltpu.make_async_copy(ys_hbm.at[pl.ds(pos2_ref[t], 1)], yb_ref.at[slot, pl.ds(r, 1)], sems.at[slot]))
    return row_copies


def _combine_kernel(pos1_ref, pos2_ref, h_ref, ga_ref, gb_ref, g_ref, ys_hbm, o_ref, ya_ref, yb_ref, sems, *, tm):
    i = pl.program_id(0)
    n = pl.num_programs(0)
    slot = lax.rem(i, 2)

    def start_tile(tile, slot):
        copies = _combine_copies(pos1_ref, pos2_ref, ys_hbm, ya_ref, yb_ref, sems, tile, slot, tm)

        def body(r, carry):
            a, b = copies(r)
            a.start()
            b.start()
            return carry
        lax.fori_loop(0, tm, body, 0)

    @pl.when(i == 0)
    def _():
        start_tile(0, 0)

    @pl.when(i + 1 < n)
    def _():
        start_tile(i + 1, 1 - slot)

    copies = _combine_copies(pos1_ref, pos2_ref, ys_hbm, ya_ref, yb_ref, sems, i, slot, tm)

    def wait_body(r, carry):
        a, b = copies(r)
        a.wait()
        b.wait()
        return carry
    lax.fori_loop(0, tm, wait_body, 0)

    h = h_ref[...] + ga_ref[...] * ya_ref[slot] + gb_ref[...] * yb_ref[slot]
    var = jnp.mean(h * h, axis=-1, keepdims=True)
    o_ref[...] = h * lax.rsqrt(var + EPS) * g_ref[...]


def _combine(pos1, pos2, h2d, ga, gb, g, ys, tm):
    T, D = h2d.shape
    row = pl.BlockSpec((tm, D), lambda i, p1, p2: (i, 0))
    colv = pl.BlockSpec((tm, 1), lambda i, p1, p2: (i, 0))
    return pl.pallas_call(
        functools.partial(_combine_kernel, tm=tm),
        out_shape=jax.ShapeDtypeStruct((T, D), F32),
        grid_spec=pltpu.PrefetchScalarGridSpec(
            num_scalar_prefetch=2,
            grid=(T // tm,),
            in_specs=[row, colv, colv, pl.BlockSpec((1, D), lambda i, p1, p2: (0, 0)),
                      pl.BlockSpec(memory_space=pl.ANY)],
            out_specs=row,
            scratch_shapes=[pltpu.VMEM((2, tm, D), F32), pltpu.VMEM((2, tm, D), F32),
                            pltpu.SemaphoreType.DMA((2,))],
        ),
        compiler_params=_cparams(("arbitrary",)),
        name="moe_combine_norm",
    )(pos1, pos2, h2d, ga, gb, g.reshape(1, D), ys)


def _moe_final(h, norm_g, final_g, router_w, router_b, w_gate, w_up, w_down):
    B, S, D = h.shape
    T = B * S
    E = router_w.shape[1]
    h2d = h.reshape(T, D)
    idx, gts, cnt = _router(h2d, norm_g, router_w, router_b, _pick_tile(T, 512))
    e1, e2, r1, r2 = idx[0], idx[1], idx[2], idx[3]
    tm = _pick_tile(T, 1024)
    counts = cnt[:, 0].astype(jnp.int32)
    padded = ((counts + tm - 1) // tm) * tm
    ends = jnp.cumsum(padded)
    offs = ends - padded
    pos1 = offs[e1] + r1
    pos2 = offs[e2] + r2
    n_tiles = (TOP_K * T) // tm + E
    tile_start = jnp.arange(n_tiles, dtype=jnp.int32) * tm
    tile_expert = jnp.minimum(jnp.searchsorted(ends, tile_start, side='right'), E - 1).astype(jnp.int32)
    n_active = (ends[-1] // tm).astype(jnp.int32).reshape(1)
    xs = _dispatch(pos1, pos2, h2d, n_tiles * tm, _pick_tile(T, 512))
    ys = _gmm(tile_expert, n_active, xs, norm_g, w_gate.astype(BF16), w_up.astype(BF16), w_down.astype(BF16),
              tm, _pick_tile(w_gate.shape[2], 512))
    out = _combine(pos1, pos2, h2d, gts[0][:, None], gts[1][:, None], final_g, ys, _pick_tile(T, 256))
    return out.reshape(B, S, D)


def kernel(x, norm_mix_g, norm_ffn_g, norm_final_g, ab_w_in, s5_lambda_re, s5_lambda_im, s5_log_dt, s5_b_re, s5_b_im, s5_c_re, s5_c_im, s5_d, s5_w_glu, mlstm_conv_w, mlstm_conv_b, mlstm_gate_b, mlstm_norm_g, ab_w_out, ffn_w_gate, ffn_w_up, ffn_w_down, cd_w_in, gmlp_norm_g, gmlp_norm_b, gmlp_w_s, gmlp_b_s, rel_bias, cd_w_out, moe_router_w, moe_router_b, moe_w_gate, moe_w_up, moe_w_down):
    B, S, D = x.shape
    h = _layer_ab(x, norm_mix_g[0], ab_w_in[0], s5_lambda_re[0], s5_lambda_im[0], s5_log_dt[0],
                  s5_b_re[0], s5_b_im[0], s5_c_re[0], s5_c_im[0], s5_d[0], s5_w_glu[0],
                  mlstm_conv_w[0], mlstm_conv_b[0], mlstm_gate_b[0], mlstm_norm_g[0], ab_w_out[0])
    h = _ffn(h.reshape(B * S, D), norm_ffn_g[0], ffn_w_gate[0].astype(BF16), ffn_w_up[0].astype(BF16),
             ffn_w_down[0].astype(BF16), _pick_tile(B * S, 512), _pick_tile(ffn_w_gate.shape[2], 1408))
    h = _layer_cd(h.reshape(B, S, D), norm_mix_g[1], cd_w_in[0], gmlp_norm_g[0], gmlp_norm_b[0],
                  gmlp_w_s[0], gmlp_b_s[0], rel_bias, cd_w_out[0])
    return _moe_final(h, norm_ffn_g[1], norm_final_g, moe_router_w[0], moe_router_b[0],
                      moe_w_gate[0], moe_w_up[0], moe_w_down[0])
```

```python
import functools
import math

import jax
import jax.numpy as jnp
from jax import lax
from jax.experimental import pallas as pl
from jax.experimental.pallas import tpu as pltpu

F32 = jnp.float32
BF16 = jnp.bfloat16
EPS = 1e-5

LANES = 128
SUBLANES = 8
VMEM_LIMIT_BYTES = 56 * 1024 * 1024

S5_GROUP = 16
S5_STATE = 64
S5_SLAB_GROUPS = 8
MLSTM_HEADS = 4
MLSTM_CHUNK = 128
CONV_WIDTH = 4
GMLP_GROUPS = 4
GMLP_CHUNK = 128
MOBA_HEADS = 4
MOBA_BLOCK = 256
MOBA_TOPK = 3
REL_BUCKETS = 32
REL_MAX_DIST = 128
N_EXPERTS = 8
TOP_K = 2


def _cparams(sem):
    return pltpu.CompilerParams(dimension_semantics=sem, vmem_limit_bytes=VMEM_LIMIT_BYTES)


def _rms_bf16(x, g):
    var = jnp.mean(x * x, axis=-1, keepdims=True)
    return (x * lax.rsqrt(var + EPS) * g).astype(BF16)


def _gelu(x):
    return jax.nn.gelu(x, approximate=True)


def _sigmoid(x):
    return 1.0 / (1.0 + jnp.exp(-x))


def _norm_proj_kernel(x_ref, g_ref, w_ref, *out_refs, splits):
    hn = _rms_bf16(x_ref[0], g_ref[...])
    for o_ref, (c0, c1) in zip(out_refs, splits):
        r = jnp.dot(hn, w_ref[:, c0:c1], preferred_element_type=F32)
        o_ref[...] = r.reshape(o_ref.shape).astype(o_ref.dtype)


def _norm_proj(x, g, w, outs, tm):
    B, S, D = x.shape
    splits = tuple((c0, c1) for c0, c1, _, _ in outs)
    out_shape, out_specs = [], []
    for c0, c1, dt, time_major in outs:
        n = c1 - c0
        if time_major:
            out_shape.append(jax.ShapeDtypeStruct((S, B * n), dt))
            out_specs.append(pl.BlockSpec((tm, n), lambda b, i: (i, b)))
        else:
            out_shape.append(jax.ShapeDtypeStruct((B, S, n), dt))
            out_specs.append(pl.BlockSpec((1, tm, n), lambda b, i: (b, i, 0)))
    return pl.pallas_call(
        functools.partial(_norm_proj_kernel, splits=splits),
        out_shape=out_shape,
        grid=(B, S // tm),
        in_specs=[
            pl.BlockSpec((1, tm, D), lambda b, i: (b, i, 0)),
            pl.BlockSpec((1, D), lambda b, i: (0, 0)),
            pl.BlockSpec(w.shape, lambda b, i: (0, 0)),
        ],
        out_specs=out_specs,
        compiler_params=_cparams(("parallel", "parallel")),
        name="norm_proj",
    )(x, g.reshape(1, D), w)


def _s5_kernel(u_ref, bmat_ref, ar_ref, ai_ref, cmat_ref, d_ref, wglu_ref, y_ref,
               buf_ref, xr_ref, xi_ref, *, lc, nb, nslab, sw):
    @pl.when(pl.program_id(0) == 0)
    def _():
        xr_ref[...] = jnp.zeros_like(xr_ref)
        xi_ref[...] = jnp.zeros_like(xi_ref)

    u = u_ref[...]
    for k in range(nslab):
        buf_ref[:, 2 * sw * k:2 * sw * (k + 1)] = jnp.dot(
            u[:, LANES * k:LANES * (k + 1)], bmat_ref[k], preferred_element_type=F32)

    for k in range(nslab):
        re_cols = slice(2 * sw * k, 2 * sw * k + sw)
        im_cols = slice(2 * sw * k + sw, 2 * sw * (k + 1))
        st_cols = slice(sw * k, sw * (k + 1))
        ar = jnp.broadcast_to(ar_ref[:, st_cols], (nb, sw))
        ai = jnp.broadcast_to(ai_ref[:, st_cols], (nb, sw))

        def step(t, carry, re_cols=re_cols, im_cols=im_cols, ar=ar, ai=ai):
            xr, xi = carry
            rows = pl.ds(pl.multiple_of(t * nb, nb), nb)
            nxr = ar * xr - ai * xi + buf_ref[rows, re_cols]
            nxi = ar * xi + ai * xr + buf_ref[rows, im_cols]
            buf_ref[rows, re_cols] = nxr
            buf_ref[rows, im_cols] = nxi
            return nxr, nxi

        xr, xi = lax.fori_loop(0, lc, step, (xr_ref[:, st_cols], xi_ref[:, st_cols]), unroll=4)
        xr_ref[:, st_cols] = xr
        xi_ref[:, st_cols] = xi

    ys = []
    for k in range(nslab):
        st = buf_ref[:, 2 * sw * k:2 * sw * (k + 1)].astype(BF16)
        ys.append(jnp.dot(st, cmat_ref[k], preferred_element_type=F32))
    y = jnp.concatenate(ys, axis=-1)
    y = _gelu(y + d_ref[...] * u.astype(F32))
    gl = jnp.dot(y.astype(BF16), wglu_ref[...], preferred_element_type=F32)
    y_ref[...] = (y * _sigmoid(gl)).astype(y_ref.dtype)


def _s5_params(lam_re, lam_im, log_dt, b_re, b_im, c_re, c_im):
    G, P = lam_re.shape
    Hc = b_re.shape[-1]
    dt = jnp.exp(log_dt.astype(F32))[:, None]
    mag = jnp.exp(lam_re * dt)
    ar = mag * jnp.cos(lam_im * dt)
    ai = mag * jnp.sin(lam_im * dt)
    den = lam_re * lam_re + lam_im * lam_im
    cr = ((ar - 1.0) * lam_re + ai * lam_im) / den
    ci = (ai * lam_re - (ar - 1.0) * lam_im) / den
    bb_re = cr[..., None] * b_re - ci[..., None] * b_im
    bb_im = cr[..., None] * b_im + ci[..., None] * b_re
    gs = S5_SLAB_GROUPS
    nslab = G // gs
    eye = jnp.eye(gs, dtype=F32)

    def bd_in(b):
        b = b.reshape(nslab, gs, P, Hc)
        return jnp.einsum('kgph,gj->kghjp', b, eye).reshape(nslab, gs * Hc, gs * P)

    def bd_out(c):
        c = c.reshape(nslab, gs, Hc, P)
        return jnp.einsum('kghp,gj->kgpjh', c, eye).reshape(nslab, gs * P, gs * Hc)

    bmat = jnp.concatenate([bd_in(bb_re), bd_in(bb_im)], axis=-1).astype(BF16)
    cmat = jnp.concatenate([bd_out(c_re), -bd_out(c_im)], axis=1).astype(BF16)
    return bmat, ar.reshape(1, G * P), ai.reshape(1, G * P), cmat


def _s5(u_tm, nb, bmat, ar, ai, cmat, d_skip, w_glu, lc):
    R, W = u_tm.shape
    S = R // nb
    nslab = bmat.shape[0]
    sw = bmat.shape[2] // 2
    rows = lc * nb
    return pl.pallas_call(
        functools.partial(_s5_kernel, lc=lc, nb=nb, nslab=nslab, sw=sw),
        out_shape=jax.ShapeDtypeStruct((R, W), BF16),
        grid=(S // lc,),
        in_specs=[
            pl.BlockSpec((rows, W), lambda i: (i, 0)),
            pl.BlockSpec(bmat.shape, lambda i: (0, 0, 0)),
            pl.BlockSpec(ar.shape, lambda i: (0, 0)),
            pl.BlockSpec(ai.shape, lambda i: (0, 0)),
            pl.BlockSpec(cmat.shape, lambda i: (0, 0, 0)),
            pl.BlockSpec((1, W), lambda i: (0, 0)),
            pl.BlockSpec(w_glu.shape, lambda i: (0, 0)),
        ],
        out_specs=pl.BlockSpec((rows, W), lambda i: (i, 0)),
        scratch_shapes=[
            pltpu.VMEM((rows, 2 * sw * nslab), F32),
            pltpu.VMEM((nb, sw * nslab), F32),
            pltpu.VMEM((nb, sw * nslab), F32),
        ],
        compiler_params=_cparams(("arbitrary",)),
        name="s5",
    )(u_tm, bmat, ar, ai, cmat, d_skip.reshape(1, W), w_glu)


def _mlstm_kernel(qk_ref, v_ref, o_ref, gt_ref, cw_ref, cb_ref, gb_ref, ng_ref, y_ref,
                  xcat_ref, c_ref, n_ref, m_ref, *, L, H, Dh):
    W = H * Dh
    halo = SUBLANES

    @pl.when(pl.program_id(1) == 0)
    def _():
        xcat_ref[0:halo, :] = jnp.zeros((halo, 2 * W), F32)
        c_ref[...] = jnp.zeros_like(c_ref)
        n_ref[...] = jnp.zeros_like(n_ref)
        m_ref[...] = jnp.zeros_like(m_ref)

    xcat_ref[halo:halo + L, :] = qk_ref[0].astype(F32)
    conv = cb_ref[...] + cw_ref[0:1, :] * xcat_ref[halo:halo + L, :]
    for j in range(1, CONV_WIDTH):
        conv = conv + cw_ref[j:j + 1, :] * xcat_ref[halo - j:halo - j + L, :]
    xcat_ref[0:halo, :] = xcat_ref[L:L + halo, :]
    qk = conv * _sigmoid(conv)
    q = qk[:, :W].astype(BF16)
    kf = qk[:, W:] * (1.0 / math.sqrt(Dh))
    v = v_ref[0]
    vb = v.astype(BF16)

    g = gt_ref[0] + gb_ref[...]
    logf = jnp.minimum(g, 0.0) - jnp.log(1.0 + jnp.exp(-jnp.abs(g)))
    row = lax.broadcasted_iota(jnp.int32, (L, L), 0)
    col = lax.broadcasted_iota(jnp.int32, (L, L), 1)
    causal = row >= col
    tril = causal.astype(F32)
    bcum = jnp.dot(tril, logf, preferred_element_type=F32, precision=lax.Precision.HIGHEST)
    lane = lax.broadcasted_iota(jnp.int32, g.shape, 1)
    a_cols = jnp.where(lane < H, g, bcum)
    a_rows = a_cols.T

    for h in range(H):
        hs = slice(h * Dh, (h + 1) * Dh)
        li_col = a_cols[:, h:h + 1]
        bc_col = a_cols[:, H + h:H + h + 1]
        li_row = a_rows[h:h + 1, :]
        bc_row = a_rows[H + h:H + h + 1, :]
        b_last = bc_row[:, L - 1:L]
        m_st = m_ref[h:h + 1, 0:1]
        n_st = n_ref[h:h + 1, :]
        ct_st = c_ref[h]
        qh = q[:, hs]
        kh = kf[:, hs]
        vh = vb[:, hs]

        d_mat = jnp.where(causal, bc_col - bc_row + li_row, -jnp.inf)
        m_inter = bc_col + m_st
        m_t = jnp.maximum(jnp.max(d_mat, axis=-1, keepdims=True), m_inter)
        s = lax.dot_general(qh, kh.astype(BF16), (((1,), (1,)), ((), ())),
                            preferred_element_type=F32) * jnp.exp(d_mat - m_t)
        w_inter = jnp.exp(m_inter - m_t)
        num = jnp.dot(s.astype(BF16), vh, preferred_element_type=F32) + w_inter * jnp.dot(
            qh, ct_st.astype(BF16), preferred_element_type=F32)
        den = jnp.sum(s, axis=-1, keepdims=True) + w_inter * jnp.sum(
            qh.astype(F32) * n_st, axis=-1, keepdims=True)
        hh = num / jnp.maximum(jnp.abs(den), jnp.exp(-m_t))
        hh = hh * lax.rsqrt(jnp.mean(hh * hh, axis=-1, keepdims=True) + EPS) * ng_ref[:, hs]
        y_ref[0, :, hs] = (_sigmoid(o_ref[0, :, hs].astype(F32)) * hh).astype(y_ref.dtype)

        a_end = b_last - bc_col + li_col
        m_loc = jnp.max(a_end, axis=0, keepdims=True)
        kw = kh * jnp.exp(a_end - m_loc)
        ct_loc = jnp.dot(kw.T.astype(BF16), vh, preferred_element_type=F32)
        n_loc = jnp.sum(kw, axis=0, keepdims=True)
        m_new = jnp.maximum(b_last + m_st, m_loc)
        s_prev = jnp.exp(b_last + m_st - m_new)
        s_loc = jnp.exp(m_loc - m_new)
        c_ref[h] = s_prev * ct_st + s_loc * ct_loc
        n_ref[h:h + 1, :] = s_prev * n_st + s_loc * n_loc
        m_ref[h:h + 1, :] = jnp.broadcast_to(m_new, (1, LANES))


def _mlstm(qk_pre, v, o_pre, gates, conv_w, conv_b, gate_b_pad, norm_g):
    B, S, W2 = qk_pre.shape
    W = W2 // 2
    H, L = MLSTM_HEADS, MLSTM_CHUNK
    Dh = W // H
    return pl.pallas_call(
        functools.partial(_mlstm_kernel, L=L, H=H, Dh=Dh),
        out_shape=jax.ShapeDtypeStruct((B, S, W), BF16),
        grid=(B, S // L),
        in_specs=[
            pl.BlockSpec((1, L, W2), lambda b, c: (b, c, 0)),
            pl.BlockSpec((1, L, W), lambda b, c: (b, c, 0)),
            pl.BlockSpec((1, L, W), lambda b, c: (b, c, 0)),
            pl.BlockSpec((1, L, LANES), lambda b, c: (b, c, 0)),
            pl.BlockSpec((CONV_WIDTH, W2), lambda b, c: (0, 0)),
            pl.BlockSpec((1, W2), lambda b, c: (0, 0)),
            pl.BlockSpec((1, LANES), lambda b, c: (0, 0)),
            pl.BlockSpec((1, W), lambda b, c: (0, 0)),
        ],
        out_specs=pl.BlockSpec((1, L, W), lambda b, c: (b, c, 0)),
        scratch_shapes=[
            pltpu.VMEM((L + SUBLANES, W2), F32),
            pltpu.VMEM((H, Dh, Dh), F32),
            pltpu.VMEM((SUBLANES, Dh), F32),
            pltpu.VMEM((SUBLANES, LANES), F32),
        ],
        compiler_params=_cparams(("parallel", "arbitrary")),
        name="mlstm",
    )(qk_pre, v, o_pre, gates, conv_w, conv_b.reshape(1, W2), gate_b_pad, norm_g.reshape(1, W))


def _out_proj_kernel(res_ref, a_ref, b_ref, wa_ref, wb_ref, o_ref):
    a = a_ref[...].reshape(a_ref.shape[-2:])
    b = b_ref[...].reshape(b_ref.shape[-2:])
    acc = jnp.dot(a, wa_ref[...], preferred_element_type=F32)
    acc = acc + jnp.dot(b, wb_ref[...], preferred_element_type=F32)
    o_ref[0] = res_ref[0] + acc


def _out_proj(res, ya, yb, w, tm, a_time_major):
    B, S, D = res.shape
    wa_n = w.shape[0] // 2
    wa, wb = w[:wa_n], w[wa_n:]
    if a_time_major:
        a_spec = pl.BlockSpec((tm, wa_n), lambda b, i: (i, b))
    else:
        a_spec = pl.BlockSpec((1, tm, wa_n), lambda b, i: (b, i, 0))
    return pl.pallas_call(
        _out_proj_kernel,
        out_shape=jax.ShapeDtypeStruct((B, S, D), F32),
        grid=(B, S // tm),
        in_specs=[
            pl.BlockSpec((1, tm, D), lambda b, i: (b, i, 0)),
            a_spec,
            pl.BlockSpec((1, tm, w.shape[0] - wa_n), lambda b, i: (b, i, 0)),
            pl.BlockSpec(wa.shape, lambda b, i: (0, 0)),
            pl.BlockSpec(wb.shape, lambda b, i: (0, 0)),
        ],
        out_specs=pl.BlockSpec((1, tm, D), lambda b, i: (b, i, 0)),
        compiler_params=_cparams(("parallel", "parallel")),
        name="out_proj",
    )(res, ya, yb, wa, wb)


def _ffn_kernel(x_ref, g_ref, wg_ref, wu_ref, wd_ref, o_ref, hn_ref, acc_ref):
    f = pl.program_id(1)

    @pl.when(f == 0)
    def _():
        hn_ref[...] = _rms_bf16(x_ref[...], g_ref[...])
        acc_ref[...] = jnp.zeros_like(acc_ref)

    hn = hn_ref[...]
    a = jnp.dot(hn, wg_ref[...], preferred_element_type=F32)
    u = jnp.dot(hn, wu_ref[...], preferred_element_type=F32)
    hmid = (a * _sigmoid(a) * u).astype(BF16)
    acc_ref[...] += jnp.dot(hmid, wd_ref[...], preferred_element_type=F32)

    @pl.when(f == pl.num_programs(1) - 1)
    def _():
        o_ref[...] = x_ref[...] + acc_ref[...]


def _ffn(x2d, g, wg, wu, wd, tm, fc):
    T, D = x2d.shape
    F = wg.shape[1]
    return pl.pallas_call(
        _ffn_kernel,
        out_shape=jax.ShapeDtypeStruct((T, D), F32),
        grid=(T // tm, F // fc),
        in_specs=[
            pl.BlockSpec((tm, D), lambda i, f: (i, 0)),
            pl.BlockSpec((1, D), lambda i, f: (0, 0)),
            pl.BlockSpec((D, fc), lambda i, f: (0, f)),
            pl.BlockSpec((D, fc), lambda i, f: (0, f)),
            pl.BlockSpec((fc, D), lambda i, f: (f, 0)),
        ],
        out_specs=pl.BlockSpec((tm, D), lambda i, f: (i, 0)),
        scratch_shapes=[pltpu.VMEM((tm, D), BF16), pltpu.VMEM((tm, D), F32)],
        compiler_params=_cparams(("parallel", "arbitrary")),
        name="ffn_swiglu",
    )(x2d, g.reshape(1, D), wg, wu, wd)


def _pick_tile(n, target):
    t = min(n, target)
    while n % t:
        t //= 2
    return t


def _layer_ab(h, norm_g, w_in, lam_re, lam_im, log_dt, b_re, b_im, c_re, c_im, d_skip, w_glu,
              conv_w, conv_b, gate_b, mlstm_norm_g, w_out):
    B, S, D = h.shape
    s5w = lam_re.shape[0] * S5_GROUP
    mw = mlstm_norm_g.shape[0]
    c_qk, c_v, c_o, c_if = s5w, s5w + 2 * mw, s5w + 3 * mw, s5w + 4 * mw
    n_if = w_in.shape[1] - c_if
    w_pad = jnp.pad(w_in, ((0, 0), (0, LANES - n_if))).astype(BF16)
    tm = _pick_tile(S, 512)
    u_tm, qk_pre, v, o_pre, gates = _norm_proj(
        h, norm_g, w_pad,
        [(0, c_qk, BF16, True), (c_qk, c_v, BF16, False), (c_v, c_o, BF16, False),
         (c_o, c_if, BF16, False), (c_if, c_if + LANES, F32, False)], tm)
    bmat, ar, ai, cmat = _s5_params(lam_re, lam_im, log_dt, b_re, b_im, c_re, c_im)
    y_a = _s5(u_tm.reshape(S * B, s5w), B, bmat, ar, ai, cmat, d_skip, w_glu.astype(BF16),
              lc=_pick_tile(S, 32))
    gate_b_pad = jnp.pad(gate_b, (0, LANES - n_if)).reshape(1, LANES)
    y_b = _mlstm(qk_pre, v, o_pre, gates, conv_w, conv_b, gate_b_pad, mlstm_norm_g)
    return _out_proj(h, y_a.reshape(S, B * s5w), y_b, w_out.astype(BF16), tm, True)


def _gmlp_kernel(u_ref, v_ref, ng_ref, nb_ref, ws_ref, bs_ref, y_ref, *, L, G, Dg, nchunk):
    u = _gelu(u_ref[0].astype(F32))
    v = _gelu(v_ref[0].astype(F32))
    mu = jnp.mean(v, axis=-1, keepdims=True)
    vc = v - mu
    var = jnp.mean(vc * vc, axis=-1, keepdims=True)
    vn = (vc * lax.rsqrt(var + EPS) * ng_ref[...] + nb_ref[...]).astype(BF16)
    for c in range(nchunk):
        rows = slice(c * L, (c + 1) * L)
        for g in range(G):
            cols = slice(g * Dg, (g + 1) * Dg)
            s = jnp.dot(ws_ref[g], vn[rows, cols], preferred_element_type=F32) + bs_ref[:, g:g + 1]
            y_ref[0, rows, cols] = (u[rows, cols] * s).astype(y_ref.dtype)


def _gmlp(u_pre, v_pre, norm_g, norm_b, w_s, b_s, tm):
    B, S, W = u_pre.shape
    G, L = GMLP_GROUPS, GMLP_CHUNK
    ws = (w_s * jnp.tril(jnp.ones((L, L), w_s.dtype))).astype(BF16)
    bs = jnp.pad(b_s.T, ((0, 0), (0, LANES - G)))
    return pl.pallas_call(
        functools.partial(_gmlp_kernel, L=L, G=G, Dg=W // G, nchunk=tm // L),
        out_shape=jax.ShapeDtypeStruct((B, S, W), BF16),
        grid=(B, S // tm),
        in_specs=[
            pl.BlockSpec((1, tm, W), lambda b, i: (b, i, 0)),
            pl.BlockSpec((1, tm, W), lambda b, i: (b, i, 0)),
            pl.BlockSpec((1, W), lambda b, i: (0, 0)),
            pl.BlockSpec((1, W), lambda b, i: (0, 0)),
            pl.BlockSpec((G, L, L), lambda b, i: (0, 0, 0)),
            pl.BlockSpec((L, LANES), lambda b, i: (0, 0)),
        ],
        out_specs=pl.BlockSpec((1, tm, W), lambda b, i: (b, i, 0)),
        compiler_params=_cparams(("parallel", "parallel")),
        name="gmlp",
    )(u_pre, v_pre, norm_g.reshape(1, W), norm_b.reshape(1, W), ws, bs)


NEG_BIG = -1e30


def _moba_kernel(q_ref, k_ref, v_ref, t0_ref, t1_ref, bfar_ref, y_ref, vext_ref, s_ref, p_ref,
                 *, BL, NB, Dh, topk):
    nt = (((1,), (1,)), ((), ()))
    scale = 1.0 / math.sqrt(Dh)
    vext_ref[:, :Dh] = v_ref[0]
    vext_ref[:, Dh:] = jnp.ones((NB * BL, Dh), BF16)
    bfar = bfar_ref[0][:, 0:1]
    kmean = jnp.concatenate(
        [jnp.mean(k_ref[0, n * BL:(n + 1) * BL, :].astype(F32), axis=0, keepdims=True) for n in range(NB)],
        axis=0)
    nrow = lax.broadcasted_iota(jnp.int32, (NB, BL), 0)
    eye = (lax.broadcasted_iota(jnp.int32, (BL, BL), 0) ==
           lax.broadcasted_iota(jnp.int32, (BL, BL), 1)).astype(BF16)

    for qi in range(NB):
        rows = slice(qi * BL, (qi + 1) * BL)
        qf = q_ref[0, rows, :].astype(F32)
        qs = (qf * scale).astype(BF16)
        neg_cols = None
        if qi > topk:
            gate = lax.dot_general(kmean, qf, nt, preferred_element_type=F32,
                                   precision=lax.Precision.HIGHEST)
            rank = jnp.zeros((NB, BL), F32)
            for m in range(qi):
                gm = gate[m:m + 1, :]
                rank = rank + jnp.where(gm > gate, 1.0, jnp.where(gm == gate, jnp.where(nrow > m, 1.0, 0.0), 0.0))
            sel_rows = jnp.where(nrow < qi, jnp.where(rank < topk, 1.0, 0.0), 0.0)
            sel_pad = jnp.concatenate([sel_rows, jnp.zeros((LANES - NB, BL), F32)], axis=0).astype(BF16)
            sel_cols = lax.dot_general(eye, sel_pad, nt, preferred_element_type=F32)
            neg_cols = (1.0 - sel_cols) * NEG_BIG

        m_tile = None
        for kb in range(qi + 1):
            s = lax.dot_general(qs, k_ref[0, kb * BL:(kb + 1) * BL, :], nt, preferred_element_type=F32)
            if kb == qi:
                s = s + t0_ref[0]
            elif kb == qi - 1:
                s = s + t1_ref[0]
                if neg_cols is not None:
                    s = s + neg_cols[:, kb:kb + 1]
            else:
                s = s + (bfar if neg_cols is None else bfar + neg_cols[:, kb:kb + 1])
            s_ref[:, kb * BL:(kb + 1) * BL] = s
            for c in range(BL // LANES):
                piece = s[:, c * LANES:(c + 1) * LANES]
                m_tile = piece if m_tile is None else jnp.maximum(m_tile, piece)
        m = jnp.max(m_tile, axis=-1, keepdims=True)
        nk = (qi + 1) * BL
        p_ref[:, :nk] = jnp.exp(s_ref[:, :nk] - m).astype(BF16)
        acc = jnp.dot(p_ref[:, :nk], vext_ref[:nk, :], preferred_element_type=F32)
        y_ref[0, rows, :] = (acc[:, :Dh] / acc[:, Dh:Dh + 1]).astype(y_ref.dtype)


def _rel_bucket(n):
    max_exact = REL_BUCKETS // 2
    nf = jnp.maximum(n, 1).astype(F32)
    large = max_exact + (jnp.log(nf / max_exact) / math.log(REL_MAX_DIST / max_exact)
                         * (REL_BUCKETS - max_exact)).astype(jnp.int32)
    large = jnp.minimum(large, REL_BUCKETS - 1)
    return jnp.where(n < max_exact, n, large)


def _moba(qkv, rel_bias):
    B, S, W3 = qkv.shape
    H, BL = MOBA_HEADS, MOBA_BLOCK
    W = W3 // 3
    Dh = W // H
    NB = S // BL
    assert BL + 1 >= REL_MAX_DIST and Dh == LANES and S % BL == 0
    biasd = rel_bias.T[:, _rel_bucket(jnp.arange(2 * BL, dtype=jnp.int32))]
    i = jnp.arange(BL, dtype=jnp.int32)[:, None]
    j = jnp.arange(BL, dtype=jnp.int32)[None, :]
    t0 = jnp.where(i >= j, biasd[:, jnp.maximum(i - j, 0)], NEG_BIG)
    t1 = biasd[:, BL + i - j]
    bfar = jnp.broadcast_to(rel_bias.T[:, REL_BUCKETS - 1][:, None, None], (H, 1, LANES))
    return pl.pallas_call(
        functools.partial(_moba_kernel, BL=BL, NB=NB, Dh=Dh, topk=min(MOBA_TOPK, NB)),
        out_shape=jax.ShapeDtypeStruct((B, S, W), BF16),
        grid=(B, H),
        in_specs=[
            pl.BlockSpec((1, S, Dh), lambda b, h: (b, 0, h)),
            pl.BlockSpec((1, S, Dh), lambda b, h: (b, 0, H + h)),
            pl.BlockSpec((1, S, Dh), lambda b, h: (b, 0, 2 * H + h)),
            pl.BlockSpec((1, BL, BL), lambda b, h: (h, 0, 0)),
            pl.BlockSpec((1, BL, BL), lambda b, h: (h, 0, 0)),
            pl.BlockSpec((1, 1, LANES), lambda b, h: (h, 0, 0)),
        ],
        out_specs=pl.BlockSpec((1, S, Dh), lambda b, h: (b, 0, h)),
        scratch_shapes=[
            pltpu.VMEM((S, 2 * Dh), BF16),
            pltpu.VMEM((BL, S), F32),
            pltpu.VMEM((BL, S), BF16),
        ],
        compiler_params=_cparams(("parallel", "parallel")),
        name="moba",
    )(qkv, qkv, qkv, t0, t1, bfar)


def _layer_cd(h, norm_g, w_in, gmlp_norm_g, gmlp_norm_b, gmlp_w_s, gmlp_b_s, rel_bias, w_out):
    B, S, D = h.shape
    gw = gmlp_norm_g.shape[0]
    tm = _pick_tile(S, 512)
    u_pre, v_pre, qkv = _norm_proj(
        h, norm_g, w_in.astype(BF16),
        [(0, gw, BF16, False), (gw, 2 * gw, BF16, False), (2 * gw, w_in.shape[1], BF16, False)], tm)
    y_c = _gmlp(u_pre, v_pre, gmlp_norm_g, gmlp_norm_b, gmlp_w_s, gmlp_b_s, tm)
    y_d = _moba(qkv, rel_bias)
    return _out_proj(h, y_c, y_d, w_out.astype(BF16), tm, False)


def _router_kernel(x_ref, g_ref, wr_ref, rb_ref, idx_ref, gts_ref, cnt_ref, carry_ref, *, E):
    @pl.when(pl.program_id(0) == 0)
    def _():
        carry_ref[...] = jnp.zeros_like(carry_ref)

    x = x_ref[...]
    tm = x.shape[0]
    var = jnp.mean(x * x, axis=-1, keepdims=True)
    hn = x * lax.rsqrt(var + EPS) * g_ref[...]
    logits = lax.dot_general(wr_ref[...], hn, (((1,), (1,)), ((), ())), preferred_element_type=F32,
                             precision=lax.Precision.HIGHEST) + rb_ref[:, 0:1]
    rowi = lax.broadcasted_iota(jnp.int32, (E, tm), 0)
    v1 = jnp.max(logits, axis=0, keepdims=True)
    e1 = jnp.min(jnp.where(logits == v1, rowi, E), axis=0, keepdims=True)
    masked = jnp.where(rowi == e1, -jnp.inf, logits)
    v2 = jnp.max(masked, axis=0, keepdims=True)
    e2 = jnp.min(jnp.where(masked == v2, rowi, E), axis=0, keepdims=True)
    ex = jnp.exp(v2 - v1)
    g1 = 1.0 / (1.0 + ex)
    g2 = ex / (1.0 + ex)
    oh1 = rowi == e1
    oh2 = rowi == e2
    cnt = jnp.where(oh1, 1.0, jnp.where(oh2, 1.0, 0.0))
    before = (lax.broadcasted_iota(jnp.int32, (tm, tm), 0) <
              lax.broadcasted_iota(jnp.int32, (tm, tm), 1)).astype(BF16)
    excl = jnp.dot(cnt.astype(BF16), before, preferred_element_type=F32) + carry_ref[:, 0:1]
    r1 = jnp.sum(jnp.where(oh1, excl, 0.0), axis=0, keepdims=True).astype(jnp.int32)
    r2 = jnp.sum(jnp.where(oh2, excl, 0.0), axis=0, keepdims=True).astype(jnp.int32)
    idx_ref[...] = jnp.where(rowi == 0, e1, jnp.where(rowi == 1, e2, jnp.where(rowi == 2, r1, jnp.where(rowi == 3, r2, 0))))
    gts_ref[...] = jnp.where(rowi == 0, g1, jnp.where(rowi == 1, g2, 0.0))
    carry_ref[...] = carry_ref[...] + jnp.sum(cnt, axis=1, keepdims=True)
    cnt_ref[...] = carry_ref[...]


def _router(x2d, g, router_w, router_b, tm):
    T, D = x2d.shape
    E = router_w.shape[1]
    assert E == SUBLANES
    return pl.pallas_call(
        functools.partial(_router_kernel, E=E),
        out_shape=[
            jax.ShapeDtypeStruct((E, T), jnp.int32),
            jax.ShapeDtypeStruct((E, T), F32),
            jax.ShapeDtypeStruct((E, LANES), F32),
        ],
        grid=(T // tm,),
        in_specs=[
            pl.BlockSpec((tm, D), lambda i: (i, 0)),
            pl.BlockSpec((1, D), lambda i: (0, 0)),
            pl.BlockSpec((E, D), lambda i: (0, 0)),
            pl.BlockSpec((E, LANES), lambda i: (0, 0)),
        ],
        out_specs=[
            pl.BlockSpec((E, tm), lambda i: (0, i)),
            pl.BlockSpec((E, tm), lambda i: (0, i)),
            pl.BlockSpec((E, LANES), lambda i: (0, 0)),
        ],
        scratch_shapes=[pltpu.VMEM((E, LANES), F32)],
        compiler_params=_cparams(("arbitrary",)),
        name="moe_router",
    )(x2d, g.reshape(1, D), router_w.T, jnp.broadcast_to(router_b[:, None], (E, LANES)))


DMA_ISSUE_UNROLL = 8


def _dispatch_kernel(pos1_ref, pos2_ref, ends_ref, h_ref, xs_hbm, zero_ref, sem, zsem, *, tm, tg, E):
    i = pl.program_id(0)

    @pl.when(i == 0)
    def _():
        zero_ref[...] = jnp.zeros_like(zero_ref)

        def zero_copy(e):
            start = ends_ref[e] - tg
            return pltpu.make_async_copy(zero_ref, xs_hbm.at[pl.ds(pl.multiple_of(start, tg), tg)], zsem)

        def nonempty(e):
            return ends_ref[e] > (ends_ref[e - 1] if e else 0)

        for e in range(E):
            @pl.when(nonempty(e))
            def _(e=e):
                zero_copy(e).start()
        for e in range(E):
            @pl.when(nonempty(e))
            def _(e=e):
                zero_copy(e).wait()

    def body(r, carry):
        t = i * tm + r
        src = h_ref.at[pl.ds(r, 1)]
        pltpu.make_async_copy(src, xs_hbm.at[pl.ds(pos1_ref[t], 1)], sem).start()
        pltpu.make_async_copy(src, xs_hbm.at[pl.ds(pos2_ref[t], 1)], sem).start()
        return carry
    lax.fori_loop(0, tm, body, 0, unroll=DMA_ISSUE_UNROLL)

    for _ in range(TOP_K):
        pltpu.make_async_copy(h_ref, xs_hbm.at[pl.ds(0, tm)], sem).wait()


def _dispatch(pos1, pos2, ends, h2d, n_rows, tm, tg):
    T, D = h2d.shape
    E = ends.shape[0]
    return pl.pallas_call(
        functools.partial(_dispatch_kernel, tm=tm, tg=tg, E=E),
        out_shape=jax.ShapeDtypeStruct((n_rows, D), h2d.dtype),
        grid_spec=pltpu.PrefetchScalarGridSpec(
            num_scalar_prefetch=3,
            grid=(T // tm,),
            in_specs=[pl.BlockSpec((tm, D), lambda i, p1, p2, en: (i, 0))],
            out_specs=pl.BlockSpec(memory_space=pl.ANY),
            scratch_shapes=[pltpu.VMEM((tg, D), h2d.dtype), pltpu.SemaphoreType.DMA, pltpu.SemaphoreType.DMA],
        ),
        compiler_params=_cparams(("arbitrary",)),
        name="moe_dispatch",
    )(pos1, pos2, ends, h2d)


def _gmm_kernel(te_ref, na_ref, x_ref, g_ref, wg_ref, wu_ref, wd_ref, o_ref, xn_ref, acc_ref):
    i = pl.program_id(0)
    f = pl.program_id(1)
    active = i < na_ref[0]

    @pl.when(f == 0)
    def _():
        acc_ref[...] = jnp.zeros_like(acc_ref)

    @pl.when(active & (f == 0))
    def _():
        xn_ref[...] = _rms_bf16(x_ref[...], g_ref[...])

    @pl.when(active)
    def _():
        x = xn_ref[...]
        a = jnp.dot(x, wg_ref[0], preferred_element_type=F32)
        u = jnp.dot(x, wu_ref[0], preferred_element_type=F32)
        hmid = (a * _sigmoid(a) * u).astype(BF16)
        acc_ref[...] += jnp.dot(hmid, wd_ref[0], preferred_element_type=F32)

    @pl.when(f == pl.num_programs(1) - 1)
    def _():
        o_ref[...] = acc_ref[...]


def _gmm(tile_expert, n_active, xs, g, wg, wu, wd, tm, fc):
    R, D = xs.shape
    F = wg.shape[2]
    nf = F // fc

    def fsel(i, f, na):
        return jnp.where(i < na[0], f, nf - 1)

    return pl.pallas_call(
        _gmm_kernel,
        out_shape=jax.ShapeDtypeStruct((R, D), F32),
        grid_spec=pltpu.PrefetchScalarGridSpec(
            num_scalar_prefetch=2,
            grid=(R // tm, nf),
            in_specs=[
                pl.BlockSpec((tm, D), lambda i, f, te, na: (i, 0)),
                pl.BlockSpec((1, D), lambda i, f, te, na: (0, 0)),
                pl.BlockSpec((1, D, fc), lambda i, f, te, na: (te[i], 0, fsel(i, f, na))),
                pl.BlockSpec((1, D, fc), lambda i, f, te, na: (te[i], 0, fsel(i, f, na))),
                pl.BlockSpec((1, fc, D), lambda i, f, te, na: (te[i], fsel(i, f, na), 0)),
            ],
            out_specs=pl.BlockSpec((tm, D), lambda i, f, te, na: (i, 0)),
            scratch_shapes=[pltpu.VMEM((tm, D), BF16), pltpu.VMEM((tm, D), F32)],
        ),
        compiler_params=_cparams(("parallel", "arbitrary")),
        name="moe_gmm",
    )(tile_expert, n_active, xs, g.reshape(1, D), wg, wu, wd)


def _combine_copies(pos1_ref, pos2_ref, ys_hbm, ya_ref, yb_ref, sems, tile, slot, tm):
    def row_copies(r):
        t = tile * tm + r
        return (pltpu.make_async_copy(ys_hbm.at[pl.ds(pos1_ref[t], 1)], ya_ref.at[slot, pl.ds(r, 1)], sems.at[slot]),
                pltpu.make_async_copy(ys_hbm.at[pl.ds(pos2_ref[t], 1)], yb_ref.at[slot, pl.ds(r, 1)], sems.at[slot]))
    return row_copies


def _combine_kernel(pos1_ref, pos2_ref, h_ref, ga_ref, gb_ref, g_ref, ys_hbm, o_ref, ya_ref, yb_ref, sems, *, tm):
    i = pl.program_id(0)
    n = pl.num_programs(0)
    slot = lax.rem(i, 2)

    def start_tile(tile, slot):
        copies = _combine_copies(pos1_ref, pos2_ref, ys_hbm, ya_ref, yb_ref, sems, tile, slot, tm)

        def body(r, carry):
            a, b = copies(r)
            a.start()
            b.start()
            return carry
        lax.fori_loop(0, tm, body, 0, unroll=DMA_ISSUE_UNROLL)

    @pl.when(i == 0)
    def _():
        start_tile(0, 0)

    @pl.when(i + 1 < n)
    def _():
        start_tile(i + 1, 1 - slot)

    pltpu.make_async_copy(ys_hbm.at[pl.ds(0, tm)], ya_ref.at[slot], sems.at[slot]).wait()
    pltpu.make_async_copy(ys_hbm.at[pl.ds(0, tm)], yb_ref.at[slot], sems.at[slot]).wait()

    h = h_ref[...] + ga_ref[...] * ya_ref[slot] + gb_ref[...] * yb_ref[slot]
    var = jnp.mean(h * h, axis=-1, keepdims=True)
    o_ref[...] = h * lax.rsqrt(var + EPS) * g_ref[...]


def _combine(pos1, pos2, h2d, ga, gb, g, ys, tm):
    T, D = h2d.shape
    row = pl.BlockSpec((tm, D), lambda i, p1, p2: (i, 0))
    colv = pl.BlockSpec((tm, 1), lambda i, p1, p2: (i, 0))
    return pl.pallas_call(
        functools.partial(_combine_kernel, tm=tm),
        out_shape=jax.ShapeDtypeStruct((T, D), F32),
        grid_spec=pltpu.PrefetchScalarGridSpec(
            num_scalar_prefetch=2,
            grid=(T // tm,),
            in_specs=[row, colv, colv, pl.BlockSpec((1, D), lambda i, p1, p2: (0, 0)),
                      pl.BlockSpec(memory_space=pl.ANY)],
            out_specs=row,
            scratch_shapes=[pltpu.VMEM((2, tm, D), F32), pltpu.VMEM((2, tm, D), F32),
                            pltpu.SemaphoreType.DMA((2,))],
        ),
        compiler_params=_cparams(("arbitrary",)),
        name="moe_combine_norm",
    )(pos1, pos2, h2d, ga, gb, g.reshape(1, D), ys)


def _moe_final(h, norm_g, final_g, router_w, router_b, w_gate, w_up, w_down):
    B, S, D = h.shape
    T = B * S
    E = router_w.shape[1]
    h2d = h.reshape(T, D)
    idx, gts, cnt = _router(h2d, norm_g, router_w, router_b, _pick_tile(T, 512))
    e1, e2, r1, r2 = idx[0], idx[1], idx[2], idx[3]
    tm = _pick_tile(T, 1024)
    counts = cnt[:, 0].astype(jnp.int32)
    padded = ((counts + tm - 1) // tm) * tm
    ends = jnp.cumsum(padded)
    offs = ends - padded
    eids = jnp.arange(E, dtype=jnp.int32)[:, None]
    pos1 = jnp.sum(jnp.where(e1[None, :] == eids, offs[:, None], 0), axis=0) + r1
    pos2 = jnp.sum(jnp.where(e2[None, :] == eids, offs[:, None], 0), axis=0) + r2
    n_tiles = (TOP_K * T) // tm + E
    tile_start = jnp.arange(n_tiles, dtype=jnp.int32) * tm
    tile_expert = jnp.minimum(jnp.searchsorted(ends, tile_start, side='right'), E - 1).astype(jnp.int32)
    n_active = (ends[-1] // tm).astype(jnp.int32).reshape(1)
    xs = _dispatch(pos1, pos2, ends.astype(jnp.int32), h2d, n_tiles * tm, _pick_tile(T, 1024), tm)
    ys = _gmm(tile_expert, n_active, xs, norm_g, w_gate.astype(BF16), w_up.astype(BF16), w_down.astype(BF16),
              tm, _pick_tile(w_gate.shape[2], 512))
    out = _combine(pos1, pos2, h2d, gts[0][:, None], gts[1][:, None], final_g, ys, _pick_tile(T, 256))
    return out.reshape(B, S, D)


def kernel(x, norm_mix_g, norm_ffn_g, norm_final_g, ab_w_in, s5_lambda_re, s5_lambda_im, s5_log_dt, s5_b_re, s5_b_im, s5_c_re, s5_c_im, s5_d, s5_w_glu, mlstm_conv_w, mlstm_conv_b, mlstm_gate_b, mlstm_norm_g, ab_w_out, ffn_w_gate, ffn_w_up, ffn_w_down, cd_w_in, gmlp_norm_g, gmlp_norm_b, gmlp_w_s, gmlp_b_s, rel_bias, cd_w_out, moe_router_w, moe_router_b, moe_w_gate, moe_w_up, moe_w_down):
    B, S, D = x.shape
    h = _layer_ab(x, norm_mix_g[0], ab_w_in[0], s5_lambda_re[0], s5_lambda_im[0], s5_log_dt[0],
                  s5_b_re[0], s5_b_im[0], s5_c_re[0], s5_c_im[0], s5_d[0], s5_w_glu[0],
                  mlstm_conv_w[0], mlstm_conv_b[0], mlstm_gate_b[0], mlstm_norm_g[0], ab_w_out[0])
    h = _ffn(h.reshape(B * S, D), norm_ffn_g[0], ffn_w_gate[0].astype(BF16), ffn_w_up[0].astype(BF16),
             ffn_w_down[0].astype(BF16), _pick_tile(B * S, 512), _pick_tile(ffn_w_gate.shape[2], 1408))
    h = _layer_cd(h.reshape(B, S, D), norm_mix_g[1], cd_w_in[0], gmlp_norm_g[0], gmlp_norm_b[0],
                  gmlp_w_s[0], gmlp_b_s[0], rel_bias, cd_w_out[0])
    return _moe_final(h, norm_ffn_g[1], norm_final_g, moe_router_w[0], moe_router_b[0],
                      moe_w_gate[0], moe_w_up[0], moe_w_down[0])
```

```python
import functools
import math

import jax
import jax.numpy as jnp
from jax import lax
from jax.experimental import pallas as pl
from jax.experimental.pallas import tpu as pltpu

F32 = jnp.float32
BF16 = jnp.bfloat16
EPS = 1e-5

LANES = 128
SUBLANES = 8
VMEM_LIMIT_BYTES = 56 * 1024 * 1024

S5_GROUP = 16
S5_STATE = 64
S5_SLAB_GROUPS = 8
MLSTM_HEADS = 4
MLSTM_CHUNK = 128
MLSTM_SEQS_PER_STEP = 2
CONV_WIDTH = 4
GMLP_GROUPS = 4
GMLP_CHUNK = 128
MOBA_HEADS = 4
MOBA_BLOCK = 256
MOBA_TOPK = 3
REL_BUCKETS = 32
REL_MAX_DIST = 128
N_EXPERTS = 8
TOP_K = 2


def _cparams(sem):
    return pltpu.CompilerParams(dimension_semantics=sem, vmem_limit_bytes=VMEM_LIMIT_BYTES)


def _rms_bf16(x, g):
    var = jnp.mean(x * x, axis=-1, keepdims=True)
    return (x * lax.rsqrt(var + EPS) * g).astype(BF16)


def _gelu(x):
    return jax.nn.gelu(x, approximate=True)


def _sigmoid(x):
    return 1.0 / (1.0 + jnp.exp(-x))


def _norm_proj_kernel(x_ref, g_ref, w_ref, *out_refs, splits):
    hn = _rms_bf16(x_ref[0], g_ref[...])
    for o_ref, (c0, c1) in zip(out_refs, splits):
        r = jnp.dot(hn, w_ref[:, c0:c1], preferred_element_type=F32)
        o_ref[...] = r.reshape(o_ref.shape).astype(o_ref.dtype)


def _norm_proj(x, g, w, outs, tm):
    B, S, D = x.shape
    splits = tuple((c0, c1) for c0, c1, _, _ in outs)
    out_shape, out_specs = [], []
    for c0, c1, dt, time_major in outs:
        n = c1 - c0
        if time_major:
            out_shape.append(jax.ShapeDtypeStruct((S, B * n), dt))
            out_specs.append(pl.BlockSpec((tm, n), lambda b, i: (i, b)))
        else:
            out_shape.append(jax.ShapeDtypeStruct((B, S, n), dt))
            out_specs.append(pl.BlockSpec((1, tm, n), lambda b, i: (b, i, 0)))
    return pl.pallas_call(
        functools.partial(_norm_proj_kernel, splits=splits),
        out_shape=out_shape,
        grid=(B, S // tm),
        in_specs=[
            pl.BlockSpec((1, tm, D), lambda b, i: (b, i, 0)),
            pl.BlockSpec((1, D), lambda b, i: (0, 0)),
            pl.BlockSpec(w.shape, lambda b, i: (0, 0)),
        ],
        out_specs=out_specs,
        compiler_params=_cparams(("parallel", "parallel")),
        name="norm_proj",
    )(x, g.reshape(1, D), w)


def _s5_kernel(u_ref, bmat_ref, ar_ref, ai_ref, cmat_ref, d_ref, wglu_ref, y_ref,
               buf_ref, xr_ref, xi_ref, *, lc, nb, nslab, sw):
    @pl.when(pl.program_id(0) == 0)
    def _():
        xr_ref[...] = jnp.zeros_like(xr_ref)
        xi_ref[...] = jnp.zeros_like(xi_ref)

    u = u_ref[...]
    for k in range(nslab):
        buf_ref[:, 2 * sw * k:2 * sw * (k + 1)] = jnp.dot(
            u[:, LANES * k:LANES * (k + 1)], bmat_ref[k], preferred_element_type=F32)

    for k in range(nslab):
        re_cols = slice(2 * sw * k, 2 * sw * k + sw)
        im_cols = slice(2 * sw * k + sw, 2 * sw * (k + 1))
        st_cols = slice(sw * k, sw * (k + 1))
        ar = jnp.broadcast_to(ar_ref[:, st_cols], (nb, sw))
        ai = jnp.broadcast_to(ai_ref[:, st_cols], (nb, sw))

        def step(t, carry, re_cols=re_cols, im_cols=im_cols, ar=ar, ai=ai):
            xr, xi = carry
            rows = pl.ds(pl.multiple_of(t * nb, nb), nb)
            nxr = ar * xr - ai * xi + buf_ref[rows, re_cols]
            nxi = ar * xi + ai * xr + buf_ref[rows, im_cols]
            buf_ref[rows, re_cols] = nxr
            buf_ref[rows, im_cols] = nxi
            return nxr, nxi

        xr, xi = lax.fori_loop(0, lc, step, (xr_ref[:, st_cols], xi_ref[:, st_cols]), unroll=4)
        xr_ref[:, st_cols] = xr
        xi_ref[:, st_cols] = xi

    ys = []
    for k in range(nslab):
        st = buf_ref[:, 2 * sw * k:2 * sw * (k + 1)].astype(BF16)
        ys.append(jnp.dot(st, cmat_ref[k], preferred_element_type=F32))
    y = jnp.concatenate(ys, axis=-1)
    y = _gelu(y + d_ref[...] * u.astype(F32))
    gl = jnp.dot(y.astype(BF16), wglu_ref[...], preferred_element_type=F32)
    y_ref[...] = (y * _sigmoid(gl)).astype(y_ref.dtype)


def _s5_params(lam_re, lam_im, log_dt, b_re, b_im, c_re, c_im):
    G, P = lam_re.shape
    Hc = b_re.shape[-1]
    dt = jnp.exp(log_dt.astype(F32))[:, None]
    mag = jnp.exp(lam_re * dt)
    ar = mag * jnp.cos(lam_im * dt)
    ai = mag * jnp.sin(lam_im * dt)
    den = lam_re * lam_re + lam_im * lam_im
    cr = ((ar - 1.0) * lam_re + ai * lam_im) / den
    ci = (ai * lam_re - (ar - 1.0) * lam_im) / den
    bb_re = cr[..., None] * b_re - ci[..., None] * b_im
    bb_im = cr[..., None] * b_im + ci[..., None] * b_re
    gs = S5_SLAB_GROUPS
    nslab = G // gs
    eye = jnp.eye(gs, dtype=F32)

    def bd_in(b):
        b = b.reshape(nslab, gs, P, Hc)
        return jnp.einsum('kgph,gj->kghjp', b, eye).reshape(nslab, gs * Hc, gs * P)

    def bd_out(c):
        c = c.reshape(nslab, gs, Hc, P)
        return jnp.einsum('kghp,gj->kgpjh', c, eye).reshape(nslab, gs * P, gs * Hc)

    bmat = jnp.concatenate([bd_in(bb_re), bd_in(bb_im)], axis=-1).astype(BF16)
    cmat = jnp.concatenate([bd_out(c_re), -bd_out(c_im)], axis=1).astype(BF16)
    return bmat, ar.reshape(1, G * P), ai.reshape(1, G * P), cmat


def _s5(u_tm, nb, bmat, ar, ai, cmat, d_skip, w_glu, lc):
    R, W = u_tm.shape
    S = R // nb
    nslab = bmat.shape[0]
    sw = bmat.shape[2] // 2
    rows = lc * nb
    return pl.pallas_call(
        functools.partial(_s5_kernel, lc=lc, nb=nb, nslab=nslab, sw=sw),
        out_shape=jax.ShapeDtypeStruct((R, W), BF16),
        grid=(S // lc,),
        in_specs=[
            pl.BlockSpec((rows, W), lambda i: (i, 0)),
            pl.BlockSpec(bmat.shape, lambda i: (0, 0, 0)),
            pl.BlockSpec(ar.shape, lambda i: (0, 0)),
            pl.BlockSpec(ai.shape, lambda i: (0, 0)),
            pl.BlockSpec(cmat.shape, lambda i: (0, 0, 0)),
            pl.BlockSpec((1, W), lambda i: (0, 0)),
            pl.BlockSpec(w_glu.shape, lambda i: (0, 0)),
        ],
        out_specs=pl.BlockSpec((rows, W), lambda i: (i, 0)),
        scratch_shapes=[
            pltpu.VMEM((rows, 2 * sw * nslab), F32),
            pltpu.VMEM((nb, sw * nslab), F32),
            pltpu.VMEM((nb, sw * nslab), F32),
        ],
        compiler_params=_cparams(("arbitrary",)),
        name="s5",
    )(u_tm, bmat, ar, ai, cmat, d_skip.reshape(1, W), w_glu)


def _mlstm_chunk(x, vb, o_pre, g, tail, cx_all, m_row, cw, cb, ng, *, L, H, Dh):
    W = H * Dh
    halo = SUBLANES
    log_scale = -0.5 * math.log(Dh)

    row = lax.broadcasted_iota(jnp.int32, (L, L), 0)
    col = lax.broadcasted_iota(jnp.int32, (L, L), 1)
    conv = cb + cw[0:1, :] * x.astype(F32)
    corr = jnp.zeros((halo, 2 * W), F32)
    for j in range(1, CONV_WIDTH):
        shift = (row - col == j).astype(BF16)
        conv = conv + cw[j:j + 1, :] * jnp.dot(shift, x, preferred_element_type=F32)
        corr = corr + cw[j:j + 1, :] * tail[halo - j:2 * halo - j, :]
    conv = jnp.concatenate([conv[:halo] + corr, conv[halo:]], axis=0)
    new_tail = x[L - halo:, :].astype(F32)
    qk = (conv * _sigmoid(conv)).astype(BF16)
    q = qk[:, :W]
    kb = qk[:, W:]
    ones = jnp.ones((L, Dh), BF16)
    sq_ones = jnp.ones((Dh, Dh), BF16)

    log_i = g[:, :LANES]
    f_pre = g[:, LANES:]
    logf = jnp.minimum(f_pre, 0.0) - jnp.log(1.0 + jnp.exp(-jnp.abs(f_pre)))
    causal = row >= col
    bcum = jnp.dot(causal.astype(F32), logf, preferred_element_type=F32, precision=lax.Precision.HIGHEST)
    w_cols = log_i - bcum
    trow = lax.broadcasted_iota(jnp.int32, (L, LANES), 0)
    cmax = w_cols
    k = 1
    while k < L:
        cmax = jnp.maximum(cmax, jnp.where(trow >= k, pltpu.roll(cmax, k, axis=0), -jnp.inf))
        k *= 2
    m_inter = bcum + m_row
    m_t = jnp.maximum(bcum + cmax, m_inter)
    u_cols = bcum - m_t + log_scale
    wi_cols = jnp.exp(m_inter - m_t)
    em_cols = jnp.exp(-m_t)
    b_last = bcum[L - 1:L, :]
    m_loc = b_last + cmax[L - 1:L, :]
    m_new = jnp.maximum(b_last + m_row, m_loc)
    s_prev = jnp.exp(b_last + m_row - m_new)
    s_loc = jnp.exp(m_loc - m_new)
    w_rows = w_cols.T

    heads = range(H)
    hs = [slice(h * Dh, (h + 1) * Dh) for h in heads]
    hl = [slice(h, h + 1) for h in heads]
    nt = (((1,), (1,)), ((), ()))
    vext = [jnp.concatenate([vb[:, hs[h]], ones], axis=1) for h in heads]
    qk_t = [lax.dot_general(q[:, hs[h]], kb[:, hs[h]], nt, preferred_element_type=F32) for h in heads]
    r2 = [jnp.dot(q[:, hs[h]], cx_all[h].astype(BF16), preferred_element_type=F32) for h in heads]
    decay = [jnp.exp(jnp.where(causal, u_cols[:, hl[h]] + w_rows[hl[h], :], -jnp.inf)) for h in heads]
    s = [(qk_t[h] * decay[h]).astype(BF16) for h in heads]
    r1 = [jnp.dot(s[h], vext[h], preferred_element_type=F32) for h in heads]
    hh = []
    for h in heads:
        wi = jnp.broadcast_to(wi_cols[:, hl[h]], (L, Dh))
        num = r1[h][:, :Dh] + wi * r2[h][:, :Dh]
        den = r1[h][:, Dh:] + wi * r2[h][:, Dh:]
        hh.append(num / jnp.maximum(jnp.abs(den), jnp.broadcast_to(em_cols[:, hl[h]], (L, Dh))))
    msq = [jnp.dot((hh[h] * hh[h]).astype(BF16), sq_ones, preferred_element_type=F32) * (1.0 / Dh) for h in heads]
    ys = [(_sigmoid(o_pre[:, hs[h]].astype(F32)) * hh[h] * lax.rsqrt(msq[h] + EPS) * ng[:, hs[h]]).astype(BF16)
          for h in heads]
    kwt = [(kb[:, hs[h]].astype(F32).T
            * jnp.exp(b_last[:, hl[h]] + w_rows[hl[h], :] - m_loc[:, hl[h]] + log_scale)).astype(BF16) for h in heads]
    cx_loc = [jnp.dot(kwt[h], vext[h], preferred_element_type=F32) for h in heads]
    new_cx = [s_prev[:, hl[h]] * cx_all[h] + s_loc[:, hl[h]] * cx_loc[h] for h in heads]
    return ys, new_tail, new_cx, m_new


def _mlstm_kernel(qk_ref, v_ref, o_ref, gt_ref, cw_ref, cb_ref, gb_ref, ng_ref, y_ref,
                  tail_ref, c_ref, m_ref, *, L, H, Dh, BB):
    @pl.when(pl.program_id(1) == 0)
    def _():
        tail_ref[...] = jnp.zeros_like(tail_ref)
        c_ref[...] = jnp.zeros_like(c_ref)
        m_ref[...] = jnp.zeros_like(m_ref)

    tails = [tail_ref[bb] for bb in range(BB)]
    cxs = [[c_ref[bb, h] for h in range(H)] for bb in range(BB)]
    m_rows = [m_ref[bb, 0:1, :] for bb in range(BB)]
    results = []
    for bb in range(BB):
        g = gt_ref[bb] + gb_ref[...]
        results.append(_mlstm_chunk(qk_ref[bb], v_ref[bb].astype(BF16), o_ref[bb], g, tails[bb], cxs[bb],
                                    m_rows[bb], cw_ref[...], cb_ref[...], ng_ref[...], L=L, H=H, Dh=Dh))
    for bb, (ys, new_tail, new_cx, m_new) in enumerate(results):
        for h in range(H):
            y_ref[bb, :, h * Dh:(h + 1) * Dh] = ys[h]
            c_ref[bb, h] = new_cx[h]
        tail_ref[bb, 0:SUBLANES, :] = new_tail
        m_ref[bb, 0:1, :] = m_new


def _mlstm(qk_pre, v, o_pre, gates, conv_w, conv_b, gate_b_pad, norm_g):
    B, S, W2 = qk_pre.shape
    W = W2 // 2
    H, L = MLSTM_HEADS, MLSTM_CHUNK
    Dh = W // H
    BB = _pick_tile(B, MLSTM_SEQS_PER_STEP)
    return pl.pallas_call(
        functools.partial(_mlstm_kernel, L=L, H=H, Dh=Dh, BB=BB),
        out_shape=jax.ShapeDtypeStruct((B, S, W), BF16),
        grid=(B // BB, S // L),
        in_specs=[
            pl.BlockSpec((BB, L, W2), lambda b, c: (b, c, 0)),
            pl.BlockSpec((BB, L, W), lambda b, c: (b, c, 0)),
            pl.BlockSpec((BB, L, W), lambda b, c: (b, c, 0)),
            pl.BlockSpec((BB, L, 2 * LANES), lambda b, c: (b, c, 0)),
            pl.BlockSpec((CONV_WIDTH, W2), lambda b, c: (0, 0)),
            pl.BlockSpec((1, W2), lambda b, c: (0, 0)),
            pl.BlockSpec((1, 2 * LANES), lambda b, c: (0, 0)),
            pl.BlockSpec((1, W), lambda b, c: (0, 0)),
        ],
        out_specs=pl.BlockSpec((BB, L, W), lambda b, c: (b, c, 0)),
        scratch_shapes=[
            pltpu.VMEM((BB, 2 * SUBLANES, W2), F32),
            pltpu.VMEM((BB, H, Dh, 2 * Dh), F32),
            pltpu.VMEM((BB, SUBLANES, LANES), F32),
        ],
        compiler_params=_cparams(("parallel", "arbitrary")),
        name="mlstm",
    )(qk_pre, v, o_pre, gates, conv_w, conv_b.reshape(1, W2), gate_b_pad, norm_g.reshape(1, W))


def _out_proj_kernel(res_ref, a_ref, b_ref, wa_ref, wb_ref, o_ref):
    a = a_ref[...].reshape(a_ref.shape[-2:])
    b = b_ref[...].reshape(b_ref.shape[-2:])
    acc = jnp.dot(a, wa_ref[...], preferred_element_type=F32)
    acc = acc + jnp.dot(b, wb_ref[...], preferred_element_type=F32)
    o_ref[0] = res_ref[0] + acc


def _out_proj(res, ya, yb, w, tm, a_time_major):
    B, S, D = res.shape
    wa_n = w.shape[0] // 2
    wa, wb = w[:wa_n], w[wa_n:]
    if a_time_major:
        a_spec = pl.BlockSpec((tm, wa_n), lambda b, i: (i, b))
    else:
        a_spec = pl.BlockSpec((1, tm, wa_n), lambda b, i: (b, i, 0))
    return pl.pallas_call(
        _out_proj_kernel,
        out_shape=jax.ShapeDtypeStruct((B, S, D), F32),
        grid=(B, S // tm),
        in_specs=[
            pl.BlockSpec((1, tm, D), lambda b, i: (b, i, 0)),
            a_spec,
            pl.BlockSpec((1, tm, w.shape[0] - wa_n), lambda b, i: (b, i, 0)),
            pl.BlockSpec(wa.shape, lambda b, i: (0, 0)),
            pl.BlockSpec(wb.shape, lambda b, i: (0, 0)),
        ],
        out_specs=pl.BlockSpec((1, tm, D), lambda b, i: (b, i, 0)),
        compiler_params=_cparams(("parallel", "parallel")),
        name="out_proj",
    )(res, ya, yb, wa, wb)


def _ffn_kernel(x_ref, g_ref, wg_ref, wu_ref, wd_ref, o_ref, hn_ref, acc_ref):
    f = pl.program_id(1)

    @pl.when(f == 0)
    def _():
        hn_ref[...] = _rms_bf16(x_ref[...], g_ref[...])
        acc_ref[...] = jnp.zeros_like(acc_ref)

    hn = hn_ref[...]
    a = jnp.dot(hn, wg_ref[...], preferred_element_type=F32)
    u = jnp.dot(hn, wu_ref[...], preferred_element_type=F32)
    hmid = (a * _sigmoid(a) * u).astype(BF16)
    acc_ref[...] += jnp.dot(hmid, wd_ref[...], preferred_element_type=F32)

    @pl.when(f == pl.num_programs(1) - 1)
    def _():
        o_ref[...] = x_ref[...] + acc_ref[...]


def _ffn(x2d, g, wg, wu, wd, tm, fc):
    T, D = x2d.shape
    F = wg.shape[1]
    return pl.pallas_call(
        _ffn_kernel,
        out_shape=jax.ShapeDtypeStruct((T, D), F32),
        grid=(T // tm, F // fc),
        in_specs=[
            pl.BlockSpec((tm, D), lambda i, f: (i, 0)),
            pl.BlockSpec((1, D), lambda i, f: (0, 0)),
            pl.BlockSpec((D, fc), lambda i, f: (0, f)),
            pl.BlockSpec((D, fc), lambda i, f: (0, f)),
            pl.BlockSpec((fc, D), lambda i, f: (f, 0)),
        ],
        out_specs=pl.BlockSpec((tm, D), lambda i, f: (i, 0)),
        scratch_shapes=[pltpu.VMEM((tm, D), BF16), pltpu.VMEM((tm, D), F32)],
        compiler_params=_cparams(("parallel", "arbitrary")),
        name="ffn_swiglu",
    )(x2d, g.reshape(1, D), wg, wu, wd)


def _pick_tile(n, target):
    t = min(n, target)
    while n % t:
        t //= 2
    return t


def _layer_ab(h, norm_g, w_in, lam_re, lam_im, log_dt, b_re, b_im, c_re, c_im, d_skip, w_glu,
              conv_w, conv_b, gate_b, mlstm_norm_g, w_out):
    B, S, D = h.shape
    s5w = lam_re.shape[0] * S5_GROUP
    mw = mlstm_norm_g.shape[0]
    c_qk, c_v, c_o, c_if = s5w, s5w + 2 * mw, s5w + 3 * mw, s5w + 4 * mw
    nh = (w_in.shape[1] - c_if) // 2
    lane_pad = ((0, 0), (0, LANES - nh))
    w_pad = jnp.concatenate([w_in[:, :c_if], jnp.pad(w_in[:, c_if:c_if + nh], lane_pad),
                             jnp.pad(w_in[:, c_if + nh:], lane_pad)], axis=1).astype(BF16)
    tm = _pick_tile(S, 512)
    u_tm, qk_pre, v, o_pre, gates = _norm_proj(
        h, norm_g, w_pad,
        [(0, c_qk, BF16, True), (c_qk, c_v, BF16, False), (c_v, c_o, BF16, False),
         (c_o, c_if, BF16, False), (c_if, c_if + 2 * LANES, F32, False)], tm)
    bmat, ar, ai, cmat = _s5_params(lam_re, lam_im, log_dt, b_re, b_im, c_re, c_im)
    y_a = _s5(u_tm.reshape(S * B, s5w), B, bmat, ar, ai, cmat, d_skip, w_glu.astype(BF16),
              lc=_pick_tile(S, 32))
    gate_b_pad = jnp.concatenate([jnp.pad(gate_b[:nh], (0, LANES - nh)),
                                  jnp.pad(gate_b[nh:], (0, LANES - nh))]).reshape(1, 2 * LANES)
    y_b = _mlstm(qk_pre, v, o_pre, gates, conv_w, conv_b, gate_b_pad, mlstm_norm_g)
    return _out_proj(h, y_a.reshape(S, B * s5w), y_b, w_out.astype(BF16), tm, True)


def _gmlp_kernel(u_ref, v_ref, ng_ref, nb_ref, ws_ref, bs_ref, y_ref, *, L, G, Dg, nchunk):
    u = _gelu(u_ref[0].astype(F32))
    v = _gelu(v_ref[0].astype(F32))
    mu = jnp.mean(v, axis=-1, keepdims=True)
    vc = v - mu
    var = jnp.mean(vc * vc, axis=-1, keepdims=True)
    vn = (vc * lax.rsqrt(var + EPS) * ng_ref[...] + nb_ref[...]).astype(BF16)
    for c in range(nchunk):
        rows = slice(c * L, (c + 1) * L)
        for g in range(G):
            cols = slice(g * Dg, (g + 1) * Dg)
            s = jnp.dot(ws_ref[g], vn[rows, cols], preferred_element_type=F32) + bs_ref[:, g:g + 1]
            y_ref[0, rows, cols] = (u[rows, cols] * s).astype(y_ref.dtype)


def _gmlp(u_pre, v_pre, norm_g, norm_b, w_s, b_s, tm):
    B, S, W = u_pre.shape
    G, L = GMLP_GROUPS, GMLP_CHUNK
    ws = (w_s * jnp.tril(jnp.ones((L, L), w_s.dtype))).astype(BF16)
    bs = jnp.pad(b_s.T, ((0, 0), (0, LANES - G)))
    return pl.pallas_call(
        functools.partial(_gmlp_kernel, L=L, G=G, Dg=W // G, nchunk=tm // L),
        out_shape=jax.ShapeDtypeStruct((B, S, W), BF16),
        grid=(B, S // tm),
        in_specs=[
            pl.BlockSpec((1, tm, W), lambda b, i: (b, i, 0)),
            pl.BlockSpec((1, tm, W), lambda b, i: (b, i, 0)),
            pl.BlockSpec((1, W), lambda b, i: (0, 0)),
            pl.BlockSpec((1, W), lambda b, i: (0, 0)),
            pl.BlockSpec((G, L, L), lambda b, i: (0, 0, 0)),
            pl.BlockSpec((L, LANES), lambda b, i: (0, 0)),
        ],
        out_specs=pl.BlockSpec((1, tm, W), lambda b, i: (b, i, 0)),
        compiler_params=_cparams(("parallel", "parallel")),
        name="gmlp",
    )(u_pre, v_pre, norm_g.reshape(1, W), norm_b.reshape(1, W), ws, bs)


NEG_BIG = -1e30


def _moba_kernel(q_ref, k_ref, v_ref, t0_ref, t1_ref, bfar_ref, y_ref, vext_ref, s_ref, p_ref,
                 *, BL, NB, Dh, topk):
    nt = (((1,), (1,)), ((), ()))
    scale = 1.0 / math.sqrt(Dh)
    vext_ref[:, :Dh] = v_ref[0]
    vext_ref[:, Dh:] = jnp.ones((NB * BL, Dh), BF16)
    bfar = bfar_ref[0][:, 0:1]
    kmean = jnp.concatenate(
        [jnp.mean(k_ref[0, n * BL:(n + 1) * BL, :].astype(F32), axis=0, keepdims=True) for n in range(NB)],
        axis=0)
    nrow = lax.broadcasted_iota(jnp.int32, (NB, BL), 0)
    eye = (lax.broadcasted_iota(jnp.int32, (BL, BL), 0) ==
           lax.broadcasted_iota(jnp.int32, (BL, BL), 1)).astype(BF16)

    for qi in range(NB):
        rows = slice(qi * BL, (qi + 1) * BL)
        qf = q_ref[0, rows, :].astype(F32)
        qs = (qf * scale).astype(BF16)
        neg_cols = None
        if qi > topk:
            gate = lax.dot_general(kmean, qf, nt, preferred_element_type=F32,
                                   precision=lax.Precision.HIGHEST)
            rank = jnp.zeros((NB, BL), F32)
            for m in range(qi):
                gm = gate[m:m + 1, :]
                rank = rank + jnp.where(gm > gate, 1.0, jnp.where(gm == gate, jnp.where(nrow > m, 1.0, 0.0), 0.0))
            sel_rows = jnp.where(nrow < qi, jnp.where(rank < topk, 1.0, 0.0), 0.0)
            sel_pad = jnp.concatenate([sel_rows, jnp.zeros((LANES - NB, BL), F32)], axis=0).astype(BF16)
            sel_cols = lax.dot_general(eye, sel_pad, nt, preferred_element_type=F32)
            neg_cols = (1.0 - sel_cols) * NEG_BIG

        m_tile = None
        for kb in range(qi + 1):
            s = lax.dot_general(qs, k_ref[0, kb * BL:(kb + 1) * BL, :], nt, preferred_element_type=F32)
            if kb == qi:
                s = s + t0_ref[0]
            elif kb == qi - 1:
                s = s + t1_ref[0]
                if neg_cols is not None:
                    s = s + neg_cols[:, kb:kb + 1]
            else:
                s = s + (bfar if neg_cols is None else bfar + neg_cols[:, kb:kb + 1])
            s_ref[:, kb * BL:(kb + 1) * BL] = s
            for c in range(BL // LANES):
                piece = s[:, c * LANES:(c + 1) * LANES]
                m_tile = piece if m_tile is None else jnp.maximum(m_tile, piece)
        m = jnp.max(m_tile, axis=-1, keepdims=True)
        nk = (qi + 1) * BL
        p_ref[:, :nk] = jnp.exp(s_ref[:, :nk] - m).astype(BF16)
        acc = jnp.dot(p_ref[:, :nk], vext_ref[:nk, :], preferred_element_type=F32)
        y_ref[0, rows, :] = (acc[:, :Dh] / acc[:, Dh:Dh + 1]).astype(y_ref.dtype)


def _rel_bucket(n):
    max_exact = REL_BUCKETS // 2
    nf = jnp.maximum(n, 1).astype(F32)
    large = max_exact + (jnp.log(nf / max_exact) / math.log(REL_MAX_DIST / max_exact)
                         * (REL_BUCKETS - max_exact)).astype(jnp.int32)
    large = jnp.minimum(large, REL_BUCKETS - 1)
    return jnp.where(n < max_exact, n, large)


def _moba(qkv, rel_bias):
    B, S, W3 = qkv.shape
    H, BL = MOBA_HEADS, MOBA_BLOCK
    W = W3 // 3
    Dh = W // H
    NB = S // BL
    assert BL + 1 >= REL_MAX_DIST and Dh == LANES and S % BL == 0
    i = jnp.arange(BL, dtype=jnp.int32)[:, None]
    j = jnp.arange(BL, dtype=jnp.int32)[None, :]
    buckets = jnp.arange(REL_BUCKETS, dtype=jnp.int32)

    def bias_tile(dist):
        onehot = (_rel_bucket(dist)[:, :, None] == buckets).astype(F32)
        return jnp.einsum('ijb,bh->hij', onehot, rel_bias, precision=lax.Precision.HIGHEST)

    t0 = jnp.where(i >= j, bias_tile(jnp.maximum(i - j, 0)), NEG_BIG)
    t1 = bias_tile(BL + i - j)
    bfar = jnp.broadcast_to(rel_bias.T[:, REL_BUCKETS - 1][:, None, None], (H, 1, LANES))
    return pl.pallas_call(
        functools.partial(_moba_kernel, BL=BL, NB=NB, Dh=Dh, topk=min(MOBA_TOPK, NB)),
        out_shape=jax.ShapeDtypeStruct((B, S, W), BF16),
        grid=(B, H),
        in_specs=[
            pl.BlockSpec((1, S, Dh), lambda b, h: (b, 0, h)),
            pl.BlockSpec((1, S, Dh), lambda b, h: (b, 0, H + h)),
            pl.BlockSpec((1, S, Dh), lambda b, h: (b, 0, 2 * H + h)),
            pl.BlockSpec((1, BL, BL), lambda b, h: (h, 0, 0)),
            pl.BlockSpec((1, BL, BL), lambda b, h: (h, 0, 0)),
            pl.BlockSpec((1, 1, LANES), lambda b, h: (h, 0, 0)),
        ],
        out_specs=pl.BlockSpec((1, S, Dh), lambda b, h: (b, 0, h)),
        scratch_shapes=[
            pltpu.VMEM((S, 2 * Dh), BF16),
            pltpu.VMEM((BL, S), F32),
            pltpu.VMEM((BL, S), BF16),
        ],
        compiler_params=_cparams(("parallel", "parallel")),
        name="moba",
    )(qkv, qkv, qkv, t0, t1, bfar)


def _layer_cd(h, norm_g, w_in, gmlp_norm_g, gmlp_norm_b, gmlp_w_s, gmlp_b_s, rel_bias, w_out):
    B, S, D = h.shape
    gw = gmlp_norm_g.shape[0]
    tm = _pick_tile(S, 512)
    u_pre, v_pre, qkv = _norm_proj(
        h, norm_g, w_in.astype(BF16),
        [(0, gw, BF16, False), (gw, 2 * gw, BF16, False), (2 * gw, w_in.shape[1], BF16, False)], tm)
    y_c = _gmlp(u_pre, v_pre, gmlp_norm_g, gmlp_norm_b, gmlp_w_s, gmlp_b_s, tm)
    y_d = _moba(qkv, rel_bias)
    return _out_proj(h, y_c, y_d, w_out.astype(BF16), tm, False)


def _router_kernel(x_ref, g_ref, wr_ref, rb_ref, idx_ref, gts_ref, cnt_ref, carry_ref, *, E):
    @pl.when(pl.program_id(0) == 0)
    def _():
        carry_ref[...] = jnp.zeros_like(carry_ref)

    x = x_ref[...]
    tm = x.shape[0]
    var = jnp.mean(x * x, axis=-1, keepdims=True)
    hn = x * lax.rsqrt(var + EPS) * g_ref[...]
    logits = lax.dot_general(wr_ref[...], hn, (((1,), (1,)), ((), ())), preferred_element_type=F32,
                             precision=lax.Precision.HIGHEST) + rb_ref[:, 0:1]
    rowi = lax.broadcasted_iota(jnp.int32, (E, tm), 0)
    v1 = jnp.max(logits, axis=0, keepdims=True)
    e1 = jnp.min(jnp.where(logits == v1, rowi, E), axis=0, keepdims=True)
    masked = jnp.where(rowi == e1, -jnp.inf, logits)
    v2 = jnp.max(masked, axis=0, keepdims=True)
    e2 = jnp.min(jnp.where(masked == v2, rowi, E), axis=0, keepdims=True)
    ex = jnp.exp(v2 - v1)
    g1 = 1.0 / (1.0 + ex)
    g2 = ex / (1.0 + ex)
    oh1 = rowi == e1
    oh2 = rowi == e2
    cnt = jnp.where(oh1, 1.0, jnp.where(oh2, 1.0, 0.0))
    before = (lax.broadcasted_iota(jnp.int32, (tm, tm), 0) <
              lax.broadcasted_iota(jnp.int32, (tm, tm), 1)).astype(BF16)
    excl = jnp.dot(cnt.astype(BF16), before, preferred_element_type=F32) + carry_ref[:, 0:1]
    r1 = jnp.sum(jnp.where(oh1, excl, 0.0), axis=0, keepdims=True).astype(jnp.int32)
    r2 = jnp.sum(jnp.where(oh2, excl, 0.0), axis=0, keepdims=True).astype(jnp.int32)
    idx_ref[...] = jnp.where(rowi == 0, e1, jnp.where(rowi == 1, e2, jnp.where(rowi == 2, r1, jnp.where(rowi == 3, r2, 0))))
    gts_ref[...] = jnp.where(rowi == 0, g1, jnp.where(rowi == 1, g2, 0.0))
    carry_ref[...] = carry_ref[...] + jnp.sum(cnt, axis=1, keepdims=True)
    cnt_ref[...] = carry_ref[...]


def _router(x2d, g, router_w, router_b, tm):
    T, D = x2d.shape
    E = router_w.shape[1]
    assert E == SUBLANES
    return pl.pallas_call(
        functools.partial(_router_kernel, E=E),
        out_shape=[
            jax.ShapeDtypeStruct((E, T), jnp.int32),
            jax.ShapeDtypeStruct((E, T), F32),
            jax.ShapeDtypeStruct((E, LANES), F32),
        ],
        grid=(T // tm,),
        in_specs=[
            pl.BlockSpec((tm, D), lambda i: (i, 0)),
            pl.BlockSpec((1, D), lambda i: (0, 0)),
            pl.BlockSpec((E, D), lambda i: (0, 0)),
            pl.BlockSpec((E, LANES), lambda i: (0, 0)),
        ],
        out_specs=[
            pl.BlockSpec((E, tm), lambda i: (0, i)),
            pl.BlockSpec((E, tm), lambda i: (0, i)),
            pl.BlockSpec((E, LANES), lambda i: (0, 0)),
        ],
        scratch_shapes=[pltpu.VMEM((E, LANES), F32)],
        compiler_params=_cparams(("arbitrary",)),
        name="moe_router",
    )(x2d, g.reshape(1, D), router_w.T, jnp.broadcast_to(router_b[:, None], (E, LANES)))


DMA_ISSUE_UNROLL = 8


def _dispatch_kernel(pos1_ref, pos2_ref, ends_ref, h_ref, xs_hbm, zero_ref, sem, zsem, *, tm, tg, E):
    i = pl.program_id(0)

    @pl.when(i == 0)
    def _():
        zero_ref[...] = jnp.zeros_like(zero_ref)

        def zero_copy(e):
            start = ends_ref[e] - tg
            return pltpu.make_async_copy(zero_ref, xs_hbm.at[pl.ds(pl.multiple_of(start, tg), tg)], zsem)

        def nonempty(e):
            return ends_ref[e] > (ends_ref[e - 1] if e else 0)

        for e in range(E):
            @pl.when(nonempty(e))
            def _(e=e):
                zero_copy(e).start()
        for e in range(E):
            @pl.when(nonempty(e))
            def _(e=e):
                zero_copy(e).wait()

    def body(r, carry):
        t = i * tm + r
        src = h_ref.at[pl.ds(r, 1)]
        pltpu.make_async_copy(src, xs_hbm.at[pl.ds(pos1_ref[t], 1)], sem).start()
        pltpu.make_async_copy(src, xs_hbm.at[pl.ds(pos2_ref[t], 1)], sem).start()
        return carry
    lax.fori_loop(0, tm, body, 0, unroll=DMA_ISSUE_UNROLL)

    for _ in range(TOP_K):
        pltpu.make_async_copy(h_ref, xs_hbm.at[pl.ds(0, tm)], sem).wait()


def _dispatch(pos1, pos2, ends, h2d, n_rows, tm, tg):
    T, D = h2d.shape
    E = ends.shape[0]
    return pl.pallas_call(
        functools.partial(_dispatch_kernel, tm=tm, tg=tg, E=E),
        out_shape=jax.ShapeDtypeStruct((n_rows, D), h2d.dtype),
        grid_spec=pltpu.PrefetchScalarGridSpec(
            num_scalar_prefetch=3,
            grid=(T // tm,),
            in_specs=[pl.BlockSpec((tm, D), lambda i, p1, p2, en: (i, 0))],
            out_specs=pl.BlockSpec(memory_space=pl.ANY),
            scratch_shapes=[pltpu.VMEM((tg, D), h2d.dtype), pltpu.SemaphoreType.DMA, pltpu.SemaphoreType.DMA],
        ),
        compiler_params=_cparams(("arbitrary",)),
        name="moe_dispatch",
    )(pos1, pos2, ends, h2d)


def _gmm_kernel(te_ref, na_ref, x_ref, g_ref, wg_ref, wu_ref, wd_ref, o_ref, xn_ref, acc_ref):
    i = pl.program_id(0)
    f = pl.program_id(1)
    active = i < na_ref[0]

    @pl.when(f == 0)
    def _():
        acc_ref[...] = jnp.zeros_like(acc_ref)

    @pl.when(active & (f == 0))
    def _():
        xn_ref[...] = _rms_bf16(x_ref[...], g_ref[...])

    @pl.when(active)
    def _():
        x = xn_ref[...]
        a = jnp.dot(x, wg_ref[0], preferred_element_type=F32)
        u = jnp.dot(x, wu_ref[0], preferred_element_type=F32)
        hmid = (a * _sigmoid(a) * u).astype(BF16)
        acc_ref[...] += jnp.dot(hmid, wd_ref[0], preferred_element_type=F32)

    @pl.when(f == pl.num_programs(1) - 1)
    def _():
        o_ref[...] = acc_ref[...]


def _gmm(tile_expert, n_active, xs, g, wg, wu, wd, tm, fc):
    R, D = xs.shape
    F = wg.shape[2]
    nf = F // fc

    def fsel(i, f, na):
        return jnp.where(i < na[0], f, nf - 1)

    return pl.pallas_call(
        _gmm_kernel,
        out_shape=jax.ShapeDtypeStruct((R, D), F32),
        grid_spec=pltpu.PrefetchScalarGridSpec(
            num_scalar_prefetch=2,
            grid=(R // tm, nf),
            in_specs=[
                pl.BlockSpec((tm, D), lambda i, f, te, na: (i, 0)),
                pl.BlockSpec((1, D), lambda i, f, te, na: (0, 0)),
                pl.BlockSpec((1, D, fc), lambda i, f, te, na: (te[i], 0, fsel(i, f, na))),
                pl.BlockSpec((1, D, fc), lambda i, f, te, na: (te[i], 0, fsel(i, f, na))),
                pl.BlockSpec((1, fc, D), lambda i, f, te, na: (te[i], fsel(i, f, na), 0)),
            ],
            out_specs=pl.BlockSpec((tm, D), lambda i, f, te, na: (i, 0)),
            scratch_shapes=[pltpu.VMEM((tm, D), BF16), pltpu.VMEM((tm, D), F32)],
        ),
        compiler_params=_cparams(("parallel", "arbitrary")),
        name="moe_gmm",
    )(tile_expert, n_active, xs, g.reshape(1, D), wg, wu, wd)


def _combine_copies(pos1_ref, pos2_ref, ys_hbm, ya_ref, yb_ref, sems, tile, slot, tm):
    def row_copies(r):
        t = tile * tm + r
        return (pltpu.make_async_copy(ys_hbm.at[pl.ds(pos1_ref[t], 1)], ya_ref.at[slot, pl.ds(r, 1)], sems.at[slot]),
                pltpu.make_async_copy(ys_hbm.at[pl.ds(pos2_ref[t], 1)], yb_ref.at[slot, pl.ds(r, 1)], sems.at[slot]))
    return row_copies


def _combine_kernel(pos1_ref, pos2_ref, h_ref, ga_ref, gb_ref, g_ref, ys_hbm, o_ref, ya_ref, yb_ref, sems, *, tm):
    i = pl.program_id(0)
    n = pl.num_programs(0)
    slot = lax.rem(i, 2)

    def start_tile(tile, slot):
        copies = _combine_copies(pos1_ref, pos2_ref, ys_hbm, ya_ref, yb_ref, sems, tile, slot, tm)

        def body(r, carry):
            a, b = copies(r)
            a.start()
            b.start()
            return carry
        lax.fori_loop(0, tm, body, 0, unroll=DMA_ISSUE_UNROLL)

    @pl.when(i == 0)
    def _():
        start_tile(0, 0)

    @pl.when(i + 1 < n)
    def _():
        start_tile(i + 1, 1 - slot)

    pltpu.make_async_copy(ys_hbm.at[pl.ds(0, tm)], ya_ref.at[slot], sems.at[slot]).wait()
    pltpu.make_async_copy(ys_hbm.at[pl.ds(0, tm)], yb_ref.at[slot], sems.at[slot]).wait()

    h = h_ref[...] + ga_ref[...] * ya_ref[slot] + gb_ref[...] * yb_ref[slot]
    var = jnp.mean(h * h, axis=-1, keepdims=True)
    o_ref[...] = h * lax.rsqrt(var + EPS) * g_ref[...]


def _combine(pos1, pos2, h2d, ga, gb, g, ys, tm):
    T, D = h2d.shape
    row = pl.BlockSpec((tm, D), lambda i, p1, p2: (i, 0))
    colv = pl.BlockSpec((tm, 1), lambda i, p1, p2: (i, 0))
    return pl.pallas_call(
        functools.partial(_combine_kernel, tm=tm),
        out_shape=jax.ShapeDtypeStruct((T, D), F32),
        grid_spec=pltpu.PrefetchScalarGridSpec(
            num_scalar_prefetch=2,
            grid=(T // tm,),
            in_specs=[row, colv, colv, pl.BlockSpec((1, D), lambda i, p1, p2: (0, 0)),
                      pl.BlockSpec(memory_space=pl.ANY)],
            out_specs=row,
            scratch_shapes=[pltpu.VMEM((2, tm, D), F32), pltpu.VMEM((2, tm, D), F32),
                            pltpu.SemaphoreType.DMA((2,))],
        ),
        compiler_params=_cparams(("arbitrary",)),
        name="moe_combine_norm",
    )(pos1, pos2, h2d, ga, gb, g.reshape(1, D), ys)


def _moe_final(h, norm_g, final_g, router_w, router_b, w_gate, w_up, w_down):
    B, S, D = h.shape
    T = B * S
    E = router_w.shape[1]
    h2d = h.reshape(T, D)
    idx, gts, cnt = _router(h2d, norm_g, router_w, router_b, _pick_tile(T, 512))
    e1, e2, r1, r2 = idx[0], idx[1], idx[2], idx[3]
    tm = _pick_tile(T, 1024)
    counts = cnt[:, 0].astype(jnp.int32)
    padded = ((counts + tm - 1) // tm) * tm
    ends = jnp.cumsum(padded)
    offs = ends - padded
    eids = jnp.arange(E, dtype=jnp.int32)[:, None]
    pos1 = jnp.sum(jnp.where(e1[None, :] == eids, offs[:, None], 0), axis=0) + r1
    pos2 = jnp.sum(jnp.where(e2[None, :] == eids, offs[:, None], 0), axis=0) + r2
    n_tiles = (TOP_K * T) // tm + E
    tile_start = jnp.arange(n_tiles, dtype=jnp.int32) * tm
    tile_expert = jnp.minimum(jnp.searchsorted(ends, tile_start, side='right'), E - 1).astype(jnp.int32)
    n_active = (ends[-1] // tm).astype(jnp.int32).reshape(1)
    xs = _dispatch(pos1, pos2, ends.astype(jnp.int32), h2d, n_tiles * tm, _pick_tile(T, 1024), tm)
    ys = _gmm(tile_expert, n_active, xs, norm_g, w_gate.astype(BF16), w_up.astype(BF16), w_down.astype(BF16),
              tm, _pick_tile(w_gate.shape[2], 512))
    out = _combine(pos1, pos2, h2d, gts[0][:, None], gts[1][:, None], final_g, ys, _pick_tile(T, 256))
    return out.reshape(B, S, D)


def kernel(x, norm_mix_g, norm_ffn_g, norm_final_g, ab_w_in, s5_lambda_re, s5_lambda_im, s5_log_dt, s5_b_re, s5_b_im, s5_c_re, s5_c_im, s5_d, s5_w_glu, mlstm_conv_w, mlstm_conv_b, mlstm_gate_b, mlstm_norm_g, ab_w_out, ffn_w_gate, ffn_w_up, ffn_w_down, cd_w_in, gmlp_norm_g, gmlp_norm_b, gmlp_w_s, gmlp_b_s, rel_bias, cd_w_out, moe_router_w, moe_router_b, moe_w_gate, moe_w_up, moe_w_down):
    B, S, D = x.shape
    h = _layer_ab(x, norm_mix_g[0], ab_w_in[0], s5_lambda_re[0], s5_lambda_im[0], s5_log_dt[0],
                  s5_b_re[0], s5_b_im[0], s5_c_re[0], s5_c_im[0], s5_d[0], s5_w_glu[0],
                  mlstm_conv_w[0], mlstm_conv_b[0], mlstm_gate_b[0], mlstm_norm_g[0], ab_w_out[0])
    h = _ffn(h.reshape(B * S, D), norm_ffn_g[0], ffn_w_gate[0].astype(BF16), ffn_w_up[0].astype(BF16),
             ffn_w_down[0].astype(BF16), _pick_tile(B * S, 512), _pick_tile(ffn_w_gate.shape[2], 1408))
    h = _layer_cd(h.reshape(B, S, D), norm_mix_g[1], cd_w_in[0], gmlp_norm_g[0], gmlp_norm_b[0],
                  gmlp_w_s[0], gmlp_b_s[0], rel_bias, cd_w_out[0])
    return _moe_final(h, norm_ffn_g[1], norm_final_g, moe_router_w[0], moe_router_b[0],
                      moe_w_gate[0], moe_w_up[0], moe_w_down[0])
```

```python
import functools
import math

import jax
import jax.numpy as jnp
from jax import lax
from jax.experimental import pallas as pl
from jax.experimental.pallas import tpu as pltpu

F32 = jnp.float32
BF16 = jnp.bfloat16
EPS = 1e-5

LANES = 128
SUBLANES = 8
VMEM_LIMIT_BYTES = 56 * 1024 * 1024

S5_GROUP = 16
S5_STATE = 64
S5_SLAB_GROUPS = 8
MLSTM_HEADS = 4
MLSTM_CHUNK = 128
MLSTM_SEQS_PER_STEP = 2
CONV_WIDTH = 4
GMLP_GROUPS = 4
GMLP_CHUNK = 128
MOBA_HEADS = 4
MOBA_BLOCK = 256
MOBA_TOPK = 3
REL_BUCKETS = 32
REL_MAX_DIST = 128
N_EXPERTS = 8
TOP_K = 2


def _cparams(sem):
    return pltpu.CompilerParams(dimension_semantics=sem, vmem_limit_bytes=VMEM_LIMIT_BYTES)


def _rms_bf16(x, g):
    var = jnp.mean(x * x, axis=-1, keepdims=True)
    return (x * lax.rsqrt(var + EPS) * g).astype(BF16)


def _gelu(x):
    return jax.nn.gelu(x, approximate=True)


def _sigmoid(x):
    return 1.0 / (1.0 + jnp.exp(-x))


def _norm_proj_kernel(x_ref, g_ref, w_ref, *out_refs, splits):
    hn = _rms_bf16(x_ref[0], g_ref[...])
    for o_ref, (c0, c1) in zip(out_refs, splits):
        r = jnp.dot(hn, w_ref[:, c0:c1], preferred_element_type=F32)
        o_ref[...] = r.reshape(o_ref.shape).astype(o_ref.dtype)


def _norm_proj(x, g, w, outs, tm):
    B, S, D = x.shape
    splits = tuple((c0, c1) for c0, c1, _, _ in outs)
    out_shape, out_specs = [], []
    for c0, c1, dt, time_major in outs:
        n = c1 - c0
        if time_major:
            out_shape.append(jax.ShapeDtypeStruct((S, B * n), dt))
            out_specs.append(pl.BlockSpec((tm, n), lambda b, i: (i, b)))
        else:
            out_shape.append(jax.ShapeDtypeStruct((B, S, n), dt))
            out_specs.append(pl.BlockSpec((1, tm, n), lambda b, i: (b, i, 0)))
    return pl.pallas_call(
        functools.partial(_norm_proj_kernel, splits=splits),
        out_shape=out_shape,
        grid=(B, S // tm),
        in_specs=[
            pl.BlockSpec((1, tm, D), lambda b, i: (b, i, 0)),
            pl.BlockSpec((1, D), lambda b, i: (0, 0)),
            pl.BlockSpec(w.shape, lambda b, i: (0, 0)),
        ],
        out_specs=out_specs,
        compiler_params=_cparams(("parallel", "parallel")),
        name="norm_proj",
    )(x, g.reshape(1, D), w)


def _s5_kernel(u_ref, bmat_ref, ar_ref, ai_ref, cmat_ref, d_ref, wglu_ref, y_ref,
               xr_ref, xi_ref, *bufs, lc, nb, nslab, sw):
    @pl.when(pl.program_id(0) == 0)
    def _():
        xr_ref[...] = jnp.zeros_like(xr_ref)
        xi_ref[...] = jnp.zeros_like(xi_ref)

    u = u_ref[...]
    xr_all = xr_ref[...]
    xi_all = xi_ref[...]

    def drive(k):
        bufs[k][...] = jnp.dot(u[:, LANES * k:LANES * (k + 1)], bmat_ref[k], preferred_element_type=F32)

    drive(0)
    ys, new_xr, new_xi = [], [], []
    for k in range(nslab):
        if k + 1 < nslab:
            drive(k + 1)
        st_cols = slice(sw * k, sw * (k + 1))
        ar = jnp.broadcast_to(ar_ref[:, st_cols], (nb, sw))
        ai = jnp.broadcast_to(ai_ref[:, st_cols], (nb, sw))
        xr, xi = xr_all[:, st_cols], xi_all[:, st_cols]
        for t in range(lc):
            rows = slice(t * nb, (t + 1) * nb)
            nxr = ar * xr - ai * xi + bufs[k][rows, :sw]
            nxi = ar * xi + ai * xr + bufs[k][rows, sw:]
            bufs[k][rows, :sw] = nxr
            bufs[k][rows, sw:] = nxi
            xr, xi = nxr, nxi
        new_xr.append(xr)
        new_xi.append(xi)
        ys.append(jnp.dot(bufs[k][...].astype(BF16), cmat_ref[k], preferred_element_type=F32))
    xr_ref[...] = jnp.concatenate(new_xr, axis=-1)
    xi_ref[...] = jnp.concatenate(new_xi, axis=-1)
    y = jnp.concatenate(ys, axis=-1)
    y = _gelu(y + d_ref[...] * u.astype(F32))
    gl = jnp.dot(y.astype(BF16), wglu_ref[...], preferred_element_type=F32)
    y_ref[...] = (y * _sigmoid(gl)).astype(y_ref.dtype)


def _s5_params(lam_re, lam_im, log_dt, b_re, b_im, c_re, c_im):
    G, P = lam_re.shape
    Hc = b_re.shape[-1]
    dt = jnp.exp(log_dt.astype(F32))[:, None]
    mag = jnp.exp(lam_re * dt)
    ar = mag * jnp.cos(lam_im * dt)
    ai = mag * jnp.sin(lam_im * dt)
    den = lam_re * lam_re + lam_im * lam_im
    cr = ((ar - 1.0) * lam_re + ai * lam_im) / den
    ci = (ai * lam_re - (ar - 1.0) * lam_im) / den
    bb_re = cr[..., None] * b_re - ci[..., None] * b_im
    bb_im = cr[..., None] * b_im + ci[..., None] * b_re
    gs = S5_SLAB_GROUPS
    nslab = G // gs
    eye = jnp.eye(gs, dtype=F32)

    def bd_in(b):
        b = b.reshape(nslab, gs, P, Hc)
        return jnp.einsum('kgph,gj->kghjp', b, eye).reshape(nslab, gs * Hc, gs * P)

    def bd_out(c):
        c = c.reshape(nslab, gs, Hc, P)
        return jnp.einsum('kghp,gj->kgpjh', c, eye).reshape(nslab, gs * P, gs * Hc)

    bmat = jnp.concatenate([bd_in(bb_re), bd_in(bb_im)], axis=-1).astype(BF16)
    cmat = jnp.concatenate([bd_out(c_re), -bd_out(c_im)], axis=1).astype(BF16)
    return bmat, ar.reshape(1, G * P), ai.reshape(1, G * P), cmat


def _s5(u_tm, nb, bmat, ar, ai, cmat, d_skip, w_glu, lc):
    R, W = u_tm.shape
    S = R // nb
    nslab = bmat.shape[0]
    sw = bmat.shape[2] // 2
    rows = lc * nb
    return pl.pallas_call(
        functools.partial(_s5_kernel, lc=lc, nb=nb, nslab=nslab, sw=sw),
        out_shape=jax.ShapeDtypeStruct((R, W), BF16),
        grid=(S // lc,),
        in_specs=[
            pl.BlockSpec((rows, W), lambda i: (i, 0)),
            pl.BlockSpec(bmat.shape, lambda i: (0, 0, 0)),
            pl.BlockSpec(ar.shape, lambda i: (0, 0)),
            pl.BlockSpec(ai.shape, lambda i: (0, 0)),
            pl.BlockSpec(cmat.shape, lambda i: (0, 0, 0)),
            pl.BlockSpec((1, W), lambda i: (0, 0)),
            pl.BlockSpec(w_glu.shape, lambda i: (0, 0)),
        ],
        out_specs=pl.BlockSpec((rows, W), lambda i: (i, 0)),
        scratch_shapes=[pltpu.VMEM((nb, sw * nslab), F32), pltpu.VMEM((nb, sw * nslab), F32)]
        + [pltpu.VMEM((rows, 2 * sw), F32) for _ in range(nslab)],
        compiler_params=_cparams(("arbitrary",)),
        name="s5",
    )(u_tm, bmat, ar, ai, cmat, d_skip.reshape(1, W), w_glu)


def _mlstm_chunk(x, vb, o_pre, g, tail, cx_all, m_row, cw, cb, ng, *, L, H, Dh):
    W = H * Dh
    halo = SUBLANES
    log_scale = -0.5 * math.log(Dh)

    row = lax.broadcasted_iota(jnp.int32, (L, L), 0)
    col = lax.broadcasted_iota(jnp.int32, (L, L), 1)
    conv = cb + cw[0:1, :] * x.astype(F32)
    corr = jnp.zeros((halo, 2 * W), F32)
    for j in range(1, CONV_WIDTH):
        shift = (row - col == j).astype(BF16)
        conv = conv + cw[j:j + 1, :] * jnp.dot(shift, x, preferred_element_type=F32)
        corr = corr + cw[j:j + 1, :] * tail[halo - j:2 * halo - j, :]
    conv = jnp.concatenate([conv[:halo] + corr, conv[halo:]], axis=0)
    new_tail = x[L - halo:, :].astype(F32)
    qk = (conv * _sigmoid(conv)).astype(BF16)
    q = qk[:, :W]
    kb = qk[:, W:]
    ones = jnp.ones((L, Dh), BF16)
    sq_ones = jnp.ones((Dh, Dh), BF16)

    log_i = g[:, :LANES]
    f_pre = g[:, LANES:]
    logf = jnp.minimum(f_pre, 0.0) - jnp.log(1.0 + jnp.exp(-jnp.abs(f_pre)))
    causal = row >= col
    bcum = jnp.dot(causal.astype(F32), logf, preferred_element_type=F32, precision=lax.Precision.HIGHEST)
    w_cols = log_i - bcum
    trow = lax.broadcasted_iota(jnp.int32, (L, LANES), 0)
    cmax = w_cols
    k = 1
    while k < L:
        cmax = jnp.maximum(cmax, jnp.where(trow >= k, pltpu.roll(cmax, k, axis=0), -jnp.inf))
        k *= 2
    m_inter = bcum + m_row
    m_t = jnp.maximum(bcum + cmax, m_inter)
    u_cols = bcum - m_t + log_scale
    wi_cols = jnp.exp(m_inter - m_t)
    em_cols = jnp.exp(-m_t)
    b_last = bcum[L - 1:L, :]
    m_loc = b_last + cmax[L - 1:L, :]
    m_new = jnp.maximum(b_last + m_row, m_loc)
    s_prev = jnp.exp(b_last + m_row - m_new)
    s_loc = jnp.exp(m_loc - m_new)
    w_rows = w_cols.T

    heads = range(H)
    hs = [slice(h * Dh, (h + 1) * Dh) for h in heads]
    hl = [slice(h, h + 1) for h in heads]
    nt = (((1,), (1,)), ((), ()))
    vext = [jnp.concatenate([vb[:, hs[h]], ones], axis=1) for h in heads]
    qk_t = [lax.dot_general(q[:, hs[h]], kb[:, hs[h]], nt, preferred_element_type=F32) for h in heads]
    r2 = [jnp.dot(q[:, hs[h]], cx_all[h].astype(BF16), preferred_element_type=F32) for h in heads]
    decay = [jnp.exp(jnp.where(causal, u_cols[:, hl[h]] + w_rows[hl[h], :], -jnp.inf)) for h in heads]
    s = [(qk_t[h] * decay[h]).astype(BF16) for h in heads]
    r1 = [jnp.dot(s[h], vext[h], preferred_element_type=F32) for h in heads]
    hh = []
    for h in heads:
        wi = jnp.broadcast_to(wi_cols[:, hl[h]], (L, Dh))
        num = r1[h][:, :Dh] + wi * r2[h][:, :Dh]
        den = r1[h][:, Dh:] + wi * r2[h][:, Dh:]
        hh.append(num / jnp.maximum(jnp.abs(den), jnp.broadcast_to(em_cols[:, hl[h]], (L, Dh))))
    msq = [jnp.dot((hh[h] * hh[h]).astype(BF16), sq_ones, preferred_element_type=F32) * (1.0 / Dh) for h in heads]
    ys = [(_sigmoid(o_pre[:, hs[h]].astype(F32)) * hh[h] * lax.rsqrt(msq[h] + EPS) * ng[:, hs[h]]).astype(BF16)
          for h in heads]
    kwt = [(kb[:, hs[h]].astype(F32).T
            * jnp.exp(b_last[:, hl[h]] + w_rows[hl[h], :] - m_loc[:, hl[h]] + log_scale)).astype(BF16) for h in heads]
    cx_loc = [jnp.dot(kwt[h], vext[h], preferred_element_type=F32) for h in heads]
    new_cx = [s_prev[:, hl[h]] * cx_all[h] + s_loc[:, hl[h]] * cx_loc[h] for h in heads]
    return ys, new_tail, new_cx, m_new


def _mlstm_kernel(qk_ref, v_ref, o_ref, gt_ref, cw_ref, cb_ref, gb_ref, ng_ref, y_ref,
                  tail_ref, c_ref, m_ref, *, L, H, Dh, BB):
    @pl.when(pl.program_id(1) == 0)
    def _():
        tail_ref[...] = jnp.zeros_like(tail_ref)
        c_ref[...] = jnp.zeros_like(c_ref)
        m_ref[...] = jnp.zeros_like(m_ref)

    tails = [tail_ref[bb] for bb in range(BB)]
    cxs = [[c_ref[bb, h] for h in range(H)] for bb in range(BB)]
    m_rows = [m_ref[bb, 0:1, :] for bb in range(BB)]
    results = []
    for bb in range(BB):
        g = gt_ref[bb] + gb_ref[...]
        results.append(_mlstm_chunk(qk_ref[bb], v_ref[bb].astype(BF16), o_ref[bb], g, tails[bb], cxs[bb],
                                    m_rows[bb], cw_ref[...], cb_ref[...], ng_ref[...], L=L, H=H, Dh=Dh))
    for bb, (ys, new_tail, new_cx, m_new) in enumerate(results):
        for h in range(H):
            y_ref[bb, :, h * Dh:(h + 1) * Dh] = ys[h]
            c_ref[bb, h] = new_cx[h]
        tail_ref[bb, 0:SUBLANES, :] = new_tail
        m_ref[bb, 0:1, :] = m_new


def _mlstm(qk_pre, v, o_pre, gates, conv_w, conv_b, gate_b_pad, norm_g):
    B, S, W2 = qk_pre.shape
    W = W2 // 2
    H, L = MLSTM_HEADS, MLSTM_CHUNK
    Dh = W // H
    BB = _pick_tile(B, MLSTM_SEQS_PER_STEP)
    return pl.pallas_call(
        functools.partial(_mlstm_kernel, L=L, H=H, Dh=Dh, BB=BB),
        out_shape=jax.ShapeDtypeStruct((B, S, W), BF16),
        grid=(B // BB, S // L),
        in_specs=[
            pl.BlockSpec((BB, L, W2), lambda b, c: (b, c, 0)),
            pl.BlockSpec((BB, L, W), lambda b, c: (b, c, 0)),
            pl.BlockSpec((BB, L, W), lambda b, c: (b, c, 0)),
            pl.BlockSpec((BB, L, 2 * LANES), lambda b, c: (b, c, 0)),
            pl.BlockSpec((CONV_WIDTH, W2), lambda b, c: (0, 0)),
            pl.BlockSpec((1, W2), lambda b, c: (0, 0)),
            pl.BlockSpec((1, 2 * LANES), lambda b, c: (0, 0)),
            pl.BlockSpec((1, W), lambda b, c: (0, 0)),
        ],
        out_specs=pl.BlockSpec((BB, L, W), lambda b, c: (b, c, 0)),
        scratch_shapes=[
            pltpu.VMEM((BB, 2 * SUBLANES, W2), F32),
            pltpu.VMEM((BB, H, Dh, 2 * Dh), F32),
            pltpu.VMEM((BB, SUBLANES, LANES), F32),
        ],
        compiler_params=_cparams(("parallel", "arbitrary")),
        name="mlstm",
    )(qk_pre, v, o_pre, gates, conv_w, conv_b.reshape(1, W2), gate_b_pad, norm_g.reshape(1, W))


def _mix_out(res, a, b, wa_ref, wb_ref):
    acc = jnp.dot(a, wa_ref[...], preferred_element_type=F32)
    return res + acc + jnp.dot(b, wb_ref[...], preferred_element_type=F32)


def _proj_ffn_kernel(res_ref, a_ref, b_ref, wa_ref, wb_ref, g_ref, wg_ref, wu_ref, wd_ref, o_ref,
                     hn_ref, acc_ref):
    f = pl.program_id(2)

    @pl.when(f == 0)
    def _():
        h = _mix_out(res_ref[0], a_ref[...], b_ref[0], wa_ref, wb_ref)
        hn_ref[...] = _rms_bf16(h, g_ref[...])
        acc_ref[...] = h

    hn = hn_ref[...]
    a = jnp.dot(hn, wg_ref[...].astype(BF16), preferred_element_type=F32)
    u = jnp.dot(hn, wu_ref[...].astype(BF16), preferred_element_type=F32)
    hmid = (a * _sigmoid(a) * u).astype(BF16)
    acc_ref[...] += jnp.dot(hmid, wd_ref[...].astype(BF16), preferred_element_type=F32)

    @pl.when(f == pl.num_programs(2) - 1)
    def _():
        o_ref[0] = acc_ref[...]


def _proj_ffn(res, ya_tm, yb, w_out, g, wg, wu, wd, tm, fc):
    B, S, D = res.shape
    wa_n = ya_tm.shape[1] // B
    wa, wb = w_out[:wa_n], w_out[wa_n:]
    F = wg.shape[1]
    return pl.pallas_call(
        _proj_ffn_kernel,
        out_shape=jax.ShapeDtypeStruct((B, S, D), F32),
        grid=(B, S // tm, F // fc),
        in_specs=[
            pl.BlockSpec((1, tm, D), lambda b, i, f: (b, i, 0)),
            pl.BlockSpec((tm, wa_n), lambda b, i, f: (i, b)),
            pl.BlockSpec((1, tm, wb.shape[0]), lambda b, i, f: (b, i, 0)),
            pl.BlockSpec(wa.shape, lambda b, i, f: (0, 0)),
            pl.BlockSpec(wb.shape, lambda b, i, f: (0, 0)),
            pl.BlockSpec((1, D), lambda b, i, f: (0, 0)),
            pl.BlockSpec((D, fc), lambda b, i, f: (0, f)),
            pl.BlockSpec((D, fc), lambda b, i, f: (0, f)),
            pl.BlockSpec((fc, D), lambda b, i, f: (f, 0)),
        ],
        out_specs=pl.BlockSpec((1, tm, D), lambda b, i, f: (b, i, 0)),
        scratch_shapes=[pltpu.VMEM((tm, D), BF16), pltpu.VMEM((tm, D), F32)],
        compiler_params=_cparams(("parallel", "parallel", "arbitrary")),
        name="proj_ffn_swiglu",
    )(res, ya_tm, yb, wa, wb, g.reshape(1, D), wg, wu, wd)


def _pick_tile(n, target):
    t = min(n, target)
    while n % t:
        t //= 2
    return t


def _mix_ab(h, norm_g, w_in, lam_re, lam_im, log_dt, b_re, b_im, c_re, c_im, d_skip, w_glu,
            conv_w, conv_b, gate_b, mlstm_norm_g):
    B, S, D = h.shape
    s5w = lam_re.shape[0] * S5_GROUP
    mw = mlstm_norm_g.shape[0]
    c_qk, c_v, c_o, c_if = s5w, s5w + 2 * mw, s5w + 3 * mw, s5w + 4 * mw
    nh = (w_in.shape[1] - c_if) // 2
    lane_pad = ((0, 0), (0, LANES - nh))
    w_pad = jnp.concatenate([w_in[:, :c_if], jnp.pad(w_in[:, c_if:c_if + nh], lane_pad),
                             jnp.pad(w_in[:, c_if + nh:], lane_pad)], axis=1).astype(BF16)
    tm = _pick_tile(S, 512)
    u_tm, qk_pre, v, o_pre, gates = _norm_proj(
        h, norm_g, w_pad,
        [(0, c_qk, BF16, True), (c_qk, c_v, BF16, False), (c_v, c_o, BF16, False),
         (c_o, c_if, BF16, False), (c_if, c_if + 2 * LANES, F32, False)], tm)
    bmat, ar, ai, cmat = _s5_params(lam_re, lam_im, log_dt, b_re, b_im, c_re, c_im)
    y_a = _s5(u_tm.reshape(S * B, s5w), B, bmat, ar, ai, cmat, d_skip, w_glu.astype(BF16),
              lc=_pick_tile(S, 32))
    gate_b_pad = jnp.concatenate([jnp.pad(gate_b[:nh], (0, LANES - nh)),
                                  jnp.pad(gate_b[nh:], (0, LANES - nh))]).reshape(1, 2 * LANES)
    y_b = _mlstm(qk_pre, v, o_pre, gates, conv_w, conv_b, gate_b_pad, mlstm_norm_g)
    return y_a.reshape(S, B * s5w), y_b


def _gmlp_kernel(u_ref, v_ref, ng_ref, nb_ref, ws_ref, bs_ref, y_ref, *, L, G, Dg, nchunk):
    u = _gelu(u_ref[0].astype(F32))
    v = _gelu(v_ref[0].astype(F32))
    mu = jnp.mean(v, axis=-1, keepdims=True)
    vc = v - mu
    var = jnp.mean(vc * vc, axis=-1, keepdims=True)
    vn = (vc * lax.rsqrt(var + EPS) * ng_ref[...] + nb_ref[...]).astype(BF16)
    for c in range(nchunk):
        rows = slice(c * L, (c + 1) * L)
        for g in range(G):
            cols = slice(g * Dg, (g + 1) * Dg)
            s = jnp.dot(ws_ref[g], vn[rows, cols], preferred_element_type=F32) + bs_ref[:, g:g + 1]
            y_ref[0, rows, cols] = (u[rows, cols] * s).astype(y_ref.dtype)


def _gmlp(u_pre, v_pre, norm_g, norm_b, w_s, b_s, tm):
    B, S, W = u_pre.shape
    G, L = GMLP_GROUPS, GMLP_CHUNK
    ws = (w_s * jnp.tril(jnp.ones((L, L), w_s.dtype))).astype(BF16)
    bs = jnp.pad(b_s.T, ((0, 0), (0, LANES - G)))
    return pl.pallas_call(
        functools.partial(_gmlp_kernel, L=L, G=G, Dg=W // G, nchunk=tm // L),
        out_shape=jax.ShapeDtypeStruct((B, S, W), BF16),
        grid=(B, S // tm),
        in_specs=[
            pl.BlockSpec((1, tm, W), lambda b, i: (b, i, 0)),
            pl.BlockSpec((1, tm, W), lambda b, i: (b, i, 0)),
            pl.BlockSpec((1, W), lambda b, i: (0, 0)),
            pl.BlockSpec((1, W), lambda b, i: (0, 0)),
            pl.BlockSpec((G, L, L), lambda b, i: (0, 0, 0)),
            pl.BlockSpec((L, LANES), lambda b, i: (0, 0)),
        ],
        out_specs=pl.BlockSpec((1, tm, W), lambda b, i: (b, i, 0)),
        compiler_params=_cparams(("parallel", "parallel")),
        name="gmlp",
    )(u_pre, v_pre, norm_g.reshape(1, W), norm_b.reshape(1, W), ws, bs)


NEG_BIG = -1e30


def _moba_kernel(q_ref, k_ref, v_ref, t0_ref, t1_ref, bfar_ref, y_ref, vext_ref, *, BL, NB, Dh, topk):
    nt = (((1,), (1,)), ((), ()))
    scale = 1.0 / math.sqrt(Dh)
    vext_ref[:, :Dh] = v_ref[0]
    vext_ref[:, Dh:] = jnp.ones((NB * BL, Dh), BF16)
    bfar = bfar_ref[0][:, 0:1]
    kmean = jnp.concatenate(
        [jnp.mean(k_ref[0, n * BL:(n + 1) * BL, :].astype(F32), axis=0, keepdims=True) for n in range(NB)],
        axis=0)
    nrow = lax.broadcasted_iota(jnp.int32, (NB, BL), 0)
    eye = (lax.broadcasted_iota(jnp.int32, (BL, BL), 0) ==
           lax.broadcasted_iota(jnp.int32, (BL, BL), 1)).astype(BF16)

    def logits(qi):
        rows = slice(qi * BL, (qi + 1) * BL)
        qf = q_ref[0, rows, :].astype(F32)
        qs = (qf * scale).astype(BF16)
        neg_cols = None
        if qi > topk:
            gate = lax.dot_general(kmean, qf, nt, preferred_element_type=F32,
                                   precision=lax.Precision.HIGHEST)
            rank = jnp.zeros((NB, BL), F32)
            for m in range(qi):
                gm = gate[m:m + 1, :]
                rank = rank + jnp.where(gm > gate, 1.0, jnp.where(gm == gate, jnp.where(nrow > m, 1.0, 0.0), 0.0))
            sel_rows = jnp.where(nrow < qi, jnp.where(rank < topk, 1.0, 0.0), 0.0)
            sel_pad = jnp.concatenate([sel_rows, jnp.zeros((LANES - NB, BL), F32)], axis=0).astype(BF16)
            sel_cols = lax.dot_general(eye, sel_pad, nt, preferred_element_type=F32)
            neg_cols = (1.0 - sel_cols) * NEG_BIG

        m_tile = None
        pieces = []
        for kb in range(qi + 1):
            s = lax.dot_general(qs, k_ref[0, kb * BL:(kb + 1) * BL, :], nt, preferred_element_type=F32)
            if kb == qi:
                s = s + t0_ref[0]
            elif kb == qi - 1:
                s = s + t1_ref[0]
                if neg_cols is not None:
                    s = s + neg_cols[:, kb:kb + 1]
            else:
                s = s + (bfar if neg_cols is None else bfar + neg_cols[:, kb:kb + 1])
            pieces.append(s)
            for c in range(BL // LANES):
                piece = s[:, c * LANES:(c + 1) * LANES]
                m_tile = piece if m_tile is None else jnp.maximum(m_tile, piece)
        return pieces, m_tile

    nxt = logits(0)
    for qi in range(NB):
        pieces, m_tile = nxt
        if qi + 1 < NB:
            nxt = logits(qi + 1)
        m = jnp.max(m_tile, axis=-1, keepdims=True)
        p = jnp.concatenate([jnp.exp(s - m).astype(BF16) for s in pieces], axis=1)
        acc = jnp.dot(p, vext_ref[:(qi + 1) * BL, :], preferred_element_type=F32)
        y_ref[0, qi * BL:(qi + 1) * BL, :] = (acc[:, :Dh] / acc[:, Dh:]).astype(y_ref.dtype)


def _rel_bucket(n):
    max_exact = REL_BUCKETS // 2
    nf = jnp.maximum(n, 1).astype(F32)
    large = max_exact + (jnp.log(nf / max_exact) / math.log(REL_MAX_DIST / max_exact)
                         * (REL_BUCKETS - max_exact)).astype(jnp.int32)
    large = jnp.minimum(large, REL_BUCKETS - 1)
    return jnp.where(n < max_exact, n, large)


def _moba(qkv, rel_bias):
    B, S, W3 = qkv.shape
    H, BL = MOBA_HEADS, MOBA_BLOCK
    W = W3 // 3
    Dh = W // H
    NB = S // BL
    assert BL + 1 >= REL_MAX_DIST and Dh == LANES and S % BL == 0
    i = jnp.arange(BL, dtype=jnp.int32)[:, None]
    j = jnp.arange(BL, dtype=jnp.int32)[None, :]
    buckets = jnp.arange(REL_BUCKETS, dtype=jnp.int32)

    def bias_tile(dist):
        onehot = (_rel_bucket(dist)[:, :, None] == buckets).astype(F32)
        return jnp.einsum('ijb,bh->hij', onehot, rel_bias, precision=lax.Precision.HIGHEST)

    t0 = jnp.where(i >= j, bias_tile(jnp.maximum(i - j, 0)), NEG_BIG)
    t1 = bias_tile(BL + i - j)
    bfar = jnp.broadcast_to(rel_bias.T[:, REL_BUCKETS - 1][:, None, None], (H, 1, LANES))
    return pl.pallas_call(
        functools.partial(_moba_kernel, BL=BL, NB=NB, Dh=Dh, topk=min(MOBA_TOPK, NB)),
        out_shape=jax.ShapeDtypeStruct((B, S, W), BF16),
        grid=(B, H),
        in_specs=[
            pl.BlockSpec((1, S, Dh), lambda b, h: (b, 0, h)),
            pl.BlockSpec((1, S, Dh), lambda b, h: (b, 0, H + h)),
            pl.BlockSpec((1, S, Dh), lambda b, h: (b, 0, 2 * H + h)),
            pl.BlockSpec((1, BL, BL), lambda b, h: (h, 0, 0)),
            pl.BlockSpec((1, BL, BL), lambda b, h: (h, 0, 0)),
            pl.BlockSpec((1, 1, LANES), lambda b, h: (h, 0, 0)),
        ],
        out_specs=pl.BlockSpec((1, S, Dh), lambda b, h: (b, 0, h)),
        scratch_shapes=[pltpu.VMEM((S, 2 * Dh), BF16)],
        compiler_params=_cparams(("parallel", "parallel")),
        name="moba",
    )(qkv, qkv, qkv, t0, t1, bfar)


def _mix_cd(h, norm_g, w_in, gmlp_norm_g, gmlp_norm_b, gmlp_w_s, gmlp_b_s, rel_bias):
    B, S, D = h.shape
    gw = gmlp_norm_g.shape[0]
    tm = _pick_tile(S, 512)
    u_pre, v_pre, qkv = _norm_proj(
        h, norm_g, w_in.astype(BF16),
        [(0, gw, BF16, False), (gw, 2 * gw, BF16, False), (2 * gw, w_in.shape[1], BF16, False)], tm)
    y_c = _gmlp(u_pre, v_pre, gmlp_norm_g, gmlp_norm_b, gmlp_w_s, gmlp_b_s, tm)
    return y_c, _moba(qkv, rel_bias)


def _router_kernel(res_ref, a_ref, b_ref, wa_ref, wb_ref, g_ref, wr_ref, rb_ref,
                   h_ref, idx_ref, gts_ref, cnt_ref, carry_ref, *, E):
    @pl.when(pl.program_id(0) == 0)
    def _():
        carry_ref[...] = jnp.zeros_like(carry_ref)

    x = _mix_out(res_ref[...], a_ref[...], b_ref[...], wa_ref, wb_ref)
    h_ref[...] = x
    tm = x.shape[0]
    var = jnp.mean(x * x, axis=-1, keepdims=True)
    hn = x * lax.rsqrt(var + EPS) * g_ref[...]
    logits = lax.dot_general(wr_ref[...], hn, (((1,), (1,)), ((), ())), preferred_element_type=F32,
                             precision=lax.Precision.HIGHEST) + rb_ref[:, 0:1]
    rowi = lax.broadcasted_iota(jnp.int32, (E, tm), 0)
    v1 = jnp.max(logits, axis=0, keepdims=True)
    e1 = jnp.min(jnp.where(logits == v1, rowi, E), axis=0, keepdims=True)
    masked = jnp.where(rowi == e1, -jnp.inf, logits)
    v2 = jnp.max(masked, axis=0, keepdims=True)
    e2 = jnp.min(jnp.where(masked == v2, rowi, E), axis=0, keepdims=True)
    ex = jnp.exp(v2 - v1)
    g1 = 1.0 / (1.0 + ex)
    g2 = ex / (1.0 + ex)
    oh1 = rowi == e1
    oh2 = rowi == e2
    cnt = jnp.where(oh1, 1.0, jnp.where(oh2, 1.0, 0.0))
    before = (lax.broadcasted_iota(jnp.int32, (tm, tm), 0) <
              lax.broadcasted_iota(jnp.int32, (tm, tm), 1)).astype(BF16)
    excl = jnp.dot(cnt.astype(BF16), before, preferred_element_type=F32) + carry_ref[:, 0:1]
    r1 = jnp.sum(jnp.where(oh1, excl, 0.0), axis=0, keepdims=True).astype(jnp.int32)
    r2 = jnp.sum(jnp.where(oh2, excl, 0.0), axis=0, keepdims=True).astype(jnp.int32)
    idx_ref[...] = jnp.where(rowi == 0, e1, jnp.where(rowi == 1, e2, jnp.where(rowi == 2, r1, jnp.where(rowi == 3, r2, 0))))
    gts_ref[...] = jnp.where(rowi == 0, g1, jnp.where(rowi == 1, g2, 0.0))
    carry_ref[...] = carry_ref[...] + jnp.sum(cnt, axis=1, keepdims=True)
    cnt_ref[...] = carry_ref[...]


def _proj_router(res2d, ya, yb, w_out, g, router_w, router_b, tm):
    T, D = res2d.shape
    E = router_w.shape[1]
    assert E == SUBLANES
    wa_n = ya.shape[1]
    wa, wb = w_out[:wa_n], w_out[wa_n:]
    return pl.pallas_call(
        functools.partial(_router_kernel, E=E),
        out_shape=[
            jax.ShapeDtypeStruct((T, D), F32),
            jax.ShapeDtypeStruct((E, T), jnp.int32),
            jax.ShapeDtypeStruct((E, T), F32),
            jax.ShapeDtypeStruct((E, LANES), F32),
        ],
        grid=(T // tm,),
        in_specs=[
            pl.BlockSpec((tm, D), lambda i: (i, 0)),
            pl.BlockSpec((tm, wa_n), lambda i: (i, 0)),
            pl.BlockSpec((tm, yb.shape[1]), lambda i: (i, 0)),
            pl.BlockSpec(wa.shape, lambda i: (0, 0)),
            pl.BlockSpec(wb.shape, lambda i: (0, 0)),
            pl.BlockSpec((1, D), lambda i: (0, 0)),
            pl.BlockSpec((E, D), lambda i: (0, 0)),
            pl.BlockSpec((E, LANES), lambda i: (0, 0)),
        ],
        out_specs=[
            pl.BlockSpec((tm, D), lambda i: (i, 0)),
            pl.BlockSpec((E, tm), lambda i: (0, i)),
            pl.BlockSpec((E, tm), lambda i: (0, i)),
            pl.BlockSpec((E, LANES), lambda i: (0, 0)),
        ],
        scratch_shapes=[pltpu.VMEM((E, LANES), F32)],
        compiler_params=_cparams(("arbitrary",)),
        name="proj_moe_router",
    )(res2d, ya, yb, wa, wb, g.reshape(1, D), router_w.T, jnp.broadcast_to(router_b[:, None], (E, LANES)))


DMA_ISSUE_UNROLL = 8


def _dispatch_kernel(pos1_ref, pos2_ref, ends_ref, h_ref, xs_hbm, zero_ref, sem, zsem, *, tm, tg, E):
    i = pl.program_id(0)

    @pl.when(i == 0)
    def _():
        zero_ref[...] = jnp.zeros_like(zero_ref)

        def zero_copy(e):
            start = ends_ref[e] - tg
            return pltpu.make_async_copy(zero_ref, xs_hbm.at[pl.ds(pl.multiple_of(start, tg), tg)], zsem)

        def nonempty(e):
            return ends_ref[e] > (ends_ref[e - 1] if e else 0)

        for e in range(E):
            @pl.when(nonempty(e))
            def _(e=e):
                zero_copy(e).start()
        for e in range(E):
            @pl.when(nonempty(e))
            def _(e=e):
                zero_copy(e).wait()

    def body(r, carry):
        t = i * tm + r
        src = h_ref.at[pl.ds(r, 1)]
        pltpu.make_async_copy(src, xs_hbm.at[pl.ds(pos1_ref[t], 1)], sem).start()
        pltpu.make_async_copy(src, xs_hbm.at[pl.ds(pos2_ref[t], 1)], sem).start()
        return carry
    lax.fori_loop(0, tm, body, 0, unroll=DMA_ISSUE_UNROLL)

    for _ in range(TOP_K):
        pltpu.make_async_copy(h_ref, xs_hbm.at[pl.ds(0, tm)], sem).wait()


def _dispatch(pos1, pos2, ends, h2d, n_rows, tm, tg):
    T, D = h2d.shape
    E = ends.shape[0]
    return pl.pallas_call(
        functools.partial(_dispatch_kernel, tm=tm, tg=tg, E=E),
        out_shape=jax.ShapeDtypeStruct((n_rows, D), h2d.dtype),
        grid_spec=pltpu.PrefetchScalarGridSpec(
            num_scalar_prefetch=3,
            grid=(T // tm,),
            in_specs=[pl.BlockSpec((tm, D), lambda i, p1, p2, en: (i, 0))],
            out_specs=pl.BlockSpec(memory_space=pl.ANY),
            scratch_shapes=[pltpu.VMEM((tg, D), h2d.dtype), pltpu.SemaphoreType.DMA, pltpu.SemaphoreType.DMA],
        ),
        compiler_params=_cparams(("arbitrary",)),
        name="moe_dispatch",
    )(pos1, pos2, ends, h2d)


def _gmm_kernel(te_ref, na_ref, x_ref, g_ref, wg_ref, wu_ref, wd_ref, o_ref, xn_ref, acc_ref):
    i = pl.program_id(0)
    f = pl.program_id(1)
    active = i < na_ref[0]

    @pl.when(f == 0)
    def _():
        acc_ref[...] = jnp.zeros_like(acc_ref)

    @pl.when(active & (f == 0))
    def _():
        xn_ref[...] = _rms_bf16(x_ref[...], g_ref[...])

    @pl.when(active)
    def _():
        x = xn_ref[...]
        a = jnp.dot(x, wg_ref[0].astype(BF16), preferred_element_type=F32)
        u = jnp.dot(x, wu_ref[0].astype(BF16), preferred_element_type=F32)
        hmid = (a * _sigmoid(a) * u).astype(BF16)
        acc_ref[...] += jnp.dot(hmid, wd_ref[0].astype(BF16), preferred_element_type=F32)

    @pl.when(f == pl.num_programs(1) - 1)
    def _():
        o_ref[...] = acc_ref[...]


def _gmm(tile_expert, n_active, xs, g, wg, wu, wd, tm, fc):
    R, D = xs.shape
    F = wg.shape[2]
    nf = F // fc

    def fsel(i, f, na):
        return jnp.where(i < na[0], f, nf - 1)

    return pl.pallas_call(
        _gmm_kernel,
        out_shape=jax.ShapeDtypeStruct((R, D), F32),
        grid_spec=pltpu.PrefetchScalarGridSpec(
            num_scalar_prefetch=2,
            grid=(R // tm, nf),
            in_specs=[
                pl.BlockSpec((tm, D), lambda i, f, te, na: (i, 0)),
                pl.BlockSpec((1, D), lambda i, f, te, na: (0, 0)),
                pl.BlockSpec((1, D, fc), lambda i, f, te, na: (te[i], 0, fsel(i, f, na))),
                pl.BlockSpec((1, D, fc), lambda i, f, te, na: (te[i], 0, fsel(i, f, na))),
                pl.BlockSpec((1, fc, D), lambda i, f, te, na: (te[i], fsel(i, f, na), 0)),
            ],
            out_specs=pl.BlockSpec((tm, D), lambda i, f, te, na: (i, 0)),
            scratch_shapes=[pltpu.VMEM((tm, D), BF16), pltpu.VMEM((tm, D), F32)],
        ),
        compiler_params=_cparams(("parallel", "arbitrary")),
        name="moe_gmm",
    )(tile_expert, n_active, xs, g.reshape(1, D), wg, wu, wd)


def _combine_copies(pos1_ref, pos2_ref, ys_hbm, ya_ref, yb_ref, sems, tile, slot, tm):
    def row_copies(r):
        t = tile * tm + r
        return (pltpu.make_async_copy(ys_hbm.at[pl.ds(pos1_ref[t], 1)], ya_ref.at[slot, pl.ds(r, 1)], sems.at[slot]),
                pltpu.make_async_copy(ys_hbm.at[pl.ds(pos2_ref[t], 1)], yb_ref.at[slot, pl.ds(r, 1)], sems.at[slot]))
    return row_copies


def _combine_kernel(pos1_ref, pos2_ref, h_ref, ga_ref, gb_ref, g_ref, ys_hbm, o_ref, ya_ref, yb_ref, sems, *, tm):
    i = pl.program_id(0)
    n = pl.num_programs(0)
    slot = lax.rem(i, 2)

    def start_tile(tile, slot):
        copies = _combine_copies(pos1_ref, pos2_ref, ys_hbm, ya_ref, yb_ref, sems, tile, slot, tm)

        def body(r, carry):
            a, b = copies(r)
            a.start()
            b.start()
            return carry
        lax.fori_loop(0, tm, body, 0, unroll=DMA_ISSUE_UNROLL)

    @pl.when(i == 0)
    def _():
        start_tile(0, 0)

    @pl.when(i + 1 < n)
    def _():
        start_tile(i + 1, 1 - slot)

    pltpu.make_async_copy(ys_hbm.at[pl.ds(0, tm)], ya_ref.at[slot], sems.at[slot]).wait()
    pltpu.make_async_copy(ys_hbm.at[pl.ds(0, tm)], yb_ref.at[slot], sems.at[slot]).wait()

    h = h_ref[...] + ga_ref[...] * ya_ref[slot] + gb_ref[...] * yb_ref[slot]
    var = jnp.mean(h * h, axis=-1, keepdims=True)
    o_ref[...] = h * lax.rsqrt(var + EPS) * g_ref[...]


def _combine(pos1, pos2, h2d, ga, gb, g, ys, tm):
    T, D = h2d.shape
    row = pl.BlockSpec((tm, D), lambda i, p1, p2: (i, 0))
    colv = pl.BlockSpec((tm, 1), lambda i, p1, p2: (i, 0))
    return pl.pallas_call(
        functools.partial(_combine_kernel, tm=tm),
        out_shape=jax.ShapeDtypeStruct((T, D), F32),
        grid_spec=pltpu.PrefetchScalarGridSpec(
            num_scalar_prefetch=2,
            grid=(T // tm,),
            in_specs=[row, colv, colv, pl.BlockSpec((1, D), lambda i, p1, p2: (0, 0)),
                      pl.BlockSpec(memory_space=pl.ANY)],
            out_specs=row,
            scratch_shapes=[pltpu.VMEM((2, tm, D), F32), pltpu.VMEM((2, tm, D), F32),
                            pltpu.SemaphoreType.DMA((2,))],
        ),
        compiler_params=_cparams(("arbitrary",)),
        name="moe_combine_norm",
    )(pos1, pos2, h2d, ga, gb, g.reshape(1, D), ys)


def _moe_final(res, y_c, y_d, w_out, norm_g, final_g, router_w, router_b, w_gate, w_up, w_down):
    B, S, D = res.shape
    T = B * S
    E = router_w.shape[1]
    h2d, idx, gts, cnt = _proj_router(res.reshape(T, D), y_c.reshape(T, -1), y_d.reshape(T, -1),
                                      w_out.astype(BF16), norm_g, router_w, router_b, _pick_tile(T, 512))
    e1, e2, r1, r2 = idx[0], idx[1], idx[2], idx[3]
    tm = _pick_tile(T, 1024)
    counts = cnt[:, 0].astype(jnp.int32)
    padded = ((counts + tm - 1) // tm) * tm
    ends = jnp.cumsum(padded)
    offs = ends - padded
    eids = jnp.arange(E, dtype=jnp.int32)[:, None]
    pos1 = jnp.sum(jnp.where(e1[None, :] == eids, offs[:, None], 0), axis=0) + r1
    pos2 = jnp.sum(jnp.where(e2[None, :] == eids, offs[:, None], 0), axis=0) + r2
    n_tiles = (TOP_K * T) // tm + E
    tile_start = jnp.arange(n_tiles, dtype=jnp.int32) * tm
    tile_expert = jnp.minimum(jnp.searchsorted(ends, tile_start, side='right'), E - 1).astype(jnp.int32)
    n_active = (ends[-1] // tm).astype(jnp.int32).reshape(1)
    xs = _dispatch(pos1, pos2, ends.astype(jnp.int32), h2d, n_tiles * tm, _pick_tile(T, 1024), tm)
    ys = _gmm(tile_expert, n_active, xs, norm_g, w_gate, w_up, w_down, tm, _pick_tile(w_gate.shape[2], 512))
    out = _combine(pos1, pos2, h2d, gts[0][:, None], gts[1][:, None], final_g, ys, _pick_tile(T, 256))
    return out.reshape(B, S, D)


def kernel(x, norm_mix_g, norm_ffn_g, norm_final_g, ab_w_in, s5_lambda_re, s5_lambda_im, s5_log_dt, s5_b_re, s5_b_im, s5_c_re, s5_c_im, s5_d, s5_w_glu, mlstm_conv_w, mlstm_conv_b, mlstm_gate_b, mlstm_norm_g, ab_w_out, ffn_w_gate, ffn_w_up, ffn_w_down, cd_w_in, gmlp_norm_g, gmlp_norm_b, gmlp_w_s, gmlp_b_s, rel_bias, cd_w_out, moe_router_w, moe_router_b, moe_w_gate, moe_w_up, moe_w_down):
    B, S, D = x.shape
    y_a, y_b = _mix_ab(x, norm_mix_g[0], ab_w_in[0], s5_lambda_re[0], s5_lambda_im[0], s5_log_dt[0],
                       s5_b_re[0], s5_b_im[0], s5_c_re[0], s5_c_im[0], s5_d[0], s5_w_glu[0],
                       mlstm_conv_w[0], mlstm_conv_b[0], mlstm_gate_b[0], mlstm_norm_g[0])
    h = _proj_ffn(x, y_a, y_b, ab_w_out[0].astype(BF16), norm_ffn_g[0], ffn_w_gate[0], ffn_w_up[0],
                  ffn_w_down[0], _pick_tile(S, 1024), _pick_tile(ffn_w_gate.shape[2], 256))
    y_c, y_d = _mix_cd(h, norm_mix_g[1], cd_w_in[0], gmlp_norm_g[0], gmlp_norm_b[0],
                       gmlp_w_s[0], gmlp_b_s[0], rel_bias)
    return _moe_final(h, y_c, y_d, cd_w_out[0], norm_ffn_g[1], norm_final_g, moe_router_w[0], moe_router_b[0],
                      moe_w_gate[0], moe_w_up[0], moe_w_down[0])
```

```python
import functools
import math

import jax
import jax.numpy as jnp
from jax import lax
from jax.experimental import pallas as pl
from jax.experimental.pallas import tpu as pltpu

F32 = jnp.float32
BF16 = jnp.bfloat16
EPS = 1e-5

LANES = 128
SUBLANES = 8
VMEM_LIMIT_BYTES = 56 * 1024 * 1024

S5_GROUP = 16
S5_STATE = 64
S5_SLAB_GROUPS = 8
MLSTM_HEADS = 4
MLSTM_CHUNK = 128
MLSTM_SEQS_PER_STEP = 2
CONV_WIDTH = 4
GMLP_GROUPS = 4
GMLP_CHUNK = 128
MOBA_HEADS = 4
MOBA_BLOCK = 256
MOBA_TOPK = 3
REL_BUCKETS = 32
REL_MAX_DIST = 128
N_EXPERTS = 8
TOP_K = 2


def _cparams(sem):
    return pltpu.CompilerParams(dimension_semantics=sem, vmem_limit_bytes=VMEM_LIMIT_BYTES)


def _rms_bf16(x, g):
    var = jnp.mean(x * x, axis=-1, keepdims=True)
    return (x * lax.rsqrt(var + EPS) * g).astype(BF16)


def _gelu(x):
    return jax.nn.gelu(x, approximate=True)


def _sigmoid(x):
    return 1.0 / (1.0 + jnp.exp(-x))


def _norm_proj_kernel(x_ref, g_ref, w_ref, *out_refs, splits):
    hn = _rms_bf16(x_ref[0], g_ref[...])
    for o_ref, (c0, c1) in zip(out_refs, splits):
        r = jnp.dot(hn, w_ref[:, c0:c1], preferred_element_type=F32)
        o_ref[...] = r.reshape(o_ref.shape).astype(o_ref.dtype)


def _norm_proj(x, g, w, outs, tm):
    B, S, D = x.shape
    splits = tuple((c0, c1) for c0, c1, _ in outs)
    out_shape, out_specs = [], []
    for c0, c1, dt in outs:
        n = c1 - c0
        out_shape.append(jax.ShapeDtypeStruct((B, S, n), dt))
        out_specs.append(pl.BlockSpec((1, tm, n), lambda b, i: (b, i, 0)))
    return pl.pallas_call(
        functools.partial(_norm_proj_kernel, splits=splits),
        out_shape=out_shape,
        grid=(B, S // tm),
        in_specs=[
            pl.BlockSpec((1, tm, D), lambda b, i: (b, i, 0)),
            pl.BlockSpec((1, D), lambda b, i: (0, 0)),
            pl.BlockSpec(w.shape, lambda b, i: (0, 0)),
        ],
        out_specs=out_specs,
        compiler_params=_cparams(("parallel", "parallel")),
        name="norm_proj",
    )(x, g.reshape(1, D), w)


def _s5_kernel(u_hbm, bmat_ref, ar_ref, ai_ref, cmat_ref, d_ref, wglu_ref, y_hbm,
               xr_ref, xi_ref, ubuf, ybuf, in_sems, out_sems, *bufs, lc, nb, nslab, sw):
    i = pl.program_id(0)
    n = pl.num_programs(0)
    slot = lax.rem(i, 2)
    W = ubuf.shape[-1]

    def in_copies(step, slot):
        t0 = pl.multiple_of(step * lc, lc)
        return [pltpu.make_async_copy(u_hbm.at[b, pl.ds(t0, lc), :], ubuf.at[slot, :, b, :], in_sems.at[slot])
                for b in range(nb)]

    def out_copies(step, slot):
        t0 = pl.multiple_of(step * lc, lc)
        return [pltpu.make_async_copy(ybuf.at[slot, :, b, :], y_hbm.at[b, pl.ds(t0, lc), :], out_sems.at[slot])
                for b in range(nb)]

    @pl.when(i == 0)
    def _():
        xr_ref[...] = jnp.zeros_like(xr_ref)
        xi_ref[...] = jnp.zeros_like(xi_ref)
        for c in in_copies(0, 0):
            c.start()

    @pl.when(i + 1 < n)
    def _():
        for c in in_copies(i + 1, 1 - slot):
            c.start()

    for c in in_copies(i, slot):
        c.wait()

    uf = ubuf[slot].reshape(lc * nb, W)
    u = uf.astype(BF16)
    xr_all = xr_ref[...]
    xi_all = xi_ref[...]

    def drive(k):
        bufs[k][...] = jnp.dot(u[:, LANES * k:LANES * (k + 1)], bmat_ref[k], preferred_element_type=F32)

    drive(0)
    ys, new_xr, new_xi = [], [], []
    for k in range(nslab):
        if k + 1 < nslab:
            drive(k + 1)
        st_cols = slice(sw * k, sw * (k + 1))
        ar = jnp.broadcast_to(ar_ref[:, st_cols], (nb, sw))
        ai = jnp.broadcast_to(ai_ref[:, st_cols], (nb, sw))
        xr, xi = xr_all[:, st_cols], xi_all[:, st_cols]
        for t in range(lc):
            rows = slice(t * nb, (t + 1) * nb)
            nxr = ar * xr - ai * xi + bufs[k][rows, :sw]
            nxi = ar * xi + ai * xr + bufs[k][rows, sw:]
            bufs[k][rows, :sw] = nxr
            bufs[k][rows, sw:] = nxi
            xr, xi = nxr, nxi
        new_xr.append(xr)
        new_xi.append(xi)
        ys.append(jnp.dot(bufs[k][...].astype(BF16), cmat_ref[k], preferred_element_type=F32))
    xr_ref[...] = jnp.concatenate(new_xr, axis=-1)
    xi_ref[...] = jnp.concatenate(new_xi, axis=-1)
    y = jnp.concatenate(ys, axis=-1)
    y = _gelu(y + d_ref[...] * uf)
    gl = jnp.dot(y.astype(BF16), wglu_ref[...], preferred_element_type=F32)

    @pl.when(i >= 2)
    def _():
        for c in out_copies(i - 2, slot):
            c.wait()

    ybuf[slot] = (y * _sigmoid(gl)).reshape(lc, nb, W)
    for c in out_copies(i, slot):
        c.start()

    @pl.when(i == n - 1)
    def _():
        for c in out_copies(i, slot):
            c.wait()

        @pl.when(n >= 2)
        def _():
            for c in out_copies(i - 1, 1 - slot):
                c.wait()


def _s5_params(lam_re, lam_im, log_dt, b_re, b_im, c_re, c_im):
    G, P = lam_re.shape
    Hc = b_re.shape[-1]
    dt = jnp.exp(log_dt.astype(F32))[:, None]
    mag = jnp.exp(lam_re * dt)
    ar = mag * jnp.cos(lam_im * dt)
    ai = mag * jnp.sin(lam_im * dt)
    den = lam_re * lam_re + lam_im * lam_im
    cr = ((ar - 1.0) * lam_re + ai * lam_im) / den
    ci = (ai * lam_re - (ar - 1.0) * lam_im) / den
    bb_re = cr[..., None] * b_re - ci[..., None] * b_im
    bb_im = cr[..., None] * b_im + ci[..., None] * b_re
    gs = S5_SLAB_GROUPS
    nslab = G // gs
    eye = jnp.eye(gs, dtype=F32)

    def bd_in(b):
        b = b.reshape(nslab, gs, P, Hc)
        return jnp.einsum('kgph,gj->kghjp', b, eye).reshape(nslab, gs * Hc, gs * P)

    def bd_out(c):
        c = c.reshape(nslab, gs, Hc, P)
        return jnp.einsum('kghp,gj->kgpjh', c, eye).reshape(nslab, gs * P, gs * Hc)

    bmat = jnp.concatenate([bd_in(bb_re), bd_in(bb_im)], axis=-1).astype(BF16)
    cmat = jnp.concatenate([bd_out(c_re), -bd_out(c_im)], axis=1).astype(BF16)
    return bmat, ar.reshape(1, G * P), ai.reshape(1, G * P), cmat


def _s5(u, bmat, ar, ai, cmat, d_skip, w_glu, lc):
    nb, S, W = u.shape
    assert nb % SUBLANES == 0
    nslab = bmat.shape[0]
    sw = bmat.shape[2] // 2
    rows = lc * nb
    return pl.pallas_call(
        functools.partial(_s5_kernel, lc=lc, nb=nb, nslab=nslab, sw=sw),
        out_shape=jax.ShapeDtypeStruct((nb, S, W), F32),
        grid=(S // lc,),
        in_specs=[
            pl.BlockSpec(memory_space=pl.ANY),
            pl.BlockSpec(bmat.shape, lambda i: (0, 0, 0)),
            pl.BlockSpec(ar.shape, lambda i: (0, 0)),
            pl.BlockSpec(ai.shape, lambda i: (0, 0)),
            pl.BlockSpec(cmat.shape, lambda i: (0, 0, 0)),
            pl.BlockSpec((1, W), lambda i: (0, 0)),
            pl.BlockSpec(w_glu.shape, lambda i: (0, 0)),
        ],
        out_specs=pl.BlockSpec(memory_space=pl.ANY),
        scratch_shapes=[pltpu.VMEM((nb, sw * nslab), F32), pltpu.VMEM((nb, sw * nslab), F32),
                        pltpu.VMEM((2, lc, nb, W), F32), pltpu.VMEM((2, lc, nb, W), F32),
                        pltpu.SemaphoreType.DMA((2,)), pltpu.SemaphoreType.DMA((2,))]
        + [pltpu.VMEM((rows, 2 * sw), F32) for _ in range(nslab)],
        compiler_params=_cparams(("arbitrary",)),
        name="s5",
    )(u, bmat, ar, ai, cmat, d_skip.reshape(1, W), w_glu)


def _mlstm_chunk(x, vb, o_pre, g, tail, cx_all, m_row, cw, cb, ng, *, L, H, Dh):
    W = H * Dh
    halo = SUBLANES
    log_scale = -0.5 * math.log(Dh)

    row = lax.broadcasted_iota(jnp.int32, (L, L), 0)
    col = lax.broadcasted_iota(jnp.int32, (L, L), 1)
    conv = cb + cw[0:1, :] * x.astype(F32)
    corr = jnp.zeros((halo, 2 * W), F32)
    for j in range(1, CONV_WIDTH):
        shift = (row - col == j).astype(BF16)
        conv = conv + cw[j:j + 1, :] * jnp.dot(shift, x, preferred_element_type=F32)
        corr = corr + cw[j:j + 1, :] * tail[halo - j:2 * halo - j, :]
    conv = jnp.concatenate([conv[:halo] + corr, conv[halo:]], axis=0)
    new_tail = x[L - halo:, :].astype(F32)
    qk = (conv * _sigmoid(conv)).astype(BF16)
    q = qk[:, :W]
    kb = qk[:, W:]
    ones = jnp.ones((L, Dh), BF16)
    sq_ones = jnp.ones((Dh, Dh), BF16)

    log_i = g[:, :LANES]
    f_pre = g[:, LANES:]
    logf = jnp.minimum(f_pre, 0.0) - jnp.log(1.0 + jnp.exp(-jnp.abs(f_pre)))
    causal = row >= col
    bcum = jnp.dot(causal.astype(F32), logf, preferred_element_type=F32, precision=lax.Precision.HIGHEST)
    w_cols = log_i - bcum
    trow = lax.broadcasted_iota(jnp.int32, (L, LANES), 0)
    cmax = w_cols
    k = 1
    while k < L:
        cmax = jnp.maximum(cmax, jnp.where(trow >= k, pltpu.roll(cmax, k, axis=0), -jnp.inf))
        k *= 2
    m_inter = bcum + m_row
    m_t = jnp.maximum(bcum + cmax, m_inter)
    u_cols = bcum - m_t + log_scale
    wi_cols = jnp.exp(m_inter - m_t)
    em_cols = jnp.exp(-m_t)
    b_last = bcum[L - 1:L, :]
    m_loc = b_last + cmax[L - 1:L, :]
    m_new = jnp.maximum(b_last + m_row, m_loc)
    s_prev = jnp.exp(b_last + m_row - m_new)
    s_loc = jnp.exp(m_loc - m_new)
    w_rows = w_cols.T

    heads = range(H)
    hs = [slice(h * Dh, (h + 1) * Dh) for h in heads]
    hl = [slice(h, h + 1) for h in heads]
    nt = (((1,), (1,)), ((), ()))
    vext = [jnp.concatenate([vb[:, hs[h]], ones], axis=1) for h in heads]
    qk_t = [lax.dot_general(q[:, hs[h]], kb[:, hs[h]], nt, preferred_element_type=F32) for h in heads]
    r2 = [jnp.dot(q[:, hs[h]], cx_all[h].astype(BF16), preferred_element_type=F32) for h in heads]
    decay = [jnp.exp(jnp.where(causal, u_cols[:, hl[h]] + w_rows[hl[h], :], -jnp.inf)) for h in heads]
    s = [(qk_t[h] * decay[h]).astype(BF16) for h in heads]
    r1 = [jnp.dot(s[h], vext[h], preferred_element_type=F32) for h in heads]
    hh = []
    for h in heads:
        wi = jnp.broadcast_to(wi_cols[:, hl[h]], (L, Dh))
        num = r1[h][:, :Dh] + wi * r2[h][:, :Dh]
        den = r1[h][:, Dh:] + wi * r2[h][:, Dh:]
        hh.append(num / jnp.maximum(jnp.abs(den), jnp.broadcast_to(em_cols[:, hl[h]], (L, Dh))))
    msq = [jnp.dot((hh[h] * hh[h]).astype(BF16), sq_ones, preferred_element_type=F32) * (1.0 / Dh) for h in heads]
    ys = [(_sigmoid(o_pre[:, hs[h]].astype(F32)) * hh[h] * lax.rsqrt(msq[h] + EPS) * ng[:, hs[h]]).astype(BF16)
          for h in heads]
    kwt = [(kb[:, hs[h]].astype(F32).T
            * jnp.exp(b_last[:, hl[h]] + w_rows[hl[h], :] - m_loc[:, hl[h]] + log_scale)).astype(BF16) for h in heads]
    cx_loc = [jnp.dot(kwt[h], vext[h], preferred_element_type=F32) for h in heads]
    new_cx = [s_prev[:, hl[h]] * cx_all[h] + s_loc[:, hl[h]] * cx_loc[h] for h in heads]
    return ys, new_tail, new_cx, m_new


def _mlstm_kernel(qk_ref, v_ref, o_ref, gt_ref, cw_ref, cb_ref, gb_ref, ng_ref, y_ref,
                  tail_ref, c_ref, m_ref, *, L, H, Dh, BB):
    @pl.when(pl.program_id(1) == 0)
    def _():
        tail_ref[...] = jnp.zeros_like(tail_ref)
        c_ref[...] = jnp.zeros_like(c_ref)
        m_ref[...] = jnp.zeros_like(m_ref)

    tails = [tail_ref[bb] for bb in range(BB)]
    cxs = [[c_ref[bb, h] for h in range(H)] for bb in range(BB)]
    m_rows = [m_ref[bb, 0:1, :] for bb in range(BB)]
    results = []
    for bb in range(BB):
        g = gt_ref[bb] + gb_ref[...]
        results.append(_mlstm_chunk(qk_ref[bb], v_ref[bb].astype(BF16), o_ref[bb], g, tails[bb], cxs[bb],
                                    m_rows[bb], cw_ref[...], cb_ref[...], ng_ref[...], L=L, H=H, Dh=Dh))
    for bb, (ys, new_tail, new_cx, m_new) in enumerate(results):
        for h in range(H):
            y_ref[bb, :, h * Dh:(h + 1) * Dh] = ys[h]
            c_ref[bb, h] = new_cx[h]
        tail_ref[bb, 0:SUBLANES, :] = new_tail
        m_ref[bb, 0:1, :] = m_new


def _mlstm(qk_pre, v, o_pre, gates, conv_w, conv_b, gate_b_pad, norm_g):
    B, S, W2 = qk_pre.shape
    W = W2 // 2
    H, L = MLSTM_HEADS, MLSTM_CHUNK
    Dh = W // H
    BB = _pick_tile(B, MLSTM_SEQS_PER_STEP)
    return pl.pallas_call(
        functools.partial(_mlstm_kernel, L=L, H=H, Dh=Dh, BB=BB),
        out_shape=jax.ShapeDtypeStruct((B, S, W), BF16),
        grid=(B // BB, S // L),
        in_specs=[
            pl.BlockSpec((BB, L, W2), lambda b, c: (b, c, 0)),
            pl.BlockSpec((BB, L, W), lambda b, c: (b, c, 0)),
            pl.BlockSpec((BB, L, W), lambda b, c: (b, c, 0)),
            pl.BlockSpec((BB, L, 2 * LANES), lambda b, c: (b, c, 0)),
            pl.BlockSpec((CONV_WIDTH, W2), lambda b, c: (0, 0)),
            pl.BlockSpec((1, W2), lambda b, c: (0, 0)),
            pl.BlockSpec((1, 2 * LANES), lambda b, c: (0, 0)),
            pl.BlockSpec((1, W), lambda b, c: (0, 0)),
        ],
        out_specs=pl.BlockSpec((BB, L, W), lambda b, c: (b, c, 0)),
        scratch_shapes=[
            pltpu.VMEM((BB, 2 * SUBLANES, W2), F32),
            pltpu.VMEM((BB, H, Dh, 2 * Dh), F32),
            pltpu.VMEM((BB, SUBLANES, LANES), F32),
        ],
        compiler_params=_cparams(("parallel", "arbitrary")),
        name="mlstm",
    )(qk_pre, v, o_pre, gates, conv_w, conv_b.reshape(1, W2), gate_b_pad, norm_g.reshape(1, W))


def _mix_out(res, a, b, wa_ref, wb_ref):
    acc = jnp.dot(a.astype(BF16), wa_ref[...], preferred_element_type=F32)
    return res + acc + jnp.dot(b.astype(BF16), wb_ref[...], preferred_element_type=F32)


def _proj_ffn_kernel(res_ref, a_ref, b_ref, wa_ref, wb_ref, g_ref, wg_ref, wu_ref, wd_ref, o_ref,
                     hn_ref, acc_ref):
    f = pl.program_id(2)

    @pl.when(f == 0)
    def _():
        h = _mix_out(res_ref[0], a_ref[0], b_ref[0], wa_ref, wb_ref)
        hn_ref[...] = _rms_bf16(h, g_ref[...])
        acc_ref[...] = h

    hn = hn_ref[...]
    a = jnp.dot(hn, wg_ref[...], preferred_element_type=F32)
    u = jnp.dot(hn, wu_ref[...], preferred_element_type=F32)
    hmid = (a * _sigmoid(a) * u).astype(BF16)
    acc_ref[...] += jnp.dot(hmid, wd_ref[...], preferred_element_type=F32)

    @pl.when(f == pl.num_programs(2) - 1)
    def _():
        o_ref[0] = acc_ref[...]


def _proj_ffn(res, ya, yb, w_out, g, wg, wu, wd, tm, fc):
    B, S, D = res.shape
    wa_n = ya.shape[2]
    wa, wb = w_out[:wa_n], w_out[wa_n:]
    F = wg.shape[1]
    return pl.pallas_call(
        _proj_ffn_kernel,
        out_shape=jax.ShapeDtypeStruct((B, S, D), F32),
        grid=(B, S // tm, F // fc),
        in_specs=[
            pl.BlockSpec((1, tm, D), lambda b, i, f: (b, i, 0)),
            pl.BlockSpec((1, tm, wa_n), lambda b, i, f: (b, i, 0)),
            pl.BlockSpec((1, tm, wb.shape[0]), lambda b, i, f: (b, i, 0)),
            pl.BlockSpec(wa.shape, lambda b, i, f: (0, 0)),
            pl.BlockSpec(wb.shape, lambda b, i, f: (0, 0)),
            pl.BlockSpec((1, D), lambda b, i, f: (0, 0)),
            pl.BlockSpec((D, fc), lambda b, i, f: (0, f)),
            pl.BlockSpec((D, fc), lambda b, i, f: (0, f)),
            pl.BlockSpec((fc, D), lambda b, i, f: (f, 0)),
        ],
        out_specs=pl.BlockSpec((1, tm, D), lambda b, i, f: (b, i, 0)),
        scratch_shapes=[pltpu.VMEM((tm, D), BF16), pltpu.VMEM((tm, D), F32)],
        compiler_params=_cparams(("parallel", "parallel", "arbitrary")),
        name="proj_ffn_swiglu",
    )(res, ya, yb, wa, wb, g.reshape(1, D), wg, wu, wd)


def _pick_tile(n, target):
    t = min(n, target)
    while n % t:
        t //= 2
    return t


def _mix_ab(h, norm_g, w_in, lam_re, lam_im, log_dt, b_re, b_im, c_re, c_im, d_skip, w_glu,
            conv_w, conv_b, gate_b, mlstm_norm_g):
    B, S, D = h.shape
    s5w = lam_re.shape[0] * S5_GROUP
    mw = mlstm_norm_g.shape[0]
    c_qk, c_v, c_o, c_if = s5w, s5w + 2 * mw, s5w + 3 * mw, s5w + 4 * mw
    nh = (w_in.shape[1] - c_if) // 2
    lane_pad = ((0, 0), (0, LANES - nh))
    w_pad = jnp.concatenate([w_in[:, :c_if], jnp.pad(w_in[:, c_if:c_if + nh], lane_pad),
                             jnp.pad(w_in[:, c_if + nh:], lane_pad)], axis=1).astype(BF16)
    tm = _pick_tile(S, 512)
    u_s5, qk_pre, v, o_pre, gates = _norm_proj(
        h, norm_g, w_pad,
        [(0, c_qk, F32), (c_qk, c_v, BF16), (c_v, c_o, BF16), (c_o, c_if, BF16), (c_if, c_if + 2 * LANES, F32)],
        tm)
    bmat, ar, ai, cmat = _s5_params(lam_re, lam_im, log_dt, b_re, b_im, c_re, c_im)
    y_a = _s5(u_s5, bmat, ar, ai, cmat, d_skip, w_glu.astype(BF16), lc=_pick_tile(S, 32))
    gate_b_pad = jnp.concatenate([jnp.pad(gate_b[:nh], (0, LANES - nh)),
                                  jnp.pad(gate_b[nh:], (0, LANES - nh))]).reshape(1, 2 * LANES)
    y_b = _mlstm(qk_pre, v, o_pre, gates, conv_w, conv_b, gate_b_pad, mlstm_norm_g)
    return y_a, y_b


def _gmlp_kernel(u_ref, v_ref, ng_ref, nb_ref, ws_ref, bs_ref, y_ref, *, L, G, Dg, nchunk):
    u = _gelu(u_ref[0].astype(F32))
    v = _gelu(v_ref[0].astype(F32))
    mu = jnp.mean(v, axis=-1, keepdims=True)
    vc = v - mu
    var = jnp.mean(vc * vc, axis=-1, keepdims=True)
    vn = (vc * lax.rsqrt(var + EPS) * ng_ref[...] + nb_ref[...]).astype(BF16)
    for c in range(nchunk):
        rows = slice(c * L, (c + 1) * L)
        for g in range(G):
            cols = slice(g * Dg, (g + 1) * Dg)
            s = jnp.dot(ws_ref[g], vn[rows, cols], preferred_element_type=F32) + bs_ref[:, g:g + 1]
            y_ref[0, rows, cols] = (u[rows, cols] * s).astype(y_ref.dtype)


def _gmlp(u_pre, v_pre, norm_g, norm_b, w_s, b_s, tm):
    B, S, W = u_pre.shape
    G, L = GMLP_GROUPS, GMLP_CHUNK
    ws = (w_s * jnp.tril(jnp.ones((L, L), w_s.dtype))).astype(BF16)
    bs = jnp.pad(b_s.T, ((0, 0), (0, LANES - G)))
    return pl.pallas_call(
        functools.partial(_gmlp_kernel, L=L, G=G, Dg=W // G, nchunk=tm // L),
        out_shape=jax.ShapeDtypeStruct((B, S, W), BF16),
        grid=(B, S // tm),
        in_specs=[
            pl.BlockSpec((1, tm, W), lambda b, i: (b, i, 0)),
            pl.BlockSpec((1, tm, W), lambda b, i: (b, i, 0)),
            pl.BlockSpec((1, W), lambda b, i: (0, 0)),
            pl.BlockSpec((1, W), lambda b, i: (0, 0)),
            pl.BlockSpec((G, L, L), lambda b, i: (0, 0, 0)),
            pl.BlockSpec((L, LANES), lambda b, i: (0, 0)),
        ],
        out_specs=pl.BlockSpec((1, tm, W), lambda b, i: (b, i, 0)),
        compiler_params=_cparams(("parallel", "parallel")),
        name="gmlp",
    )(u_pre, v_pre, norm_g.reshape(1, W), norm_b.reshape(1, W), ws, bs)


NEG_BIG = -1e30


def _moba_kernel(q_ref, k_ref, v_ref, t0_ref, t1_ref, bfar_ref, y_ref, vext_ref, *, BL, NB, Dh, topk):
    nt = (((1,), (1,)), ((), ()))
    scale = 1.0 / math.sqrt(Dh)
    vext_ref[:, :Dh] = v_ref[0]
    vext_ref[:, Dh:] = jnp.ones((NB * BL, Dh), BF16)
    bfar = bfar_ref[0][:, 0:1]
    kmean = jnp.concatenate(
        [jnp.mean(k_ref[0, n * BL:(n + 1) * BL, :].astype(F32), axis=0, keepdims=True) for n in range(NB)],
        axis=0)
    nrow = lax.broadcasted_iota(jnp.int32, (NB, BL), 0)
    eye = (lax.broadcasted_iota(jnp.int32, (BL, BL), 0) ==
           lax.broadcasted_iota(jnp.int32, (BL, BL), 1)).astype(BF16)

    def logits(qi):
        rows = slice(qi * BL, (qi + 1) * BL)
        qf = q_ref[0, rows, :].astype(F32)
        qs = (qf * scale).astype(BF16)
        neg_cols = None
        if qi > topk:
            gate = lax.dot_general(kmean, qf, nt, preferred_element_type=F32,
                                   precision=lax.Precision.HIGHEST)
            rank = jnp.zeros((NB, BL), F32)
            for m in range(qi):
                gm = gate[m:m + 1, :]
                rank = rank + jnp.where(gm > gate, 1.0, jnp.where(gm == gate, jnp.where(nrow > m, 1.0, 0.0), 0.0))
            sel_rows = jnp.where(nrow < qi, jnp.where(rank < topk, 1.0, 0.0), 0.0)
            sel_pad = jnp.concatenate([sel_rows, jnp.zeros((LANES - NB, BL), F32)], axis=0).astype(BF16)
            sel_cols = lax.dot_general(eye, sel_pad, nt, preferred_element_type=F32)
            neg_cols = (1.0 - sel_cols) * NEG_BIG

        m_tile = None
        pieces = []
        for kb in range(qi + 1):
            s = lax.dot_general(qs, k_ref[0, kb * BL:(kb + 1) * BL, :], nt, preferred_element_type=F32)
            if kb == qi:
                s = s + t0_ref[0]
            elif kb == qi - 1:
                s = s + t1_ref[0]
                if neg_cols is not None:
                    s = s + neg_cols[:, kb:kb + 1]
            else:
                s = s + (bfar if neg_cols is None else bfar + neg_cols[:, kb:kb + 1])
            pieces.append(s)
            for c in range(BL // LANES):
                piece = s[:, c * LANES:(c + 1) * LANES]
                m_tile = piece if m_tile is None else jnp.maximum(m_tile, piece)
        return pieces, m_tile

    nxt = logits(0)
    for qi in range(NB):
        pieces, m_tile = nxt
        if qi + 1 < NB:
            nxt = logits(qi + 1)
        m = jnp.max(m_tile, axis=-1, keepdims=True)
        p = jnp.concatenate([jnp.exp(s - m).astype(BF16) for s in pieces], axis=1)
        acc = jnp.dot(p, vext_ref[:(qi + 1) * BL, :], preferred_element_type=F32)
        y_ref[0, qi * BL:(qi + 1) * BL, :] = (acc[:, :Dh] / acc[:, Dh:]).astype(y_ref.dtype)


def _rel_bucket(n):
    max_exact = REL_BUCKETS // 2
    nf = jnp.maximum(n, 1).astype(F32)
    large = max_exact + (jnp.log(nf / max_exact) / math.log(REL_MAX_DIST / max_exact)
                         * (REL_BUCKETS - max_exact)).astype(jnp.int32)
    large = jnp.minimum(large, REL_BUCKETS - 1)
    return jnp.where(n < max_exact, n, large)


def _moba(qkv, rel_bias):
    B, S, W3 = qkv.shape
    H, BL = MOBA_HEADS, MOBA_BLOCK
    W = W3 // 3
    Dh = W // H
    NB = S // BL
    assert BL + 1 >= REL_MAX_DIST and Dh == LANES and S % BL == 0
    i = jnp.arange(BL, dtype=jnp.int32)[:, None]
    j = jnp.arange(BL, dtype=jnp.int32)[None, :]
    buckets = jnp.arange(REL_BUCKETS, dtype=jnp.int32)

    def bias_tile(dist):
        onehot = (_rel_bucket(dist)[:, :, None] == buckets).astype(F32)
        return jnp.einsum('ijb,bh->hij', onehot, rel_bias, precision=lax.Precision.HIGHEST)

    t0 = jnp.where(i >= j, bias_tile(jnp.maximum(i - j, 0)), NEG_BIG)
    t1 = bias_tile(BL + i - j)
    bfar = jnp.broadcast_to(rel_bias.T[:, REL_BUCKETS - 1][:, None, None], (H, 1, LANES))
    return pl.pallas_call(
        functools.partial(_moba_kernel, BL=BL, NB=NB, Dh=Dh, topk=min(MOBA_TOPK, NB)),
        out_shape=jax.ShapeDtypeStruct((B, S, W), BF16),
        grid=(B, H),
        in_specs=[
            pl.BlockSpec((1, S, Dh), lambda b, h: (b, 0, h)),
            pl.BlockSpec((1, S, Dh), lambda b, h: (b, 0, H + h)),
            pl.BlockSpec((1, S, Dh), lambda b, h: (b, 0, 2 * H + h)),
            pl.BlockSpec((1, BL, BL), lambda b, h: (h, 0, 0)),
            pl.BlockSpec((1, BL, BL), lambda b, h: (h, 0, 0)),
            pl.BlockSpec((1, 1, LANES), lambda b, h: (h, 0, 0)),
        ],
        out_specs=pl.BlockSpec((1, S, Dh), lambda b, h: (b, 0, h)),
        scratch_shapes=[pltpu.VMEM((S, 2 * Dh), BF16)],
        compiler_params=_cparams(("parallel", "parallel")),
        name="moba",
    )(qkv, qkv, qkv, t0, t1, bfar)


def _mix_cd(h, norm_g, w_in, gmlp_norm_g, gmlp_norm_b, gmlp_w_s, gmlp_b_s, rel_bias):
    B, S, D = h.shape
    gw = gmlp_norm_g.shape[0]
    tm = _pick_tile(S, 512)
    u_pre, v_pre, qkv = _norm_proj(
        h, norm_g, w_in.astype(BF16),
        [(0, gw, BF16), (gw, 2 * gw, BF16), (2 * gw, w_in.shape[1], BF16)], tm)
    y_c = _gmlp(u_pre, v_pre, gmlp_norm_g, gmlp_norm_b, gmlp_w_s, gmlp_b_s, tm)
    return y_c, _moba(qkv, rel_bias)


def _router_kernel(res_ref, a_ref, b_ref, wa_ref, wb_ref, g_ref, wr_ref, rb_ref,
                   h_ref, idx_ref, gts_ref, cnt_ref, carry_ref, *, E):
    @pl.when(pl.program_id(0) == 0)
    def _():
        carry_ref[...] = jnp.zeros_like(carry_ref)

    x = _mix_out(res_ref[...], a_ref[...], b_ref[...], wa_ref, wb_ref)
    h_ref[...] = x
    tm = x.shape[0]
    var = jnp.mean(x * x, axis=-1, keepdims=True)
    hn = x * lax.rsqrt(var + EPS) * g_ref[...]
    logits = lax.dot_general(wr_ref[...], hn, (((1,), (1,)), ((), ())), preferred_element_type=F32,
                             precision=lax.Precision.HIGHEST) + rb_ref[:, 0:1]
    rowi = lax.broadcasted_iota(jnp.int32, (E, tm), 0)
    v1 = jnp.max(logits, axis=0, keepdims=True)
    e1 = jnp.min(jnp.where(logits == v1, rowi, E), axis=0, keepdims=True)
    masked = jnp.where(rowi == e1, -jnp.inf, logits)
    v2 = jnp.max(masked, axis=0, keepdims=True)
    e2 = jnp.min(jnp.where(masked == v2, rowi, E), axis=0, keepdims=True)
    ex = jnp.exp(v2 - v1)
    g1 = 1.0 / (1.0 + ex)
    g2 = ex / (1.0 + ex)
    oh1 = rowi == e1
    oh2 = rowi == e2
    cnt = jnp.where(oh1, 1.0, jnp.where(oh2, 1.0, 0.0))
    before = (lax.broadcasted_iota(jnp.int32, (tm, tm), 0) <
              lax.broadcasted_iota(jnp.int32, (tm, tm), 1)).astype(BF16)
    excl = jnp.dot(cnt.astype(BF16), before, preferred_element_type=F32) + carry_ref[:, 0:1]
    r1 = jnp.sum(jnp.where(oh1, excl, 0.0), axis=0, keepdims=True).astype(jnp.int32)
    r2 = jnp.sum(jnp.where(oh2, excl, 0.0), axis=0, keepdims=True).astype(jnp.int32)
    idx_ref[...] = jnp.where(rowi == 0, e1, jnp.where(rowi == 1, e2, jnp.where(rowi == 2, r1, jnp.where(rowi == 3, r2, 0))))
    gts_ref[...] = jnp.where(rowi == 0, g1, jnp.where(rowi == 1, g2, 0.0))
    carry_ref[...] = carry_ref[...] + jnp.sum(cnt, axis=1, keepdims=True)
    cnt_ref[...] = carry_ref[...]


def _proj_router(res2d, ya, yb, w_out, g, router_w, router_b, tm):
    T, D = res2d.shape
    E = router_w.shape[1]
    assert E == SUBLANES
    wa_n = ya.shape[1]
    wa, wb = w_out[:wa_n], w_out[wa_n:]
    return pl.pallas_call(
        functools.partial(_router_kernel, E=E),
        out_shape=[
            jax.ShapeDtypeStruct((T, D), F32),
            jax.ShapeDtypeStruct((E, T), jnp.int32),
            jax.ShapeDtypeStruct((E, T), F32),
            jax.ShapeDtypeStruct((E, LANES), F32),
        ],
        grid=(T // tm,),
        in_specs=[
            pl.BlockSpec((tm, D), lambda i: (i, 0)),
            pl.BlockSpec((tm, wa_n), lambda i: (i, 0)),
            pl.BlockSpec((tm, yb.shape[1]), lambda i: (i, 0)),
            pl.BlockSpec(wa.shape, lambda i: (0, 0)),
            pl.BlockSpec(wb.shape, lambda i: (0, 0)),
            pl.BlockSpec((1, D), lambda i: (0, 0)),
            pl.BlockSpec((E, D), lambda i: (0, 0)),
            pl.BlockSpec((E, LANES), lambda i: (0, 0)),
        ],
        out_specs=[
            pl.BlockSpec((tm, D), lambda i: (i, 0)),
            pl.BlockSpec((E, tm), lambda i: (0, i)),
            pl.BlockSpec((E, tm), lambda i: (0, i)),
            pl.BlockSpec((E, LANES), lambda i: (0, 0)),
        ],
        scratch_shapes=[pltpu.VMEM((E, LANES), F32)],
        compiler_params=_cparams(("arbitrary",)),
        name="proj_moe_router",
    )(res2d, ya, yb, wa, wb, g.reshape(1, D), router_w.T, jnp.broadcast_to(router_b[:, None], (E, LANES)))


DMA_ISSUE_UNROLL = 8


def _dispatch_kernel(pos1_ref, pos2_ref, ends_ref, h_ref, xs_hbm, zero_ref, sem, zsem, *, tm, tg, E):
    i = pl.program_id(0)

    @pl.when(i == 0)
    def _():
        zero_ref[...] = jnp.zeros_like(zero_ref)

        def zero_copy(e):
            start = ends_ref[e] - tg
            return pltpu.make_async_copy(zero_ref, xs_hbm.at[pl.ds(pl.multiple_of(start, tg), tg)], zsem)

        def nonempty(e):
            return ends_ref[e] > (ends_ref[e - 1] if e else 0)

        for e in range(E):
            @pl.when(nonempty(e))
            def _(e=e):
                zero_copy(e).start()
        for e in range(E):
            @pl.when(nonempty(e))
            def _(e=e):
                zero_copy(e).wait()

    def body(r, carry):
        t = i * tm + r
        src = h_ref.at[pl.ds(r, 1)]
        pltpu.make_async_copy(src, xs_hbm.at[pl.ds(pos1_ref[t], 1)], sem).start()
        pltpu.make_async_copy(src, xs_hbm.at[pl.ds(pos2_ref[t], 1)], sem).start()
        return carry
    lax.fori_loop(0, tm, body, 0, unroll=DMA_ISSUE_UNROLL)

    for _ in range(TOP_K):
        pltpu.make_async_copy(h_ref, xs_hbm.at[pl.ds(0, tm)], sem).wait()


def _dispatch(pos1, pos2, ends, h2d, n_rows, tm, tg):
    T, D = h2d.shape
    E = ends.shape[0]
    return pl.pallas_call(
        functools.partial(_dispatch_kernel, tm=tm, tg=tg, E=E),
        out_shape=jax.ShapeDtypeStruct((n_rows, D), h2d.dtype),
        grid_spec=pltpu.PrefetchScalarGridSpec(
            num_scalar_prefetch=3,
            grid=(T // tm,),
            in_specs=[pl.BlockSpec((tm, D), lambda i, p1, p2, en: (i, 0))],
            out_specs=pl.BlockSpec(memory_space=pl.ANY),
            scratch_shapes=[pltpu.VMEM((tg, D), h2d.dtype), pltpu.SemaphoreType.DMA, pltpu.SemaphoreType.DMA],
        ),
        compiler_params=_cparams(("arbitrary",)),
        name="moe_dispatch",
    )(pos1, pos2, ends, h2d)


def _gmm_kernel(te_ref, na_ref, x_ref, g_ref, wg_ref, wu_ref, wd_ref, o_ref, xn_ref, acc_ref):
    i = pl.program_id(0)
    f = pl.program_id(1)
    active = i < na_ref[0]

    @pl.when(f == 0)
    def _():
        acc_ref[...] = jnp.zeros_like(acc_ref)

    @pl.when(active & (f == 0))
    def _():
        xn_ref[...] = _rms_bf16(x_ref[...], g_ref[...])

    @pl.when(active)
    def _():
        x = xn_ref[...]
        a = jnp.dot(x, wg_ref[0].astype(BF16), preferred_element_type=F32)
        u = jnp.dot(x, wu_ref[0].astype(BF16), preferred_element_type=F32)
        hmid = (a * _sigmoid(a) * u).astype(BF16)
        acc_ref[...] += jnp.dot(hmid, wd_ref[0].astype(BF16), preferred_element_type=F32)

    @pl.when(f == pl.num_programs(1) - 1)
    def _():
        o_ref[...] = acc_ref[...]


def _gmm(tile_expert, n_active, xs, g, wg, wu, wd, tm, fc):
    R, D = xs.shape
    F = wg.shape[2]
    nf = F // fc

    def fsel(i, f, na):
        return jnp.where(i < na[0], f, nf - 1)

    return pl.pallas_call(
        _gmm_kernel,
        out_shape=jax.ShapeDtypeStruct((R, D), F32),
        grid_spec=pltpu.PrefetchScalarGridSpec(
            num_scalar_prefetch=2,
            grid=(R // tm, nf),
            in_specs=[
                pl.BlockSpec((tm, D), lambda i, f, te, na: (i, 0)),
                pl.BlockSpec((1, D), lambda i, f, te, na: (0, 0)),
                pl.BlockSpec((1, D, fc), lambda i, f, te, na: (te[i], 0, fsel(i, f, na))),
                pl.BlockSpec((1, D, fc), lambda i, f, te, na: (te[i], 0, fsel(i, f, na))),
                pl.BlockSpec((1, fc, D), lambda i, f, te, na: (te[i], fsel(i, f, na), 0)),
            ],
            out_specs=pl.BlockSpec((tm, D), lambda i, f, te, na: (i, 0)),
            scratch_shapes=[pltpu.VMEM((tm, D), BF16), pltpu.VMEM((tm, D), F32)],
        ),
        compiler_params=_cparams(("parallel", "arbitrary")),
        name="moe_gmm",
    )(tile_expert, n_active, xs, g.reshape(1, D), wg, wu, wd)


def _combine_copies(pos1_ref, pos2_ref, ys_hbm, ya_ref, yb_ref, sems, tile, slot, tm):
    def row_copies(r):
        t = tile * tm + r
        return (pltpu.make_async_copy(ys_hbm.at[pl.ds(pos1_ref[t], 1)], ya_ref.at[slot, pl.ds(r, 1)], sems.at[slot]),
                pltpu.make_async_copy(ys_hbm.at[pl.ds(pos2_ref[t], 1)], yb_ref.at[slot, pl.ds(r, 1)], sems.at[slot]))
    return row_copies


def _combine_kernel(pos1_ref, pos2_ref, h_ref, ga_ref, gb_ref, g_ref, ys_hbm, o_ref, ya_ref, yb_ref, sems, *, tm):
    i = pl.program_id(0)
    n = pl.num_programs(0)
    slot = lax.rem(i, 2)

    def start_tile(tile, slot):
        copies = _combine_copies(pos1_ref, pos2_ref, ys_hbm, ya_ref, yb_ref, sems, tile, slot, tm)

        def body(r, carry):
            a, b = copies(r)
            a.start()
            b.start()
            return carry
        lax.fori_loop(0, tm, body, 0, unroll=DMA_ISSUE_UNROLL)

    @pl.when(i == 0)
    def _():
        start_tile(0, 0)

    @pl.when(i + 1 < n)
    def _():
        start_tile(i + 1, 1 - slot)

    pltpu.make_async_copy(ys_hbm.at[pl.ds(0, tm)], ya_ref.at[slot], sems.at[slot]).wait()
    pltpu.make_async_copy(ys_hbm.at[pl.ds(0, tm)], yb_ref.at[slot], sems.at[slot]).wait()

    h = h_ref[...] + ga_ref[...] * ya_ref[slot] + gb_ref[...] * yb_ref[slot]
    var = jnp.mean(h * h, axis=-1, keepdims=True)
    o_ref[...] = h * lax.rsqrt(var + EPS) * g_ref[...]


def _combine(pos1, pos2, h2d, ga, gb, g, ys, tm):
    T, D = h2d.shape
    row = pl.BlockSpec((tm, D), lambda i, p1, p2: (i, 0))
    colv = pl.BlockSpec((tm, 1), lambda i, p1, p2: (i, 0))
    return pl.pallas_call(
        functools.partial(_combine_kernel, tm=tm),
        out_shape=jax.ShapeDtypeStruct((T, D), F32),
        grid_spec=pltpu.PrefetchScalarGridSpec(
            num_scalar_prefetch=2,
            grid=(T // tm,),
            in_specs=[row, colv, colv, pl.BlockSpec((1, D), lambda i, p1, p2: (0, 0)),
                      pl.BlockSpec(memory_space=pl.ANY)],
            out_specs=row,
            scratch_shapes=[pltpu.VMEM((2, tm, D), F32), pltpu.VMEM((2, tm, D), F32),
                            pltpu.SemaphoreType.DMA((2,))],
        ),
        compiler_params=_cparams(("arbitrary",)),
        name="moe_combine_norm",
    )(pos1, pos2, h2d, ga, gb, g.reshape(1, D), ys)


def _moe_final(res, y_c, y_d, w_out, norm_g, final_g, router_w, router_b, w_gate, w_up, w_down):
    B, S, D = res.shape
    T = B * S
    E = router_w.shape[1]
    h2d, idx, gts, cnt = _proj_router(res.reshape(T, D), y_c.reshape(T, -1), y_d.reshape(T, -1),
                                      w_out.astype(BF16), norm_g, router_w, router_b, _pick_tile(T, 512))
    e1, e2, r1, r2 = idx[0], idx[1], idx[2], idx[3]
    tm = _pick_tile(T, 1024)
    counts = cnt[:, 0].astype(jnp.int32)
    padded = ((counts + tm - 1) // tm) * tm
    ends = jnp.cumsum(padded)
    offs = ends - padded
    eids = jnp.arange(E, dtype=jnp.int32)[:, None]
    pos1 = jnp.sum(jnp.where(e1[None, :] == eids, offs[:, None], 0), axis=0) + r1
    pos2 = jnp.sum(jnp.where(e2[None, :] == eids, offs[:, None], 0), axis=0) + r2
    n_tiles = (TOP_K * T) // tm + E
    tile_start = jnp.arange(n_tiles, dtype=jnp.int32) * tm
    tile_expert = jnp.minimum(jnp.searchsorted(ends, tile_start, side='right'), E - 1).astype(jnp.int32)
    n_active = (ends[-1] // tm).astype(jnp.int32).reshape(1)
    xs = _dispatch(pos1, pos2, ends.astype(jnp.int32), h2d, n_tiles * tm, _pick_tile(T, 1024), tm)
    ys = _gmm(tile_expert, n_active, xs, norm_g, w_gate, w_up, w_down, tm, _pick_tile(w_gate.shape[2], 512))
    out = _combine(pos1, pos2, h2d, gts[0][:, None], gts[1][:, None], final_g, ys, _pick_tile(T, 256))
    return out.reshape(B, S, D)


def kernel(x, norm_mix_g, norm_ffn_g, norm_final_g, ab_w_in, s5_lambda_re, s5_lambda_im, s5_log_dt, s5_b_re, s5_b_im, s5_c_re, s5_c_im, s5_d, s5_w_glu, mlstm_conv_w, mlstm_conv_b, mlstm_gate_b, mlstm_norm_g, ab_w_out, ffn_w_gate, ffn_w_up, ffn_w_down, cd_w_in, gmlp_norm_g, gmlp_norm_b, gmlp_w_s, gmlp_b_s, rel_bias, cd_w_out, moe_router_w, moe_router_b, moe_w_gate, moe_w_up, moe_w_down):
    B, S, D = x.shape
    y_a, y_b = _mix_ab(x, norm_mix_g[0], ab_w_in[0], s5_lambda_re[0], s5_lambda_im[0], s5_log_dt[0],
                       s5_b_re[0], s5_b_im[0], s5_c_re[0], s5_c_im[0], s5_d[0], s5_w_glu[0],
                       mlstm_conv_w[0], mlstm_conv_b[0], mlstm_gate_b[0], mlstm_norm_g[0])
    h = _proj_ffn(x, y_a, y_b, ab_w_out[0].astype(BF16), norm_ffn_g[0], ffn_w_gate[0].astype(BF16),
                  ffn_w_up[0].astype(BF16), ffn_w_down[0].astype(BF16),
                  _pick_tile(S, 512), _pick_tile(ffn_w_gate.shape[2], 1408))
    y_c, y_d = _mix_cd(h, norm_mix_g[1], cd_w_in[0], gmlp_norm_g[0], gmlp_norm_b[0],
                       gmlp_w_s[0], gmlp_b_s[0], rel_bias)
    return _moe_final(h, y_c, y_d, cd_w_out[0], norm_ffn_g[1], norm_final_g, moe_router_w[0], moe_router_b[0],
                      moe_w_gate[0], moe_w_up[0], moe_w_down[0])
```

```python
import functools
import math

import jax
import jax.numpy as jnp
from jax import lax
from jax.experimental import pallas as pl
from jax.experimental.pallas import tpu as pltpu

F32 = jnp.float32
BF16 = jnp.bfloat16
EPS = 1e-5

LANES = 128
SUBLANES = 8
VMEM_LIMIT_BYTES = 56 * 1024 * 1024

S5_GROUP = 16
S5_STATE = 64
S5_SLAB_GROUPS = 8
MLSTM_HEADS = 4
MLSTM_CHUNK = 128
MLSTM_SEQS_PER_STEP = 2
CONV_WIDTH = 4
GMLP_GROUPS = 4
GMLP_CHUNK = 128
MOBA_HEADS = 4
MOBA_BLOCK = 256
MOBA_TOPK = 3
REL_BUCKETS = 32
REL_MAX_DIST = 128
N_EXPERTS = 8
TOP_K = 2
ROUTER_SUB_TILE = 512


def _cparams(sem):
    return pltpu.CompilerParams(dimension_semantics=sem, vmem_limit_bytes=VMEM_LIMIT_BYTES)


def _rms_bf16(x, g):
    var = jnp.mean(x * x, axis=-1, keepdims=True)
    return (x * lax.rsqrt(var + EPS) * g).astype(BF16)


def _gelu(x):
    return jax.nn.gelu(x, approximate=True)


def _sigmoid(x):
    return 1.0 / (1.0 + jnp.exp(-x))


def _norm_proj_kernel(x_ref, g_ref, w_ref, *out_refs, splits):
    hn = _rms_bf16(x_ref[0], g_ref[...])
    for o_ref, (c0, c1) in zip(out_refs, splits):
        r = jnp.dot(hn, w_ref[:, c0:c1], preferred_element_type=F32)
        o_ref[...] = r.reshape(o_ref.shape).astype(o_ref.dtype)


def _norm_proj(x, g, w, outs, tm):
    B, S, D = x.shape
    splits = tuple((c0, c1) for c0, c1, _ in outs)
    out_shape, out_specs = [], []
    for c0, c1, dt in outs:
        n = c1 - c0
        out_shape.append(jax.ShapeDtypeStruct((B, S, n), dt))
        out_specs.append(pl.BlockSpec((1, tm, n), lambda b, i: (b, i, 0)))
    return pl.pallas_call(
        functools.partial(_norm_proj_kernel, splits=splits),
        out_shape=out_shape,
        grid=(B, S // tm),
        in_specs=[
            pl.BlockSpec((1, tm, D), lambda b, i: (b, i, 0)),
            pl.BlockSpec((1, D), lambda b, i: (0, 0)),
            pl.BlockSpec(w.shape, lambda b, i: (0, 0)),
        ],
        out_specs=out_specs,
        compiler_params=_cparams(("parallel", "parallel")),
        name="norm_proj",
    )(x, g.reshape(1, D), w)


def _s5_kernel(u_hbm, bmat_ref, ar_ref, ai_ref, cmat_ref, d_ref, wglu_ref, y_hbm,
               xr_ref, xi_ref, ubuf, ybuf, in_sems, out_sems, *bufs, lc, nb, nslab, sw):
    i = pl.program_id(0)
    n = pl.num_programs(0)
    slot = lax.rem(i, 2)
    W = ubuf.shape[-1]

    def in_copies(step, slot):
        t0 = pl.multiple_of(step * lc, lc)
        return [pltpu.make_async_copy(u_hbm.at[b, pl.ds(t0, lc), :], ubuf.at[slot, :, b, :], in_sems.at[slot])
                for b in range(nb)]

    def out_copies(step, slot):
        t0 = pl.multiple_of(step * lc, lc)
        return [pltpu.make_async_copy(ybuf.at[slot, :, b, :], y_hbm.at[b, pl.ds(t0, lc), :], out_sems.at[slot])
                for b in range(nb)]

    @pl.when(i == 0)
    def _():
        xr_ref[...] = jnp.zeros_like(xr_ref)
        xi_ref[...] = jnp.zeros_like(xi_ref)
        for c in in_copies(0, 0):
            c.start()

    @pl.when(i + 1 < n)
    def _():
        for c in in_copies(i + 1, 1 - slot):
            c.start()

    for c in in_copies(i, slot):
        c.wait()

    uf = ubuf[slot].reshape(lc * nb, W)
    u = uf.astype(BF16)
    xr_all = xr_ref[...]
    xi_all = xi_ref[...]

    def drive(k):
        bufs[k][...] = jnp.dot(u[:, LANES * k:LANES * (k + 1)], bmat_ref[k], preferred_element_type=F32)

    drive(0)
    ys, new_xr, new_xi = [], [], []
    for k in range(nslab):
        if k + 1 < nslab:
            drive(k + 1)
        st_cols = slice(sw * k, sw * (k + 1))
        ar = jnp.broadcast_to(ar_ref[:, st_cols], (nb, sw))
        ai = jnp.broadcast_to(ai_ref[:, st_cols], (nb, sw))
        xr, xi = xr_all[:, st_cols], xi_all[:, st_cols]
        for t in range(lc):
            rows = slice(t * nb, (t + 1) * nb)
            nxr = ar * xr - ai * xi + bufs[k][rows, :sw]
            nxi = ar * xi + ai * xr + bufs[k][rows, sw:]
            bufs[k][rows, :sw] = nxr
            bufs[k][rows, sw:] = nxi
            xr, xi = nxr, nxi
        new_xr.append(xr)
        new_xi.append(xi)
        ys.append(jnp.dot(bufs[k][...].astype(BF16), cmat_ref[k], preferred_element_type=F32))
    xr_ref[...] = jnp.concatenate(new_xr, axis=-1)
    xi_ref[...] = jnp.concatenate(new_xi, axis=-1)
    y = jnp.concatenate(ys, axis=-1)
    y = _gelu(y + d_ref[...] * uf)
    gl = jnp.dot(y.astype(BF16), wglu_ref[...], preferred_element_type=F32)

    @pl.when(i >= 2)
    def _():
        for c in out_copies(i - 2, slot):
            c.wait()

    ybuf[slot] = (y * _sigmoid(gl)).reshape(lc, nb, W)
    for c in out_copies(i, slot):
        c.start()

    @pl.when(i == n - 1)
    def _():
        for c in out_copies(i, slot):
            c.wait()

        @pl.when(n >= 2)
        def _():
            for c in out_copies(i - 1, 1 - slot):
                c.wait()


def _s5_params(lam_re, lam_im, log_dt, b_re, b_im, c_re, c_im):
    G, P = lam_re.shape
    Hc = b_re.shape[-1]
    dt = jnp.exp(log_dt.astype(F32))[:, None]
    mag = jnp.exp(lam_re * dt)
    ar = mag * jnp.cos(lam_im * dt)
    ai = mag * jnp.sin(lam_im * dt)
    den = lam_re * lam_re + lam_im * lam_im
    cr = ((ar - 1.0) * lam_re + ai * lam_im) / den
    ci = (ai * lam_re - (ar - 1.0) * lam_im) / den
    bb_re = cr[..., None] * b_re - ci[..., None] * b_im
    bb_im = cr[..., None] * b_im + ci[..., None] * b_re
    gs = S5_SLAB_GROUPS
    nslab = G // gs
    eye = jnp.eye(gs, dtype=F32)

    def bd_in(b):
        b = b.reshape(nslab, gs, P, Hc)
        return jnp.einsum('kgph,gj->kghjp', b, eye).reshape(nslab, gs * Hc, gs * P)

    def bd_out(c):
        c = c.reshape(nslab, gs, Hc, P)
        return jnp.einsum('kghp,gj->kgpjh', c, eye).reshape(nslab, gs * P, gs * Hc)

    bmat = jnp.concatenate([bd_in(bb_re), bd_in(bb_im)], axis=-1).astype(BF16)
    cmat = jnp.concatenate([bd_out(c_re), -bd_out(c_im)], axis=1).astype(BF16)
    return bmat, ar.reshape(1, G * P), ai.reshape(1, G * P), cmat


def _s5(u, bmat, ar, ai, cmat, d_skip, w_glu, lc):
    nb, S, W = u.shape
    assert nb % SUBLANES == 0
    nslab = bmat.shape[0]
    sw = bmat.shape[2] // 2
    rows = lc * nb
    return pl.pallas_call(
        functools.partial(_s5_kernel, lc=lc, nb=nb, nslab=nslab, sw=sw),
        out_shape=jax.ShapeDtypeStruct((nb, S, W), F32),
        grid=(S // lc,),
        in_specs=[
            pl.BlockSpec(memory_space=pl.ANY),
            pl.BlockSpec(bmat.shape, lambda i: (0, 0, 0)),
            pl.BlockSpec(ar.shape, lambda i: (0, 0)),
            pl.BlockSpec(ai.shape, lambda i: (0, 0)),
            pl.BlockSpec(cmat.shape, lambda i: (0, 0, 0)),
            pl.BlockSpec((1, W), lambda i: (0, 0)),
            pl.BlockSpec(w_glu.shape, lambda i: (0, 0)),
        ],
        out_specs=pl.BlockSpec(memory_space=pl.ANY),
        scratch_shapes=[pltpu.VMEM((nb, sw * nslab), F32), pltpu.VMEM((nb, sw * nslab), F32),
                        pltpu.VMEM((2, lc, nb, W), F32), pltpu.VMEM((2, lc, nb, W), F32),
                        pltpu.SemaphoreType.DMA((2,)), pltpu.SemaphoreType.DMA((2,))]
        + [pltpu.VMEM((rows, 2 * sw), F32) for _ in range(nslab)],
        compiler_params=_cparams(("arbitrary",)),
        name="s5",
    )(u, bmat, ar, ai, cmat, d_skip.reshape(1, W), w_glu)


def _mlstm_chunk(x, vb, o_pre, g, tail, cx_all, m_row, cw, cb, ng, *, L, H, Dh):
    W = H * Dh
    halo = SUBLANES
    log_scale = -0.5 * math.log(Dh)

    row = lax.broadcasted_iota(jnp.int32, (L, L), 0)
    col = lax.broadcasted_iota(jnp.int32, (L, L), 1)
    conv = cb + cw[0:1, :] * x.astype(F32)
    corr = jnp.zeros((halo, 2 * W), F32)
    for j in range(1, CONV_WIDTH):
        shift = (row - col == j).astype(BF16)
        conv = conv + cw[j:j + 1, :] * jnp.dot(shift, x, preferred_element_type=F32)
        corr = corr + cw[j:j + 1, :] * tail[halo - j:2 * halo - j, :]
    conv = jnp.concatenate([conv[:halo] + corr, conv[halo:]], axis=0)
    new_tail = x[L - halo:, :].astype(F32)
    qk = (conv * _sigmoid(conv)).astype(BF16)
    q = qk[:, :W]
    kb = qk[:, W:]
    ones = jnp.ones((L, Dh), BF16)
    sq_ones = jnp.ones((Dh, Dh), BF16)

    log_i = g[:, :LANES]
    f_pre = g[:, LANES:]
    logf = jnp.minimum(f_pre, 0.0) - jnp.log(1.0 + jnp.exp(-jnp.abs(f_pre)))
    causal = row >= col
    bcum = jnp.dot(causal.astype(F32), logf, preferred_element_type=F32, precision=lax.Precision.HIGHEST)
    w_cols = log_i - bcum
    trow = lax.broadcasted_iota(jnp.int32, (L, LANES), 0)
    cmax = w_cols
    k = 1
    while k < L:
        cmax = jnp.maximum(cmax, jnp.where(trow >= k, pltpu.roll(cmax, k, axis=0), -jnp.inf))
        k *= 2
    m_inter = bcum + m_row
    m_t = jnp.maximum(bcum + cmax, m_inter)
    u_cols = bcum - m_t + log_scale
    wi_cols = jnp.exp(m_inter - m_t)
    em_cols = jnp.exp(-m_t)
    b_last = bcum[L - 1:L, :]
    m_loc = b_last + cmax[L - 1:L, :]
    m_new = jnp.maximum(b_last + m_row, m_loc)
    s_prev = jnp.exp(b_last + m_row - m_new)
    s_loc = jnp.exp(m_loc - m_new)
    w_rows = w_cols.T

    heads = range(H)
    hs = [slice(h * Dh, (h + 1) * Dh) for h in heads]
    hl = [slice(h, h + 1) for h in heads]
    nt = (((1,), (1,)), ((), ()))
    vext = [jnp.concatenate([vb[:, hs[h]], ones], axis=1) for h in heads]
    qk_t = [lax.dot_general(q[:, hs[h]], kb[:, hs[h]], nt, preferred_element_type=F32) for h in heads]
    r2 = [jnp.dot(q[:, hs[h]], cx_all[h].astype(BF16), preferred_element_type=F32) for h in heads]
    decay = [jnp.exp(jnp.where(causal, u_cols[:, hl[h]] + w_rows[hl[h], :], -jnp.inf)) for h in heads]
    s = [(qk_t[h] * decay[h]).astype(BF16) for h in heads]
    r1 = [jnp.dot(s[h], vext[h], preferred_element_type=F32) for h in heads]
    hh = []
    for h in heads:
        wi = jnp.broadcast_to(wi_cols[:, hl[h]], (L, Dh))
        num = r1[h][:, :Dh] + wi * r2[h][:, :Dh]
        den = r1[h][:, Dh:] + wi * r2[h][:, Dh:]
        hh.append(num / jnp.maximum(jnp.abs(den), jnp.broadcast_to(em_cols[:, hl[h]], (L, Dh))))
    msq = [jnp.dot((hh[h] * hh[h]).astype(BF16), sq_ones, preferred_element_type=F32) * (1.0 / Dh) for h in heads]
    ys = [(_sigmoid(o_pre[:, hs[h]].astype(F32)) * hh[h] * lax.rsqrt(msq[h] + EPS) * ng[:, hs[h]]).astype(BF16)
          for h in heads]
    kwt = [(kb[:, hs[h]].astype(F32).T
            * jnp.exp(b_last[:, hl[h]] + w_rows[hl[h], :] - m_loc[:, hl[h]] + log_scale)).astype(BF16) for h in heads]
    cx_loc = [jnp.dot(kwt[h], vext[h], preferred_element_type=F32) for h in heads]
    new_cx = [s_prev[:, hl[h]] * cx_all[h] + s_loc[:, hl[h]] * cx_loc[h] for h in heads]
    return ys, new_tail, new_cx, m_new


def _mlstm_kernel(qk_ref, v_ref, o_ref, gt_ref, cw_ref, cb_ref, gb_ref, ng_ref, y_ref,
                  tail_ref, c_ref, m_ref, *, L, H, Dh, BB):
    @pl.when(pl.program_id(1) == 0)
    def _():
        tail_ref[...] = jnp.zeros_like(tail_ref)
        c_ref[...] = jnp.zeros_like(c_ref)
        m_ref[...] = jnp.zeros_like(m_ref)

    tails = [tail_ref[bb] for bb in range(BB)]
    cxs = [[c_ref[bb, h] for h in range(H)] for bb in range(BB)]
    m_rows = [m_ref[bb, 0:1, :] for bb in range(BB)]
    results = []
    for bb in range(BB):
        g = gt_ref[bb] + gb_ref[...]
        results.append(_mlstm_chunk(qk_ref[bb], v_ref[bb].astype(BF16), o_ref[bb], g, tails[bb], cxs[bb],
                                    m_rows[bb], cw_ref[...], cb_ref[...], ng_ref[...], L=L, H=H, Dh=Dh))
    for bb, (ys, new_tail, new_cx, m_new) in enumerate(results):
        for h in range(H):
            y_ref[bb, :, h * Dh:(h + 1) * Dh] = ys[h]
            c_ref[bb, h] = new_cx[h]
        tail_ref[bb, 0:SUBLANES, :] = new_tail
        m_ref[bb, 0:1, :] = m_new


def _mlstm(qk_pre, v, o_pre, gates, conv_w, conv_b, gate_b_pad, norm_g):
    B, S, W2 = qk_pre.shape
    W = W2 // 2
    H, L = MLSTM_HEADS, MLSTM_CHUNK
    Dh = W // H
    BB = _pick_tile(B, MLSTM_SEQS_PER_STEP)
    return pl.pallas_call(
        functools.partial(_mlstm_kernel, L=L, H=H, Dh=Dh, BB=BB),
        out_shape=jax.ShapeDtypeStruct((B, S, W), BF16),
        grid=(B // BB, S // L),
        in_specs=[
            pl.BlockSpec((BB, L, W2), lambda b, c: (b, c, 0)),
            pl.BlockSpec((BB, L, W), lambda b, c: (b, c, 0)),
            pl.BlockSpec((BB, L, W), lambda b, c: (b, c, 0)),
            pl.BlockSpec((BB, L, 2 * LANES), lambda b, c: (b, c, 0)),
            pl.BlockSpec((CONV_WIDTH, W2), lambda b, c: (0, 0)),
            pl.BlockSpec((1, W2), lambda b, c: (0, 0)),
            pl.BlockSpec((1, 2 * LANES), lambda b, c: (0, 0)),
            pl.BlockSpec((1, W), lambda b, c: (0, 0)),
        ],
        out_specs=pl.BlockSpec((BB, L, W), lambda b, c: (b, c, 0)),
        scratch_shapes=[
            pltpu.VMEM((BB, 2 * SUBLANES, W2), F32),
            pltpu.VMEM((BB, H, Dh, 2 * Dh), F32),
            pltpu.VMEM((BB, SUBLANES, LANES), F32),
        ],
        compiler_params=_cparams(("parallel", "arbitrary")),
        name="mlstm",
    )(qk_pre, v, o_pre, gates, conv_w, conv_b.reshape(1, W2), gate_b_pad, norm_g.reshape(1, W))


def _mix_out(res, a, b, wa_ref, wb_ref):
    acc = jnp.dot(a.astype(BF16), wa_ref[...], preferred_element_type=F32)
    return res + acc + jnp.dot(b.astype(BF16), wb_ref[...], preferred_element_type=F32)


def _proj_ffn_kernel(res_ref, a_ref, b_ref, wa_ref, wb_ref, g_ref, wg_ref, wu_ref, wd_ref, o_ref,
                     hn_ref, acc_ref):
    f = pl.program_id(2)

    @pl.when(f == 0)
    def _():
        h = _mix_out(res_ref[0], a_ref[0], b_ref[0], wa_ref, wb_ref)
        hn_ref[...] = _rms_bf16(h, g_ref[...])
        acc_ref[...] = h

    hn = hn_ref[...]
    a = jnp.dot(hn, wg_ref[...], preferred_element_type=F32)
    u = jnp.dot(hn, wu_ref[...], preferred_element_type=F32)
    hmid = (a * _sigmoid(a) * u).astype(BF16)
    acc_ref[...] += jnp.dot(hmid, wd_ref[...], preferred_element_type=F32)

    @pl.when(f == pl.num_programs(2) - 1)
    def _():
        o_ref[0] = acc_ref[...]


def _proj_ffn(res, ya, yb, w_out, g, wg, wu, wd, tm, fc):
    B, S, D = res.shape
    wa_n = ya.shape[2]
    wa, wb = w_out[:wa_n], w_out[wa_n:]
    F = wg.shape[1]
    return pl.pallas_call(
        _proj_ffn_kernel,
        out_shape=jax.ShapeDtypeStruct((B, S, D), F32),
        grid=(B, S // tm, F // fc),
        in_specs=[
            pl.BlockSpec((1, tm, D), lambda b, i, f: (b, i, 0)),
            pl.BlockSpec((1, tm, wa_n), lambda b, i, f: (b, i, 0)),
            pl.BlockSpec((1, tm, wb.shape[0]), lambda b, i, f: (b, i, 0)),
            pl.BlockSpec(wa.shape, lambda b, i, f: (0, 0)),
            pl.BlockSpec(wb.shape, lambda b, i, f: (0, 0)),
            pl.BlockSpec((1, D), lambda b, i, f: (0, 0)),
            pl.BlockSpec((D, fc), lambda b, i, f: (0, f)),
            pl.BlockSpec((D, fc), lambda b, i, f: (0, f)),
            pl.BlockSpec((fc, D), lambda b, i, f: (f, 0)),
        ],
        out_specs=pl.BlockSpec((1, tm, D), lambda b, i, f: (b, i, 0)),
        scratch_shapes=[pltpu.VMEM((tm, D), BF16), pltpu.VMEM((tm, D), F32)],
        compiler_params=_cparams(("parallel", "parallel", "arbitrary")),
        name="proj_ffn_swiglu",
    )(res, ya, yb, wa, wb, g.reshape(1, D), wg, wu, wd)


def _pick_tile(n, target):
    t = min(n, target)
    while n % t:
        t //= 2
    return t


def _mix_ab(h, norm_g, w_in, lam_re, lam_im, log_dt, b_re, b_im, c_re, c_im, d_skip, w_glu,
            conv_w, conv_b, gate_b, mlstm_norm_g):
    B, S, D = h.shape
    s5w = lam_re.shape[0] * S5_GROUP
    mw = mlstm_norm_g.shape[0]
    c_qk, c_v, c_o, c_if = s5w, s5w + 2 * mw, s5w + 3 * mw, s5w + 4 * mw
    nh = (w_in.shape[1] - c_if) // 2
    lane_pad = ((0, 0), (0, LANES - nh))
    w_pad = jnp.concatenate([w_in[:, :c_if], jnp.pad(w_in[:, c_if:c_if + nh], lane_pad),
                             jnp.pad(w_in[:, c_if + nh:], lane_pad)], axis=1).astype(BF16)
    tm = _pick_tile(S, 512)
    u_s5, qk_pre, v, o_pre, gates = _norm_proj(
        h, norm_g, w_pad,
        [(0, c_qk, F32), (c_qk, c_v, BF16), (c_v, c_o, BF16), (c_o, c_if, BF16), (c_if, c_if + 2 * LANES, F32)],
        tm)
    bmat, ar, ai, cmat = _s5_params(lam_re, lam_im, log_dt, b_re, b_im, c_re, c_im)
    y_a = _s5(u_s5, bmat, ar, ai, cmat, d_skip, w_glu.astype(BF16), lc=_pick_tile(S, 32))
    gate_b_pad = jnp.concatenate([jnp.pad(gate_b[:nh], (0, LANES - nh)),
                                  jnp.pad(gate_b[nh:], (0, LANES - nh))]).reshape(1, 2 * LANES)
    y_b = _mlstm(qk_pre, v, o_pre, gates, conv_w, conv_b, gate_b_pad, mlstm_norm_g)
    return y_a, y_b


def _gmlp_kernel(u_ref, v_ref, ng_ref, nb_ref, ws_ref, bs_ref, y_ref, *, L, G, Dg, nchunk):
    u = _gelu(u_ref[0].astype(F32))
    v = _gelu(v_ref[0].astype(F32))
    mu = jnp.mean(v, axis=-1, keepdims=True)
    vc = v - mu
    var = jnp.mean(vc * vc, axis=-1, keepdims=True)
    vn = (vc * lax.rsqrt(var + EPS) * ng_ref[...] + nb_ref[...]).astype(BF16)
    for c in range(nchunk):
        rows = slice(c * L, (c + 1) * L)
        for g in range(G):
            cols = slice(g * Dg, (g + 1) * Dg)
            s = jnp.dot(ws_ref[g], vn[rows, cols], preferred_element_type=F32) + bs_ref[:, g:g + 1]
            y_ref[0, rows, cols] = (u[rows, cols] * s).astype(y_ref.dtype)


def _gmlp(u_pre, v_pre, norm_g, norm_b, w_s, b_s, tm):
    B, S, W = u_pre.shape
    G, L = GMLP_GROUPS, GMLP_CHUNK
    ws = (w_s * jnp.tril(jnp.ones((L, L), w_s.dtype))).astype(BF16)
    bs = jnp.pad(b_s.T, ((0, 0), (0, LANES - G)))
    return pl.pallas_call(
        functools.partial(_gmlp_kernel, L=L, G=G, Dg=W // G, nchunk=tm // L),
        out_shape=jax.ShapeDtypeStruct((B, S, W), BF16),
        grid=(B, S // tm),
        in_specs=[
            pl.BlockSpec((1, tm, W), lambda b, i: (b, i, 0)),
            pl.BlockSpec((1, tm, W), lambda b, i: (b, i, 0)),
            pl.BlockSpec((1, W), lambda b, i: (0, 0)),
            pl.BlockSpec((1, W), lambda b, i: (0, 0)),
            pl.BlockSpec((G, L, L), lambda b, i: (0, 0, 0)),
            pl.BlockSpec((L, LANES), lambda b, i: (0, 0)),
        ],
        out_specs=pl.BlockSpec((1, tm, W), lambda b, i: (b, i, 0)),
        compiler_params=_cparams(("parallel", "parallel")),
        name="gmlp",
    )(u_pre, v_pre, norm_g.reshape(1, W), norm_b.reshape(1, W), ws, bs)


NEG_BIG = -1e30


def _moba_kernel(q_ref, k_ref, v_ref, t0_ref, t1_ref, bfar_ref, y_ref, vext_ref, *, BL, NB, Dh, topk):
    nt = (((1,), (1,)), ((), ()))
    scale = 1.0 / math.sqrt(Dh)
    vext_ref[:, :Dh] = v_ref[0]
    vext_ref[:, Dh:] = jnp.ones((NB * BL, Dh), BF16)
    bfar = bfar_ref[0][:, 0:1]
    kmean = jnp.concatenate(
        [jnp.mean(k_ref[0, n * BL:(n + 1) * BL, :].astype(F32), axis=0, keepdims=True) for n in range(NB)],
        axis=0)
    nrow = lax.broadcasted_iota(jnp.int32, (NB, BL), 0)
    eye = (lax.broadcasted_iota(jnp.int32, (BL, BL), 0) ==
           lax.broadcasted_iota(jnp.int32, (BL, BL), 1)).astype(BF16)

    def logits(qi):
        rows = slice(qi * BL, (qi + 1) * BL)
        qf = q_ref[0, rows, :].astype(F32)
        qs = (qf * scale).astype(BF16)
        neg_cols = None
        if qi > topk:
            gate = lax.dot_general(kmean, qf, nt, preferred_element_type=F32,
                                   precision=lax.Precision.HIGHEST)
            rank = jnp.zeros((NB, BL), F32)
            for m in range(qi):
                gm = gate[m:m + 1, :]
                rank = rank + jnp.where(gm > gate, 1.0, jnp.where(gm == gate, jnp.where(nrow > m, 1.0, 0.0), 0.0))
            sel_rows = jnp.where(nrow < qi, jnp.where(rank < topk, 1.0, 0.0), 0.0)
            sel_pad = jnp.concatenate([sel_rows, jnp.zeros((LANES - NB, BL), F32)], axis=0).astype(BF16)
            sel_cols = lax.dot_general(eye, sel_pad, nt, preferred_element_type=F32)
            neg_cols = (1.0 - sel_cols) * NEG_BIG

        m_tile = None
        pieces = []
        for kb in range(qi + 1):
            s = lax.dot_general(qs, k_ref[0, kb * BL:(kb + 1) * BL, :], nt, preferred_element_type=F32)
            if kb == qi:
                s = s + t0_ref[0]
            elif kb == qi - 1:
                s = s + t1_ref[0]
                if neg_cols is not None:
                    s = s + neg_cols[:, kb:kb + 1]
            else:
                s = s + (bfar if neg_cols is None else bfar + neg_cols[:, kb:kb + 1])
            pieces.append(s)
            for c in range(BL // LANES):
                piece = s[:, c * LANES:(c + 1) * LANES]
                m_tile = piece if m_tile is None else jnp.maximum(m_tile, piece)
        return pieces, m_tile

    nxt = logits(0)
    for qi in range(NB):
        pieces, m_tile = nxt
        if qi + 1 < NB:
            nxt = logits(qi + 1)
        m = jnp.max(m_tile, axis=-1, keepdims=True)
        p = jnp.concatenate([jnp.exp(s - m).astype(BF16) for s in pieces], axis=1)
        acc = jnp.dot(p, vext_ref[:(qi + 1) * BL, :], preferred_element_type=F32)
        y_ref[0, qi * BL:(qi + 1) * BL, :] = (acc[:, :Dh] / acc[:, Dh:]).astype(y_ref.dtype)


def _rel_bucket(n):
    max_exact = REL_BUCKETS // 2
    nf = jnp.maximum(n, 1).astype(F32)
    large = max_exact + (jnp.log(nf / max_exact) / math.log(REL_MAX_DIST / max_exact)
                         * (REL_BUCKETS - max_exact)).astype(jnp.int32)
    large = jnp.minimum(large, REL_BUCKETS - 1)
    return jnp.where(n < max_exact, n, large)


def _moba(qkv, rel_bias):
    B, S, W3 = qkv.shape
    H, BL = MOBA_HEADS, MOBA_BLOCK
    W = W3 // 3
    Dh = W // H
    NB = S // BL
    assert BL + 1 >= REL_MAX_DIST and Dh == LANES and S % BL == 0
    i = jnp.arange(BL, dtype=jnp.int32)[:, None]
    j = jnp.arange(BL, dtype=jnp.int32)[None, :]
    buckets = jnp.arange(REL_BUCKETS, dtype=jnp.int32)

    def bias_tile(dist):
        onehot = (_rel_bucket(dist)[:, :, None] == buckets).astype(F32)
        return jnp.einsum('ijb,bh->hij', onehot, rel_bias, precision=lax.Precision.HIGHEST)

    t0 = jnp.where(i >= j, bias_tile(jnp.maximum(i - j, 0)), NEG_BIG)
    t1 = bias_tile(BL + i - j)
    bfar = jnp.broadcast_to(rel_bias.T[:, REL_BUCKETS - 1][:, None, None], (H, 1, LANES))
    return pl.pallas_call(
        functools.partial(_moba_kernel, BL=BL, NB=NB, Dh=Dh, topk=min(MOBA_TOPK, NB)),
        out_shape=jax.ShapeDtypeStruct((B, S, W), BF16),
        grid=(B, H),
        in_specs=[
            pl.BlockSpec((1, S, Dh), lambda b, h: (b, 0, h)),
            pl.BlockSpec((1, S, Dh), lambda b, h: (b, 0, H + h)),
            pl.BlockSpec((1, S, Dh), lambda b, h: (b, 0, 2 * H + h)),
            pl.BlockSpec((1, BL, BL), lambda b, h: (h, 0, 0)),
            pl.BlockSpec((1, BL, BL), lambda b, h: (h, 0, 0)),
            pl.BlockSpec((1, 1, LANES), lambda b, h: (h, 0, 0)),
        ],
        out_specs=pl.BlockSpec((1, S, Dh), lambda b, h: (b, 0, h)),
        scratch_shapes=[pltpu.VMEM((S, 2 * Dh), BF16)],
        compiler_params=_cparams(("parallel", "parallel")),
        name="moba",
    )(qkv, qkv, qkv, t0, t1, bfar)


def _mix_cd(h, norm_g, w_in, gmlp_norm_g, gmlp_norm_b, gmlp_w_s, gmlp_b_s, rel_bias):
    B, S, D = h.shape
    gw = gmlp_norm_g.shape[0]
    tm = _pick_tile(S, 512)
    u_pre, v_pre, qkv = _norm_proj(
        h, norm_g, w_in.astype(BF16),
        [(0, gw, BF16), (gw, 2 * gw, BF16), (2 * gw, w_in.shape[1], BF16)], tm)
    y_c = _gmlp(u_pre, v_pre, gmlp_norm_g, gmlp_norm_b, gmlp_w_s, gmlp_b_s, tm)
    return y_c, _moba(qkv, rel_bias)


def _route(x, carry, g, wr, rb, E):
    tm = x.shape[0]
    var = jnp.mean(x * x, axis=-1, keepdims=True)
    hn = x * lax.rsqrt(var + EPS) * g
    hn_hi = hn.astype(BF16)
    hn_lo = (hn - hn_hi.astype(F32)).astype(BF16)
    wr_hi, wr_lo = wr
    lg = (jnp.dot(hn_hi, wr_hi, preferred_element_type=F32) + jnp.dot(hn_hi, wr_lo, preferred_element_type=F32)
          + jnp.dot(hn_lo, wr_hi, preferred_element_type=F32))
    logits = lg.T[:E, :] + rb
    rowi = lax.broadcasted_iota(jnp.int32, (E, tm), 0)
    v1 = jnp.max(logits, axis=0, keepdims=True)
    e1 = jnp.min(jnp.where(logits == v1, rowi, E), axis=0, keepdims=True)
    masked = jnp.where(rowi == e1, -jnp.inf, logits)
    v2 = jnp.max(masked, axis=0, keepdims=True)
    e2 = jnp.min(jnp.where(masked == v2, rowi, E), axis=0, keepdims=True)
    ex = jnp.exp(v2 - v1)
    g1 = 1.0 / (1.0 + ex)
    g2 = ex / (1.0 + ex)
    oh1 = rowi == e1
    oh2 = rowi == e2
    cnt = jnp.where(oh1, 1.0, jnp.where(oh2, 1.0, 0.0))
    before = (lax.broadcasted_iota(jnp.int32, (tm, tm), 0) <
              lax.broadcasted_iota(jnp.int32, (tm, tm), 1)).astype(BF16)
    excl = jnp.dot(cnt.astype(BF16), before, preferred_element_type=F32) + carry
    r1 = jnp.sum(jnp.where(oh1, excl, 0.0), axis=0, keepdims=True).astype(jnp.int32)
    r2 = jnp.sum(jnp.where(oh2, excl, 0.0), axis=0, keepdims=True).astype(jnp.int32)
    idx = jnp.where(rowi == 0, e1, jnp.where(rowi == 1, e2, jnp.where(rowi == 2, r1, jnp.where(rowi == 3, r2, 0))))
    gts = jnp.where(rowi == 0, g1, jnp.where(rowi == 1, g2, 0.0))
    return idx, gts, jnp.sum(cnt, axis=1, keepdims=True)


def _router_kernel(res_ref, a_ref, b_ref, wa_ref, wb_ref, g_ref, wr_ref, rb_ref,
                   h_ref, idx_ref, gts_ref, cnt_ref, carry_ref, *, E, sub):
    @pl.when(pl.program_id(0) == 0)
    def _():
        carry_ref[...] = jnp.zeros_like(carry_ref)

    nsub = res_ref.shape[0] // sub
    spans = [slice(j * sub, (j + 1) * sub) for j in range(nsub)]
    xs = [_mix_out(res_ref[sp, :], a_ref[sp, :], b_ref[sp, :], wa_ref, wb_ref) for sp in spans]
    carry = carry_ref[:, 0:1]
    for sp, x in zip(spans, xs):
        h_ref[sp, :] = x
        idx, gts, n_new = _route(x, carry, g_ref[...], (wr_ref[0], wr_ref[1]), rb_ref[:, 0:1], E)
        idx_ref[:, sp] = idx
        gts_ref[:, sp] = gts
        carry = carry + n_new
    carry_ref[...] = jnp.broadcast_to(carry, carry_ref.shape)
    cnt_ref[...] = carry_ref[...]


def _proj_router(res2d, ya, yb, w_out, g, router_w, router_b, tm):
    T, D = res2d.shape
    E = router_w.shape[1]
    assert E == SUBLANES
    wa_n = ya.shape[1]
    wa, wb = w_out[:wa_n], w_out[wa_n:]
    wr = jnp.pad(router_w, ((0, 0), (0, LANES - E)))
    wr_hi = wr.astype(BF16)
    wr_pair = jnp.stack([wr_hi, (wr - wr_hi.astype(F32)).astype(BF16)])
    return pl.pallas_call(
        functools.partial(_router_kernel, E=E, sub=_pick_tile(tm, ROUTER_SUB_TILE)),
        out_shape=[
            jax.ShapeDtypeStruct((T, D), F32),
            jax.ShapeDtypeStruct((E, T), jnp.int32),
            jax.ShapeDtypeStruct((E, T), F32),
            jax.ShapeDtypeStruct((E, LANES), F32),
        ],
        grid=(T // tm,),
        in_specs=[
            pl.BlockSpec((tm, D), lambda i: (i, 0)),
            pl.BlockSpec((tm, wa_n), lambda i: (i, 0)),
            pl.BlockSpec((tm, yb.shape[1]), lambda i: (i, 0)),
            pl.BlockSpec(wa.shape, lambda i: (0, 0)),
            pl.BlockSpec(wb.shape, lambda i: (0, 0)),
            pl.BlockSpec((1, D), lambda i: (0, 0)),
            pl.BlockSpec((2, D, LANES), lambda i: (0, 0, 0)),
            pl.BlockSpec((E, LANES), lambda i: (0, 0)),
        ],
        out_specs=[
            pl.BlockSpec((tm, D), lambda i: (i, 0)),
            pl.BlockSpec((E, tm), lambda i: (0, i)),
            pl.BlockSpec((E, tm), lambda i: (0, i)),
            pl.BlockSpec((E, LANES), lambda i: (0, 0)),
        ],
        scratch_shapes=[pltpu.VMEM((E, LANES), F32)],
        compiler_params=_cparams(("arbitrary",)),
        name="proj_moe_router",
    )(res2d, ya, yb, wa, wb, g.reshape(1, D), wr_pair, jnp.broadcast_to(router_b[:, None], (E, LANES)))


DMA_ISSUE_UNROLL = True


def _dispatch_kernel(pos1_ref, pos2_ref, ends_ref, h_ref, xs_hbm, zero_ref, sem, zsem, *, tm, tg, E):
    i = pl.program_id(0)

    @pl.when(i == 0)
    def _():
        zero_ref[...] = jnp.zeros_like(zero_ref)

        def zero_copy(e):
            start = ends_ref[e] - tg
            return pltpu.make_async_copy(zero_ref, xs_hbm.at[pl.ds(pl.multiple_of(start, tg), tg)], zsem)

        def nonempty(e):
            return ends_ref[e] > (ends_ref[e - 1] if e else 0)

        for e in range(E):
            @pl.when(nonempty(e))
            def _(e=e):
                zero_copy(e).start()
        for e in range(E):
            @pl.when(nonempty(e))
            def _(e=e):
                zero_copy(e).wait()

    def body(r, carry):
        t = i * tm + r
        src = h_ref.at[pl.ds(r, 1)]
        pltpu.make_async_copy(src, xs_hbm.at[pl.ds(pos1_ref[t], 1)], sem).start()
        pltpu.make_async_copy(src, xs_hbm.at[pl.ds(pos2_ref[t], 1)], sem).start()
        return carry
    lax.fori_loop(0, tm, body, 0, unroll=DMA_ISSUE_UNROLL)

    for _ in range(TOP_K):
        pltpu.make_async_copy(h_ref, xs_hbm.at[pl.ds(0, tm)], sem).wait()


def _dispatch(pos1, pos2, ends, h2d, n_rows, tm, tg):
    T, D = h2d.shape
    E = ends.shape[0]
    return pl.pallas_call(
        functools.partial(_dispatch_kernel, tm=tm, tg=tg, E=E),
        out_shape=jax.ShapeDtypeStruct((n_rows, D), h2d.dtype),
        grid_spec=pltpu.PrefetchScalarGridSpec(
            num_scalar_prefetch=3,
            grid=(T // tm,),
            in_specs=[pl.BlockSpec((tm, D), lambda i, p1, p2, en: (i, 0))],
            out_specs=pl.BlockSpec(memory_space=pl.ANY),
            scratch_shapes=[pltpu.VMEM((tg, D), h2d.dtype), pltpu.SemaphoreType.DMA, pltpu.SemaphoreType.DMA],
        ),
        compiler_params=_cparams(("arbitrary",)),
        name="moe_dispatch",
    )(pos1, pos2, ends, h2d)


def _gmm_kernel(te_ref, na_ref, x_ref, g_ref, wg_ref, wu_ref, wd_ref, o_ref, xn_ref, acc_ref):
    i = pl.program_id(0)
    f = pl.program_id(1)
    active = i < na_ref[0]

    @pl.when(f == 0)
    def _():
        acc_ref[...] = jnp.zeros_like(acc_ref)

    @pl.when(active & (f == 0))
    def _():
        xn_ref[...] = _rms_bf16(x_ref[...], g_ref[...])

    @pl.when(active)
    def _():
        x = xn_ref[...]
        a = jnp.dot(x, wg_ref[0].astype(BF16), preferred_element_type=F32)
        u = jnp.dot(x, wu_ref[0].astype(BF16), preferred_element_type=F32)
        hmid = (a * _sigmoid(a) * u).astype(BF16)
        acc_ref[...] += jnp.dot(hmid, wd_ref[0].astype(BF16), preferred_element_type=F32)

    @pl.when(f == pl.num_programs(1) - 1)
    def _():
        o_ref[...] = acc_ref[...]


def _gmm(tile_expert, n_active, xs, g, wg, wu, wd, tm, fc):
    R, D = xs.shape
    F = wg.shape[2]
    nf = F // fc

    def fsel(i, f, na):
        return jnp.where(i < na[0], f, nf - 1)

    return pl.pallas_call(
        _gmm_kernel,
        out_shape=jax.ShapeDtypeStruct((R, D), F32),
        grid_spec=pltpu.PrefetchScalarGridSpec(
            num_scalar_prefetch=2,
            grid=(R // tm, nf),
            in_specs=[
                pl.BlockSpec((tm, D), lambda i, f, te, na: (i, 0)),
                pl.BlockSpec((1, D), lambda i, f, te, na: (0, 0)),
                pl.BlockSpec((1, D, fc), lambda i, f, te, na: (te[i], 0, fsel(i, f, na))),
                pl.BlockSpec((1, D, fc), lambda i, f, te, na: (te[i], 0, fsel(i, f, na))),
                pl.BlockSpec((1, fc, D), lambda i, f, te, na: (te[i], fsel(i, f, na), 0)),
            ],
            out_specs=pl.BlockSpec((tm, D), lambda i, f, te, na: (i, 0)),
            scratch_shapes=[pltpu.VMEM((tm, D), BF16), pltpu.VMEM((tm, D), F32)],
        ),
        compiler_params=_cparams(("parallel", "arbitrary")),
        name="moe_gmm",
    )(tile_expert, n_active, xs, g.reshape(1, D), wg, wu, wd)


def _combine_copies(pos1_ref, pos2_ref, ys_hbm, ya_ref, yb_ref, sems, tile, slot, tm):
    def row_copies(r):
        t = tile * tm + r
        return (pltpu.make_async_copy(ys_hbm.at[pl.ds(pos1_ref[t], 1)], ya_ref.at[slot, pl.ds(r, 1)], sems.at[slot]),
                pltpu.make_async_copy(ys_hbm.at[pl.ds(pos2_ref[t], 1)], yb_ref.at[slot, pl.ds(r, 1)], sems.at[slot]))
    return row_copies


def _combine_kernel(pos1_ref, pos2_ref, h_ref, ga_ref, gb_ref, g_ref, ys_hbm, o_ref, ya_ref, yb_ref, sems, *, tm):
    i = pl.program_id(0)
    n = pl.num_programs(0)
    slot = lax.rem(i, 2)

    def start_tile(tile, slot):
        copies = _combine_copies(pos1_ref, pos2_ref, ys_hbm, ya_ref, yb_ref, sems, tile, slot, tm)

        def body(r, carry):
            a, b = copies(r)
            a.start()
            b.start()
            return carry
        lax.fori_loop(0, tm, body, 0, unroll=DMA_ISSUE_UNROLL)

    @pl.when(i == 0)
    def _():
        start_tile(0, 0)

    @pl.when(i + 1 < n)
    def _():
        start_tile(i + 1, 1 - slot)

    pltpu.make_async_copy(ys_hbm.at[pl.ds(0, tm)], ya_ref.at[slot], sems.at[slot]).wait()
    pltpu.make_async_copy(ys_hbm.at[pl.ds(0, tm)], yb_ref.at[slot], sems.at[slot]).wait()

    h = h_ref[...] + ga_ref[...] * ya_ref[slot] + gb_ref[...] * yb_ref[slot]
    var = jnp.mean(h * h, axis=-1, keepdims=True)
    o_ref[...] = h * lax.rsqrt(var + EPS) * g_ref[...]


def _combine(pos1, pos2, h2d, ga, gb, g, ys, tm):
    T, D = h2d.shape
    row = pl.BlockSpec((tm, D), lambda i, p1, p2: (i, 0))
    colv = pl.BlockSpec((tm, 1), lambda i, p1, p2: (i, 0))
    return pl.pallas_call(
        functools.partial(_combine_kernel, tm=tm),
        out_shape=jax.ShapeDtypeStruct((T, D), F32),
        grid_spec=pltpu.PrefetchScalarGridSpec(
            num_scalar_prefetch=2,
            grid=(T // tm,),
            in_specs=[row, colv, colv, pl.BlockSpec((1, D), lambda i, p1, p2: (0, 0)),
                      pl.BlockSpec(memory_space=pl.ANY)],
            out_specs=row,
            scratch_shapes=[pltpu.VMEM((2, tm, D), F32), pltpu.VMEM((2, tm, D), F32),
                            pltpu.SemaphoreType.DMA((2,))],
        ),
        compiler_params=_cparams(("arbitrary",)),
        name="moe_combine_norm",
    )(pos1, pos2, h2d, ga, gb, g.reshape(1, D), ys)


def _moe_final(res, y_c, y_d, w_out, norm_g, final_g, router_w, router_b, w_gate, w_up, w_down):
    B, S, D = res.shape
    T = B * S
    E = router_w.shape[1]
    h2d, idx, gts, cnt = _proj_router(res.reshape(T, D), y_c.reshape(T, -1), y_d.reshape(T, -1),
                                      w_out.astype(BF16), norm_g, router_w, router_b, _pick_tile(T, 1024))
    e1, e2, r1, r2 = idx[0], idx[1], idx[2], idx[3]
    tm = _pick_tile(T, 1024)
    counts = cnt[:, 0].astype(jnp.int32)
    padded = ((counts + tm - 1) // tm) * tm
    ends = jnp.cumsum(padded)
    offs = ends - padded
    eids = jnp.arange(E, dtype=jnp.int32)[:, None]
    pos1 = jnp.sum(jnp.where(e1[None, :] == eids, offs[:, None], 0), axis=0) + r1
    pos2 = jnp.sum(jnp.where(e2[None, :] == eids, offs[:, None], 0), axis=0) + r2
    n_tiles = (TOP_K * T) // tm + E
    tile_start = jnp.arange(n_tiles, dtype=jnp.int32) * tm
    tile_expert = jnp.minimum(jnp.searchsorted(ends, tile_start, side='right'), E - 1).astype(jnp.int32)
    n_active = (ends[-1] // tm).astype(jnp.int32).reshape(1)
    xs = _dispatch(pos1, pos2, ends.astype(jnp.int32), h2d, n_tiles * tm, _pick_tile(T, 1024), tm)
    ys = _gmm(tile_expert, n_active, xs, norm_g, w_gate, w_up, w_down, tm, _pick_tile(w_gate.shape[2], 512))
    out = _combine(pos1, pos2, h2d, gts[0][:, None], gts[1][:, None], final_g, ys, _pick_tile(T, 256))
    return out.reshape(B, S, D)


def kernel(x, norm_mix_g, norm_ffn_g, norm_final_g, ab_w_in, s5_lambda_re, s5_lambda_im, s5_log_dt, s5_b_re, s5_b_im, s5_c_re, s5_c_im, s5_d, s5_w_glu, mlstm_conv_w, mlstm_conv_b, mlstm_gate_b, mlstm_norm_g, ab_w_out, ffn_w_gate, ffn_w_up, ffn_w_down, cd_w_in, gmlp_norm_g, gmlp_norm_b, gmlp_w_s, gmlp_b_s, rel_bias, cd_w_out, moe_router_w, moe_router_b, moe_w_gate, moe_w_up, moe_w_down):
    B, S, D = x.shape
    y_a, y_b = _mix_ab(x, norm_mix_g[0], ab_w_in[0], s5_lambda_re[0], s5_lambda_im[0], s5_log_dt[0],
                       s5_b_re[0], s5_b_im[0], s5_c_re[0], s5_c_im[0], s5_d[0], s5_w_glu[0],
                       mlstm_conv_w[0], mlstm_conv_b[0], mlstm_gate_b[0], mlstm_norm_g[0])
    h = _proj_ffn(x, y_a, y_b, ab_w_out[0].astype(BF16), norm_ffn_g[0], ffn_w_gate[0].astype(BF16),
                  ffn_w_up[0].astype(BF16), ffn_w_down[0].astype(BF16),
                  _pick_tile(S, 512), _pick_tile(ffn_w_gate.shape[2], 1408))
    y_c, y_d = _mix_cd(h, norm_mix_g[1], cd_w_in[0], gmlp_norm_g[0], gmlp_norm_b[0],
                       gmlp_w_s[0], gmlp_b_s[0], rel_bias)
    return _moe_final(h, y_c, y_d, cd_w_out[0], norm_ffn_g[1], norm_final_g, moe_router_w[0], moe_router_b[0],
                      moe_w_gate[0], moe_w_up[0], moe_w_down[0])
```

```python
import functools
import math

import jax
import jax.numpy as jnp
from jax import lax
from jax.experimental import pallas as pl
from jax.experimental.pallas import tpu as pltpu

F32 = jnp.float32
BF16 = jnp.bfloat16
EPS = 1e-5

LANES = 128
SUBLANES = 8
VMEM_LIMIT_BYTES = 56 * 1024 * 1024

S5_GROUP = 16
S5_STATE = 64
S5_SLAB_GROUPS = 8
MLSTM_HEADS = 4
MLSTM_CHUNK = 128
MLSTM_SEQS_PER_STEP = 2
CONV_WIDTH = 4
GMLP_GROUPS = 4
GMLP_CHUNK = 128
MOBA_HEADS = 4
MOBA_BLOCK = 256
MOBA_TOPK = 3
REL_BUCKETS = 32
REL_MAX_DIST = 128
N_EXPERTS = 8
TOP_K = 2
ROUTER_SUB_TILE = 512
GMM_TAIL_ROW_BLOCK = 256
MOE_ROW_TILE = 1536


def _cparams(sem):
    return pltpu.CompilerParams(dimension_semantics=sem, vmem_limit_bytes=VMEM_LIMIT_BYTES)


def _rms_bf16(x, g):
    var = jnp.mean(x * x, axis=-1, keepdims=True)
    return (x * lax.rsqrt(var + EPS) * g).astype(BF16)


def _gelu(x):
    return jax.nn.gelu(x, approximate=True)


def _sigmoid(x):
    return 1.0 / (1.0 + jnp.exp(-x))


def _norm_proj_kernel(x_ref, g_ref, w_ref, *out_refs, splits):
    hn = _rms_bf16(x_ref[0], g_ref[...])
    for o_ref, (c0, c1) in zip(out_refs, splits):
        r = jnp.dot(hn, w_ref[:, c0:c1], preferred_element_type=F32)
        o_ref[...] = r.reshape(o_ref.shape).astype(o_ref.dtype)


def _norm_proj(x, g, w, outs, tm):
    B, S, D = x.shape
    splits = tuple((c0, c1) for c0, c1, _ in outs)
    out_shape, out_specs = [], []
    for c0, c1, dt in outs:
        n = c1 - c0
        out_shape.append(jax.ShapeDtypeStruct((B, S, n), dt))
        out_specs.append(pl.BlockSpec((1, tm, n), lambda b, i: (b, i, 0)))
    return pl.pallas_call(
        functools.partial(_norm_proj_kernel, splits=splits),
        out_shape=out_shape,
        grid=(B, S // tm),
        in_specs=[
            pl.BlockSpec((1, tm, D), lambda b, i: (b, i, 0)),
            pl.BlockSpec((1, D), lambda b, i: (0, 0)),
            pl.BlockSpec(w.shape, lambda b, i: (0, 0)),
        ],
        out_specs=out_specs,
        compiler_params=_cparams(("parallel", "parallel")),
        name="norm_proj",
    )(x, g.reshape(1, D), w)


def _s5_kernel(u_hbm, bmat_ref, ar_ref, ai_ref, cmat_ref, d_ref, wglu_ref, y_hbm,
               xr_ref, xi_ref, ubuf, ybuf, in_sems, out_sems, *bufs, lc, nb, nslab, sw):
    i = pl.program_id(0)
    n = pl.num_programs(0)
    slot = lax.rem(i, 2)
    W = ubuf.shape[-1]

    def in_copies(step, slot):
        t0 = pl.multiple_of(step * lc, lc)
        return [pltpu.make_async_copy(u_hbm.at[b, pl.ds(t0, lc), :], ubuf.at[slot, :, b, :], in_sems.at[slot])
                for b in range(nb)]

    def out_copies(step, slot):
        t0 = pl.multiple_of(step * lc, lc)
        return [pltpu.make_async_copy(ybuf.at[slot, :, b, :], y_hbm.at[b, pl.ds(t0, lc), :], out_sems.at[slot])
                for b in range(nb)]

    @pl.when(i == 0)
    def _():
        xr_ref[...] = jnp.zeros_like(xr_ref)
        xi_ref[...] = jnp.zeros_like(xi_ref)
        for c in in_copies(0, 0):
            c.start()

    @pl.when(i + 1 < n)
    def _():
        for c in in_copies(i + 1, 1 - slot):
            c.start()

    for c in in_copies(i, slot):
        c.wait()

    uf = ubuf[slot].reshape(lc * nb, W)
    u = uf.astype(BF16)
    xr_all = xr_ref[...]
    xi_all = xi_ref[...]

    def drive(k):
        bufs[k][...] = jnp.dot(u[:, LANES * k:LANES * (k + 1)], bmat_ref[k], preferred_element_type=F32)

    drive(0)
    ys, new_xr, new_xi = [], [], []
    for k in range(nslab):
        if k + 1 < nslab:
            drive(k + 1)
        st_cols = slice(sw * k, sw * (k + 1))
        ar = jnp.broadcast_to(ar_ref[:, st_cols], (nb, sw))
        ai = jnp.broadcast_to(ai_ref[:, st_cols], (nb, sw))
        xr, xi = xr_all[:, st_cols], xi_all[:, st_cols]
        for t in range(lc):
            rows = slice(t * nb, (t + 1) * nb)
            nxr = ar * xr - ai * xi + bufs[k][rows, :sw]
            nxi = ar * xi + ai * xr + bufs[k][rows, sw:]
            bufs[k][rows, :sw] = nxr
            bufs[k][rows, sw:] = nxi
            xr, xi = nxr, nxi
        new_xr.append(xr)
        new_xi.append(xi)
        ys.append(jnp.dot(bufs[k][...].astype(BF16), cmat_ref[k], preferred_element_type=F32))
    xr_ref[...] = jnp.concatenate(new_xr, axis=-1)
    xi_ref[...] = jnp.concatenate(new_xi, axis=-1)
    y = jnp.concatenate(ys, axis=-1)
    y = _gelu(y + d_ref[...] * uf)
    gl = jnp.dot(y.astype(BF16), wglu_ref[...], preferred_element_type=F32)

    @pl.when(i >= 2)
    def _():
        for c in out_copies(i - 2, slot):
            c.wait()

    ybuf[slot] = (y * _sigmoid(gl)).reshape(lc, nb, W)
    for c in out_copies(i, slot):
        c.start()

    @pl.when(i == n - 1)
    def _():
        for c in out_copies(i, slot):
            c.wait()

        @pl.when(n >= 2)
        def _():
            for c in out_copies(i - 1, 1 - slot):
                c.wait()


def _s5_params(lam_re, lam_im, log_dt, b_re, b_im, c_re, c_im):
    G, P = lam_re.shape
    Hc = b_re.shape[-1]
    dt = jnp.exp(log_dt.astype(F32))[:, None]
    mag = jnp.exp(lam_re * dt)
    ar = mag * jnp.cos(lam_im * dt)
    ai = mag * jnp.sin(lam_im * dt)
    den = lam_re * lam_re + lam_im * lam_im
    cr = ((ar - 1.0) * lam_re + ai * lam_im) / den
    ci = (ai * lam_re - (ar - 1.0) * lam_im) / den
    bb_re = cr[..., None] * b_re - ci[..., None] * b_im
    bb_im = cr[..., None] * b_im + ci[..., None] * b_re
    gs = S5_SLAB_GROUPS
    nslab = G // gs
    eye = jnp.eye(gs, dtype=F32)

    def bd_in(b):
        b = b.reshape(nslab, gs, P, Hc)
        return jnp.einsum('kgph,gj->kghjp', b, eye).reshape(nslab, gs * Hc, gs * P)

    def bd_out(c):
        c = c.reshape(nslab, gs, Hc, P)
        return jnp.einsum('kghp,gj->kgpjh', c, eye).reshape(nslab, gs * P, gs * Hc)

    bmat = jnp.concatenate([bd_in(bb_re), bd_in(bb_im)], axis=-1).astype(BF16)
    cmat = jnp.concatenate([bd_out(c_re), -bd_out(c_im)], axis=1).astype(BF16)
    return bmat, ar.reshape(1, G * P), ai.reshape(1, G * P), cmat


def _s5(u, bmat, ar, ai, cmat, d_skip, w_glu, lc):
    nb, S, W = u.shape
    assert nb % SUBLANES == 0
    nslab = bmat.shape[0]
    sw = bmat.shape[2] // 2
    rows = lc * nb
    return pl.pallas_call(
        functools.partial(_s5_kernel, lc=lc, nb=nb, nslab=nslab, sw=sw),
        out_shape=jax.ShapeDtypeStruct((nb, S, W), F32),
        grid=(S // lc,),
        in_specs=[
            pl.BlockSpec(memory_space=pl.ANY),
            pl.BlockSpec(bmat.shape, lambda i: (0, 0, 0)),
            pl.BlockSpec(ar.shape, lambda i: (0, 0)),
            pl.BlockSpec(ai.shape, lambda i: (0, 0)),
            pl.BlockSpec(cmat.shape, lambda i: (0, 0, 0)),
            pl.BlockSpec((1, W), lambda i: (0, 0)),
            pl.BlockSpec(w_glu.shape, lambda i: (0, 0)),
        ],
        out_specs=pl.BlockSpec(memory_space=pl.ANY),
        scratch_shapes=[pltpu.VMEM((nb, sw * nslab), F32), pltpu.VMEM((nb, sw * nslab), F32),
                        pltpu.VMEM((2, lc, nb, W), F32), pltpu.VMEM((2, lc, nb, W), F32),
                        pltpu.SemaphoreType.DMA((2,)), pltpu.SemaphoreType.DMA((2,))]
        + [pltpu.VMEM((rows, 2 * sw), F32) for _ in range(nslab)],
        compiler_params=_cparams(("arbitrary",)),
        name="s5",
    )(u, bmat, ar, ai, cmat, d_skip.reshape(1, W), w_glu)


def _mlstm_chunk(x, vb, o_pre, g, tail, cx_all, m_row, cw, cb, ng, *, L, H, Dh):
    W = H * Dh
    halo = SUBLANES
    log_scale = -0.5 * math.log(Dh)

    row = lax.broadcasted_iota(jnp.int32, (L, L), 0)
    col = lax.broadcasted_iota(jnp.int32, (L, L), 1)
    conv = cb + cw[0:1, :] * x.astype(F32)
    corr = jnp.zeros((halo, 2 * W), F32)
    for j in range(1, CONV_WIDTH):
        shift = (row - col == j).astype(BF16)
        conv = conv + cw[j:j + 1, :] * jnp.dot(shift, x, preferred_element_type=F32)
        corr = corr + cw[j:j + 1, :] * tail[halo - j:2 * halo - j, :]
    conv = jnp.concatenate([conv[:halo] + corr, conv[halo:]], axis=0)
    new_tail = x[L - halo:, :].astype(F32)
    qk = (conv * _sigmoid(conv)).astype(BF16)
    q = qk[:, :W]
    kb = qk[:, W:]
    ones = jnp.ones((L, Dh), BF16)
    sq_ones = jnp.ones((Dh, Dh), BF16)

    log_i = g[:, :LANES]
    f_pre = g[:, LANES:]
    logf = jnp.minimum(f_pre, 0.0) - jnp.log(1.0 + jnp.exp(-jnp.abs(f_pre)))
    causal = row >= col
    bcum = jnp.dot(causal.astype(F32), logf, preferred_element_type=F32, precision=lax.Precision.HIGHEST)
    w_cols = log_i - bcum
    trow = lax.broadcasted_iota(jnp.int32, (L, LANES), 0)
    cmax = w_cols
    k = 1
    while k < L:
        cmax = jnp.maximum(cmax, jnp.where(trow >= k, pltpu.roll(cmax, k, axis=0), -jnp.inf))
        k *= 2
    m_inter = bcum + m_row
    m_t = jnp.maximum(bcum + cmax, m_inter)
    u_cols = bcum - m_t + log_scale
    wi_cols = jnp.exp(m_inter - m_t)
    em_cols = jnp.exp(-m_t)
    b_last = bcum[L - 1:L, :]
    m_loc = b_last + cmax[L - 1:L, :]
    m_new = jnp.maximum(b_last + m_row, m_loc)
    s_prev = jnp.exp(b_last + m_row - m_new)
    s_loc = jnp.exp(m_loc - m_new)
    w_rows = w_cols.T

    heads = range(H)
    hs = [slice(h * Dh, (h + 1) * Dh) for h in heads]
    hl = [slice(h, h + 1) for h in heads]
    nt = (((1,), (1,)), ((), ()))
    vext = [jnp.concatenate([vb[:, hs[h]], ones], axis=1) for h in heads]
    qk_t = [lax.dot_general(q[:, hs[h]], kb[:, hs[h]], nt, preferred_element_type=F32) for h in heads]
    r2 = [jnp.dot(q[:, hs[h]], cx_all[h].astype(BF16), preferred_element_type=F32) for h in heads]
    decay = [jnp.exp(jnp.where(causal, u_cols[:, hl[h]] + w_rows[hl[h], :], -jnp.inf)) for h in heads]
    s = [(qk_t[h] * decay[h]).astype(BF16) for h in heads]
    r1 = [jnp.dot(s[h], vext[h], preferred_element_type=F32) for h in heads]
    hh = []
    for h in heads:
        wi = jnp.broadcast_to(wi_cols[:, hl[h]], (L, Dh))
        num = r1[h][:, :Dh] + wi * r2[h][:, :Dh]
        den = r1[h][:, Dh:] + wi * r2[h][:, Dh:]
        hh.append(num / jnp.maximum(jnp.abs(den), jnp.broadcast_to(em_cols[:, hl[h]], (L, Dh))))
    msq = [jnp.dot((hh[h] * hh[h]).astype(BF16), sq_ones, preferred_element_type=F32) * (1.0 / Dh) for h in heads]
    ys = [(_sigmoid(o_pre[:, hs[h]].astype(F32)) * hh[h] * lax.rsqrt(msq[h] + EPS) * ng[:, hs[h]]).astype(BF16)
          for h in heads]
    kwt = [(kb[:, hs[h]].astype(F32).T
            * jnp.exp(b_last[:, hl[h]] + w_rows[hl[h], :] - m_loc[:, hl[h]] + log_scale)).astype(BF16) for h in heads]
    cx_loc = [jnp.dot(kwt[h], vext[h], preferred_element_type=F32) for h in heads]
    new_cx = [s_prev[:, hl[h]] * cx_all[h] + s_loc[:, hl[h]] * cx_loc[h] for h in heads]
    return ys, new_tail, new_cx, m_new


def _mlstm_kernel(qk_ref, v_ref, o_ref, gt_ref, cw_ref, cb_ref, gb_ref, ng_ref, y_ref,
                  tail_ref, c_ref, m_ref, *, L, H, Dh, BB):
    @pl.when(pl.program_id(1) == 0)
    def _():
        tail_ref[...] = jnp.zeros_like(tail_ref)
        c_ref[...] = jnp.zeros_like(c_ref)
        m_ref[...] = jnp.zeros_like(m_ref)

    tails = [tail_ref[bb] for bb in range(BB)]
    cxs = [[c_ref[bb, h] for h in range(H)] for bb in range(BB)]
    m_rows = [m_ref[bb, 0:1, :] for bb in range(BB)]
    results = []
    for bb in range(BB):
        g = gt_ref[bb] + gb_ref[...]
        results.append(_mlstm_chunk(qk_ref[bb], v_ref[bb].astype(BF16), o_ref[bb], g, tails[bb], cxs[bb],
                                    m_rows[bb], cw_ref[...], cb_ref[...], ng_ref[...], L=L, H=H, Dh=Dh))
    for bb, (ys, new_tail, new_cx, m_new) in enumerate(results):
        for h in range(H):
            y_ref[bb, :, h * Dh:(h + 1) * Dh] = ys[h]
            c_ref[bb, h] = new_cx[h]
        tail_ref[bb, 0:SUBLANES, :] = new_tail
        m_ref[bb, 0:1, :] = m_new


def _mlstm(qk_pre, v, o_pre, gates, conv_w, conv_b, gate_b_pad, norm_g):
    B, S, W2 = qk_pre.shape
    W = W2 // 2
    H, L = MLSTM_HEADS, MLSTM_CHUNK
    Dh = W // H
    BB = _pick_tile(B, MLSTM_SEQS_PER_STEP)
    return pl.pallas_call(
        functools.partial(_mlstm_kernel, L=L, H=H, Dh=Dh, BB=BB),
        out_shape=jax.ShapeDtypeStruct((B, S, W), BF16),
        grid=(B // BB, S // L),
        in_specs=[
            pl.BlockSpec((BB, L, W2), lambda b, c: (b, c, 0)),
            pl.BlockSpec((BB, L, W), lambda b, c: (b, c, 0)),
            pl.BlockSpec((BB, L, W), lambda b, c: (b, c, 0)),
            pl.BlockSpec((BB, L, 2 * LANES), lambda b, c: (b, c, 0)),
            pl.BlockSpec((CONV_WIDTH, W2), lambda b, c: (0, 0)),
            pl.BlockSpec((1, W2), lambda b, c: (0, 0)),
            pl.BlockSpec((1, 2 * LANES), lambda b, c: (0, 0)),
            pl.BlockSpec((1, W), lambda b, c: (0, 0)),
        ],
        out_specs=pl.BlockSpec((BB, L, W), lambda b, c: (b, c, 0)),
        scratch_shapes=[
            pltpu.VMEM((BB, 2 * SUBLANES, W2), F32),
            pltpu.VMEM((BB, H, Dh, 2 * Dh), F32),
            pltpu.VMEM((BB, SUBLANES, LANES), F32),
        ],
        compiler_params=_cparams(("parallel", "arbitrary")),
        name="mlstm",
    )(qk_pre, v, o_pre, gates, conv_w, conv_b.reshape(1, W2), gate_b_pad, norm_g.reshape(1, W))


def _mix_out(res, a, b, wa_ref, wb_ref):
    acc = jnp.dot(a.astype(BF16), wa_ref[...], preferred_element_type=F32)
    return res + acc + jnp.dot(b.astype(BF16), wb_ref[...], preferred_element_type=F32)


def _proj_ffn_kernel(res_ref, a_ref, b_ref, wa_ref, wb_ref, g_ref, wg_ref, wu_ref, wd_ref, o_ref,
                     hn_ref, acc_ref):
    f = pl.program_id(2)

    @pl.when(f == 0)
    def _():
        h = _mix_out(res_ref[0], a_ref[0], b_ref[0], wa_ref, wb_ref)
        hn_ref[...] = _rms_bf16(h, g_ref[...])
        acc_ref[...] = h

    hn = hn_ref[...]
    a = jnp.dot(hn, wg_ref[...], preferred_element_type=F32)
    u = jnp.dot(hn, wu_ref[...], preferred_element_type=F32)
    hmid = (a * _sigmoid(a) * u).astype(BF16)
    acc_ref[...] += jnp.dot(hmid, wd_ref[...], preferred_element_type=F32)

    @pl.when(f == pl.num_programs(2) - 1)
    def _():
        o_ref[0] = acc_ref[...]


def _proj_ffn(res, ya, yb, w_out, g, wg, wu, wd, tm, fc):
    B, S, D = res.shape
    wa_n = ya.shape[2]
    wa, wb = w_out[:wa_n], w_out[wa_n:]
    F = wg.shape[1]
    return pl.pallas_call(
        _proj_ffn_kernel,
        out_shape=jax.ShapeDtypeStruct((B, S, D), F32),
        grid=(B, S // tm, F // fc),
        in_specs=[
            pl.BlockSpec((1, tm, D), lambda b, i, f: (b, i, 0)),
            pl.BlockSpec((1, tm, wa_n), lambda b, i, f: (b, i, 0)),
            pl.BlockSpec((1, tm, wb.shape[0]), lambda b, i, f: (b, i, 0)),
            pl.BlockSpec(wa.shape, lambda b, i, f: (0, 0)),
            pl.BlockSpec(wb.shape, lambda b, i, f: (0, 0)),
            pl.BlockSpec((1, D), lambda b, i, f: (0, 0)),
            pl.BlockSpec((D, fc), lambda b, i, f: (0, f)),
            pl.BlockSpec((D, fc), lambda b, i, f: (0, f)),
            pl.BlockSpec((fc, D), lambda b, i, f: (f, 0)),
        ],
        out_specs=pl.BlockSpec((1, tm, D), lambda b, i, f: (b, i, 0)),
        scratch_shapes=[pltpu.VMEM((tm, D), BF16), pltpu.VMEM((tm, D), F32)],
        compiler_params=_cparams(("parallel", "parallel", "arbitrary")),
        name="proj_ffn_swiglu",
    )(res, ya, yb, wa, wb, g.reshape(1, D), wg, wu, wd)


def _pick_tile(n, target):
    t = min(n, target)
    while n % t:
        t //= 2
    return t


def _mix_ab(h, norm_g, w_in, lam_re, lam_im, log_dt, b_re, b_im, c_re, c_im, d_skip, w_glu,
            conv_w, conv_b, gate_b, mlstm_norm_g):
    B, S, D = h.shape
    s5w = lam_re.shape[0] * S5_GROUP
    mw = mlstm_norm_g.shape[0]
    c_qk, c_v, c_o, c_if = s5w, s5w + 2 * mw, s5w + 3 * mw, s5w + 4 * mw
    nh = (w_in.shape[1] - c_if) // 2
    lane_pad = ((0, 0), (0, LANES - nh))
    w_pad = jnp.concatenate([w_in[:, :c_if], jnp.pad(w_in[:, c_if:c_if + nh], lane_pad),
                             jnp.pad(w_in[:, c_if + nh:], lane_pad)], axis=1).astype(BF16)
    tm = _pick_tile(S, 512)
    u_s5, qk_pre, v, o_pre, gates = _norm_proj(
        h, norm_g, w_pad,
        [(0, c_qk, F32), (c_qk, c_v, BF16), (c_v, c_o, BF16), (c_o, c_if, BF16), (c_if, c_if + 2 * LANES, F32)],
        tm)
    bmat, ar, ai, cmat = _s5_params(lam_re, lam_im, log_dt, b_re, b_im, c_re, c_im)
    y_a = _s5(u_s5, bmat, ar, ai, cmat, d_skip, w_glu.astype(BF16), lc=_pick_tile(S, 32))
    gate_b_pad = jnp.concatenate([jnp.pad(gate_b[:nh], (0, LANES - nh)),
                                  jnp.pad(gate_b[nh:], (0, LANES - nh))]).reshape(1, 2 * LANES)
    y_b = _mlstm(qk_pre, v, o_pre, gates, conv_w, conv_b, gate_b_pad, mlstm_norm_g)
    return y_a, y_b


def _gmlp_kernel(u_ref, v_ref, ng_ref, nb_ref, ws_ref, bs_ref, y_ref, *, L, G, Dg, nchunk):
    u = _gelu(u_ref[0].astype(F32))
    v = _gelu(v_ref[0].astype(F32))
    mu = jnp.mean(v, axis=-1, keepdims=True)
    vc = v - mu
    var = jnp.mean(vc * vc, axis=-1, keepdims=True)
    vn = (vc * lax.rsqrt(var + EPS) * ng_ref[...] + nb_ref[...]).astype(BF16)
    for c in range(nchunk):
        rows = slice(c * L, (c + 1) * L)
        for g in range(G):
            cols = slice(g * Dg, (g + 1) * Dg)
            s = jnp.dot(ws_ref[g], vn[rows, cols], preferred_element_type=F32) + bs_ref[:, g:g + 1]
            y_ref[0, rows, cols] = (u[rows, cols] * s).astype(y_ref.dtype)


def _gmlp(u_pre, v_pre, norm_g, norm_b, w_s, b_s, tm):
    B, S, W = u_pre.shape
    G, L = GMLP_GROUPS, GMLP_CHUNK
    ws = (w_s * jnp.tril(jnp.ones((L, L), w_s.dtype))).astype(BF16)
    bs = jnp.pad(b_s.T, ((0, 0), (0, LANES - G)))
    return pl.pallas_call(
        functools.partial(_gmlp_kernel, L=L, G=G, Dg=W // G, nchunk=tm // L),
        out_shape=jax.ShapeDtypeStruct((B, S, W), BF16),
        grid=(B, S // tm),
        in_specs=[
            pl.BlockSpec((1, tm, W), lambda b, i: (b, i, 0)),
            pl.BlockSpec((1, tm, W), lambda b, i: (b, i, 0)),
            pl.BlockSpec((1, W), lambda b, i: (0, 0)),
            pl.BlockSpec((1, W), lambda b, i: (0, 0)),
            pl.BlockSpec((G, L, L), lambda b, i: (0, 0, 0)),
            pl.BlockSpec((L, LANES), lambda b, i: (0, 0)),
        ],
        out_specs=pl.BlockSpec((1, tm, W), lambda b, i: (b, i, 0)),
        compiler_params=_cparams(("parallel", "parallel")),
        name="gmlp",
    )(u_pre, v_pre, norm_g.reshape(1, W), norm_b.reshape(1, W), ws, bs)


NEG_BIG = -1e30


def _moba_kernel(q_ref, k_ref, v_ref, t0_ref, t1_ref, bfar_ref, y_ref, vext_ref, *, BL, NB, Dh, topk):
    nt = (((1,), (1,)), ((), ()))
    scale = 1.0 / math.sqrt(Dh)
    vext_ref[:, :Dh] = v_ref[0]
    vext_ref[:, Dh:] = jnp.ones((NB * BL, Dh), BF16)
    bfar = bfar_ref[0][:, 0:1]
    kmean = jnp.concatenate(
        [jnp.mean(k_ref[0, n * BL:(n + 1) * BL, :].astype(F32), axis=0, keepdims=True) for n in range(NB)],
        axis=0)
    nrow = lax.broadcasted_iota(jnp.int32, (NB, BL), 0)
    eye = (lax.broadcasted_iota(jnp.int32, (BL, BL), 0) ==
           lax.broadcasted_iota(jnp.int32, (BL, BL), 1)).astype(BF16)

    def logits(qi):
        rows = slice(qi * BL, (qi + 1) * BL)
        qf = q_ref[0, rows, :].astype(F32)
        qs = (qf * scale).astype(BF16)
        neg_cols = None
        if qi > topk:
            gate = lax.dot_general(kmean, qf, nt, preferred_element_type=F32,
                                   precision=lax.Precision.HIGHEST)
            rank = jnp.zeros((NB, BL), F32)
            for m in range(qi):
                gm = gate[m:m + 1, :]
                rank = rank + jnp.where(gm > gate, 1.0, jnp.where(gm == gate, jnp.where(nrow > m, 1.0, 0.0), 0.0))
            sel_rows = jnp.where(nrow < qi, jnp.where(rank < topk, 1.0, 0.0), 0.0)
            sel_pad = jnp.concatenate([sel_rows, jnp.zeros((LANES - NB, BL), F32)], axis=0).astype(BF16)
            sel_cols = lax.dot_general(eye, sel_pad, nt, preferred_element_type=F32)
            neg_cols = (1.0 - sel_cols) * NEG_BIG

        m_tile = None
        pieces = []
        for kb in range(qi + 1):
            s = lax.dot_general(qs, k_ref[0, kb * BL:(kb + 1) * BL, :], nt, preferred_element_type=F32)
            if kb == qi:
                s = s + t0_ref[0]
            elif kb == qi - 1:
                s = s + t1_ref[0]
                if neg_cols is not None:
                    s = s + neg_cols[:, kb:kb + 1]
            else:
                s = s + (bfar if neg_cols is None else bfar + neg_cols[:, kb:kb + 1])
            pieces.append(s)
            for c in range(BL // LANES):
                piece = s[:, c * LANES:(c + 1) * LANES]
                m_tile = piece if m_tile is None else jnp.maximum(m_tile, piece)
        return pieces, m_tile

    nxt = logits(0)
    for qi in range(NB):
        pieces, m_tile = nxt
        if qi + 1 < NB:
            nxt = logits(qi + 1)
        m = jnp.max(m_tile, axis=-1, keepdims=True)
        p = jnp.concatenate([jnp.exp(s - m).astype(BF16) for s in pieces], axis=1)
        acc = jnp.dot(p, vext_ref[:(qi + 1) * BL, :], preferred_element_type=F32)
        y_ref[0, qi * BL:(qi + 1) * BL, :] = (acc[:, :Dh] / acc[:, Dh:]).astype(y_ref.dtype)


def _rel_bucket(n):
    max_exact = REL_BUCKETS // 2
    nf = jnp.maximum(n, 1).astype(F32)
    large = max_exact + (jnp.log(nf / max_exact) / math.log(REL_MAX_DIST / max_exact)
                         * (REL_BUCKETS - max_exact)).astype(jnp.int32)
    large = jnp.minimum(large, REL_BUCKETS - 1)
    return jnp.where(n < max_exact, n, large)


def _moba(qkv, rel_bias):
    B, S, W3 = qkv.shape
    H, BL = MOBA_HEADS, MOBA_BLOCK
    W = W3 // 3
    Dh = W // H
    NB = S // BL
    assert BL + 1 >= REL_MAX_DIST and Dh == LANES and S % BL == 0
    i = jnp.arange(BL, dtype=jnp.int32)[:, None]
    j = jnp.arange(BL, dtype=jnp.int32)[None, :]
    buckets = jnp.arange(REL_BUCKETS, dtype=jnp.int32)

    def bias_tile(dist):
        onehot = (_rel_bucket(dist)[:, :, None] == buckets).astype(F32)
        return jnp.einsum('ijb,bh->hij', onehot, rel_bias, precision=lax.Precision.HIGHEST)

    t0 = jnp.where(i >= j, bias_tile(jnp.maximum(i - j, 0)), NEG_BIG)
    t1 = bias_tile(BL + i - j)
    bfar = jnp.broadcast_to(rel_bias.T[:, REL_BUCKETS - 1][:, None, None], (H, 1, LANES))
    return pl.pallas_call(
        functools.partial(_moba_kernel, BL=BL, NB=NB, Dh=Dh, topk=min(MOBA_TOPK, NB)),
        out_shape=jax.ShapeDtypeStruct((B, S, W), BF16),
        grid=(B, H),
        in_specs=[
            pl.BlockSpec((1, S, Dh), lambda b, h: (b, 0, h)),
            pl.BlockSpec((1, S, Dh), lambda b, h: (b, 0, H + h)),
            pl.BlockSpec((1, S, Dh), lambda b, h: (b, 0, 2 * H + h)),
            pl.BlockSpec((1, BL, BL), lambda b, h: (h, 0, 0)),
            pl.BlockSpec((1, BL, BL), lambda b, h: (h, 0, 0)),
            pl.BlockSpec((1, 1, LANES), lambda b, h: (h, 0, 0)),
        ],
        out_specs=pl.BlockSpec((1, S, Dh), lambda b, h: (b, 0, h)),
        scratch_shapes=[pltpu.VMEM((S, 2 * Dh), BF16)],
        compiler_params=_cparams(("parallel", "parallel")),
        name="moba",
    )(qkv, qkv, qkv, t0, t1, bfar)


def _mix_cd(h, norm_g, w_in, gmlp_norm_g, gmlp_norm_b, gmlp_w_s, gmlp_b_s, rel_bias):
    B, S, D = h.shape
    gw = gmlp_norm_g.shape[0]
    tm = _pick_tile(S, 512)
    u_pre, v_pre, qkv = _norm_proj(
        h, norm_g, w_in.astype(BF16),
        [(0, gw, BF16), (gw, 2 * gw, BF16), (2 * gw, w_in.shape[1], BF16)], tm)
    y_c = _gmlp(u_pre, v_pre, gmlp_norm_g, gmlp_norm_b, gmlp_w_s, gmlp_b_s, tm)
    return y_c, _moba(qkv, rel_bias)


def _route(x, carry, g, wr, rb, E):
    tm = x.shape[0]
    var = jnp.mean(x * x, axis=-1, keepdims=True)
    hn = x * lax.rsqrt(var + EPS) * g
    hn_hi = hn.astype(BF16)
    hn_lo = (hn - hn_hi.astype(F32)).astype(BF16)
    wr_hi, wr_lo = wr
    lg = (jnp.dot(hn_hi, wr_hi, preferred_element_type=F32) + jnp.dot(hn_hi, wr_lo, preferred_element_type=F32)
          + jnp.dot(hn_lo, wr_hi, preferred_element_type=F32))
    logits = lg.T[:E, :] + rb
    rowi = lax.broadcasted_iota(jnp.int32, (E, tm), 0)
    v1 = jnp.max(logits, axis=0, keepdims=True)
    e1 = jnp.min(jnp.where(logits == v1, rowi, E), axis=0, keepdims=True)
    masked = jnp.where(rowi == e1, -jnp.inf, logits)
    v2 = jnp.max(masked, axis=0, keepdims=True)
    e2 = jnp.min(jnp.where(masked == v2, rowi, E), axis=0, keepdims=True)
    ex = jnp.exp(v2 - v1)
    g1 = 1.0 / (1.0 + ex)
    g2 = ex / (1.0 + ex)
    oh1 = rowi == e1
    oh2 = rowi == e2
    cnt = jnp.where(oh1, 1.0, jnp.where(oh2, 1.0, 0.0))
    before = (lax.broadcasted_iota(jnp.int32, (tm, tm), 0) <
              lax.broadcasted_iota(jnp.int32, (tm, tm), 1)).astype(BF16)
    excl = jnp.dot(cnt.astype(BF16), before, preferred_element_type=F32) + carry
    r1 = jnp.sum(jnp.where(oh1, excl, 0.0), axis=0, keepdims=True).astype(jnp.int32)
    r2 = jnp.sum(jnp.where(oh2, excl, 0.0), axis=0, keepdims=True).astype(jnp.int32)
    idx = jnp.where(rowi == 0, e1, jnp.where(rowi == 1, e2, jnp.where(rowi == 2, r1, jnp.where(rowi == 3, r2, 0))))
    gts = jnp.where(rowi == 0, g1, jnp.where(rowi == 1, g2, 0.0))
    return idx, gts, jnp.sum(cnt, axis=1, keepdims=True)


def _router_kernel(res_ref, a_ref, b_ref, wa_ref, wb_ref, g_ref, wr_ref, rb_ref,
                   h_ref, idx_ref, gts_ref, cnt_ref, carry_ref, *, E, sub):
    @pl.when(pl.program_id(0) == 0)
    def _():
        carry_ref[...] = jnp.zeros_like(carry_ref)

    nsub = res_ref.shape[0] // sub
    spans = [slice(j * sub, (j + 1) * sub) for j in range(nsub)]
    xs = [_mix_out(res_ref[sp, :], a_ref[sp, :], b_ref[sp, :], wa_ref, wb_ref) for sp in spans]
    carry = carry_ref[:, 0:1]
    for sp, x in zip(spans, xs):
        h_ref[sp, :] = x
        idx, gts, n_new = _route(x, carry, g_ref[...], (wr_ref[0], wr_ref[1]), rb_ref[:, 0:1], E)
        idx_ref[:, sp] = idx
        gts_ref[:, sp] = gts
        carry = carry + n_new
    carry_ref[...] = jnp.broadcast_to(carry, carry_ref.shape)
    cnt_ref[...] = carry_ref[...]


def _proj_router(res2d, ya, yb, w_out, g, router_w, router_b, tm):
    T, D = res2d.shape
    E = router_w.shape[1]
    assert E == SUBLANES
    wa_n = ya.shape[1]
    wa, wb = w_out[:wa_n], w_out[wa_n:]
    wr = jnp.pad(router_w, ((0, 0), (0, LANES - E)))
    wr_hi = wr.astype(BF16)
    wr_pair = jnp.stack([wr_hi, (wr - wr_hi.astype(F32)).astype(BF16)])
    return pl.pallas_call(
        functools.partial(_router_kernel, E=E, sub=_pick_tile(tm, ROUTER_SUB_TILE)),
        out_shape=[
            jax.ShapeDtypeStruct((T, D), F32),
            jax.ShapeDtypeStruct((E, T), jnp.int32),
            jax.ShapeDtypeStruct((E, T), F32),
            jax.ShapeDtypeStruct((E, LANES), F32),
        ],
        grid=(T // tm,),
        in_specs=[
            pl.BlockSpec((tm, D), lambda i: (i, 0)),
            pl.BlockSpec((tm, wa_n), lambda i: (i, 0)),
            pl.BlockSpec((tm, yb.shape[1]), lambda i: (i, 0)),
            pl.BlockSpec(wa.shape, lambda i: (0, 0)),
            pl.BlockSpec(wb.shape, lambda i: (0, 0)),
            pl.BlockSpec((1, D), lambda i: (0, 0)),
            pl.BlockSpec((2, D, LANES), lambda i: (0, 0, 0)),
            pl.BlockSpec((E, LANES), lambda i: (0, 0)),
        ],
        out_specs=[
            pl.BlockSpec((tm, D), lambda i: (i, 0)),
            pl.BlockSpec((E, tm), lambda i: (0, i)),
            pl.BlockSpec((E, tm), lambda i: (0, i)),
            pl.BlockSpec((E, LANES), lambda i: (0, 0)),
        ],
        scratch_shapes=[pltpu.VMEM((E, LANES), F32)],
        compiler_params=_cparams(("arbitrary",)),
        name="proj_moe_router",
    )(res2d, ya, yb, wa, wb, g.reshape(1, D), wr_pair, jnp.broadcast_to(router_b[:, None], (E, LANES)))


DMA_ISSUE_UNROLL = True


def _dispatch_kernel(pos1_ref, pos2_ref, ends_ref, h_ref, xs_hbm, zero_ref, sem, zsem, *, tm, tg, E):
    i = pl.program_id(0)

    @pl.when(i == 0)
    def _():
        zero_ref[...] = jnp.zeros_like(zero_ref)

        def zero_copy(e):
            start = ends_ref[e] - tg
            return pltpu.make_async_copy(zero_ref, xs_hbm.at[pl.ds(pl.multiple_of(start, tg), tg)], zsem)

        def nonempty(e):
            return ends_ref[e] > (ends_ref[e - 1] if e else 0)

        for e in range(E):
            @pl.when(nonempty(e))
            def _(e=e):
                zero_copy(e).start()
        for e in range(E):
            @pl.when(nonempty(e))
            def _(e=e):
                zero_copy(e).wait()

    def body(r, carry):
        t = i * tm + r
        src = h_ref.at[pl.ds(r, 1)]
        pltpu.make_async_copy(src, xs_hbm.at[pl.ds(pos1_ref[t], 1)], sem).start()
        pltpu.make_async_copy(src, xs_hbm.at[pl.ds(pos2_ref[t], 1)], sem).start()
        return carry
    lax.fori_loop(0, tm, body, 0, unroll=DMA_ISSUE_UNROLL)

    for _ in range(TOP_K):
        pltpu.make_async_copy(h_ref, xs_hbm.at[pl.ds(0, tm)], sem).wait()


def _dispatch(pos1, pos2, ends, h2d, n_rows, tm, tg):
    T, D = h2d.shape
    E = ends.shape[0]
    return pl.pallas_call(
        functools.partial(_dispatch_kernel, tm=tm, tg=tg, E=E),
        out_shape=jax.ShapeDtypeStruct((n_rows, D), h2d.dtype),
        grid_spec=pltpu.PrefetchScalarGridSpec(
            num_scalar_prefetch=3,
            grid=(T // tm,),
            in_specs=[pl.BlockSpec((tm, D), lambda i, p1, p2, en: (i, 0))],
            out_specs=pl.BlockSpec(memory_space=pl.ANY),
            scratch_shapes=[pltpu.VMEM((tg, D), h2d.dtype), pltpu.SemaphoreType.DMA, pltpu.SemaphoreType.DMA],
        ),
        compiler_params=_cparams(("arbitrary",)),
        name="moe_dispatch",
    )(pos1, pos2, ends, h2d)


def _gmm_kernel(te_ref, tv_ref, x_ref, g_ref, wg_ref, wu_ref, wd_ref, o_ref, xn_ref, acc_ref, *, rb):
    i = pl.program_id(0)
    f = pl.program_id(1)
    tm = xn_ref.shape[0]
    valid = tv_ref[i]

    @pl.when(f == 0)
    def _():
        acc_ref[...] = jnp.zeros_like(acc_ref)

    @pl.when((valid > 0) & (f == 0))
    def _():
        xn_ref[...] = _rms_bf16(x_ref[...], g_ref[...])

    def swiglu_rows(rows):
        x = xn_ref[rows, :]
        a = jnp.dot(x, wg_ref[0].astype(BF16), preferred_element_type=F32)
        u = jnp.dot(x, wu_ref[0].astype(BF16), preferred_element_type=F32)
        hmid = (a * _sigmoid(a) * u).astype(BF16)
        acc_ref[rows, :] += jnp.dot(hmid, wd_ref[0].astype(BF16), preferred_element_type=F32)

    @pl.when(valid == tm)
    def _():
        swiglu_rows(slice(None))

    for r in range(tm // rb):
        @pl.when((valid < tm) & (valid > r * rb))
        def _(r=r):
            swiglu_rows(slice(r * rb, (r + 1) * rb))

    @pl.when(f == pl.num_programs(1) - 1)
    def _():
        o_ref[...] = acc_ref[...]


def _gmm(tile_expert, tile_valid, xs, g, wg, wu, wd, tm, fc):
    R, D = xs.shape
    F = wg.shape[2]
    nf = F // fc

    def fsel(i, f, tv):
        return jnp.where(tv[i] > 0, f, nf - 1)

    return pl.pallas_call(
        functools.partial(_gmm_kernel, rb=_pick_tile(tm, GMM_TAIL_ROW_BLOCK)),
        out_shape=jax.ShapeDtypeStruct((R, D), F32),
        grid_spec=pltpu.PrefetchScalarGridSpec(
            num_scalar_prefetch=2,
            grid=(R // tm, nf),
            in_specs=[
                pl.BlockSpec((tm, D), lambda i, f, te, na: (i, 0)),
                pl.BlockSpec((1, D), lambda i, f, te, na: (0, 0)),
                pl.BlockSpec((1, D, fc), lambda i, f, te, na: (te[i], 0, fsel(i, f, na))),
                pl.BlockSpec((1, D, fc), lambda i, f, te, na: (te[i], 0, fsel(i, f, na))),
                pl.BlockSpec((1, fc, D), lambda i, f, te, na: (te[i], fsel(i, f, na), 0)),
            ],
            out_specs=pl.BlockSpec((tm, D), lambda i, f, te, na: (i, 0)),
            scratch_shapes=[pltpu.VMEM((tm, D), BF16), pltpu.VMEM((tm, D), F32)],
        ),
        compiler_params=_cparams(("parallel", "arbitrary")),
        name="moe_gmm",
    )(tile_expert, tile_valid, xs, g.reshape(1, D), wg, wu, wd)


def _combine_copies(pos1_ref, pos2_ref, ys_hbm, ya_ref, yb_ref, sems, tile, slot, tm):
    def row_copies(r):
        t = tile * tm + r
        return (pltpu.make_async_copy(ys_hbm.at[pl.ds(pos1_ref[t], 1)], ya_ref.at[slot, pl.ds(r, 1)], sems.at[slot]),
                pltpu.make_async_copy(ys_hbm.at[pl.ds(pos2_ref[t], 1)], yb_ref.at[slot, pl.ds(r, 1)], sems.at[slot]))
    return row_copies


def _combine_kernel(pos1_ref, pos2_ref, h_ref, ga_ref, gb_ref, g_ref, ys_hbm, o_ref, ya_ref, yb_ref, sems, *, tm):
    i = pl.program_id(0)
    n = pl.num_programs(0)
    slot = lax.rem(i, 2)

    def start_tile(tile, slot):
        copies = _combine_copies(pos1_ref, pos2_ref, ys_hbm, ya_ref, yb_ref, sems, tile, slot, tm)

        def body(r, carry):
            a, b = copies(r)
            a.start()
            b.start()
            return carry
        lax.fori_loop(0, tm, body, 0, unroll=DMA_ISSUE_UNROLL)

    @pl.when(i == 0)
    def _():
        start_tile(0, 0)

    @pl.when(i + 1 < n)
    def _():
        start_tile(i + 1, 1 - slot)

    pltpu.make_async_copy(ys_hbm.at[pl.ds(0, tm)], ya_ref.at[slot], sems.at[slot]).wait()
    pltpu.make_async_copy(ys_hbm.at[pl.ds(0, tm)], yb_ref.at[slot], sems.at[slot]).wait()

    h = h_ref[...] + ga_ref[...] * ya_ref[slot] + gb_ref[...] * yb_ref[slot]
    var = jnp.mean(h * h, axis=-1, keepdims=True)
    o_ref[...] = h * lax.rsqrt(var + EPS) * g_ref[...]


def _combine(pos1, pos2, h2d, ga, gb, g, ys, tm):
    T, D = h2d.shape
    row = pl.BlockSpec((tm, D), lambda i, p1, p2: (i, 0))
    colv = pl.BlockSpec((tm, 1), lambda i, p1, p2: (i, 0))
    return pl.pallas_call(
        functools.partial(_combine_kernel, tm=tm),
        out_shape=jax.ShapeDtypeStruct((T, D), F32),
        grid_spec=pltpu.PrefetchScalarGridSpec(
            num_scalar_prefetch=2,
            grid=(T // tm,),
            in_specs=[row, colv, colv, pl.BlockSpec((1, D), lambda i, p1, p2: (0, 0)),
                      pl.BlockSpec(memory_space=pl.ANY)],
            out_specs=row,
            scratch_shapes=[pltpu.VMEM((2, tm, D), F32), pltpu.VMEM((2, tm, D), F32),
                            pltpu.SemaphoreType.DMA((2,))],
        ),
        compiler_params=_cparams(("arbitrary",)),
        name="moe_combine_norm",
    )(pos1, pos2, h2d, ga, gb, g.reshape(1, D), ys)


def _moe_final(res, y_c, y_d, w_out, norm_g, final_g, router_w, router_b, w_gate, w_up, w_down):
    B, S, D = res.shape
    T = B * S
    E = router_w.shape[1]
    h2d, idx, gts, cnt = _proj_router(res.reshape(T, D), y_c.reshape(T, -1), y_d.reshape(T, -1),
                                      w_out.astype(BF16), norm_g, router_w, router_b, _pick_tile(T, 1024))
    e1, e2, r1, r2 = idx[0], idx[1], idx[2], idx[3]
    tm = MOE_ROW_TILE if TOP_K * T >= E * MOE_ROW_TILE else _pick_tile(T, 512)
    counts = cnt[:, 0].astype(jnp.int32)
    padded = ((counts + tm - 1) // tm) * tm
    ends = jnp.cumsum(padded)
    offs = ends - padded
    eids = jnp.arange(E, dtype=jnp.int32)[:, None]
    pos1 = jnp.sum(jnp.where(e1[None, :] == eids, offs[:, None], 0), axis=0) + r1
    pos2 = jnp.sum(jnp.where(e2[None, :] == eids, offs[:, None], 0), axis=0) + r2
    n_tiles = -(-TOP_K * T // tm) + E
    tile_start = jnp.arange(n_tiles, dtype=jnp.int32) * tm
    tile_expert = jnp.minimum(jnp.sum(tile_start[:, None] >= ends[None, :], axis=1), E - 1).astype(jnp.int32)
    tile_valid = jnp.clip((offs + counts)[tile_expert] - tile_start, 0, tm).astype(jnp.int32)
    xs = _dispatch(pos1, pos2, ends.astype(jnp.int32), h2d, n_tiles * tm, _pick_tile(T, 1024), tm)
    ys = _gmm(tile_expert, tile_valid, xs, norm_g, w_gate, w_up, w_down, tm, _pick_tile(w_gate.shape[2], 512))
    out = _combine(pos1, pos2, h2d, gts[0][:, None], gts[1][:, None], final_g, ys, _pick_tile(T, 256))
    return out.reshape(B, S, D)


def kernel(x, norm_mix_g, norm_ffn_g, norm_final_g, ab_w_in, s5_lambda_re, s5_lambda_im, s5_log_dt, s5_b_re, s5_b_im, s5_c_re, s5_c_im, s5_d, s5_w_glu, mlstm_conv_w, mlstm_conv_b, mlstm_gate_b, mlstm_norm_g, ab_w_out, ffn_w_gate, ffn_w_up, ffn_w_down, cd_w_in, gmlp_norm_g, gmlp_norm_b, gmlp_w_s, gmlp_b_s, rel_bias, cd_w_out, moe_router_w, moe_router_b, moe_w_gate, moe_w_up, moe_w_down):
    B, S, D = x.shape
    y_a, y_b = _mix_ab(x, norm_mix_g[0], ab_w_in[0], s5_lambda_re[0], s5_lambda_im[0], s5_log_dt[0],
                       s5_b_re[0], s5_b_im[0], s5_c_re[0], s5_c_im[0], s5_d[0], s5_w_glu[0],
                       mlstm_conv_w[0], mlstm_conv_b[0], mlstm_gate_b[0], mlstm_norm_g[0])
    h = _proj_ffn(x, y_a, y_b, ab_w_out[0].astype(BF16), norm_ffn_g[0], ffn_w_gate[0].astype(BF16),
                  ffn_w_up[0].astype(BF16), ffn_w_down[0].astype(BF16),
                  _pick_tile(S, 512), _pick_tile(ffn_w_gate.shape[2], 1408))
    y_c, y_d = _mix_cd(h, norm_mix_g[1], cd_w_in[0], gmlp_norm_g[0], gmlp_norm_b[0],
                       gmlp_w_s[0], gmlp_b_s[0], rel_bias)
    return _moe_final(h, y_c, y_d, cd_w_out[0], norm_ffn_g[1], norm_final_g, moe_router_w[0], moe_router_b[0],
                      moe_w_gate[0], moe_w_up[0], moe_w_down[0])
```

```python
import functools
import math

import jax
import jax.numpy as jnp
from jax import lax
from jax.experimental import pallas as pl
from jax.experimental.pallas import tpu as pltpu

F32 = jnp.float32
BF16 = jnp.bfloat16
EPS = 1e-5

LANES = 128
SUBLANES = 8
VMEM_LIMIT_BYTES = 56 * 1024 * 1024

S5_GROUP = 16
S5_SLAB_GROUPS = 8
MLSTM_HEADS = 4
MLSTM_CHUNK = 128
MLSTM_SEQS_PER_STEP = 2
CONV_WIDTH = 4
GMLP_GROUPS = 4
GMLP_CHUNK = 128
MOBA_HEADS = 4
MOBA_BLOCK = 256
MOBA_TOPK = 3
REL_BUCKETS = 32
REL_MAX_DIST = 128
TOP_K = 2
ROUTER_SUB_TILE = 512
GMM_TAIL_ROW_BLOCK = 256
MOE_ROW_TILE = 1536


def _cparams(sem):
    return pltpu.CompilerParams(dimension_semantics=sem, vmem_limit_bytes=VMEM_LIMIT_BYTES)


def _rms_bf16(x, g):
    var = jnp.mean(x * x, axis=-1, keepdims=True)
    return (x * lax.rsqrt(var + EPS) * g).astype(BF16)


def _gelu(x):
    return jax.nn.gelu(x, approximate=True)


def _sigmoid(x):
    return 1.0 / (1.0 + jnp.exp(-x))


def _norm_proj_kernel(x_ref, g_ref, w_ref, *out_refs, splits):
    hn = _rms_bf16(x_ref[0], g_ref[...])
    for o_ref, (c0, c1) in zip(out_refs, splits):
        r = jnp.dot(hn, w_ref[:, c0:c1], preferred_element_type=F32)
        o_ref[...] = r.reshape(o_ref.shape).astype(o_ref.dtype)


def _norm_proj(x, g, w, outs, tm):
    B, S, D = x.shape
    splits = tuple((c0, c1) for c0, c1, _ in outs)
    out_shape, out_specs = [], []
    for c0, c1, dt in outs:
        n = c1 - c0
        out_shape.append(jax.ShapeDtypeStruct((B, S, n), dt))
        out_specs.append(pl.BlockSpec((1, tm, n), lambda b, i: (b, i, 0)))
    return pl.pallas_call(
        functools.partial(_norm_proj_kernel, splits=splits),
        out_shape=out_shape,
        grid=(B, S // tm),
        in_specs=[
            pl.BlockSpec((1, tm, D), lambda b, i: (b, i, 0)),
            pl.BlockSpec((1, D), lambda b, i: (0, 0)),
            pl.BlockSpec(w.shape, lambda b, i: (0, 0)),
        ],
        out_specs=out_specs,
        compiler_params=_cparams(("parallel", "parallel")),
        name="norm_proj",
    )(x, g.reshape(1, D), w)


def _s5_kernel(u_hbm, bmat_ref, ar_ref, ai_ref, cmat_ref, d_ref, wglu_ref, y_hbm,
               xr_ref, xi_ref, ubuf, ybuf, in_sems, out_sems, *bufs, lc, nb, nslab, sw):
    i = pl.program_id(0)
    n = pl.num_programs(0)
    slot = lax.rem(i, 2)
    W = ubuf.shape[-1]

    def in_copies(step, slot):
        t0 = pl.multiple_of(step * lc, lc)
        return [pltpu.make_async_copy(u_hbm.at[b, pl.ds(t0, lc), :], ubuf.at[slot, :, b, :], in_sems.at[slot])
                for b in range(nb)]

    def out_copies(step, slot):
        t0 = pl.multiple_of(step * lc, lc)
        return [pltpu.make_async_copy(ybuf.at[slot, :, b, :], y_hbm.at[b, pl.ds(t0, lc), :], out_sems.at[slot])
                for b in range(nb)]

    @pl.when(i == 0)
    def _():
        xr_ref[...] = jnp.zeros_like(xr_ref)
        xi_ref[...] = jnp.zeros_like(xi_ref)
        for c in in_copies(0, 0):
            c.start()

    @pl.when(i + 1 < n)
    def _():
        for c in in_copies(i + 1, 1 - slot):
            c.start()

    for c in in_copies(i, slot):
        c.wait()

    uf = ubuf[slot].reshape(lc * nb, W)
    u = uf.astype(BF16)
    xr_all = xr_ref[...]
    xi_all = xi_ref[...]

    def drive(k):
        bufs[k][...] = jnp.dot(u[:, LANES * k:LANES * (k + 1)], bmat_ref[k], preferred_element_type=F32)

    drive(0)
    ys, new_xr, new_xi = [], [], []
    for k in range(nslab):
        if k + 1 < nslab:
            drive(k + 1)
        st_cols = slice(sw * k, sw * (k + 1))
        ar = jnp.broadcast_to(ar_ref[:, st_cols], (nb, sw))
        ai = jnp.broadcast_to(ai_ref[:, st_cols], (nb, sw))
        xr, xi = xr_all[:, st_cols], xi_all[:, st_cols]
        for t in range(lc):
            rows = slice(t * nb, (t + 1) * nb)
            nxr = ar * xr - ai * xi + bufs[k][rows, :sw]
            nxi = ar * xi + ai * xr + bufs[k][rows, sw:]
            bufs[k][rows, :sw] = nxr
            bufs[k][rows, sw:] = nxi
            xr, xi = nxr, nxi
        new_xr.append(xr)
        new_xi.append(xi)
        ys.append(jnp.dot(bufs[k][...].astype(BF16), cmat_ref[k], preferred_element_type=F32))
    xr_ref[...] = jnp.concatenate(new_xr, axis=-1)
    xi_ref[...] = jnp.concatenate(new_xi, axis=-1)
    y = jnp.concatenate(ys, axis=-1)
    y = _gelu(y + d_ref[...] * uf)
    gl = jnp.dot(y.astype(BF16), wglu_ref[...], preferred_element_type=F32)

    @pl.when(i >= 2)
    def _():
        for c in out_copies(i - 2, slot):
            c.wait()

    ybuf[slot] = (y * _sigmoid(gl)).reshape(lc, nb, W)
    for c in out_copies(i, slot):
        c.start()

    @pl.when(i == n - 1)
    def _():
        for c in out_copies(i, slot):
            c.wait()

        @pl.when(n >= 2)
        def _():
            for c in out_copies(i - 1, 1 - slot):
                c.wait()


def _s5_params(lam_re, lam_im, log_dt, b_re, b_im, c_re, c_im):
    G, P = lam_re.shape
    Hc = b_re.shape[-1]
    dt = jnp.exp(log_dt.astype(F32))[:, None]
    mag = jnp.exp(lam_re * dt)
    ar = mag * jnp.cos(lam_im * dt)
    ai = mag * jnp.sin(lam_im * dt)
    den = lam_re * lam_re + lam_im * lam_im
    cr = ((ar - 1.0) * lam_re + ai * lam_im) / den
    ci = (ai * lam_re - (ar - 1.0) * lam_im) / den
    bb_re = cr[..., None] * b_re - ci[..., None] * b_im
    bb_im = cr[..., None] * b_im + ci[..., None] * b_re
    gs = S5_SLAB_GROUPS
    nslab = G // gs
    eye = jnp.eye(gs, dtype=F32)

    def bd_in(b):
        b = b.reshape(nslab, gs, P, Hc)
        return jnp.einsum('kgph,gj->kghjp', b, eye).reshape(nslab, gs * Hc, gs * P)

    def bd_out(c):
        c = c.reshape(nslab, gs, Hc, P)
        return jnp.einsum('kghp,gj->kgpjh', c, eye).reshape(nslab, gs * P, gs * Hc)

    bmat = jnp.concatenate([bd_in(bb_re), bd_in(bb_im)], axis=-1).astype(BF16)
    cmat = jnp.concatenate([bd_out(c_re), -bd_out(c_im)], axis=1).astype(BF16)
    return bmat, ar.reshape(1, G * P), ai.reshape(1, G * P), cmat


def _s5(u, bmat, ar, ai, cmat, d_skip, w_glu, lc):
    nb, S, W = u.shape
    assert nb % SUBLANES == 0
    nslab = bmat.shape[0]
    sw = bmat.shape[2] // 2
    rows = lc * nb
    return pl.pallas_call(
        functools.partial(_s5_kernel, lc=lc, nb=nb, nslab=nslab, sw=sw),
        out_shape=jax.ShapeDtypeStruct((nb, S, W), F32),
        grid=(S // lc,),
        in_specs=[
            pl.BlockSpec(memory_space=pl.ANY),
            pl.BlockSpec(bmat.shape, lambda i: (0, 0, 0)),
            pl.BlockSpec(ar.shape, lambda i: (0, 0)),
            pl.BlockSpec(ai.shape, lambda i: (0, 0)),
            pl.BlockSpec(cmat.shape, lambda i: (0, 0, 0)),
            pl.BlockSpec((1, W), lambda i: (0, 0)),
            pl.BlockSpec(w_glu.shape, lambda i: (0, 0)),
        ],
        out_specs=pl.BlockSpec(memory_space=pl.ANY),
        scratch_shapes=[pltpu.VMEM((nb, sw * nslab), F32), pltpu.VMEM((nb, sw * nslab), F32),
                        pltpu.VMEM((2, lc, nb, W), F32), pltpu.VMEM((2, lc, nb, W), F32),
                        pltpu.SemaphoreType.DMA((2,)), pltpu.SemaphoreType.DMA((2,))]
        + [pltpu.VMEM((rows, 2 * sw), F32) for _ in range(nslab)],
        compiler_params=_cparams(("arbitrary",)),
        name="s5",
    )(u, bmat, ar, ai, cmat, d_skip.reshape(1, W), w_glu)


def _mlstm_chunk(x, vb, o_pre, g, tail, cx_all, m_row, cw, cb, ng, *, L, H, Dh):
    W = H * Dh
    halo = SUBLANES
    log_scale = -0.5 * math.log(Dh)

    row = lax.broadcasted_iota(jnp.int32, (L, L), 0)
    col = lax.broadcasted_iota(jnp.int32, (L, L), 1)
    conv = cb + cw[0:1, :] * x.astype(F32)
    corr = jnp.zeros((halo, 2 * W), F32)
    for j in range(1, CONV_WIDTH):
        shift = (row - col == j).astype(BF16)
        conv = conv + cw[j:j + 1, :] * jnp.dot(shift, x, preferred_element_type=F32)
        corr = corr + cw[j:j + 1, :] * tail[halo - j:2 * halo - j, :]
    conv = jnp.concatenate([conv[:halo] + corr, conv[halo:]], axis=0)
    new_tail = x[L - halo:, :].astype(F32)
    qk = (conv * _sigmoid(conv)).astype(BF16)
    q = qk[:, :W]
    kb = qk[:, W:]
    ones = jnp.ones((L, Dh), BF16)
    sq_ones = jnp.ones((Dh, Dh), BF16)

    log_i = g[:, :LANES]
    f_pre = g[:, LANES:]
    logf = jnp.minimum(f_pre, 0.0) - jnp.log(1.0 + jnp.exp(-jnp.abs(f_pre)))
    causal = row >= col
    bcum = jnp.dot(causal.astype(F32), logf, preferred_element_type=F32, precision=lax.Precision.HIGHEST)
    w_cols = log_i - bcum
    trow = lax.broadcasted_iota(jnp.int32, (L, LANES), 0)
    cmax = w_cols
    k = 1
    while k < L:
        cmax = jnp.maximum(cmax, jnp.where(trow >= k, pltpu.roll(cmax, k, axis=0), -jnp.inf))
        k *= 2
    m_inter = bcum + m_row
    m_t = jnp.maximum(bcum + cmax, m_inter)
    u_cols = bcum - m_t + log_scale
    wi_cols = jnp.exp(m_inter - m_t)
    em_cols = jnp.exp(-m_t)
    b_last = bcum[L - 1:L, :]
    m_loc = b_last + cmax[L - 1:L, :]
    m_new = jnp.maximum(b_last + m_row, m_loc)
    s_prev = jnp.exp(b_last + m_row - m_new)
    s_loc = jnp.exp(m_loc - m_new)
    w_rows = w_cols.T

    heads = range(H)
    hs = [slice(h * Dh, (h + 1) * Dh) for h in heads]
    hl = [slice(h, h + 1) for h in heads]
    nt = (((1,), (1,)), ((), ()))
    vext = [jnp.concatenate([vb[:, hs[h]], ones], axis=1) for h in heads]
    qk_t = [lax.dot_general(q[:, hs[h]], kb[:, hs[h]], nt, preferred_element_type=F32) for h in heads]
    r2 = [jnp.dot(q[:, hs[h]], cx_all[h].astype(BF16), preferred_element_type=F32) for h in heads]
    decay = [jnp.exp(jnp.where(causal, u_cols[:, hl[h]] + w_rows[hl[h], :], -jnp.inf)) for h in heads]
    s = [(qk_t[h] * decay[h]).astype(BF16) for h in heads]
    r1 = [jnp.dot(s[h], vext[h], preferred_element_type=F32) for h in heads]
    hh = []
    for h in heads:
        wi = jnp.broadcast_to(wi_cols[:, hl[h]], (L, Dh))
        num = r1[h][:, :Dh] + wi * r2[h][:, :Dh]
        den = r1[h][:, Dh:] + wi * r2[h][:, Dh:]
        hh.append(num / jnp.maximum(jnp.abs(den), jnp.broadcast_to(em_cols[:, hl[h]], (L, Dh))))
    msq = [jnp.dot((hh[h] * hh[h]).astype(BF16), sq_ones, preferred_element_type=F32) * (1.0 / Dh) for h in heads]
    ys = [(_sigmoid(o_pre[:, hs[h]].astype(F32)) * hh[h] * lax.rsqrt(msq[h] + EPS) * ng[:, hs[h]]).astype(BF16)
          for h in heads]
    kwt = [(kb[:, hs[h]].astype(F32).T
            * jnp.exp(b_last[:, hl[h]] + w_rows[hl[h], :] - m_loc[:, hl[h]] + log_scale)).astype(BF16) for h in heads]
    cx_loc = [jnp.dot(kwt[h], vext[h], preferred_element_type=F32) for h in heads]
    new_cx = [s_prev[:, hl[h]] * cx_all[h] + s_loc[:, hl[h]] * cx_loc[h] for h in heads]
    return ys, new_tail, new_cx, m_new


def _mlstm_kernel(qk_ref, v_ref, o_ref, gt_ref, cw_ref, cb_ref, gb_ref, ng_ref, y_ref,
                  tail_ref, c_ref, m_ref, *, L, H, Dh, BB):
    @pl.when(pl.program_id(1) == 0)
    def _():
        tail_ref[...] = jnp.zeros_like(tail_ref)
        c_ref[...] = jnp.zeros_like(c_ref)
        m_ref[...] = jnp.zeros_like(m_ref)

    tails = [tail_ref[bb] for bb in range(BB)]
    cxs = [[c_ref[bb, h] for h in range(H)] for bb in range(BB)]
    m_rows = [m_ref[bb, 0:1, :] for bb in range(BB)]
    results = []
    for bb in range(BB):
        g = gt_ref[bb] + gb_ref[...]
        results.append(_mlstm_chunk(qk_ref[bb], v_ref[bb].astype(BF16), o_ref[bb], g, tails[bb], cxs[bb],
                                    m_rows[bb], cw_ref[...], cb_ref[...], ng_ref[...], L=L, H=H, Dh=Dh))
    for bb, (ys, new_tail, new_cx, m_new) in enumerate(results):
        for h in range(H):
            y_ref[bb, :, h * Dh:(h + 1) * Dh] = ys[h]
            c_ref[bb, h] = new_cx[h]
        tail_ref[bb, 0:SUBLANES, :] = new_tail
        m_ref[bb, 0:1, :] = m_new


def _mlstm(qk_pre, v, o_pre, gates, conv_w, conv_b, gate_b_pad, norm_g):
    B, S, W2 = qk_pre.shape
    W = W2 // 2
    H, L = MLSTM_HEADS, MLSTM_CHUNK
    Dh = W // H
    BB = _pick_tile(B, MLSTM_SEQS_PER_STEP)
    return pl.pallas_call(
        functools.partial(_mlstm_kernel, L=L, H=H, Dh=Dh, BB=BB),
        out_shape=jax.ShapeDtypeStruct((B, S, W), BF16),
        grid=(B // BB, S // L),
        in_specs=[
            pl.BlockSpec((BB, L, W2), lambda b, c: (b, c, 0)),
            pl.BlockSpec((BB, L, W), lambda b, c: (b, c, 0)),
            pl.BlockSpec((BB, L, W), lambda b, c: (b, c, 0)),
            pl.BlockSpec((BB, L, 2 * LANES), lambda b, c: (b, c, 0)),
            pl.BlockSpec((CONV_WIDTH, W2), lambda b, c: (0, 0)),
            pl.BlockSpec((1, W2), lambda b, c: (0, 0)),
            pl.BlockSpec((1, 2 * LANES), lambda b, c: (0, 0)),
            pl.BlockSpec((1, W), lambda b, c: (0, 0)),
        ],
        out_specs=pl.BlockSpec((BB, L, W), lambda b, c: (b, c, 0)),
        scratch_shapes=[
            pltpu.VMEM((BB, 2 * SUBLANES, W2), F32),
            pltpu.VMEM((BB, H, Dh, 2 * Dh), F32),
            pltpu.VMEM((BB, SUBLANES, LANES), F32),
        ],
        compiler_params=_cparams(("parallel", "arbitrary")),
        name="mlstm",
    )(qk_pre, v, o_pre, gates, conv_w, conv_b.reshape(1, W2), gate_b_pad, norm_g.reshape(1, W))


def _mix_out(res, a, b, wa_ref, wb_ref):
    acc = jnp.dot(a.astype(BF16), wa_ref[...], preferred_element_type=F32)
    return res + acc + jnp.dot(b.astype(BF16), wb_ref[...], preferred_element_type=F32)


def _proj_ffn_kernel(res_ref, a_ref, b_ref, wa_ref, wb_ref, g_ref, wg_ref, wu_ref, wd_ref, o_ref,
                     hn_ref, acc_ref):
    f = pl.program_id(2)

    @pl.when(f == 0)
    def _():
        h = _mix_out(res_ref[0], a_ref[0], b_ref[0], wa_ref, wb_ref)
        hn_ref[...] = _rms_bf16(h, g_ref[...])
        acc_ref[...] = h

    hn = hn_ref[...]
    a = jnp.dot(hn, wg_ref[...], preferred_element_type=F32)
    u = jnp.dot(hn, wu_ref[...], preferred_element_type=F32)
    hmid = (a * _sigmoid(a) * u).astype(BF16)
    acc_ref[...] += jnp.dot(hmid, wd_ref[...], preferred_element_type=F32)

    @pl.when(f == pl.num_programs(2) - 1)
    def _():
        o_ref[0] = acc_ref[...]


def _proj_ffn(res, ya, yb, w_out, g, wg, wu, wd, tm, fc):
    B, S, D = res.shape
    wa_n = ya.shape[2]
    wa, wb = w_out[:wa_n], w_out[wa_n:]
    F = wg.shape[1]
    return pl.pallas_call(
        _proj_ffn_kernel,
        out_shape=jax.ShapeDtypeStruct((B, S, D), F32),
        grid=(B, S // tm, F // fc),
        in_specs=[
            pl.BlockSpec((1, tm, D), lambda b, i, f: (b, i, 0)),
            pl.BlockSpec((1, tm, wa_n), lambda b, i, f: (b, i, 0)),
            pl.BlockSpec((1, tm, wb.shape[0]), lambda b, i, f: (b, i, 0)),
            pl.BlockSpec(wa.shape, lambda b, i, f: (0, 0)),
            pl.BlockSpec(wb.shape, lambda b, i, f: (0, 0)),
            pl.BlockSpec((1, D), lambda b, i, f: (0, 0)),
            pl.BlockSpec((D, fc), lambda b, i, f: (0, f)),
            pl.BlockSpec((D, fc), lambda b, i, f: (0, f)),
            pl.BlockSpec((fc, D), lambda b, i, f: (f, 0)),
        ],
        out_specs=pl.BlockSpec((1, tm, D), lambda b, i, f: (b, i, 0)),
        scratch_shapes=[pltpu.VMEM((tm, D), BF16), pltpu.VMEM((tm, D), F32)],
        compiler_params=_cparams(("parallel", "parallel", "arbitrary")),
        name="proj_ffn_swiglu",
    )(res, ya, yb, wa, wb, g.reshape(1, D), wg, wu, wd)


def _pick_tile(n, target):
    t = min(n, target)
    while n % t:
        t //= 2
    return t


def _mix_ab(h, norm_g, w_in, lam_re, lam_im, log_dt, b_re, b_im, c_re, c_im, d_skip, w_glu,
            conv_w, conv_b, gate_b, mlstm_norm_g):
    B, S, D = h.shape
    s5w = lam_re.shape[0] * S5_GROUP
    mw = mlstm_norm_g.shape[0]
    c_qk, c_v, c_o, c_if = s5w, s5w + 2 * mw, s5w + 3 * mw, s5w + 4 * mw
    nh = (w_in.shape[1] - c_if) // 2
    lane_pad = ((0, 0), (0, LANES - nh))
    w_pad = jnp.concatenate([w_in[:, :c_if], jnp.pad(w_in[:, c_if:c_if + nh], lane_pad),
                             jnp.pad(w_in[:, c_if + nh:], lane_pad)], axis=1).astype(BF16)
    tm = _pick_tile(S, 512)
    u_s5, qk_pre, v, o_pre, gates = _norm_proj(
        h, norm_g, w_pad,
        [(0, c_qk, F32), (c_qk, c_v, BF16), (c_v, c_o, BF16), (c_o, c_if, BF16), (c_if, c_if + 2 * LANES, F32)],
        tm)
    bmat, ar, ai, cmat = _s5_params(lam_re, lam_im, log_dt, b_re, b_im, c_re, c_im)
    y_a = _s5(u_s5, bmat, ar, ai, cmat, d_skip, w_glu.astype(BF16), lc=_pick_tile(S, 32))
    gate_b_pad = jnp.concatenate([jnp.pad(gate_b[:nh], (0, LANES - nh)),
                                  jnp.pad(gate_b[nh:], (0, LANES - nh))]).reshape(1, 2 * LANES)
    y_b = _mlstm(qk_pre, v, o_pre, gates, conv_w, conv_b, gate_b_pad, mlstm_norm_g)
    return y_a, y_b


def _gmlp_kernel(u_ref, v_ref, ng_ref, nb_ref, ws_ref, bs_ref, y_ref, *, L, G, Dg, nchunk):
    u = _gelu(u_ref[0].astype(F32))
    v = _gelu(v_ref[0].astype(F32))
    mu = jnp.mean(v, axis=-1, keepdims=True)
    vc = v - mu
    var = jnp.mean(vc * vc, axis=-1, keepdims=True)
    vn = (vc * lax.rsqrt(var + EPS) * ng_ref[...] + nb_ref[...]).astype(BF16)
    for c in range(nchunk):
        rows = slice(c * L, (c + 1) * L)
        for g in range(G):
            cols = slice(g * Dg, (g + 1) * Dg)
            s = jnp.dot(ws_ref[g], vn[rows, cols], preferred_element_type=F32) + bs_ref[:, g:g + 1]
            y_ref[0, rows, cols] = (u[rows, cols] * s).astype(y_ref.dtype)


def _gmlp(u_pre, v_pre, norm_g, norm_b, w_s, b_s, tm):
    B, S, W = u_pre.shape
    G, L = GMLP_GROUPS, GMLP_CHUNK
    ws = (w_s * jnp.tril(jnp.ones((L, L), w_s.dtype))).astype(BF16)
    bs = jnp.pad(b_s.T, ((0, 0), (0, LANES - G)))
    return pl.pallas_call(
        functools.partial(_gmlp_kernel, L=L, G=G, Dg=W // G, nchunk=tm // L),
        out_shape=jax.ShapeDtypeStruct((B, S, W), BF16),
        grid=(B, S // tm),
        in_specs=[
            pl.BlockSpec((1, tm, W), lambda b, i: (b, i, 0)),
            pl.BlockSpec((1, tm, W), lambda b, i: (b, i, 0)),
            pl.BlockSpec((1, W), lambda b, i: (0, 0)),
            pl.BlockSpec((1, W), lambda b, i: (0, 0)),
            pl.BlockSpec((G, L, L), lambda b, i: (0, 0, 0)),
            pl.BlockSpec((L, LANES), lambda b, i: (0, 0)),
        ],
        out_specs=pl.BlockSpec((1, tm, W), lambda b, i: (b, i, 0)),
        compiler_params=_cparams(("parallel", "parallel")),
        name="gmlp",
    )(u_pre, v_pre, norm_g.reshape(1, W), norm_b.reshape(1, W), ws, bs)


NEG_BIG = -1e30


def _moba_kernel(q_ref, k_ref, v_ref, t0_ref, t1_ref, bfar_ref, y_ref, vext_ref, *, BL, NB, Dh, topk):
    nt = (((1,), (1,)), ((), ()))
    scale = 1.0 / math.sqrt(Dh)
    vext_ref[:, :Dh] = v_ref[0]
    vext_ref[:, Dh:] = jnp.ones((NB * BL, Dh), BF16)
    bfar = bfar_ref[0][:, 0:1]
    kmean = jnp.concatenate(
        [jnp.mean(k_ref[0, n * BL:(n + 1) * BL, :].astype(F32), axis=0, keepdims=True) for n in range(NB)],
        axis=0)
    nrow = lax.broadcasted_iota(jnp.int32, (NB, BL), 0)
    eye = (lax.broadcasted_iota(jnp.int32, (BL, BL), 0) ==
           lax.broadcasted_iota(jnp.int32, (BL, BL), 1)).astype(BF16)

    def logits(qi):
        rows = slice(qi * BL, (qi + 1) * BL)
        qf = q_ref[0, rows, :].astype(F32)
        qs = (qf * scale).astype(BF16)
        neg_cols = None
        if qi > topk:
            gate = lax.dot_general(kmean, qf, nt, preferred_element_type=F32,
                                   precision=lax.Precision.HIGHEST)
            rank = jnp.zeros((NB, BL), F32)
            for m in range(qi):
                gm = gate[m:m + 1, :]
                rank = rank + jnp.where(gm > gate, 1.0, jnp.where(gm == gate, jnp.where(nrow > m, 1.0, 0.0), 0.0))
            sel_rows = jnp.where(nrow < qi, jnp.where(rank < topk, 1.0, 0.0), 0.0)
            sel_pad = jnp.concatenate([sel_rows, jnp.zeros((LANES - NB, BL), F32)], axis=0).astype(BF16)
            sel_cols = lax.dot_general(eye, sel_pad, nt, preferred_element_type=F32)
            neg_cols = (1.0 - sel_cols) * NEG_BIG

        m_tile = None
        pieces = []
        for kb in range(qi + 1):
            s = lax.dot_general(qs, k_ref[0, kb * BL:(kb + 1) * BL, :], nt, preferred_element_type=F32)
            if kb == qi:
                s = s + t0_ref[0]
            elif kb == qi - 1:
                s = s + t1_ref[0]
                if neg_cols is not None:
                    s = s + neg_cols[:, kb:kb + 1]
            else:
                s = s + (bfar if neg_cols is None else bfar + neg_cols[:, kb:kb + 1])
            pieces.append(s)
            for c in range(BL // LANES):
                piece = s[:, c * LANES:(c + 1) * LANES]
                m_tile = piece if m_tile is None else jnp.maximum(m_tile, piece)
        return pieces, m_tile

    nxt = logits(0)
    for qi in range(NB):
        pieces, m_tile = nxt
        if qi + 1 < NB:
            nxt = logits(qi + 1)
        m = jnp.max(m_tile, axis=-1, keepdims=True)
        p = jnp.concatenate([jnp.exp(s - m).astype(BF16) for s in pieces], axis=1)
        acc = jnp.dot(p, vext_ref[:(qi + 1) * BL, :], preferred_element_type=F32)
        y_ref[0, qi * BL:(qi + 1) * BL, :] = (acc[:, :Dh] / acc[:, Dh:]).astype(y_ref.dtype)


def _rel_bucket(n):
    max_exact = REL_BUCKETS // 2
    nf = jnp.maximum(n, 1).astype(F32)
    large = max_exact + (jnp.log(nf / max_exact) / math.log(REL_MAX_DIST / max_exact)
                         * (REL_BUCKETS - max_exact)).astype(jnp.int32)
    large = jnp.minimum(large, REL_BUCKETS - 1)
    return jnp.where(n < max_exact, n, large)


def _moba(qkv, rel_bias):
    B, S, W3 = qkv.shape
    H, BL = MOBA_HEADS, MOBA_BLOCK
    W = W3 // 3
    Dh = W // H
    NB = S // BL
    assert BL + 1 >= REL_MAX_DIST and Dh == LANES and S % BL == 0
    i = jnp.arange(BL, dtype=jnp.int32)[:, None]
    j = jnp.arange(BL, dtype=jnp.int32)[None, :]
    buckets = jnp.arange(REL_BUCKETS, dtype=jnp.int32)

    def bias_tile(dist):
        onehot = (_rel_bucket(dist)[:, :, None] == buckets).astype(F32)
        return jnp.einsum('ijb,bh->hij', onehot, rel_bias, precision=lax.Precision.HIGHEST)

    t0 = jnp.where(i >= j, bias_tile(jnp.maximum(i - j, 0)), NEG_BIG)
    t1 = bias_tile(BL + i - j)
    bfar = jnp.broadcast_to(rel_bias.T[:, REL_BUCKETS - 1][:, None, None], (H, 1, LANES))
    return pl.pallas_call(
        functools.partial(_moba_kernel, BL=BL, NB=NB, Dh=Dh, topk=min(MOBA_TOPK, NB)),
        out_shape=jax.ShapeDtypeStruct((B, S, W), BF16),
        grid=(B, H),
        in_specs=[
            pl.BlockSpec((1, S, Dh), lambda b, h: (b, 0, h)),
            pl.BlockSpec((1, S, Dh), lambda b, h: (b, 0, H + h)),
            pl.BlockSpec((1, S, Dh), lambda b, h: (b, 0, 2 * H + h)),
            pl.BlockSpec((1, BL, BL), lambda b, h: (h, 0, 0)),
            pl.BlockSpec((1, BL, BL), lambda b, h: (h, 0, 0)),
            pl.BlockSpec((1, 1, LANES), lambda b, h: (h, 0, 0)),
        ],
        out_specs=pl.BlockSpec((1, S, Dh), lambda b, h: (b, 0, h)),
        scratch_shapes=[pltpu.VMEM((S, 2 * Dh), BF16)],
        compiler_params=_cparams(("parallel", "parallel")),
        name="moba",
    )(qkv, qkv, qkv, t0, t1, bfar)


def _mix_cd(h, norm_g, w_in, gmlp_norm_g, gmlp_norm_b, gmlp_w_s, gmlp_b_s, rel_bias):
    B, S, D = h.shape
    gw = gmlp_norm_g.shape[0]
    tm = _pick_tile(S, 512)
    u_pre, v_pre, qkv = _norm_proj(
        h, norm_g, w_in.astype(BF16),
        [(0, gw, BF16), (gw, 2 * gw, BF16), (2 * gw, w_in.shape[1], BF16)], tm)
    y_c = _gmlp(u_pre, v_pre, gmlp_norm_g, gmlp_norm_b, gmlp_w_s, gmlp_b_s, tm)
    return y_c, _moba(qkv, rel_bias)


def _route_logits(x, g, wr, rb, E):
    var = jnp.mean(x * x, axis=-1, keepdims=True)
    hn = x * lax.rsqrt(var + EPS) * g
    hn_hi = hn.astype(BF16)
    hn_lo = (hn - hn_hi.astype(F32)).astype(BF16)
    wr_hi, wr_lo = wr
    lg = (jnp.dot(hn_hi, wr_hi, preferred_element_type=F32) + jnp.dot(hn_hi, wr_lo, preferred_element_type=F32)
          + jnp.dot(hn_lo, wr_hi, preferred_element_type=F32))
    return lg.T[:E, :] + rb


def _route_assign(logits, carry, E):
    tm = logits.shape[1]
    rowi = lax.broadcasted_iota(jnp.int32, (E, tm), 0)
    v1 = jnp.max(logits, axis=0, keepdims=True)
    e1 = jnp.min(jnp.where(logits == v1, rowi, E), axis=0, keepdims=True)
    masked = jnp.where(rowi == e1, -jnp.inf, logits)
    v2 = jnp.max(masked, axis=0, keepdims=True)
    e2 = jnp.min(jnp.where(masked == v2, rowi, E), axis=0, keepdims=True)
    ex = jnp.exp(v2 - v1)
    g1 = 1.0 / (1.0 + ex)
    g2 = ex / (1.0 + ex)
    oh1 = rowi == e1
    oh2 = rowi == e2
    cnt = jnp.where(oh1, 1.0, jnp.where(oh2, 1.0, 0.0))
    before = (lax.broadcasted_iota(jnp.int32, (tm, tm), 0) <
              lax.broadcasted_iota(jnp.int32, (tm, tm), 1)).astype(BF16)
    excl = jnp.dot(cnt.astype(BF16), before, preferred_element_type=F32) + carry
    r1 = jnp.sum(jnp.where(oh1, excl, 0.0), axis=0, keepdims=True).astype(jnp.int32)
    r2 = jnp.sum(jnp.where(oh2, excl, 0.0), axis=0, keepdims=True).astype(jnp.int32)
    idx = jnp.where(rowi == 0, e1, jnp.where(rowi == 1, e2, jnp.where(rowi == 2, r1, jnp.where(rowi == 3, r2, 0))))
    gts = jnp.where(rowi == 0, g1, jnp.where(rowi == 1, g2, 0.0))
    return idx, gts, jnp.sum(cnt, axis=1, keepdims=True)


def _router_kernel(res_ref, a_ref, b_ref, wa_ref, wb_ref, g_ref, wr_ref, rb_ref,
                   h_ref, idx_ref, gts_ref, cnt_ref, carry_ref, *, E, sub):
    @pl.when(pl.program_id(0) == 0)
    def _():
        carry_ref[...] = jnp.zeros_like(carry_ref)

    nsub = res_ref.shape[0] // sub
    spans = [slice(j * sub, (j + 1) * sub) for j in range(nsub)]
    xs = [_mix_out(res_ref[sp, :], a_ref[sp, :], b_ref[sp, :], wa_ref, wb_ref) for sp in spans]
    for sp, x in zip(spans, xs):
        h_ref[sp, :] = x
    logits = [_route_logits(x, g_ref[...], (wr_ref[0], wr_ref[1]), rb_ref[:, 0:1], E) for x in xs]
    carry = carry_ref[:, 0:1]
    for sp, lg in zip(spans, logits):
        idx, gts, n_new = _route_assign(lg, carry, E)
        idx_ref[:, sp] = idx
        gts_ref[:, sp] = gts
        carry = carry + n_new
    carry_ref[...] = jnp.broadcast_to(carry, carry_ref.shape)
    cnt_ref[...] = carry_ref[...]


def _proj_router(res2d, ya, yb, w_out, g, router_w, router_b, tm):
    T, D = res2d.shape
    E = router_w.shape[1]
    assert E == SUBLANES
    wa_n = ya.shape[1]
    wa, wb = w_out[:wa_n], w_out[wa_n:]
    wr = jnp.pad(router_w, ((0, 0), (0, LANES - E)))
    wr_hi = wr.astype(BF16)
    wr_pair = jnp.stack([wr_hi, (wr - wr_hi.astype(F32)).astype(BF16)])
    return pl.pallas_call(
        functools.partial(_router_kernel, E=E, sub=_pick_tile(tm, ROUTER_SUB_TILE)),
        out_shape=[
            jax.ShapeDtypeStruct((T, D), F32),
            jax.ShapeDtypeStruct((E, T), jnp.int32),
            jax.ShapeDtypeStruct((E, T), F32),
            jax.ShapeDtypeStruct((E, LANES), F32),
        ],
        grid=(T // tm,),
        in_specs=[
            pl.BlockSpec((tm, D), lambda i: (i, 0)),
            pl.BlockSpec((tm, wa_n), lambda i: (i, 0)),
            pl.BlockSpec((tm, yb.shape[1]), lambda i: (i, 0)),
            pl.BlockSpec(wa.shape, lambda i: (0, 0)),
            pl.BlockSpec(wb.shape, lambda i: (0, 0)),
            pl.BlockSpec((1, D), lambda i: (0, 0)),
            pl.BlockSpec((2, D, LANES), lambda i: (0, 0, 0)),
            pl.BlockSpec((E, LANES), lambda i: (0, 0)),
        ],
        out_specs=[
            pl.BlockSpec((tm, D), lambda i: (i, 0)),
            pl.BlockSpec((E, tm), lambda i: (0, i)),
            pl.BlockSpec((E, tm), lambda i: (0, i)),
            pl.BlockSpec((E, LANES), lambda i: (0, 0)),
        ],
        scratch_shapes=[pltpu.VMEM((E, LANES), F32)],
        compiler_params=_cparams(("arbitrary",)),
        name="proj_moe_router",
    )(res2d, ya, yb, wa, wb, g.reshape(1, D), wr_pair, jnp.broadcast_to(router_b[:, None], (E, LANES)))


DMA_ISSUE_UNROLL = True


def _dispatch_kernel(pos1_ref, pos2_ref, ends_ref, h_ref, xs_hbm, zero_ref, sem, zsem, *, tm, tg, E):
    i = pl.program_id(0)

    @pl.when(i == 0)
    def _():
        zero_ref[...] = jnp.zeros_like(zero_ref)

        def zero_copy(e):
            start = ends_ref[e] - tg
            return pltpu.make_async_copy(zero_ref, xs_hbm.at[pl.ds(pl.multiple_of(start, tg), tg)], zsem)

        def nonempty(e):
            return ends_ref[e] > (ends_ref[e - 1] if e else 0)

        for e in range(E):
            @pl.when(nonempty(e))
            def _(e=e):
                zero_copy(e).start()
        for e in range(E):
            @pl.when(nonempty(e))
            def _(e=e):
                zero_copy(e).wait()

    def body(r, carry):
        t = i * tm + r
        src = h_ref.at[pl.ds(r, 1)]
        pltpu.make_async_copy(src, xs_hbm.at[pl.ds(pos1_ref[t], 1)], sem).start()
        pltpu.make_async_copy(src, xs_hbm.at[pl.ds(pos2_ref[t], 1)], sem).start()
        return carry
    lax.fori_loop(0, tm, body, 0, unroll=DMA_ISSUE_UNROLL)

    for _ in range(TOP_K):
        pltpu.make_async_copy(h_ref, xs_hbm.at[pl.ds(0, tm)], sem).wait()


def _dispatch(pos1, pos2, ends, h2d, n_rows, tm, tg):
    T, D = h2d.shape
    E = ends.shape[0]
    return pl.pallas_call(
        functools.partial(_dispatch_kernel, tm=tm, tg=tg, E=E),
        out_shape=jax.ShapeDtypeStruct((n_rows, D), h2d.dtype),
        grid_spec=pltpu.PrefetchScalarGridSpec(
            num_scalar_prefetch=3,
            grid=(T // tm,),
            in_specs=[pl.BlockSpec((tm, D), lambda i, p1, p2, en: (i, 0))],
            out_specs=pl.BlockSpec(memory_space=pl.ANY),
            scratch_shapes=[pltpu.VMEM((tg, D), h2d.dtype), pltpu.SemaphoreType.DMA, pltpu.SemaphoreType.DMA],
        ),
        compiler_params=_cparams(("arbitrary",)),
        name="moe_dispatch",
    )(pos1, pos2, ends, h2d)


def _gmm_kernel(te_ref, tv_ref, x_ref, g_ref, wg_ref, wu_ref, wd_ref, o_ref, xn_ref, acc_ref, *, rb):
    i = pl.program_id(0)
    f = pl.program_id(1)
    tm = xn_ref.shape[0]
    valid = tv_ref[i]

    @pl.when(f == 0)
    def _():
        acc_ref[...] = jnp.zeros_like(acc_ref)

    @pl.when((valid > 0) & (f == 0))
    def _():
        xn_ref[...] = _rms_bf16(x_ref[...], g_ref[...])

    def swiglu_rows(rows):
        x = xn_ref[rows, :]
        a = jnp.dot(x, wg_ref[0].astype(BF16), preferred_element_type=F32)
        u = jnp.dot(x, wu_ref[0].astype(BF16), preferred_element_type=F32)
        hmid = (a * _sigmoid(a) * u).astype(BF16)
        acc_ref[rows, :] += jnp.dot(hmid, wd_ref[0].astype(BF16), preferred_element_type=F32)

    @pl.when(valid == tm)
    def _():
        swiglu_rows(slice(None))

    for r in range(tm // rb):
        @pl.when((valid < tm) & (valid > r * rb))
        def _(r=r):
            swiglu_rows(slice(r * rb, (r + 1) * rb))

    @pl.when(f == pl.num_programs(1) - 1)
    def _():
        o_ref[...] = acc_ref[...]


def _gmm(tile_expert, tile_valid, xs, g, wg, wu, wd, tm, fc):
    R, D = xs.shape
    F = wg.shape[2]
    nf = F // fc

    def fsel(i, f, tv):
        return jnp.where(tv[i] > 0, f, nf - 1)

    return pl.pallas_call(
        functools.partial(_gmm_kernel, rb=_pick_tile(tm, GMM_TAIL_ROW_BLOCK)),
        out_shape=jax.ShapeDtypeStruct((R, D), F32),
        grid_spec=pltpu.PrefetchScalarGridSpec(
            num_scalar_prefetch=2,
            grid=(R // tm, nf),
            in_specs=[
                pl.BlockSpec((tm, D), lambda i, f, te, tv: (i, 0)),
                pl.BlockSpec((1, D), lambda i, f, te, tv: (0, 0)),
                pl.BlockSpec((1, D, fc), lambda i, f, te, tv: (te[i], 0, fsel(i, f, tv))),
                pl.BlockSpec((1, D, fc), lambda i, f, te, tv: (te[i], 0, fsel(i, f, tv))),
                pl.BlockSpec((1, fc, D), lambda i, f, te, tv: (te[i], fsel(i, f, tv), 0)),
            ],
            out_specs=pl.BlockSpec((tm, D), lambda i, f, te, tv: (i, 0)),
            scratch_shapes=[pltpu.VMEM((tm, D), BF16), pltpu.VMEM((tm, D), F32)],
        ),
        compiler_params=_cparams(("parallel", "arbitrary")),
        name="moe_gmm",
    )(tile_expert, tile_valid, xs, g.reshape(1, D), wg, wu, wd)


def _combine_copies(pos1_ref, pos2_ref, ys_hbm, ya_ref, yb_ref, sems, tile, slot, tm):
    def row_copies(r):
        t = tile * tm + r
        return (pltpu.make_async_copy(ys_hbm.at[pl.ds(pos1_ref[t], 1)], ya_ref.at[slot, pl.ds(r, 1)], sems.at[slot]),
                pltpu.make_async_copy(ys_hbm.at[pl.ds(pos2_ref[t], 1)], yb_ref.at[slot, pl.ds(r, 1)], sems.at[slot]))
    return row_copies


def _combine_kernel(pos1_ref, pos2_ref, h_ref, ga_ref, gb_ref, g_ref, ys_hbm, o_ref, ya_ref, yb_ref, sems, *, tm):
    i = pl.program_id(0)
    n = pl.num_programs(0)
    slot = lax.rem(i, 2)

    def start_tile(tile, slot):
        copies = _combine_copies(pos1_ref, pos2_ref, ys_hbm, ya_ref, yb_ref, sems, tile, slot, tm)

        def body(r, carry):
            a, b = copies(r)
            a.start()
            b.start()
            return carry
        lax.fori_loop(0, tm, body, 0, unroll=DMA_ISSUE_UNROLL)

    @pl.when(i == 0)
    def _():
        start_tile(0, 0)

    @pl.when(i + 1 < n)
    def _():
        start_tile(i + 1, 1 - slot)

    pltpu.make_async_copy(ys_hbm.at[pl.ds(0, tm)], ya_ref.at[slot], sems.at[slot]).wait()
    pltpu.make_async_copy(ys_hbm.at[pl.ds(0, tm)], yb_ref.at[slot], sems.at[slot]).wait()

    h = h_ref[...] + ga_ref[...] * ya_ref[slot] + gb_ref[...] * yb_ref[slot]
    var = jnp.mean(h * h, axis=-1, keepdims=True)
    o_ref[...] = h * lax.rsqrt(var + EPS) * g_ref[...]


def _combine(pos1, pos2, h2d, ga, gb, g, ys, tm):
    T, D = h2d.shape
    row = pl.BlockSpec((tm, D), lambda i, p1, p2: (i, 0))
    colv = pl.BlockSpec((tm, 1), lambda i, p1, p2: (i, 0))
    return pl.pallas_call(
        functools.partial(_combine_kernel, tm=tm),
        out_shape=jax.ShapeDtypeStruct((T, D), F32),
        grid_spec=pltpu.PrefetchScalarGridSpec(
            num_scalar_prefetch=2,
            grid=(T // tm,),
            in_specs=[row, colv, colv, pl.BlockSpec((1, D), lambda i, p1, p2: (0, 0)),
                      pl.BlockSpec(memory_space=pl.ANY)],
            out_specs=row,
            scratch_shapes=[pltpu.VMEM((2, tm, D), F32), pltpu.VMEM((2, tm, D), F32),
                            pltpu.SemaphoreType.DMA((2,))],
        ),
        compiler_params=_cparams(("arbitrary",)),
        name="moe_combine_norm",
    )(pos1, pos2, h2d, ga, gb, g.reshape(1, D), ys)


def _moe_final(res, y_c, y_d, w_out, norm_g, final_g, router_w, router_b, w_gate, w_up, w_down):
    B, S, D = res.shape
    T = B * S
    E = router_w.shape[1]
    h2d, idx, gts, cnt = _proj_router(res.reshape(T, D), y_c.reshape(T, -1), y_d.reshape(T, -1),
                                      w_out.astype(BF16), norm_g, router_w, router_b, _pick_tile(T, 1024))
    e1, e2, r1, r2 = idx[0], idx[1], idx[2], idx[3]
    tm = MOE_ROW_TILE if TOP_K * T >= E * MOE_ROW_TILE else _pick_tile(T, 512)
    counts = cnt[:, 0].astype(jnp.int32)
    padded = ((counts + tm - 1) // tm) * tm
    ends = jnp.cumsum(padded)
    offs = ends - padded
    eids = jnp.arange(E, dtype=jnp.int32)[:, None]
    pos1 = jnp.sum(jnp.where(e1[None, :] == eids, offs[:, None], 0), axis=0) + r1
    pos2 = jnp.sum(jnp.where(e2[None, :] == eids, offs[:, None], 0), axis=0) + r2
    n_tiles = -(-TOP_K * T // tm) + E
    tile_start = jnp.arange(n_tiles, dtype=jnp.int32) * tm
    tile_expert = jnp.minimum(jnp.sum(tile_start[:, None] >= ends[None, :], axis=1), E - 1).astype(jnp.int32)
    tile_valid = jnp.clip((offs + counts)[tile_expert] - tile_start, 0, tm).astype(jnp.int32)
    xs = _dispatch(pos1, pos2, ends.astype(jnp.int32), h2d, n_tiles * tm, _pick_tile(T, 1024), tm)
    ys = _gmm(tile_expert, tile_valid, xs, norm_g, w_gate, w_up, w_down, tm, _pick_tile(w_gate.shape[2], 512))
    out = _combine(pos1, pos2, h2d, gts[0][:, None], gts[1][:, None], final_g, ys, _pick_tile(T, 256))
    return out.reshape(B, S, D)


def kernel(x, norm_mix_g, norm_ffn_g, norm_final_g, ab_w_in, s5_lambda_re, s5_lambda_im, s5_log_dt, s5_b_re, s5_b_im, s5_c_re, s5_c_im, s5_d, s5_w_glu, mlstm_conv_w, mlstm_conv_b, mlstm_gate_b, mlstm_norm_g, ab_w_out, ffn_w_gate, ffn_w_up, ffn_w_down, cd_w_in, gmlp_norm_g, gmlp_norm_b, gmlp_w_s, gmlp_b_s, rel_bias, cd_w_out, moe_router_w, moe_router_b, moe_w_gate, moe_w_up, moe_w_down):
    B, S, D = x.shape
    y_a, y_b = _mix_ab(x, norm_mix_g[0], ab_w_in[0], s5_lambda_re[0], s5_lambda_im[0], s5_log_dt[0],
                       s5_b_re[0], s5_b_im[0], s5_c_re[0], s5_c_im[0], s5_d[0], s5_w_glu[0],
                       mlstm_conv_w[0], mlstm_conv_b[0], mlstm_gate_b[0], mlstm_norm_g[0])
    h = _proj_ffn(x, y_a, y_b, ab_w_out[0].astype(BF16), norm_ffn_g[0], ffn_w_gate[0].astype(BF16),
                  ffn_w_up[0].astype(BF16), ffn_w_down[0].astype(BF16),
                  _pick_tile(S, 512), _pick_tile(ffn_w_gate.shape[2], 1408))
    y_c, y_d = _mix_cd(h, norm_mix_g[1], cd_w_in[0], gmlp_norm_g[0], gmlp_norm_b[0],
                       gmlp_w_s[0], gmlp_b_s[0], rel_bias)
    return _moe_final(h, y_c, y_d, cd_w_out[0], norm_ffn_g[1], norm_final_g, moe_router_w[0], moe_router_b[0],
                      moe_w_gate[0], moe_w_up[0], moe_w_down[0])
```

```python
import functools
import math

import jax
import jax.numpy as jnp
from jax import lax
from jax.experimental import pallas as pl
from jax.experimental.pallas import tpu as pltpu

F32 = jnp.float32
BF16 = jnp.bfloat16
EPS = 1e-5

LANES = 128
SUBLANES = 8
VMEM_LIMIT_BYTES = 56 * 1024 * 1024

S5_GROUP = 16
S5_SLAB_GROUPS = 8
MLSTM_HEADS = 4
MLSTM_CHUNK = 128
MLSTM_SEQS_PER_STEP = 2
CONV_WIDTH = 4
GMLP_GROUPS = 4
GMLP_CHUNK = 128
MOBA_HEADS = 4
MOBA_BLOCK = 256
MOBA_TOPK = 3
REL_BUCKETS = 32
REL_MAX_DIST = 128
TOP_K = 2
ROUTER_SUB_TILE = 512
GMM_TAIL_ROW_BLOCK = 256
MOE_ROW_TILE = 1536


def _cparams(sem):
    return pltpu.CompilerParams(dimension_semantics=sem, vmem_limit_bytes=VMEM_LIMIT_BYTES)


def _rms_bf16(x, g):
    var = jnp.mean(x * x, axis=-1, keepdims=True)
    return (x * lax.rsqrt(var + EPS) * g).astype(BF16)


def _gelu(x):
    return jax.nn.gelu(x, approximate=True)


def _sigmoid(x):
    return 1.0 / (1.0 + jnp.exp(-x))


def _norm_proj_kernel(x_ref, g_ref, w_ref, *out_refs, splits):
    hn = _rms_bf16(x_ref[0], g_ref[...])
    for o_ref, (c0, c1) in zip(out_refs, splits):
        r = jnp.dot(hn, w_ref[:, c0:c1], preferred_element_type=F32)
        o_ref[...] = r.reshape(o_ref.shape).astype(o_ref.dtype)


def _norm_proj(x, g, w, outs, tm):
    B, S, D = x.shape
    splits = tuple((c0, c1) for c0, c1, _ in outs)
    out_shape, out_specs = [], []
    for c0, c1, dt in outs:
        n = c1 - c0
        out_shape.append(jax.ShapeDtypeStruct((B, S, n), dt))
        out_specs.append(pl.BlockSpec((1, tm, n), lambda b, i: (b, i, 0)))
    return pl.pallas_call(
        functools.partial(_norm_proj_kernel, splits=splits),
        out_shape=out_shape,
        grid=(B, S // tm),
        in_specs=[
            pl.BlockSpec((1, tm, D), lambda b, i: (b, i, 0)),
            pl.BlockSpec((1, D), lambda b, i: (0, 0)),
            pl.BlockSpec(w.shape, lambda b, i: (0, 0)),
        ],
        out_specs=out_specs,
        compiler_params=_cparams(("parallel", "parallel")),
        name="norm_proj",
    )(x, g.reshape(1, D), w)


def _s5_kernel(u_hbm, bmat_ref, ar_ref, ai_ref, cmat_ref, d_ref, wglu_ref, y_hbm,
               xr_ref, xi_ref, ubuf, ybuf, in_sems, out_sems, *bufs, lc, nb, nslab, sw):
    i = pl.program_id(0)
    n = pl.num_programs(0)
    slot = lax.rem(i, 2)
    W = ubuf.shape[-1]

    def in_copies(step, slot):
        t0 = pl.multiple_of(step * lc, lc)
        return [pltpu.make_async_copy(u_hbm.at[b, pl.ds(t0, lc), :], ubuf.at[slot, :, b, :], in_sems.at[slot])
                for b in range(nb)]

    def out_copies(step, slot):
        t0 = pl.multiple_of(step * lc, lc)
        return [pltpu.make_async_copy(ybuf.at[slot, :, b, :], y_hbm.at[b, pl.ds(t0, lc), :], out_sems.at[slot])
                for b in range(nb)]

    @pl.when(i == 0)
    def _():
        xr_ref[...] = jnp.zeros_like(xr_ref)
        xi_ref[...] = jnp.zeros_like(xi_ref)
        for c in in_copies(0, 0):
            c.start()

    @pl.when(i + 1 < n)
    def _():
        for c in in_copies(i + 1, 1 - slot):
            c.start()

    for c in in_copies(i, slot):
        c.wait()

    uf = ubuf[slot].reshape(lc * nb, W)
    u = uf.astype(BF16)
    xr_all = xr_ref[...]
    xi_all = xi_ref[...]

    def drive(k):
        bufs[k][...] = jnp.dot(u[:, LANES * k:LANES * (k + 1)], bmat_ref[k], preferred_element_type=F32)

    drive(0)
    ys, new_xr, new_xi = [], [], []
    for k in range(nslab):
        if k + 1 < nslab:
            drive(k + 1)
        st_cols = slice(sw * k, sw * (k + 1))
        ar = jnp.broadcast_to(ar_ref[:, st_cols], (nb, sw))
        ai = jnp.broadcast_to(ai_ref[:, st_cols], (nb, sw))
        xr, xi = xr_all[:, st_cols], xi_all[:, st_cols]
        for t in range(lc):
            rows = slice(t * nb, (t + 1) * nb)
            nxr = ar * xr - ai * xi + bufs[k][rows, :sw]
            nxi = ar * xi + ai * xr + bufs[k][rows, sw:]
            bufs[k][rows, :sw] = nxr
            bufs[k][rows, sw:] = nxi
            xr, xi = nxr, nxi
        new_xr.append(xr)
        new_xi.append(xi)
        ys.append(jnp.dot(bufs[k][...].astype(BF16), cmat_ref[k], preferred_element_type=F32))
    xr_ref[...] = jnp.concatenate(new_xr, axis=-1)
    xi_ref[...] = jnp.concatenate(new_xi, axis=-1)
    y = jnp.concatenate(ys, axis=-1)
    y = _gelu(y + d_ref[...] * uf)
    gl = jnp.dot(y.astype(BF16), wglu_ref[...], preferred_element_type=F32)

    @pl.when(i >= 2)
    def _():
        for c in out_copies(i - 2, slot):
            c.wait()

    ybuf[slot] = (y * _sigmoid(gl)).reshape(lc, nb, W)
    for c in out_copies(i, slot):
        c.start()

    @pl.when(i == n - 1)
    def _():
        for c in out_copies(i, slot):
            c.wait()

        @pl.when(n >= 2)
        def _():
            for c in out_copies(i - 1, 1 - slot):
                c.wait()


def _s5_params(lam_re, lam_im, log_dt, b_re, b_im, c_re, c_im):
    G, P = lam_re.shape
    Hc = b_re.shape[-1]
    dt = jnp.exp(log_dt.astype(F32))[:, None]
    mag = jnp.exp(lam_re * dt)
    ar = mag * jnp.cos(lam_im * dt)
    ai = mag * jnp.sin(lam_im * dt)
    den = lam_re * lam_re + lam_im * lam_im
    cr = ((ar - 1.0) * lam_re + ai * lam_im) / den
    ci = (ai * lam_re - (ar - 1.0) * lam_im) / den
    bb_re = cr[..., None] * b_re - ci[..., None] * b_im
    bb_im = cr[..., None] * b_im + ci[..., None] * b_re
    gs = S5_SLAB_GROUPS
    nslab = G // gs
    eye = jnp.eye(gs, dtype=F32)

    def bd_in(b):
        b = b.reshape(nslab, gs, P, Hc)
        return jnp.einsum('kgph,gj->kghjp', b, eye).reshape(nslab, gs * Hc, gs * P)

    def bd_out(c):
        c = c.reshape(nslab, gs, Hc, P)
        return jnp.einsum('kghp,gj->kgpjh', c, eye).reshape(nslab, gs * P, gs * Hc)

    bmat = jnp.concatenate([bd_in(bb_re), bd_in(bb_im)], axis=-1).astype(BF16)
    cmat = jnp.concatenate([bd_out(c_re), -bd_out(c_im)], axis=1).astype(BF16)
    return bmat, ar.reshape(1, G * P), ai.reshape(1, G * P), cmat


def _s5(u, bmat, ar, ai, cmat, d_skip, w_glu, lc):
    nb, S, W = u.shape
    assert nb % SUBLANES == 0
    nslab = bmat.shape[0]
    sw = bmat.shape[2] // 2
    rows = lc * nb
    return pl.pallas_call(
        functools.partial(_s5_kernel, lc=lc, nb=nb, nslab=nslab, sw=sw),
        out_shape=jax.ShapeDtypeStruct((nb, S, W), F32),
        grid=(S // lc,),
        in_specs=[
            pl.BlockSpec(memory_space=pl.ANY),
            pl.BlockSpec(bmat.shape, lambda i: (0, 0, 0)),
            pl.BlockSpec(ar.shape, lambda i: (0, 0)),
            pl.BlockSpec(ai.shape, lambda i: (0, 0)),
            pl.BlockSpec(cmat.shape, lambda i: (0, 0, 0)),
            pl.BlockSpec((1, W), lambda i: (0, 0)),
            pl.BlockSpec(w_glu.shape, lambda i: (0, 0)),
        ],
        out_specs=pl.BlockSpec(memory_space=pl.ANY),
        scratch_shapes=[pltpu.VMEM((nb, sw * nslab), F32), pltpu.VMEM((nb, sw * nslab), F32),
                        pltpu.VMEM((2, lc, nb, W), F32), pltpu.VMEM((2, lc, nb, W), F32),
                        pltpu.SemaphoreType.DMA((2,)), pltpu.SemaphoreType.DMA((2,))]
        + [pltpu.VMEM((rows, 2 * sw), F32) for _ in range(nslab)],
        compiler_params=_cparams(("arbitrary",)),
        name="s5",
    )(u, bmat, ar, ai, cmat, d_skip.reshape(1, W), w_glu)


def _mlstm_pre(x, g, tail, m_row, cw, cb, *, L, H, Dh):
    W = H * Dh
    halo = SUBLANES
    log_scale = -0.5 * math.log(Dh)

    row = lax.broadcasted_iota(jnp.int32, (L, L), 0)
    col = lax.broadcasted_iota(jnp.int32, (L, L), 1)
    conv = cb + cw[0:1, :] * x.astype(F32)
    corr = jnp.zeros((halo, 2 * W), F32)
    for j in range(1, CONV_WIDTH):
        shift = (row - col == j).astype(BF16)
        conv = conv + cw[j:j + 1, :] * jnp.dot(shift, x, preferred_element_type=F32)
        corr = corr + cw[j:j + 1, :] * tail[halo - j:2 * halo - j, :]
    conv = jnp.concatenate([conv[:halo] + corr, conv[halo:]], axis=0)
    new_tail = x[L - halo:, :].astype(F32)
    qk = (conv * _sigmoid(conv)).astype(BF16)
    log_i = g[:, :LANES]
    f_pre = g[:, LANES:]
    logf = jnp.minimum(f_pre, 0.0) - jnp.log(1.0 + jnp.exp(-jnp.abs(f_pre)))
    causal = row >= col
    bcum = jnp.dot(causal.astype(F32), logf, preferred_element_type=F32, precision=lax.Precision.HIGHEST)
    w_cols = log_i - bcum
    trow = lax.broadcasted_iota(jnp.int32, (L, LANES), 0)
    cmax = w_cols
    k = 1
    while k < L:
        cmax = jnp.maximum(cmax, jnp.where(trow >= k, pltpu.roll(cmax, k, axis=0), -jnp.inf))
        k *= 2
    m_inter = bcum + m_row
    m_t = jnp.maximum(bcum + cmax, m_inter)
    u_cols = bcum - m_t + log_scale
    wi_cols = jnp.exp(m_inter - m_t)
    em_cols = jnp.exp(-m_t)
    b_last = bcum[L - 1:L, :]
    m_loc = b_last + cmax[L - 1:L, :]
    m_new = jnp.maximum(b_last + m_row, m_loc)
    s_prev = jnp.exp(b_last + m_row - m_new)
    s_loc = jnp.exp(m_loc - m_new)
    w_rows = w_cols.T
    return dict(q=qk[:, :W], k=qk[:, W:], causal=causal, u_cols=u_cols, wi_cols=wi_cols, em_cols=em_cols,
                w_rows=w_rows, b_last=b_last, m_loc=m_loc, m_new=m_new, s_prev=s_prev, s_loc=s_loc,
                new_tail=new_tail, log_scale=log_scale)


def _mlstm_heads(pres, vbs, o_pres, cxs, ng, *, L, H, Dh):
    nt = (((1,), (1,)), ((), ()))
    pairs = [(b, h) for b in range(len(pres)) for h in range(H)]
    hs = [slice(h * Dh, (h + 1) * Dh) for h in range(H)]
    hl = [slice(h, h + 1) for h in range(H)]
    ones = jnp.ones((L, Dh), BF16)
    sq_ones = jnp.ones((Dh, Dh), BF16)
    vext = {(b, h): jnp.concatenate([vbs[b][:, hs[h]], ones], axis=1) for b, h in pairs}
    qk_t = {(b, h): lax.dot_general(pres[b]['q'][:, hs[h]], pres[b]['k'][:, hs[h]], nt,
                                    preferred_element_type=F32) for b, h in pairs}
    r2 = {(b, h): jnp.dot(pres[b]['q'][:, hs[h]], cxs[b][h].astype(BF16), preferred_element_type=F32)
          for b, h in pairs}
    s = {}
    for b, h in pairs:
        p = pres[b]
        decay = jnp.exp(jnp.where(p['causal'], p['u_cols'][:, hl[h]] + p['w_rows'][hl[h], :], -jnp.inf))
        s[b, h] = (qk_t[b, h] * decay).astype(BF16)
    r1 = {bh: jnp.dot(s[bh], vext[bh], preferred_element_type=F32) for bh in pairs}
    hh = {}
    for b, h in pairs:
        p = pres[b]
        wi = jnp.broadcast_to(p['wi_cols'][:, hl[h]], (L, Dh))
        num = r1[b, h][:, :Dh] + wi * r2[b, h][:, :Dh]
        den = r1[b, h][:, Dh:] + wi * r2[b, h][:, Dh:]
        hh[b, h] = num / jnp.maximum(jnp.abs(den), jnp.broadcast_to(p['em_cols'][:, hl[h]], (L, Dh)))
    msq = {bh: jnp.dot((hh[bh] * hh[bh]).astype(BF16), sq_ones, preferred_element_type=F32) * (1.0 / Dh)
           for bh in pairs}
    ys = {(b, h): (_sigmoid(o_pres[b][:, hs[h]].astype(F32)) * hh[b, h] * lax.rsqrt(msq[b, h] + EPS)
                   * ng[:, hs[h]]).astype(BF16) for b, h in pairs}
    kwt = {}
    for b, h in pairs:
        p = pres[b]
        wexp = jnp.exp(p['b_last'][:, hl[h]] + p['w_rows'][hl[h], :] - p['m_loc'][:, hl[h]] + p['log_scale'])
        kwt[b, h] = (p['k'][:, hs[h]].astype(F32).T * wexp).astype(BF16)
    cx_loc = {bh: jnp.dot(kwt[bh], vext[bh], preferred_element_type=F32) for bh in pairs}
    new_cx = {(b, h): pres[b]['s_prev'][:, hl[h]] * cxs[b][h] + pres[b]['s_loc'][:, hl[h]] * cx_loc[b, h]
              for b, h in pairs}
    return ys, new_cx


def _mlstm_kernel(qk_ref, v_ref, o_ref, gt_ref, cw_ref, cb_ref, gb_ref, ng_ref, y_ref,
                  tail_ref, c_ref, m_ref, *, L, H, Dh, BB):
    @pl.when(pl.program_id(1) == 0)
    def _():
        tail_ref[...] = jnp.zeros_like(tail_ref)
        c_ref[...] = jnp.zeros_like(c_ref)
        m_ref[...] = jnp.zeros_like(m_ref)

    tails = [tail_ref[bb] for bb in range(BB)]
    cxs = [[c_ref[bb, h] for h in range(H)] for bb in range(BB)]
    m_rows = [m_ref[bb, 0:1, :] for bb in range(BB)]
    pres = [_mlstm_pre(qk_ref[bb], gt_ref[bb] + gb_ref[...], tails[bb], m_rows[bb], cw_ref[...], cb_ref[...],
                       L=L, H=H, Dh=Dh) for bb in range(BB)]
    ys, new_cx = _mlstm_heads(pres, [v_ref[bb].astype(BF16) for bb in range(BB)], [o_ref[bb] for bb in range(BB)],
                              cxs, ng_ref[...], L=L, H=H, Dh=Dh)
    for bb in range(BB):
        for h in range(H):
            y_ref[bb, :, h * Dh:(h + 1) * Dh] = ys[bb, h]
            c_ref[bb, h] = new_cx[bb, h]
        tail_ref[bb, 0:SUBLANES, :] = pres[bb]['new_tail']
        m_ref[bb, 0:1, :] = pres[bb]['m_new']


def _mlstm(qk_pre, v, o_pre, gates, conv_w, conv_b, gate_b_pad, norm_g):
    B, S, W2 = qk_pre.shape
    W = W2 // 2
    H, L = MLSTM_HEADS, MLSTM_CHUNK
    Dh = W // H
    BB = _pick_tile(B, MLSTM_SEQS_PER_STEP)
    return pl.pallas_call(
        functools.partial(_mlstm_kernel, L=L, H=H, Dh=Dh, BB=BB),
        out_shape=jax.ShapeDtypeStruct((B, S, W), BF16),
        grid=(B // BB, S // L),
        in_specs=[
            pl.BlockSpec((BB, L, W2), lambda b, c: (b, c, 0)),
            pl.BlockSpec((BB, L, W), lambda b, c: (b, c, 0)),
            pl.BlockSpec((BB, L, W), lambda b, c: (b, c, 0)),
            pl.BlockSpec((BB, L, 2 * LANES), lambda b, c: (b, c, 0)),
            pl.BlockSpec((CONV_WIDTH, W2), lambda b, c: (0, 0)),
            pl.BlockSpec((1, W2), lambda b, c: (0, 0)),
            pl.BlockSpec((1, 2 * LANES), lambda b, c: (0, 0)),
            pl.BlockSpec((1, W), lambda b, c: (0, 0)),
        ],
        out_specs=pl.BlockSpec((BB, L, W), lambda b, c: (b, c, 0)),
        scratch_shapes=[
            pltpu.VMEM((BB, 2 * SUBLANES, W2), F32),
            pltpu.VMEM((BB, H, Dh, 2 * Dh), F32),
            pltpu.VMEM((BB, SUBLANES, LANES), F32),
        ],
        compiler_params=_cparams(("parallel", "arbitrary")),
        name="mlstm",
    )(qk_pre, v, o_pre, gates, conv_w, conv_b.reshape(1, W2), gate_b_pad, norm_g.reshape(1, W))


def _mix_out(res, a, b, wa_ref, wb_ref):
    acc = jnp.dot(a.astype(BF16), wa_ref[...], preferred_element_type=F32)
    return res + acc + jnp.dot(b.astype(BF16), wb_ref[...], preferred_element_type=F32)


def _proj_ffn_kernel(res_ref, a_ref, b_ref, wa_ref, wb_ref, g_ref, wg_ref, wu_ref, wd_ref, o_ref,
                     hn_ref, acc_ref):
    f = pl.program_id(2)

    @pl.when(f == 0)
    def _():
        h = _mix_out(res_ref[0], a_ref[0], b_ref[0], wa_ref, wb_ref)
        hn_ref[...] = _rms_bf16(h, g_ref[...])
        acc_ref[...] = h

    hn = hn_ref[...]
    a = jnp.dot(hn, wg_ref[...], preferred_element_type=F32)
    u = jnp.dot(hn, wu_ref[...], preferred_element_type=F32)
    hmid = (a * _sigmoid(a) * u).astype(BF16)
    acc_ref[...] += jnp.dot(hmid, wd_ref[...], preferred_element_type=F32)

    @pl.when(f == pl.num_programs(2) - 1)
    def _():
        o_ref[0] = acc_ref[...]


def _proj_ffn(res, ya, yb, w_out, g, wg, wu, wd, tm, fc):
    B, S, D = res.shape
    wa_n = ya.shape[2]
    wa, wb = w_out[:wa_n], w_out[wa_n:]
    F = wg.shape[1]
    return pl.pallas_call(
        _proj_ffn_kernel,
        out_shape=jax.ShapeDtypeStruct((B, S, D), F32),
        grid=(B, S // tm, F // fc),
        in_specs=[
            pl.BlockSpec((1, tm, D), lambda b, i, f: (b, i, 0)),
            pl.BlockSpec((1, tm, wa_n), lambda b, i, f: (b, i, 0)),
            pl.BlockSpec((1, tm, wb.shape[0]), lambda b, i, f: (b, i, 0)),
            pl.BlockSpec(wa.shape, lambda b, i, f: (0, 0)),
            pl.BlockSpec(wb.shape, lambda b, i, f: (0, 0)),
            pl.BlockSpec((1, D), lambda b, i, f: (0, 0)),
            pl.BlockSpec((D, fc), lambda b, i, f: (0, f)),
            pl.BlockSpec((D, fc), lambda b, i, f: (0, f)),
            pl.BlockSpec((fc, D), lambda b, i, f: (f, 0)),
        ],
        out_specs=pl.BlockSpec((1, tm, D), lambda b, i, f: (b, i, 0)),
        scratch_shapes=[pltpu.VMEM((tm, D), BF16), pltpu.VMEM((tm, D), F32)],
        compiler_params=_cparams(("parallel", "parallel", "arbitrary")),
        name="proj_ffn_swiglu",
    )(res, ya, yb, wa, wb, g.reshape(1, D), wg, wu, wd)


def _pick_tile(n, target):
    t = min(n, target)
    while n % t:
        t //= 2
    return t


def _mix_ab(h, norm_g, w_in, lam_re, lam_im, log_dt, b_re, b_im, c_re, c_im, d_skip, w_glu,
            conv_w, conv_b, gate_b, mlstm_norm_g):
    B, S, D = h.shape
    s5w = lam_re.shape[0] * S5_GROUP
    mw = mlstm_norm_g.shape[0]
    c_qk, c_v, c_o, c_if = s5w, s5w + 2 * mw, s5w + 3 * mw, s5w + 4 * mw
    nh = (w_in.shape[1] - c_if) // 2
    lane_pad = ((0, 0), (0, LANES - nh))
    w_pad = jnp.concatenate([w_in[:, :c_if], jnp.pad(w_in[:, c_if:c_if + nh], lane_pad),
                             jnp.pad(w_in[:, c_if + nh:], lane_pad)], axis=1).astype(BF16)
    tm = _pick_tile(S, 512)
    u_s5, qk_pre, v, o_pre, gates = _norm_proj(
        h, norm_g, w_pad,
        [(0, c_qk, F32), (c_qk, c_v, BF16), (c_v, c_o, BF16), (c_o, c_if, BF16), (c_if, c_if + 2 * LANES, F32)],
        tm)
    bmat, ar, ai, cmat = _s5_params(lam_re, lam_im, log_dt, b_re, b_im, c_re, c_im)
    y_a = _s5(u_s5, bmat, ar, ai, cmat, d_skip, w_glu.astype(BF16), lc=_pick_tile(S, 32))
    gate_b_pad = jnp.concatenate([jnp.pad(gate_b[:nh], (0, LANES - nh)),
                                  jnp.pad(gate_b[nh:], (0, LANES - nh))]).reshape(1, 2 * LANES)
    y_b = _mlstm(qk_pre, v, o_pre, gates, conv_w, conv_b, gate_b_pad, mlstm_norm_g)
    return y_a, y_b


def _gmlp_kernel(u_ref, v_ref, ng_ref, nb_ref, ws_ref, bs_ref, y_ref, *, L, G, Dg, nchunk):
    u = _gelu(u_ref[0].astype(F32))
    v = _gelu(v_ref[0].astype(F32))
    mu = jnp.mean(v, axis=-1, keepdims=True)
    vc = v - mu
    var = jnp.mean(vc * vc, axis=-1, keepdims=True)
    vn = (vc * lax.rsqrt(var + EPS) * ng_ref[...] + nb_ref[...]).astype(BF16)
    for c in range(nchunk):
        rows = slice(c * L, (c + 1) * L)
        for g in range(G):
            cols = slice(g * Dg, (g + 1) * Dg)
            s = jnp.dot(ws_ref[g], vn[rows, cols], preferred_element_type=F32) + bs_ref[:, g:g + 1]
            y_ref[0, rows, cols] = (u[rows, cols] * s).astype(y_ref.dtype)


def _gmlp(u_pre, v_pre, norm_g, norm_b, w_s, b_s, tm):
    B, S, W = u_pre.shape
    G, L = GMLP_GROUPS, GMLP_CHUNK
    ws = (w_s * jnp.tril(jnp.ones((L, L), w_s.dtype))).astype(BF16)
    bs = jnp.pad(b_s.T, ((0, 0), (0, LANES - G)))
    return pl.pallas_call(
        functools.partial(_gmlp_kernel, L=L, G=G, Dg=W // G, nchunk=tm // L),
        out_shape=jax.ShapeDtypeStruct((B, S, W), BF16),
        grid=(B, S // tm),
        in_specs=[
            pl.BlockSpec((1, tm, W), lambda b, i: (b, i, 0)),
            pl.BlockSpec((1, tm, W), lambda b, i: (b, i, 0)),
            pl.BlockSpec((1, W), lambda b, i: (0, 0)),
            pl.BlockSpec((1, W), lambda b, i: (0, 0)),
            pl.BlockSpec((G, L, L), lambda b, i: (0, 0, 0)),
            pl.BlockSpec((L, LANES), lambda b, i: (0, 0)),
        ],
        out_specs=pl.BlockSpec((1, tm, W), lambda b, i: (b, i, 0)),
        compiler_params=_cparams(("parallel", "parallel")),
        name="gmlp",
    )(u_pre, v_pre, norm_g.reshape(1, W), norm_b.reshape(1, W), ws, bs)


NEG_BIG = -1e30


def _moba_kernel(q_ref, k_ref, v_ref, t0_ref, t1_ref, bfar_ref, y_ref, vext_ref, *, BL, NB, Dh, topk):
    nt = (((1,), (1,)), ((), ()))
    scale = 1.0 / math.sqrt(Dh)
    vext_ref[:, :Dh] = v_ref[0]
    vext_ref[:, Dh:] = jnp.ones((NB * BL, Dh), BF16)
    bfar = bfar_ref[0][:, 0:1]
    kmean = jnp.concatenate(
        [jnp.mean(k_ref[0, n * BL:(n + 1) * BL, :].astype(F32), axis=0, keepdims=True) for n in range(NB)],
        axis=0)
    nrow = lax.broadcasted_iota(jnp.int32, (NB, BL), 0)
    eye = (lax.broadcasted_iota(jnp.int32, (BL, BL), 0) ==
           lax.broadcasted_iota(jnp.int32, (BL, BL), 1)).astype(BF16)

    def selection_mask(qi):
        qf = q_ref[0, qi * BL:(qi + 1) * BL, :].astype(F32)
        gate = lax.dot_general(kmean, qf, nt, preferred_element_type=F32, precision=lax.Precision.HIGHEST)
        rank = jnp.zeros((NB, BL), F32)
        for m in range(qi):
            gm = gate[m:m + 1, :]
            rank = rank + jnp.where(gm > gate, 1.0, jnp.where(gm == gate, jnp.where(nrow > m, 1.0, 0.0), 0.0))
        sel_rows = jnp.where(nrow < qi, jnp.where(rank < topk, 1.0, 0.0), 0.0)
        sel_pad = jnp.concatenate([sel_rows, jnp.zeros((LANES - NB, BL), F32)], axis=0).astype(BF16)
        sel_cols = lax.dot_general(eye, sel_pad, nt, preferred_element_type=F32)
        return (1.0 - sel_cols) * NEG_BIG

    def logits(qi):
        qs = (q_ref[0, qi * BL:(qi + 1) * BL, :].astype(F32) * scale).astype(BF16)
        raw = [lax.dot_general(qs, k_ref[0, kb * BL:(kb + 1) * BL, :], nt, preferred_element_type=F32)
               for kb in range(qi + 1)]
        neg_cols = selection_mask(qi) if qi > topk else None
        m_tile = None
        pieces = []
        for kb in range(qi + 1):
            s = raw[kb]
            if kb == qi:
                s = s + t0_ref[0]
            elif kb == qi - 1:
                s = s + t1_ref[0]
                if neg_cols is not None:
                    s = s + neg_cols[:, kb:kb + 1]
            else:
                s = s + (bfar if neg_cols is None else bfar + neg_cols[:, kb:kb + 1])
            pieces.append(s)
            for c in range(BL // LANES):
                piece = s[:, c * LANES:(c + 1) * LANES]
                m_tile = piece if m_tile is None else jnp.maximum(m_tile, piece)
        return pieces, m_tile

    nxt = logits(0)
    for qi in range(NB):
        pieces, m_tile = nxt
        if qi + 1 < NB:
            nxt = logits(qi + 1)
        m = jnp.max(m_tile, axis=-1, keepdims=True)
        p = jnp.concatenate([jnp.exp(s - m).astype(BF16) for s in pieces], axis=1)
        acc = jnp.dot(p, vext_ref[:(qi + 1) * BL, :], preferred_element_type=F32)
        y_ref[0, qi * BL:(qi + 1) * BL, :] = (acc[:, :Dh] / acc[:, Dh:]).astype(y_ref.dtype)


def _rel_bucket(n):
    max_exact = REL_BUCKETS // 2
    nf = jnp.maximum(n, 1).astype(F32)
    large = max_exact + (jnp.log(nf / max_exact) / math.log(REL_MAX_DIST / max_exact)
                         * (REL_BUCKETS - max_exact)).astype(jnp.int32)
    large = jnp.minimum(large, REL_BUCKETS - 1)
    return jnp.where(n < max_exact, n, large)


def _moba(qkv, rel_bias):
    B, S, W3 = qkv.shape
    H, BL = MOBA_HEADS, MOBA_BLOCK
    W = W3 // 3
    Dh = W // H
    NB = S // BL
    assert BL + 1 >= REL_MAX_DIST and Dh == LANES and S % BL == 0
    i = jnp.arange(BL, dtype=jnp.int32)[:, None]
    j = jnp.arange(BL, dtype=jnp.int32)[None, :]
    buckets = jnp.arange(REL_BUCKETS, dtype=jnp.int32)

    def bias_tile(dist):
        onehot = (_rel_bucket(dist)[:, :, None] == buckets).astype(F32)
        return jnp.einsum('ijb,bh->hij', onehot, rel_bias, precision=lax.Precision.HIGHEST)

    t0 = jnp.where(i >= j, bias_tile(jnp.maximum(i - j, 0)), NEG_BIG)
    t1 = bias_tile(BL + i - j)
    bfar = jnp.broadcast_to(rel_bias.T[:, REL_BUCKETS - 1][:, None, None], (H, 1, LANES))
    return pl.pallas_call(
        functools.partial(_moba_kernel, BL=BL, NB=NB, Dh=Dh, topk=min(MOBA_TOPK, NB)),
        out_shape=jax.ShapeDtypeStruct((B, S, W), BF16),
        grid=(B, H),
        in_specs=[
            pl.BlockSpec((1, S, Dh), lambda b, h: (b, 0, h)),
            pl.BlockSpec((1, S, Dh), lambda b, h: (b, 0, H + h)),
            pl.BlockSpec((1, S, Dh), lambda b, h: (b, 0, 2 * H + h)),
            pl.BlockSpec((1, BL, BL), lambda b, h: (h, 0, 0)),
            pl.BlockSpec((1, BL, BL), lambda b, h: (h, 0, 0)),
            pl.BlockSpec((1, 1, LANES), lambda b, h: (h, 0, 0)),
        ],
        out_specs=pl.BlockSpec((1, S, Dh), lambda b, h: (b, 0, h)),
        scratch_shapes=[pltpu.VMEM((S, 2 * Dh), BF16)],
        compiler_params=_cparams(("parallel", "parallel")),
        name="moba",
    )(qkv, qkv, qkv, t0, t1, bfar)


def _mix_cd(h, norm_g, w_in, gmlp_norm_g, gmlp_norm_b, gmlp_w_s, gmlp_b_s, rel_bias):
    B, S, D = h.shape
    gw = gmlp_norm_g.shape[0]
    tm = _pick_tile(S, 512)
    u_pre, v_pre, qkv = _norm_proj(
        h, norm_g, w_in.astype(BF16),
        [(0, gw, BF16), (gw, 2 * gw, BF16), (2 * gw, w_in.shape[1], BF16)], tm)
    y_c = _gmlp(u_pre, v_pre, gmlp_norm_g, gmlp_norm_b, gmlp_w_s, gmlp_b_s, tm)
    return y_c, _moba(qkv, rel_bias)


def _route_logits(x, g, wr, rb, E):
    var = jnp.mean(x * x, axis=-1, keepdims=True)
    hn = x * lax.rsqrt(var + EPS) * g
    hn_hi = hn.astype(BF16)
    hn_lo = (hn - hn_hi.astype(F32)).astype(BF16)
    wr_hi, wr_lo = wr
    lg = (jnp.dot(hn_hi, wr_hi, preferred_element_type=F32) + jnp.dot(hn_hi, wr_lo, preferred_element_type=F32)
          + jnp.dot(hn_lo, wr_hi, preferred_element_type=F32))
    return lg.T[:E, :] + rb


def _route_assign(logits, carry, E):
    tm = logits.shape[1]
    rowi = lax.broadcasted_iota(jnp.int32, (E, tm), 0)
    v1 = jnp.max(logits, axis=0, keepdims=True)
    e1 = jnp.min(jnp.where(logits == v1, rowi, E), axis=0, keepdims=True)
    masked = jnp.where(rowi == e1, -jnp.inf, logits)
    v2 = jnp.max(masked, axis=0, keepdims=True)
    e2 = jnp.min(jnp.where(masked == v2, rowi, E), axis=0, keepdims=True)
    ex = jnp.exp(v2 - v1)
    g1 = 1.0 / (1.0 + ex)
    g2 = ex / (1.0 + ex)
    oh1 = rowi == e1
    oh2 = rowi == e2
    cnt = jnp.where(oh1, 1.0, jnp.where(oh2, 1.0, 0.0))
    before = (lax.broadcasted_iota(jnp.int32, (tm, tm), 0) <
              lax.broadcasted_iota(jnp.int32, (tm, tm), 1)).astype(BF16)
    excl = jnp.dot(cnt.astype(BF16), before, preferred_element_type=F32) + carry
    r1 = jnp.sum(jnp.where(oh1, excl, 0.0), axis=0, keepdims=True).astype(jnp.int32)
    r2 = jnp.sum(jnp.where(oh2, excl, 0.0), axis=0, keepdims=True).astype(jnp.int32)
    idx = jnp.where(rowi == 0, e1, jnp.where(rowi == 1, e2, jnp.where(rowi == 2, r1, jnp.where(rowi == 3, r2, 0))))
    gts = jnp.where(rowi == 0, g1, jnp.where(rowi == 1, g2, 0.0))
    return idx, gts, jnp.sum(cnt, axis=1, keepdims=True)


def _router_kernel(res_ref, a_ref, b_ref, wa_ref, wb_ref, g_ref, wr_ref, rb_ref,
                   h_ref, idx_ref, gts_ref, cnt_ref, carry_ref, *, E, sub):
    @pl.when(pl.program_id(0) == 0)
    def _():
        carry_ref[...] = jnp.zeros_like(carry_ref)

    nsub = res_ref.shape[0] // sub
    spans = [slice(j * sub, (j + 1) * sub) for j in range(nsub)]
    xs = [_mix_out(res_ref[sp, :], a_ref[sp, :], b_ref[sp, :], wa_ref, wb_ref) for sp in spans]
    for sp, x in zip(spans, xs):
        h_ref[sp, :] = x
    logits = [_route_logits(x, g_ref[...], (wr_ref[0], wr_ref[1]), rb_ref[:, 0:1], E) for x in xs]
    carry = carry_ref[:, 0:1]
    for sp, lg in zip(spans, logits):
        idx, gts, n_new = _route_assign(lg, carry, E)
        idx_ref[:, sp] = idx
        gts_ref[:, sp] = gts
        carry = carry + n_new
    carry_ref[...] = jnp.broadcast_to(carry, carry_ref.shape)
    cnt_ref[...] = carry_ref[...]


def _proj_router(res2d, ya, yb, w_out, g, router_w, router_b, tm):
    T, D = res2d.shape
    E = router_w.shape[1]
    assert E == SUBLANES
    wa_n = ya.shape[1]
    wa, wb = w_out[:wa_n], w_out[wa_n:]
    wr = jnp.pad(router_w, ((0, 0), (0, LANES - E)))
    wr_hi = wr.astype(BF16)
    wr_pair = jnp.stack([wr_hi, (wr - wr_hi.astype(F32)).astype(BF16)])
    return pl.pallas_call(
        functools.partial(_router_kernel, E=E, sub=_pick_tile(tm, ROUTER_SUB_TILE)),
        out_shape=[
            jax.ShapeDtypeStruct((T, D), F32),
            jax.ShapeDtypeStruct((E, T), jnp.int32),
            jax.ShapeDtypeStruct((E, T), F32),
            jax.ShapeDtypeStruct((E, LANES), F32),
        ],
        grid=(T // tm,),
        in_specs=[
            pl.BlockSpec((tm, D), lambda i: (i, 0)),
            pl.BlockSpec((tm, wa_n), lambda i: (i, 0)),
            pl.BlockSpec((tm, yb.shape[1]), lambda i: (i, 0)),
            pl.BlockSpec(wa.shape, lambda i: (0, 0)),
            pl.BlockSpec(wb.shape, lambda i: (0, 0)),
            pl.BlockSpec((1, D), lambda i: (0, 0)),
            pl.BlockSpec((2, D, LANES), lambda i: (0, 0, 0)),
            pl.BlockSpec((E, LANES), lambda i: (0, 0)),
        ],
        out_specs=[
            pl.BlockSpec((tm, D), lambda i: (i, 0)),
            pl.BlockSpec((E, tm), lambda i: (0, i)),
            pl.BlockSpec((E, tm), lambda i: (0, i)),
            pl.BlockSpec((E, LANES), lambda i: (0, 0)),
        ],
        scratch_shapes=[pltpu.VMEM((E, LANES), F32)],
        compiler_params=_cparams(("arbitrary",)),
        name="proj_moe_router",
    )(res2d, ya, yb, wa, wb, g.reshape(1, D), wr_pair, jnp.broadcast_to(router_b[:, None], (E, LANES)))


DMA_ISSUE_UNROLL = True


def _dispatch_kernel(pos1_ref, pos2_ref, ends_ref, h_ref, xs_hbm, zero_ref, sem, zsem, *, tm, tg, E):
    i = pl.program_id(0)

    @pl.when(i == 0)
    def _():
        zero_ref[...] = jnp.zeros_like(zero_ref)

        def zero_copy(e):
            start = ends_ref[e] - tg
            return pltpu.make_async_copy(zero_ref, xs_hbm.at[pl.ds(pl.multiple_of(start, tg), tg)], zsem)

        def nonempty(e):
            return ends_ref[e] > (ends_ref[e - 1] if e else 0)

        for e in range(E):
            @pl.when(nonempty(e))
            def _(e=e):
                zero_copy(e).start()
        for e in range(E):
            @pl.when(nonempty(e))
            def _(e=e):
                zero_copy(e).wait()

    def body(r, carry):
        t = i * tm + r
        src = h_ref.at[pl.ds(r, 1)]
        pltpu.make_async_copy(src, xs_hbm.at[pl.ds(pos1_ref[t], 1)], sem).start()
        pltpu.make_async_copy(src, xs_hbm.at[pl.ds(pos2_ref[t], 1)], sem).start()
        return carry
    lax.fori_loop(0, tm, body, 0, unroll=DMA_ISSUE_UNROLL)

    for _ in range(TOP_K):
        pltpu.make_async_copy(h_ref, xs_hbm.at[pl.ds(0, tm)], sem).wait()


def _dispatch(pos1, pos2, ends, h2d, n_rows, tm, tg):
    T, D = h2d.shape
    E = ends.shape[0]
    return pl.pallas_call(
        functools.partial(_dispatch_kernel, tm=tm, tg=tg, E=E),
        out_shape=jax.ShapeDtypeStruct((n_rows, D), h2d.dtype),
        grid_spec=pltpu.PrefetchScalarGridSpec(
            num_scalar_prefetch=3,
            grid=(T // tm,),
            in_specs=[pl.BlockSpec((tm, D), lambda i, p1, p2, en: (i, 0))],
            out_specs=pl.BlockSpec(memory_space=pl.ANY),
            scratch_shapes=[pltpu.VMEM((tg, D), h2d.dtype), pltpu.SemaphoreType.DMA, pltpu.SemaphoreType.DMA],
        ),
        compiler_params=_cparams(("arbitrary",)),
        name="moe_dispatch",
    )(pos1, pos2, ends, h2d)


def _gmm_kernel(te_ref, tv_ref, x_ref, g_ref, wg_ref, wu_ref, wd_ref, o_ref, xn_ref, acc_ref, *, rb):
    i = pl.program_id(0)
    f = pl.program_id(1)
    tm = xn_ref.shape[0]
    valid = tv_ref[i]

    @pl.when(f == 0)
    def _():
        acc_ref[...] = jnp.zeros_like(acc_ref)

    @pl.when((valid > 0) & (f == 0))
    def _():
        xn_ref[...] = _rms_bf16(x_ref[...], g_ref[...])

    def swiglu_rows(rows):
        x = xn_ref[rows, :]
        a = jnp.dot(x, wg_ref[0].astype(BF16), preferred_element_type=F32)
        u = jnp.dot(x, wu_ref[0].astype(BF16), preferred_element_type=F32)
        hmid = (a * _sigmoid(a) * u).astype(BF16)
        acc_ref[rows, :] += jnp.dot(hmid, wd_ref[0].astype(BF16), preferred_element_type=F32)

    @pl.when(valid == tm)
    def _():
        swiglu_rows(slice(None))

    for r in range(tm // rb):
        @pl.when((valid < tm) & (valid > r * rb))
        def _(r=r):
            swiglu_rows(slice(r * rb, (r + 1) * rb))

    @pl.when(f == pl.num_programs(1) - 1)
    def _():
        o_ref[...] = acc_ref[...]


def _gmm(tile_expert, tile_valid, xs, g, wg, wu, wd, tm, fc):
    R, D = xs.shape
    F = wg.shape[2]
    nf = F // fc

    def fsel(i, f, tv):
        return jnp.where(tv[i] > 0, f, nf - 1)

    return pl.pallas_call(
        functools.partial(_gmm_kernel, rb=_pick_tile(tm, GMM_TAIL_ROW_BLOCK)),
        out_shape=jax.ShapeDtypeStruct((R, D), F32),
        grid_spec=pltpu.PrefetchScalarGridSpec(
            num_scalar_prefetch=2,
            grid=(R // tm, nf),
            in_specs=[
                pl.BlockSpec((tm, D), lambda i, f, te, tv: (i, 0)),
                pl.BlockSpec((1, D), lambda i, f, te, tv: (0, 0)),
                pl.BlockSpec((1, D, fc), lambda i, f, te, tv: (te[i], 0, fsel(i, f, tv))),
                pl.BlockSpec((1, D, fc), lambda i, f, te, tv: (te[i], 0, fsel(i, f, tv))),
                pl.BlockSpec((1, fc, D), lambda i, f, te, tv: (te[i], fsel(i, f, tv), 0)),
            ],
            out_specs=pl.BlockSpec((tm, D), lambda i, f, te, tv: (i, 0)),
            scratch_shapes=[pltpu.VMEM((tm, D), BF16), pltpu.VMEM((tm, D), F32)],
        ),
        compiler_params=_cparams(("parallel", "arbitrary")),
        name="moe_gmm",
    )(tile_expert, tile_valid, xs, g.reshape(1, D), wg, wu, wd)


def _combine_copies(pos1_ref, pos2_ref, ys_hbm, ya_ref, yb_ref, sems, tile, slot, tm):
    def row_copies(r):
        t = tile * tm + r
        return (pltpu.make_async_copy(ys_hbm.at[pl.ds(pos1_ref[t], 1)], ya_ref.at[slot, pl.ds(r, 1)], sems.at[slot]),
                pltpu.make_async_copy(ys_hbm.at[pl.ds(pos2_ref[t], 1)], yb_ref.at[slot, pl.ds(r, 1)], sems.at[slot]))
    return row_copies


def _combine_kernel(pos1_ref, pos2_ref, h_ref, ga_ref, gb_ref, g_ref, ys_hbm, o_ref, ya_ref, yb_ref, sems, *, tm):
    i = pl.program_id(0)
    n = pl.num_programs(0)
    slot = lax.rem(i, 2)

    def start_tile(tile, slot):
        copies = _combine_copies(pos1_ref, pos2_ref, ys_hbm, ya_ref, yb_ref, sems, tile, slot, tm)

        def body(r, carry):
            a, b = copies(r)
            a.start()
            b.start()
            return carry
        lax.fori_loop(0, tm, body, 0, unroll=DMA_ISSUE_UNROLL)

    @pl.when(i == 0)
    def _():
        start_tile(0, 0)

    @pl.when(i + 1 < n)
    def _():
        start_tile(i + 1, 1 - slot)

    pltpu.make_async_copy(ys_hbm.at[pl.ds(0, tm)], ya_ref.at[slot], sems.at[slot]).wait()
    pltpu.make_async_copy(ys_hbm.at[pl.ds(0, tm)], yb_ref.at[slot], sems.at[slot]).wait()

    h = h_ref[...] + ga_ref[...] * ya_ref[slot] + gb_ref[...] * yb_ref[slot]
    var = jnp.mean(h * h, axis=-1, keepdims=True)
    o_ref[...] = h * lax.rsqrt(var + EPS) * g_ref[...]


def _combine(pos1, pos2, h2d, ga, gb, g, ys, tm):
    T, D = h2d.shape
    row = pl.BlockSpec((tm, D), lambda i, p1, p2: (i, 0))
    colv = pl.BlockSpec((tm, 1), lambda i, p1, p2: (i, 0))
    return pl.pallas_call(
        functools.partial(_combine_kernel, tm=tm),
        out_shape=jax.ShapeDtypeStruct((T, D), F32),
        grid_spec=pltpu.PrefetchScalarGridSpec(
            num_scalar_prefetch=2,
            grid=(T // tm,),
            in_specs=[row, colv, colv, pl.BlockSpec((1, D), lambda i, p1, p2: (0, 0)),
                      pl.BlockSpec(memory_space=pl.ANY)],
            out_specs=row,
            scratch_shapes=[pltpu.VMEM((2, tm, D), F32), pltpu.VMEM((2, tm, D), F32),
                            pltpu.SemaphoreType.DMA((2,))],
        ),
        compiler_params=_cparams(("arbitrary",)),
        name="moe_combine_norm",
    )(pos1, pos2, h2d, ga, gb, g.reshape(1, D), ys)


def _moe_final(res, y_c, y_d, w_out, norm_g, final_g, router_w, router_b, w_gate, w_up, w_down):
    B, S, D = res.shape
    T = B * S
    E = router_w.shape[1]
    h2d, idx, gts, cnt = _proj_router(res.reshape(T, D), y_c.reshape(T, -1), y_d.reshape(T, -1),
                                      w_out.astype(BF16), norm_g, router_w, router_b, _pick_tile(T, 1024))
    e1, e2, r1, r2 = idx[0], idx[1], idx[2], idx[3]
    tm = MOE_ROW_TILE if TOP_K * T >= E * MOE_ROW_TILE else _pick_tile(T, 512)
    counts = cnt[:, 0].astype(jnp.int32)
    padded = ((counts + tm - 1) // tm) * tm
    ends = jnp.cumsum(padded)
    offs = ends - padded
    eids = jnp.arange(E, dtype=jnp.int32)[:, None]
    pos1 = jnp.sum(jnp.where(e1[None, :] == eids, offs[:, None], 0), axis=0) + r1
    pos2 = jnp.sum(jnp.where(e2[None, :] == eids, offs[:, None], 0), axis=0) + r2
    n_tiles = -(-TOP_K * T // tm) + E
    tile_start = jnp.arange(n_tiles, dtype=jnp.int32) * tm
    tile_expert = jnp.minimum(jnp.sum(tile_start[:, None] >= ends[None, :], axis=1), E - 1).astype(jnp.int32)
    tile_valid = jnp.clip((offs + counts)[tile_expert] - tile_start, 0, tm).astype(jnp.int32)
    xs = _dispatch(pos1, pos2, ends.astype(jnp.int32), h2d, n_tiles * tm, _pick_tile(T, 1024), tm)
    ys = _gmm(tile_expert, tile_valid, xs, norm_g, w_gate, w_up, w_down, tm, _pick_tile(w_gate.shape[2], 512))
    out = _combine(pos1, pos2, h2d, gts[0][:, None], gts[1][:, None], final_g, ys, _pick_tile(T, 256))
    return out.reshape(B, S, D)


def kernel(x, norm_mix_g, norm_ffn_g, norm_final_g, ab_w_in, s5_lambda_re, s5_lambda_im, s5_log_dt, s5_b_re, s5_b_im, s5_c_re, s5_c_im, s5_d, s5_w_glu, mlstm_conv_w, mlstm_conv_b, mlstm_gate_b, mlstm_norm_g, ab_w_out, ffn_w_gate, ffn_w_up, ffn_w_down, cd_w_in, gmlp_norm_g, gmlp_norm_b, gmlp_w_s, gmlp_b_s, rel_bias, cd_w_out, moe_router_w, moe_router_b, moe_w_gate, moe_w_up, moe_w_down):
    B, S, D = x.shape
    y_a, y_b = _mix_ab(x, norm_mix_g[0], ab_w_in[0], s5_lambda_re[0], s5_lambda_im[0], s5_log_dt[0],
                       s5_b_re[0], s5_b_im[0], s5_c_re[0], s5_c_im[0], s5_d[0], s5_w_glu[0],
                       mlstm_conv_w[0], mlstm_conv_b[0], mlstm_gate_b[0], mlstm_norm_g[0])
    h = _proj_ffn(x, y_a, y_b, ab_w_out[0].astype(BF16), norm_ffn_g[0], ffn_w_gate[0].astype(BF16),
                  ffn_w_up[0].astype(BF16), ffn_w_down[0].astype(BF16),
                  _pick_tile(S, 512), _pick_tile(ffn_w_gate.shape[2], 1408))
    y_c, y_d = _mix_cd(h, norm_mix_g[1], cd_w_in[0], gmlp_norm_g[0], gmlp_norm_b[0],
                       gmlp_w_s[0], gmlp_b_s[0], rel_bias)
    return _moe_final(h, y_c, y_d, cd_w_out[0], norm_ffn_g[1], norm_final_g, moe_router_w[0], moe_router_b[0],
                      moe_w_gate[0], moe_w_up[0], moe_w_down[0])
```

```python
import functools
import math

import jax
import jax.numpy as jnp
from jax import lax
from jax.experimental import pallas as pl
from jax.experimental.pallas import tpu as pltpu

F32 = jnp.float32
BF16 = jnp.bfloat16
EPS = 1e-5

LANES = 128
SUBLANES = 8
VMEM_LIMIT_BYTES = 56 * 1024 * 1024

S5_GROUP = 16
S5_SLAB_GROUPS = 8
MLSTM_HEADS = 4
MLSTM_CHUNK = 128
MLSTM_SEQS_PER_STEP = 2
CONV_WIDTH = 4
GMLP_GROUPS = 4
GMLP_CHUNK = 128
MOBA_HEADS = 4
MOBA_BLOCK = 256
MOBA_TOPK = 3
REL_BUCKETS = 32
REL_MAX_DIST = 128
TOP_K = 2
ROUTER_SUB_TILE = 512
GMM_TAIL_ROW_BLOCK = 256
MOE_ROW_TILE = 1536


def _cparams(sem):
    return pltpu.CompilerParams(dimension_semantics=sem, vmem_limit_bytes=VMEM_LIMIT_BYTES)


def _rms_bf16(x, g):
    var = jnp.mean(x * x, axis=-1, keepdims=True)
    return (x * lax.rsqrt(var + EPS) * g).astype(BF16)


def _gelu(x):
    return jax.nn.gelu(x, approximate=True)


def _sigmoid(x):
    return 1.0 / (1.0 + jnp.exp(-x))


def _norm_proj_kernel(x_ref, g_ref, w_ref, *out_refs, splits):
    hn = _rms_bf16(x_ref[0], g_ref[...])
    for o_ref, (c0, c1) in zip(out_refs, splits):
        r = jnp.dot(hn, w_ref[:, c0:c1], preferred_element_type=F32)
        o_ref[...] = r.reshape(o_ref.shape).astype(o_ref.dtype)


def _norm_proj(x, g, w, outs, tm):
    B, S, D = x.shape
    splits = tuple((c0, c1) for c0, c1, _ in outs)
    out_shape, out_specs = [], []
    for c0, c1, dt in outs:
        n = c1 - c0
        out_shape.append(jax.ShapeDtypeStruct((B, S, n), dt))
        out_specs.append(pl.BlockSpec((1, tm, n), lambda b, i: (b, i, 0)))
    return pl.pallas_call(
        functools.partial(_norm_proj_kernel, splits=splits),
        out_shape=out_shape,
        grid=(B, S // tm),
        in_specs=[
            pl.BlockSpec((1, tm, D), lambda b, i: (b, i, 0)),
            pl.BlockSpec((1, D), lambda b, i: (0, 0)),
            pl.BlockSpec(w.shape, lambda b, i: (0, 0)),
        ],
        out_specs=out_specs,
        compiler_params=_cparams(("parallel", "parallel")),
        name="norm_proj",
    )(x, g.reshape(1, D), w)


def _s5_kernel(u_hbm, bmat_ref, ar_ref, ai_ref, cmat_ref, d_ref, wglu_ref, y_hbm,
               xr_ref, xi_ref, ubuf, ybuf, in_sems, out_sems, *bufs, lc, nb, nslab, sw):
    i = pl.program_id(0)
    n = pl.num_programs(0)
    slot = lax.rem(i, 2)
    W = ubuf.shape[-1]

    def in_copies(step, slot):
        t0 = pl.multiple_of(step * lc, lc)
        return [pltpu.make_async_copy(u_hbm.at[b, pl.ds(t0, lc), :], ubuf.at[slot, :, b, :], in_sems.at[slot])
                for b in range(nb)]

    def out_copies(step, slot):
        t0 = pl.multiple_of(step * lc, lc)
        return [pltpu.make_async_copy(ybuf.at[slot, :, b, :], y_hbm.at[b, pl.ds(t0, lc), :], out_sems.at[slot])
                for b in range(nb)]

    @pl.when(i == 0)
    def _():
        xr_ref[...] = jnp.zeros_like(xr_ref)
        xi_ref[...] = jnp.zeros_like(xi_ref)
        for c in in_copies(0, 0):
            c.start()

    @pl.when(i + 1 < n)
    def _():
        for c in in_copies(i + 1, 1 - slot):
            c.start()

    for c in in_copies(i, slot):
        c.wait()

    uf = ubuf[slot].reshape(lc * nb, W)
    u = uf.astype(BF16)
    xr_all = xr_ref[...]
    xi_all = xi_ref[...]

    def drive(k):
        bufs[k][...] = jnp.dot(u[:, LANES * k:LANES * (k + 1)], bmat_ref[k], preferred_element_type=F32)

    drive(0)
    ys, new_xr, new_xi = [], [], []
    for k in range(nslab):
        if k + 1 < nslab:
            drive(k + 1)
        st_cols = slice(sw * k, sw * (k + 1))
        ar = jnp.broadcast_to(ar_ref[:, st_cols], (nb, sw))
        ai = jnp.broadcast_to(ai_ref[:, st_cols], (nb, sw))
        xr, xi = xr_all[:, st_cols], xi_all[:, st_cols]
        for t in range(lc):
            rows = slice(t * nb, (t + 1) * nb)
            nxr = ar * xr - ai * xi + bufs[k][rows, :sw]
            nxi = ar * xi + ai * xr + bufs[k][rows, sw:]
            bufs[k][rows, :sw] = nxr
            bufs[k][rows, sw:] = nxi
            xr, xi = nxr, nxi
        new_xr.append(xr)
        new_xi.append(xi)
        ys.append(jnp.dot(bufs[k][...].astype(BF16), cmat_ref[k], preferred_element_type=F32))
    xr_ref[...] = jnp.concatenate(new_xr, axis=-1)
    xi_ref[...] = jnp.concatenate(new_xi, axis=-1)
    y = jnp.concatenate(ys, axis=-1)
    y = _gelu(y + d_ref[...] * uf)
    gl = jnp.dot(y.astype(BF16), wglu_ref[...], preferred_element_type=F32)

    @pl.when(i >= 2)
    def _():
        for c in out_copies(i - 2, slot):
            c.wait()

    ybuf[slot] = (y * _sigmoid(gl)).reshape(lc, nb, W)
    for c in out_copies(i, slot):
        c.start()

    @pl.when(i == n - 1)
    def _():
        for c in out_copies(i, slot):
            c.wait()

        @pl.when(n >= 2)
        def _():
            for c in out_copies(i - 1, 1 - slot):
                c.wait()


def _s5_params(lam_re, lam_im, log_dt, b_re, b_im, c_re, c_im):
    G, P = lam_re.shape
    Hc = b_re.shape[-1]
    dt = jnp.exp(log_dt.astype(F32))[:, None]
    mag = jnp.exp(lam_re * dt)
    ar = mag * jnp.cos(lam_im * dt)
    ai = mag * jnp.sin(lam_im * dt)
    den = lam_re * lam_re + lam_im * lam_im
    cr = ((ar - 1.0) * lam_re + ai * lam_im) / den
    ci = (ai * lam_re - (ar - 1.0) * lam_im) / den
    bb_re = cr[..., None] * b_re - ci[..., None] * b_im
    bb_im = cr[..., None] * b_im + ci[..., None] * b_re
    gs = S5_SLAB_GROUPS
    nslab = G // gs
    eye = jnp.eye(gs, dtype=F32)

    def bd_in(b):
        b = b.reshape(nslab, gs, P, Hc)
        return jnp.einsum('kgph,gj->kghjp', b, eye).reshape(nslab, gs * Hc, gs * P)

    def bd_out(c):
        c = c.reshape(nslab, gs, Hc, P)
        return jnp.einsum('kghp,gj->kgpjh', c, eye).reshape(nslab, gs * P, gs * Hc)

    bmat = jnp.concatenate([bd_in(bb_re), bd_in(bb_im)], axis=-1).astype(BF16)
    cmat = jnp.concatenate([bd_out(c_re), -bd_out(c_im)], axis=1).astype(BF16)
    return bmat, ar.reshape(1, G * P), ai.reshape(1, G * P), cmat


def _s5(u, bmat, ar, ai, cmat, d_skip, w_glu, lc):
    nb, S, W = u.shape
    assert nb % SUBLANES == 0
    nslab = bmat.shape[0]
    sw = bmat.shape[2] // 2
    rows = lc * nb
    return pl.pallas_call(
        functools.partial(_s5_kernel, lc=lc, nb=nb, nslab=nslab, sw=sw),
        out_shape=jax.ShapeDtypeStruct((nb, S, W), F32),
        grid=(S // lc,),
        in_specs=[
            pl.BlockSpec(memory_space=pl.ANY),
            pl.BlockSpec(bmat.shape, lambda i: (0, 0, 0)),
            pl.BlockSpec(ar.shape, lambda i: (0, 0)),
            pl.BlockSpec(ai.shape, lambda i: (0, 0)),
            pl.BlockSpec(cmat.shape, lambda i: (0, 0, 0)),
            pl.BlockSpec((1, W), lambda i: (0, 0)),
            pl.BlockSpec(w_glu.shape, lambda i: (0, 0)),
        ],
        out_specs=pl.BlockSpec(memory_space=pl.ANY),
        scratch_shapes=[pltpu.VMEM((nb, sw * nslab), F32), pltpu.VMEM((nb, sw * nslab), F32),
                        pltpu.VMEM((2, lc, nb, W), F32), pltpu.VMEM((2, lc, nb, W), F32),
                        pltpu.SemaphoreType.DMA((2,)), pltpu.SemaphoreType.DMA((2,))]
        + [pltpu.VMEM((rows, 2 * sw), F32) for _ in range(nslab)],
        compiler_params=_cparams(("arbitrary",)),
        name="s5",
    )(u, bmat, ar, ai, cmat, d_skip.reshape(1, W), w_glu)


def _mlstm_pre(x, g, tail, m_row, cw, cb, *, L, H, Dh):
    W = H * Dh
    halo = SUBLANES
    log_scale = -0.5 * math.log(Dh)

    row = lax.broadcasted_iota(jnp.int32, (L, L), 0)
    col = lax.broadcasted_iota(jnp.int32, (L, L), 1)
    conv = cb + cw[0:1, :] * x.astype(F32)
    corr = jnp.zeros((halo, 2 * W), F32)
    for j in range(1, CONV_WIDTH):
        shift = (row - col == j).astype(BF16)
        conv = conv + cw[j:j + 1, :] * jnp.dot(shift, x, preferred_element_type=F32)
        corr = corr + cw[j:j + 1, :] * tail[halo - j:2 * halo - j, :]
    conv = jnp.concatenate([conv[:halo] + corr, conv[halo:]], axis=0)
    new_tail = x[L - halo:, :].astype(F32)
    qk = (conv * _sigmoid(conv)).astype(BF16)
    log_i = g[:, :LANES]
    f_pre = g[:, LANES:]
    logf = jnp.minimum(f_pre, 0.0) - jnp.log(1.0 + jnp.exp(-jnp.abs(f_pre)))
    causal = row >= col
    bcum = jnp.dot(causal.astype(F32), logf, preferred_element_type=F32, precision=lax.Precision.HIGHEST)
    w_cols = log_i - bcum
    trow = lax.broadcasted_iota(jnp.int32, (L, LANES), 0)
    cmax = w_cols
    k = 1
    while k < L:
        cmax = jnp.maximum(cmax, jnp.where(trow >= k, pltpu.roll(cmax, k, axis=0), -jnp.inf))
        k *= 2
    m_inter = bcum + m_row
    m_t = jnp.maximum(bcum + cmax, m_inter)
    u_cols = bcum - m_t + log_scale
    wi_cols = jnp.exp(m_inter - m_t)
    em_cols = jnp.exp(-m_t)
    b_last = bcum[L - 1:L, :]
    m_loc = b_last + cmax[L - 1:L, :]
    m_new = jnp.maximum(b_last + m_row, m_loc)
    s_prev = jnp.exp(b_last + m_row - m_new)
    s_loc = jnp.exp(m_loc - m_new)
    w_rows = w_cols.T
    return dict(q=qk[:, :W], k=qk[:, W:], causal=causal, u_cols=u_cols, wi_cols=wi_cols, em_cols=em_cols,
                w_rows=w_rows, b_last=b_last, m_loc=m_loc, m_new=m_new, s_prev=s_prev, s_loc=s_loc,
                new_tail=new_tail, log_scale=log_scale)


def _mlstm_heads(pres, vbs, o_pres, cxs, ng, *, L, H, Dh):
    nt = (((1,), (1,)), ((), ()))
    pairs = [(b, h) for b in range(len(pres)) for h in range(H)]
    hs = [slice(h * Dh, (h + 1) * Dh) for h in range(H)]
    hl = [slice(h, h + 1) for h in range(H)]
    ones = jnp.ones((L, Dh), BF16)
    sq_ones = jnp.ones((Dh, Dh), BF16)
    vext = {(b, h): jnp.concatenate([vbs[b][:, hs[h]], ones], axis=1) for b, h in pairs}
    qk_t = {(b, h): lax.dot_general(pres[b]['q'][:, hs[h]], pres[b]['k'][:, hs[h]], nt,
                                    preferred_element_type=F32) for b, h in pairs}
    r2 = {(b, h): jnp.dot(pres[b]['q'][:, hs[h]], cxs[b][h].astype(BF16), preferred_element_type=F32)
          for b, h in pairs}
    s = {}
    for b, h in pairs:
        p = pres[b]
        decay = jnp.exp(jnp.where(p['causal'], p['u_cols'][:, hl[h]] + p['w_rows'][hl[h], :], -jnp.inf))
        s[b, h] = (qk_t[b, h] * decay).astype(BF16)
    r1 = {bh: jnp.dot(s[bh], vext[bh], preferred_element_type=F32) for bh in pairs}
    hh = {}
    for b, h in pairs:
        p = pres[b]
        wi = jnp.broadcast_to(p['wi_cols'][:, hl[h]], (L, Dh))
        num = r1[b, h][:, :Dh] + wi * r2[b, h][:, :Dh]
        den = r1[b, h][:, Dh:] + wi * r2[b, h][:, Dh:]
        hh[b, h] = num / jnp.maximum(jnp.abs(den), jnp.broadcast_to(p['em_cols'][:, hl[h]], (L, Dh)))
    msq = {bh: jnp.dot((hh[bh] * hh[bh]).astype(BF16), sq_ones, preferred_element_type=F32) * (1.0 / Dh)
           for bh in pairs}
    ys = {(b, h): (_sigmoid(o_pres[b][:, hs[h]].astype(F32)) * hh[b, h] * lax.rsqrt(msq[b, h] + EPS)
                   * ng[:, hs[h]]).astype(BF16) for b, h in pairs}
    kwt = {}
    for b, h in pairs:
        p = pres[b]
        wexp = jnp.exp(p['b_last'][:, hl[h]] + p['w_rows'][hl[h], :] - p['m_loc'][:, hl[h]] + p['log_scale'])
        kwt[b, h] = (p['k'][:, hs[h]].astype(F32).T * wexp).astype(BF16)
    cx_loc = {bh: jnp.dot(kwt[bh], vext[bh], preferred_element_type=F32) for bh in pairs}
    new_cx = {(b, h): pres[b]['s_prev'][:, hl[h]] * cxs[b][h] + pres[b]['s_loc'][:, hl[h]] * cx_loc[b, h]
              for b, h in pairs}
    return ys, new_cx


def _mlstm_kernel(qk_ref, v_ref, o_ref, gt_ref, cw_ref, cb_ref, gb_ref, ng_ref, y_ref,
                  tail_ref, c_ref, m_ref, *, L, H, Dh, BB):
    @pl.when(pl.program_id(1) == 0)
    def _():
        tail_ref[...] = jnp.zeros_like(tail_ref)
        c_ref[...] = jnp.zeros_like(c_ref)
        m_ref[...] = jnp.zeros_like(m_ref)

    tails = [tail_ref[bb] for bb in range(BB)]
    cxs = [[c_ref[bb, h] for h in range(H)] for bb in range(BB)]
    m_rows = [m_ref[bb, 0:1, :] for bb in range(BB)]
    pres = [_mlstm_pre(qk_ref[bb], gt_ref[bb] + gb_ref[...], tails[bb], m_rows[bb], cw_ref[...], cb_ref[...],
                       L=L, H=H, Dh=Dh) for bb in range(BB)]
    ys, new_cx = _mlstm_heads(pres, [v_ref[bb].astype(BF16) for bb in range(BB)], [o_ref[bb] for bb in range(BB)],
                              cxs, ng_ref[...], L=L, H=H, Dh=Dh)
    for bb in range(BB):
        for h in range(H):
            y_ref[bb, :, h * Dh:(h + 1) * Dh] = ys[bb, h]
            c_ref[bb, h] = new_cx[bb, h]
        tail_ref[bb, 0:SUBLANES, :] = pres[bb]['new_tail']
        m_ref[bb, 0:1, :] = pres[bb]['m_new']


def _mlstm(qk_pre, v, o_pre, gates, conv_w, conv_b, gate_b_pad, norm_g):
    B, S, W2 = qk_pre.shape
    W = W2 // 2
    H, L = MLSTM_HEADS, MLSTM_CHUNK
    Dh = W // H
    BB = _pick_tile(B, MLSTM_SEQS_PER_STEP)
    return pl.pallas_call(
        functools.partial(_mlstm_kernel, L=L, H=H, Dh=Dh, BB=BB),
        out_shape=jax.ShapeDtypeStruct((B, S, W), BF16),
        grid=(B // BB, S // L),
        in_specs=[
            pl.BlockSpec((BB, L, W2), lambda b, c: (b, c, 0)),
            pl.BlockSpec((BB, L, W), lambda b, c: (b, c, 0)),
            pl.BlockSpec((BB, L, W), lambda b, c: (b, c, 0)),
            pl.BlockSpec((BB, L, 2 * LANES), lambda b, c: (b, c, 0)),
            pl.BlockSpec((CONV_WIDTH, W2), lambda b, c: (0, 0)),
            pl.BlockSpec((1, W2), lambda b, c: (0, 0)),
            pl.BlockSpec((1, 2 * LANES), lambda b, c: (0, 0)),
            pl.BlockSpec((1, W), lambda b, c: (0, 0)),
        ],
        out_specs=pl.BlockSpec((BB, L, W), lambda b, c: (b, c, 0)),
        scratch_shapes=[
            pltpu.VMEM((BB, 2 * SUBLANES, W2), F32),
            pltpu.VMEM((BB, H, Dh, 2 * Dh), F32),
            pltpu.VMEM((BB, SUBLANES, LANES), F32),
        ],
        compiler_params=_cparams(("parallel", "arbitrary")),
        name="mlstm",
    )(qk_pre, v, o_pre, gates, conv_w, conv_b.reshape(1, W2), gate_b_pad, norm_g.reshape(1, W))


def _mix_out(res, a, b, wa_ref, wb_ref):
    acc = jnp.dot(a.astype(BF16), wa_ref[...], preferred_element_type=F32)
    return res + acc + jnp.dot(b.astype(BF16), wb_ref[...], preferred_element_type=F32)


def _proj_ffn_kernel(res_ref, a_ref, b_ref, wa_ref, wb_ref, g_ref, wg_ref, wu_ref, wd_ref, o_ref,
                     hn_ref, acc_ref):
    f = pl.program_id(2)

    @pl.when(f == 0)
    def _():
        h = _mix_out(res_ref[0], a_ref[0], b_ref[0], wa_ref, wb_ref)
        hn_ref[...] = _rms_bf16(h, g_ref[...])
        acc_ref[...] = h

    hn = hn_ref[...]
    a = jnp.dot(hn, wg_ref[...], preferred_element_type=F32)
    u = jnp.dot(hn, wu_ref[...], preferred_element_type=F32)
    hmid = (a * _sigmoid(a) * u).astype(BF16)
    acc_ref[...] += jnp.dot(hmid, wd_ref[...], preferred_element_type=F32)

    @pl.when(f == pl.num_programs(2) - 1)
    def _():
        o_ref[0] = acc_ref[...]


def _proj_ffn(res, ya, yb, w_out, g, wg, wu, wd, tm, fc):
    B, S, D = res.shape
    wa_n = ya.shape[2]
    wa, wb = w_out[:wa_n], w_out[wa_n:]
    F = wg.shape[1]
    return pl.pallas_call(
        _proj_ffn_kernel,
        out_shape=jax.ShapeDtypeStruct((B, S, D), F32),
        grid=(B, S // tm, F // fc),
        in_specs=[
            pl.BlockSpec((1, tm, D), lambda b, i, f: (b, i, 0)),
            pl.BlockSpec((1, tm, wa_n), lambda b, i, f: (b, i, 0)),
            pl.BlockSpec((1, tm, wb.shape[0]), lambda b, i, f: (b, i, 0)),
            pl.BlockSpec(wa.shape, lambda b, i, f: (0, 0)),
            pl.BlockSpec(wb.shape, lambda b, i, f: (0, 0)),
            pl.BlockSpec((1, D), lambda b, i, f: (0, 0)),
            pl.BlockSpec((D, fc), lambda b, i, f: (0, f)),
            pl.BlockSpec((D, fc), lambda b, i, f: (0, f)),
            pl.BlockSpec((fc, D), lambda b, i, f: (f, 0)),
        ],
        out_specs=pl.BlockSpec((1, tm, D), lambda b, i, f: (b, i, 0)),
        scratch_shapes=[pltpu.VMEM((tm, D), BF16), pltpu.VMEM((tm, D), F32)],
        compiler_params=_cparams(("parallel", "parallel", "arbitrary")),
        name="proj_ffn_swiglu",
    )(res, ya, yb, wa, wb, g.reshape(1, D), wg, wu, wd)


def _pick_tile(n, target):
    t = min(n, target)
    while n % t:
        t //= 2
    return t


def _mix_ab(h, norm_g, w_in, lam_re, lam_im, log_dt, b_re, b_im, c_re, c_im, d_skip, w_glu,
            conv_w, conv_b, gate_b, mlstm_norm_g):
    B, S, D = h.shape
    s5w = lam_re.shape[0] * S5_GROUP
    mw = mlstm_norm_g.shape[0]
    c_qk, c_v, c_o, c_if = s5w, s5w + 2 * mw, s5w + 3 * mw, s5w + 4 * mw
    nh = (w_in.shape[1] - c_if) // 2
    lane_pad = ((0, 0), (0, LANES - nh))
    w_pad = jnp.concatenate([w_in[:, :c_if], jnp.pad(w_in[:, c_if:c_if + nh], lane_pad),
                             jnp.pad(w_in[:, c_if + nh:], lane_pad)], axis=1).astype(BF16)
    tm = _pick_tile(S, 512)
    u_s5, qk_pre, v, o_pre, gates = _norm_proj(
        h, norm_g, w_pad,
        [(0, c_qk, F32), (c_qk, c_v, BF16), (c_v, c_o, BF16), (c_o, c_if, BF16), (c_if, c_if + 2 * LANES, F32)],
        tm)
    bmat, ar, ai, cmat = _s5_params(lam_re, lam_im, log_dt, b_re, b_im, c_re, c_im)
    y_a = _s5(u_s5, bmat, ar, ai, cmat, d_skip, w_glu.astype(BF16), lc=_pick_tile(S, 32))
    gate_b_pad = jnp.concatenate([jnp.pad(gate_b[:nh], (0, LANES - nh)),
                                  jnp.pad(gate_b[nh:], (0, LANES - nh))]).reshape(1, 2 * LANES)
    y_b = _mlstm(qk_pre, v, o_pre, gates, conv_w, conv_b, gate_b_pad, mlstm_norm_g)
    return y_a, y_b


def _gmlp_kernel(u_ref, v_ref, ng_ref, nb_ref, ws_ref, bs_ref, y_ref, *, L, G, Dg, nchunk):
    u = _gelu(u_ref[0].astype(F32))
    v = _gelu(v_ref[0].astype(F32))
    mu = jnp.mean(v, axis=-1, keepdims=True)
    vc = v - mu
    var = jnp.mean(vc * vc, axis=-1, keepdims=True)
    vn = (vc * lax.rsqrt(var + EPS) * ng_ref[...] + nb_ref[...]).astype(BF16)
    for c in range(nchunk):
        rows = slice(c * L, (c + 1) * L)
        for g in range(G):
            cols = slice(g * Dg, (g + 1) * Dg)
            s = jnp.dot(ws_ref[g], vn[rows, cols], preferred_element_type=F32) + bs_ref[:, g:g + 1]
            y_ref[0, rows, cols] = (u[rows, cols] * s).astype(y_ref.dtype)


def _gmlp(u_pre, v_pre, norm_g, norm_b, w_s, b_s, tm):
    B, S, W = u_pre.shape
    G, L = GMLP_GROUPS, GMLP_CHUNK
    ws = (w_s * jnp.tril(jnp.ones((L, L), w_s.dtype))).astype(BF16)
    bs = jnp.pad(b_s.T, ((0, 0), (0, LANES - G)))
    return pl.pallas_call(
        functools.partial(_gmlp_kernel, L=L, G=G, Dg=W // G, nchunk=tm // L),
        out_shape=jax.ShapeDtypeStruct((B, S, W), BF16),
        grid=(B, S // tm),
        in_specs=[
            pl.BlockSpec((1, tm, W), lambda b, i: (b, i, 0)),
            pl.BlockSpec((1, tm, W), lambda b, i: (b, i, 0)),
            pl.BlockSpec((1, W), lambda b, i: (0, 0)),
            pl.BlockSpec((1, W), lambda b, i: (0, 0)),
            pl.BlockSpec((G, L, L), lambda b, i: (0, 0, 0)),
            pl.BlockSpec((L, LANES), lambda b, i: (0, 0)),
        ],
        out_specs=pl.BlockSpec((1, tm, W), lambda b, i: (b, i, 0)),
        compiler_params=_cparams(("parallel", "parallel")),
        name="gmlp",
    )(u_pre, v_pre, norm_g.reshape(1, W), norm_b.reshape(1, W), ws, bs)


NEG_BIG = -1e30


def _moba_kernel(q_ref, k_ref, v_ref, t0_ref, t1_ref, bfar_ref, y_ref, vext_ref, *, BL, NB, Dh, topk):
    nt = (((1,), (1,)), ((), ()))
    scale = 1.0 / math.sqrt(Dh)
    vext_ref[:, :Dh] = v_ref[0]
    vext_ref[:, Dh:] = jnp.ones((NB * BL, Dh), BF16)
    bfar = bfar_ref[0][:, 0:1]
    kmean = jnp.concatenate(
        [jnp.mean(k_ref[0, n * BL:(n + 1) * BL, :].astype(F32), axis=0, keepdims=True) for n in range(NB)],
        axis=0)
    nrow = lax.broadcasted_iota(jnp.int32, (NB, BL), 0)
    eye = (lax.broadcasted_iota(jnp.int32, (BL, BL), 0) ==
           lax.broadcasted_iota(jnp.int32, (BL, BL), 1)).astype(BF16)

    def selection_mask(qi):
        qf = q_ref[0, qi * BL:(qi + 1) * BL, :].astype(F32)
        gate = lax.dot_general(kmean, qf, nt, preferred_element_type=F32, precision=lax.Precision.HIGHEST)
        rank = jnp.zeros((NB, BL), F32)
        for m in range(qi):
            gm = gate[m:m + 1, :]
            rank = rank + jnp.where(gm > gate, 1.0, jnp.where(gm == gate, jnp.where(nrow > m, 1.0, 0.0), 0.0))
        sel_rows = jnp.where(nrow < qi, jnp.where(rank < topk, 1.0, 0.0), 0.0)
        sel_pad = jnp.concatenate([sel_rows, jnp.zeros((LANES - NB, BL), F32)], axis=0).astype(BF16)
        sel_cols = lax.dot_general(eye, sel_pad, nt, preferred_element_type=F32)
        return (1.0 - sel_cols) * NEG_BIG

    def logits(qi):
        qs = (q_ref[0, qi * BL:(qi + 1) * BL, :].astype(F32) * scale).astype(BF16)
        raw = [lax.dot_general(qs, k_ref[0, kb * BL:(kb + 1) * BL, :], nt, preferred_element_type=F32)
               for kb in range(qi + 1)]
        neg_cols = selection_mask(qi) if qi > topk else None
        m_tile = None
        pieces = []
        for kb in range(qi + 1):
            s = raw[kb]
            if kb == qi:
                s = s + t0_ref[0]
            elif kb == qi - 1:
                s = s + t1_ref[0]
                if neg_cols is not None:
                    s = s + neg_cols[:, kb:kb + 1]
            else:
                s = s + (bfar if neg_cols is None else bfar + neg_cols[:, kb:kb + 1])
            pieces.append(s)
            for c in range(BL // LANES):
                piece = s[:, c * LANES:(c + 1) * LANES]
                m_tile = piece if m_tile is None else jnp.maximum(m_tile, piece)
        return pieces, m_tile

    nxt = logits(0)
    for qi in range(NB):
        pieces, m_tile = nxt
        if qi + 1 < NB:
            nxt = logits(qi + 1)
        m = jnp.max(m_tile, axis=-1, keepdims=True)
        p = jnp.concatenate([jnp.exp(s - m).astype(BF16) for s in pieces], axis=1)
        acc = jnp.dot(p, vext_ref[:(qi + 1) * BL, :], preferred_element_type=F32)
        y_ref[0, qi * BL:(qi + 1) * BL, :] = (acc[:, :Dh] / acc[:, Dh:]).astype(y_ref.dtype)


def _rel_bucket(n):
    max_exact = REL_BUCKETS // 2
    nf = jnp.maximum(n, 1).astype(F32)
    large = max_exact + (jnp.log(nf / max_exact) / math.log(REL_MAX_DIST / max_exact)
                         * (REL_BUCKETS - max_exact)).astype(jnp.int32)
    large = jnp.minimum(large, REL_BUCKETS - 1)
    return jnp.where(n < max_exact, n, large)


def _moba(qkv, rel_bias):
    B, S, W3 = qkv.shape
    H, BL = MOBA_HEADS, MOBA_BLOCK
    W = W3 // 3
    Dh = W // H
    NB = S // BL
    assert BL + 1 >= REL_MAX_DIST and Dh == LANES and S % BL == 0
    i = jnp.arange(BL, dtype=jnp.int32)[:, None]
    j = jnp.arange(BL, dtype=jnp.int32)[None, :]
    buckets = jnp.arange(REL_BUCKETS, dtype=jnp.int32)

    def bias_tile(dist):
        onehot = (_rel_bucket(dist)[:, :, None] == buckets).astype(F32)
        return jnp.einsum('ijb,bh->hij', onehot, rel_bias, precision=lax.Precision.HIGHEST)

    t0 = jnp.where(i >= j, bias_tile(jnp.maximum(i - j, 0)), NEG_BIG)
    t1 = bias_tile(BL + i - j)
    bfar = jnp.broadcast_to(rel_bias.T[:, REL_BUCKETS - 1][:, None, None], (H, 1, LANES))
    return pl.pallas_call(
        functools.partial(_moba_kernel, BL=BL, NB=NB, Dh=Dh, topk=min(MOBA_TOPK, NB)),
        out_shape=jax.ShapeDtypeStruct((B, S, W), BF16),
        grid=(B, H),
        in_specs=[
            pl.BlockSpec((1, S, Dh), lambda b, h: (b, 0, h)),
            pl.BlockSpec((1, S, Dh), lambda b, h: (b, 0, H + h)),
            pl.BlockSpec((1, S, Dh), lambda b, h: (b, 0, 2 * H + h)),
            pl.BlockSpec((1, BL, BL), lambda b, h: (h, 0, 0)),
            pl.BlockSpec((1, BL, BL), lambda b, h: (h, 0, 0)),
            pl.BlockSpec((1, 1, LANES), lambda b, h: (h, 0, 0)),
        ],
        out_specs=pl.BlockSpec((1, S, Dh), lambda b, h: (b, 0, h)),
        scratch_shapes=[pltpu.VMEM((S, 2 * Dh), BF16)],
        compiler_params=_cparams(("parallel", "parallel")),
        name="moba",
    )(qkv, qkv, qkv, t0, t1, bfar)


def _mix_cd(h, norm_g, w_in, gmlp_norm_g, gmlp_norm_b, gmlp_w_s, gmlp_b_s, rel_bias):
    B, S, D = h.shape
    gw = gmlp_norm_g.shape[0]
    tm = _pick_tile(S, 512)
    u_pre, v_pre, qkv = _norm_proj(
        h, norm_g, w_in.astype(BF16),
        [(0, gw, BF16), (gw, 2 * gw, BF16), (2 * gw, w_in.shape[1], BF16)], tm)
    y_c = _gmlp(u_pre, v_pre, gmlp_norm_g, gmlp_norm_b, gmlp_w_s, gmlp_b_s, tm)
    return y_c, _moba(qkv, rel_bias)


def _route_logits(hn, wr, rb, E):
    hn_hi = hn.astype(BF16)
    hn_lo = (hn - hn_hi.astype(F32)).astype(BF16)
    wr_hi, wr_lo = wr
    lg = (jnp.dot(hn_hi, wr_hi, preferred_element_type=F32) + jnp.dot(hn_hi, wr_lo, preferred_element_type=F32)
          + jnp.dot(hn_lo, wr_hi, preferred_element_type=F32))
    return lg.T[:E, :] + rb


def _route_assign(logits, carry, E):
    tm = logits.shape[1]
    rowi = lax.broadcasted_iota(jnp.int32, (E, tm), 0)
    v1 = jnp.max(logits, axis=0, keepdims=True)
    e1 = jnp.min(jnp.where(logits == v1, rowi, E), axis=0, keepdims=True)
    masked = jnp.where(rowi == e1, -jnp.inf, logits)
    v2 = jnp.max(masked, axis=0, keepdims=True)
    e2 = jnp.min(jnp.where(masked == v2, rowi, E), axis=0, keepdims=True)
    ex = jnp.exp(v2 - v1)
    g1 = 1.0 / (1.0 + ex)
    g2 = ex / (1.0 + ex)
    oh1 = rowi == e1
    oh2 = rowi == e2
    cnt = jnp.where(oh1, 1.0, jnp.where(oh2, 1.0, 0.0))
    before = (lax.broadcasted_iota(jnp.int32, (tm, tm), 0) <
              lax.broadcasted_iota(jnp.int32, (tm, tm), 1)).astype(BF16)
    excl = jnp.dot(cnt.astype(BF16), before, preferred_element_type=F32) + carry
    r1 = jnp.sum(jnp.where(oh1, excl, 0.0), axis=0, keepdims=True).astype(jnp.int32)
    r2 = jnp.sum(jnp.where(oh2, excl, 0.0), axis=0, keepdims=True).astype(jnp.int32)
    idx = jnp.where(rowi == 0, e1, jnp.where(rowi == 1, e2, jnp.where(rowi == 2, r1, jnp.where(rowi == 3, r2, 0))))
    gts = jnp.where(rowi == 0, g1, jnp.where(rowi == 1, g2, 0.0))
    return idx, gts, jnp.sum(cnt, axis=1, keepdims=True)


def _router_kernel(res_ref, a_ref, b_ref, wa_ref, wb_ref, g_ref, wr_ref, rb_ref,
                   h_ref, hn_ref, idx_ref, gts_ref, cnt_ref, carry_ref, *, E, sub):
    @pl.when(pl.program_id(0) == 0)
    def _():
        carry_ref[...] = jnp.zeros_like(carry_ref)

    nsub = res_ref.shape[0] // sub
    spans = [slice(j * sub, (j + 1) * sub) for j in range(nsub)]
    xs = [_mix_out(res_ref[sp, :], a_ref[sp, :], b_ref[sp, :], wa_ref, wb_ref) for sp in spans]
    hns = []
    for sp, x in zip(spans, xs):
        h_ref[sp, :] = x
        var = jnp.mean(x * x, axis=-1, keepdims=True)
        hns.append(x * lax.rsqrt(var + EPS) * g_ref[...])
        hn_ref[sp, :] = hns[-1]
    logits = [_route_logits(hn, (wr_ref[0], wr_ref[1]), rb_ref[:, 0:1], E) for hn in hns]
    carry = carry_ref[:, 0:1]
    for sp, lg in zip(spans, logits):
        idx, gts, n_new = _route_assign(lg, carry, E)
        idx_ref[:, sp] = idx
        gts_ref[:, sp] = gts
        carry = carry + n_new
    carry_ref[...] = jnp.broadcast_to(carry, carry_ref.shape)
    cnt_ref[...] = carry_ref[...]


def _proj_router(res2d, ya, yb, w_out, g, router_w, router_b, tm):
    T, D = res2d.shape
    E = router_w.shape[1]
    assert E == SUBLANES
    wa_n = ya.shape[1]
    wa, wb = w_out[:wa_n], w_out[wa_n:]
    wr = jnp.pad(router_w, ((0, 0), (0, LANES - E)))
    wr_hi = wr.astype(BF16)
    wr_pair = jnp.stack([wr_hi, (wr - wr_hi.astype(F32)).astype(BF16)])
    return pl.pallas_call(
        functools.partial(_router_kernel, E=E, sub=_pick_tile(tm, ROUTER_SUB_TILE)),
        out_shape=[
            jax.ShapeDtypeStruct((T, D), F32),
            jax.ShapeDtypeStruct((T, D), F32),
            jax.ShapeDtypeStruct((E, T), jnp.int32),
            jax.ShapeDtypeStruct((E, T), F32),
            jax.ShapeDtypeStruct((E, LANES), F32),
        ],
        grid=(T // tm,),
        in_specs=[
            pl.BlockSpec((tm, D), lambda i: (i, 0)),
            pl.BlockSpec((tm, wa_n), lambda i: (i, 0)),
            pl.BlockSpec((tm, yb.shape[1]), lambda i: (i, 0)),
            pl.BlockSpec(wa.shape, lambda i: (0, 0)),
            pl.BlockSpec(wb.shape, lambda i: (0, 0)),
            pl.BlockSpec((1, D), lambda i: (0, 0)),
            pl.BlockSpec((2, D, LANES), lambda i: (0, 0, 0)),
            pl.BlockSpec((E, LANES), lambda i: (0, 0)),
        ],
        out_specs=[
            pl.BlockSpec((tm, D), lambda i: (i, 0)),
            pl.BlockSpec((tm, D), lambda i: (i, 0)),
            pl.BlockSpec((E, tm), lambda i: (0, i)),
            pl.BlockSpec((E, tm), lambda i: (0, i)),
            pl.BlockSpec((E, LANES), lambda i: (0, 0)),
        ],
        scratch_shapes=[pltpu.VMEM((E, LANES), F32)],
        compiler_params=_cparams(("arbitrary",)),
        name="proj_moe_router",
    )(res2d, ya, yb, wa, wb, g.reshape(1, D), wr_pair, jnp.broadcast_to(router_b[:, None], (E, LANES)))


DMA_ISSUE_UNROLL = True


def _dispatch_kernel(pos1_ref, pos2_ref, ends_ref, h_ref, xs_hbm, zero_ref, sem, zsem, *, tm, tg, E):
    i = pl.program_id(0)

    @pl.when(i == 0)
    def _():
        zero_ref[...] = jnp.zeros_like(zero_ref)

        def zero_copy(e):
            start = ends_ref[e] - tg
            return pltpu.make_async_copy(zero_ref, xs_hbm.at[pl.ds(pl.multiple_of(start, tg), tg)], zsem)

        def nonempty(e):
            return ends_ref[e] > (ends_ref[e - 1] if e else 0)

        for e in range(E):
            @pl.when(nonempty(e))
            def _(e=e):
                zero_copy(e).start()
        for e in range(E):
            @pl.when(nonempty(e))
            def _(e=e):
                zero_copy(e).wait()

    def body(r, carry):
        t = i * tm + r
        src = h_ref.at[pl.ds(r, 1)]
        pltpu.make_async_copy(src, xs_hbm.at[pl.ds(pos1_ref[t], 1)], sem).start()
        pltpu.make_async_copy(src, xs_hbm.at[pl.ds(pos2_ref[t], 1)], sem).start()
        return carry
    lax.fori_loop(0, tm, body, 0, unroll=DMA_ISSUE_UNROLL)

    for _ in range(TOP_K):
        pltpu.make_async_copy(h_ref, xs_hbm.at[pl.ds(0, tm)], sem).wait()


def _dispatch(pos1, pos2, ends, h2d, n_rows, tm, tg):
    T, D = h2d.shape
    E = ends.shape[0]
    return pl.pallas_call(
        functools.partial(_dispatch_kernel, tm=tm, tg=tg, E=E),
        out_shape=jax.ShapeDtypeStruct((n_rows, D), h2d.dtype),
        grid_spec=pltpu.PrefetchScalarGridSpec(
            num_scalar_prefetch=3,
            grid=(T // tm,),
            in_specs=[pl.BlockSpec((tm, D), lambda i, p1, p2, en: (i, 0))],
            out_specs=pl.BlockSpec(memory_space=pl.ANY),
            scratch_shapes=[pltpu.VMEM((tg, D), h2d.dtype), pltpu.SemaphoreType.DMA, pltpu.SemaphoreType.DMA],
        ),
        compiler_params=_cparams(("arbitrary",)),
        name="moe_dispatch",
    )(pos1, pos2, ends, h2d)


def _gmm_kernel(te_ref, tv_ref, x_ref, wg_ref, wu_ref, wd_ref, o_ref, xn_ref, acc_ref, *, rb):
    i = pl.program_id(0)
    f = pl.program_id(1)
    tm = xn_ref.shape[0]
    valid = tv_ref[i]

    @pl.when(f == 0)
    def _():
        acc_ref[...] = jnp.zeros_like(acc_ref)

    @pl.when((valid > 0) & (f == 0))
    def _():
        xn_ref[...] = x_ref[...].astype(BF16)

    def swiglu_rows(rows):
        x = xn_ref[rows, :]
        a = jnp.dot(x, wg_ref[0].astype(BF16), preferred_element_type=F32)
        u = jnp.dot(x, wu_ref[0].astype(BF16), preferred_element_type=F32)
        hmid = (a * _sigmoid(a) * u).astype(BF16)
        acc_ref[rows, :] += jnp.dot(hmid, wd_ref[0].astype(BF16), preferred_element_type=F32)

    @pl.when(valid == tm)
    def _():
        swiglu_rows(slice(None))

    for r in range(tm // rb):
        @pl.when((valid < tm) & (valid > r * rb))
        def _(r=r):
            swiglu_rows(slice(r * rb, (r + 1) * rb))

    @pl.when(f == pl.num_programs(1) - 1)
    def _():
        o_ref[...] = acc_ref[...]


def _gmm(tile_expert, tile_valid, xs, wg, wu, wd, tm, fc):
    R, D = xs.shape
    F = wg.shape[2]
    nf = F // fc

    def fsel(i, f, tv):
        return jnp.where(tv[i] > 0, f, nf - 1)

    return pl.pallas_call(
        functools.partial(_gmm_kernel, rb=_pick_tile(tm, GMM_TAIL_ROW_BLOCK)),
        out_shape=jax.ShapeDtypeStruct((R, D), F32),
        grid_spec=pltpu.PrefetchScalarGridSpec(
            num_scalar_prefetch=2,
            grid=(R // tm, nf),
            in_specs=[
                pl.BlockSpec((tm, D), lambda i, f, te, tv: (i, 0)),
                pl.BlockSpec((1, D, fc), lambda i, f, te, tv: (te[i], 0, fsel(i, f, tv))),
                pl.BlockSpec((1, D, fc), lambda i, f, te, tv: (te[i], 0, fsel(i, f, tv))),
                pl.BlockSpec((1, fc, D), lambda i, f, te, tv: (te[i], fsel(i, f, tv), 0)),
            ],
            out_specs=pl.BlockSpec((tm, D), lambda i, f, te, tv: (i, 0)),
            scratch_shapes=[pltpu.VMEM((tm, D), BF16), pltpu.VMEM((tm, D), F32)],
        ),
        compiler_params=_cparams(("parallel", "arbitrary")),
        name="moe_gmm",
    )(tile_expert, tile_valid, xs, wg, wu, wd)


def _combine_copies(pos1_ref, pos2_ref, ys_hbm, ya_ref, yb_ref, sems, tile, slot, tm):
    def row_copies(r):
        t = tile * tm + r
        return (pltpu.make_async_copy(ys_hbm.at[pl.ds(pos1_ref[t], 1)], ya_ref.at[slot, pl.ds(r, 1)], sems.at[slot]),
                pltpu.make_async_copy(ys_hbm.at[pl.ds(pos2_ref[t], 1)], yb_ref.at[slot, pl.ds(r, 1)], sems.at[slot]))
    return row_copies


def _combine_kernel(pos1_ref, pos2_ref, h_ref, ga_ref, gb_ref, g_ref, ys_hbm, o_ref, ya_ref, yb_ref, sems, *, tm):
    i = pl.program_id(0)
    n = pl.num_programs(0)
    slot = lax.rem(i, 2)

    def start_tile(tile, slot):
        copies = _combine_copies(pos1_ref, pos2_ref, ys_hbm, ya_ref, yb_ref, sems, tile, slot, tm)

        def body(r, carry):
            a, b = copies(r)
            a.start()
            b.start()
            return carry
        lax.fori_loop(0, tm, body, 0, unroll=DMA_ISSUE_UNROLL)

    @pl.when(i == 0)
    def _():
        start_tile(0, 0)

    @pl.when(i + 1 < n)
    def _():
        start_tile(i + 1, 1 - slot)

    pltpu.make_async_copy(ys_hbm.at[pl.ds(0, tm)], ya_ref.at[slot], sems.at[slot]).wait()
    pltpu.make_async_copy(ys_hbm.at[pl.ds(0, tm)], yb_ref.at[slot], sems.at[slot]).wait()

    h = h_ref[...] + ga_ref[...] * ya_ref[slot] + gb_ref[...] * yb_ref[slot]
    var = jnp.mean(h * h, axis=-1, keepdims=True)
    o_ref[...] = h * lax.rsqrt(var + EPS) * g_ref[...]


def _combine(pos1, pos2, h2d, ga, gb, g, ys, tm):
    T, D = h2d.shape
    row = pl.BlockSpec((tm, D), lambda i, p1, p2: (i, 0))
    colv = pl.BlockSpec((tm, 1), lambda i, p1, p2: (i, 0))
    return pl.pallas_call(
        functools.partial(_combine_kernel, tm=tm),
        out_shape=jax.ShapeDtypeStruct((T, D), F32),
        grid_spec=pltpu.PrefetchScalarGridSpec(
            num_scalar_prefetch=2,
            grid=(T // tm,),
            in_specs=[row, colv, colv, pl.BlockSpec((1, D), lambda i, p1, p2: (0, 0)),
                      pl.BlockSpec(memory_space=pl.ANY)],
            out_specs=row,
            scratch_shapes=[pltpu.VMEM((2, tm, D), F32), pltpu.VMEM((2, tm, D), F32),
                            pltpu.SemaphoreType.DMA((2,))],
        ),
        compiler_params=_cparams(("arbitrary",)),
        name="moe_combine_norm",
    )(pos1, pos2, h2d, ga, gb, g.reshape(1, D), ys)


def _moe_final(res, y_c, y_d, w_out, norm_g, final_g, router_w, router_b, w_gate, w_up, w_down):
    B, S, D = res.shape
    T = B * S
    E = router_w.shape[1]
    h2d, hn2d, idx, gts, cnt = _proj_router(res.reshape(T, D), y_c.reshape(T, -1), y_d.reshape(T, -1),
                                            w_out.astype(BF16), norm_g, router_w, router_b, _pick_tile(T, 1024))
    e1, e2, r1, r2 = idx[0], idx[1], idx[2], idx[3]
    tm = MOE_ROW_TILE if TOP_K * T >= E * MOE_ROW_TILE else _pick_tile(T, 512)
    counts = cnt[:, 0].astype(jnp.int32)
    padded = ((counts + tm - 1) // tm) * tm
    ends = jnp.cumsum(padded)
    offs = ends - padded
    eids = jnp.arange(E, dtype=jnp.int32)[:, None]
    pos1 = jnp.sum(jnp.where(e1[None, :] == eids, offs[:, None], 0), axis=0) + r1
    pos2 = jnp.sum(jnp.where(e2[None, :] == eids, offs[:, None], 0), axis=0) + r2
    n_tiles = -(-TOP_K * T // tm) + E
    tile_start = jnp.arange(n_tiles, dtype=jnp.int32) * tm
    tile_expert = jnp.minimum(jnp.sum(tile_start[:, None] >= ends[None, :], axis=1), E - 1).astype(jnp.int32)
    tile_valid = jnp.clip((offs + counts)[tile_expert] - tile_start, 0, tm).astype(jnp.int32)
    xs = _dispatch(pos1, pos2, ends.astype(jnp.int32), hn2d, n_tiles * tm, _pick_tile(T, 1024), tm)
    ys = _gmm(tile_expert, tile_valid, xs, w_gate, w_up, w_down, tm, _pick_tile(w_gate.shape[2], 512))
    out = _combine(pos1, pos2, h2d, gts[0][:, None], gts[1][:, None], final_g, ys, _pick_tile(T, 256))
    return out.reshape(B, S, D)


def kernel(x, norm_mix_g, norm_ffn_g, norm_final_g, ab_w_in, s5_lambda_re, s5_lambda_im, s5_log_dt, s5_b_re, s5_b_im, s5_c_re, s5_c_im, s5_d, s5_w_glu, mlstm_conv_w, mlstm_conv_b, mlstm_gate_b, mlstm_norm_g, ab_w_out, ffn_w_gate, ffn_w_up, ffn_w_down, cd_w_in, gmlp_norm_g, gmlp_norm_b, gmlp_w_s, gmlp_b_s, rel_bias, cd_w_out, moe_router_w, moe_router_b, moe_w_gate, moe_w_up, moe_w_down):
    B, S, D = x.shape
    y_a, y_b = _mix_ab(x, norm_mix_g[0], ab_w_in[0], s5_lambda_re[0], s5_lambda_im[0], s5_log_dt[0],
                       s5_b_re[0], s5_b_im[0], s5_c_re[0], s5_c_im[0], s5_d[0], s5_w_glu[0],
                       mlstm_conv_w[0], mlstm_conv_b[0], mlstm_gate_b[0], mlstm_norm_g[0])
    h = _proj_ffn(x, y_a, y_b, ab_w_out[0].astype(BF16), norm_ffn_g[0], ffn_w_gate[0].astype(BF16),
                  ffn_w_up[0].astype(BF16), ffn_w_down[0].astype(BF16),
                  _pick_tile(S, 512), _pick_tile(ffn_w_gate.shape[2], 1408))
    y_c, y_d = _mix_cd(h, norm_mix_g[1], cd_w_in[0], gmlp_norm_g[0], gmlp_norm_b[0],
                       gmlp_w_s[0], gmlp_b_s[0], rel_bias)
    return _moe_final(h, y_c, y_d, cd_w_out[0], norm_ffn_g[1], norm_final_g, moe_router_w[0], moe_router_b[0],
                      moe_w_gate[0], moe_w_up[0], moe_w_down[0])
```

```python
import functools
import math

import jax
import jax.numpy as jnp
from jax import lax
from jax.experimental import pallas as pl
from jax.experimental.pallas import tpu as pltpu

F32 = jnp.float32
BF16 = jnp.bfloat16
EPS = 1e-5

LANES = 128
SUBLANES = 8
VMEM_LIMIT_BYTES = 56 * 1024 * 1024

S5_GROUP = 16
S5_SLAB_GROUPS = 8
MLSTM_HEADS = 4
MLSTM_CHUNK = 128
MLSTM_SEQS_PER_STEP = 2
CONV_WIDTH = 4
GMLP_GROUPS = 4
GMLP_CHUNK = 128
MOBA_HEADS = 4
MOBA_BLOCK = 256
MOBA_TOPK = 3
REL_BUCKETS = 32
REL_MAX_DIST = 128
TOP_K = 2
ROUTER_SUB_TILE = 512
GMM_TAIL_ROW_BLOCK = 256
MOE_ROW_TILE = 1536


def _cparams(sem):
    return pltpu.CompilerParams(dimension_semantics=sem, vmem_limit_bytes=VMEM_LIMIT_BYTES)


def _rms_bf16(x, g):
    var = jnp.mean(x * x, axis=-1, keepdims=True)
    return (x * lax.rsqrt(var + EPS) * g).astype(BF16)


def _gelu(x):
    return jax.nn.gelu(x, approximate=True)


def _sigmoid(x):
    return 1.0 / (1.0 + jnp.exp(-x))


def _norm_proj_kernel(x_ref, g_ref, w_ref, *out_refs, splits):
    hn = _rms_bf16(x_ref[0], g_ref[...])
    for o_ref, (c0, c1) in zip(out_refs, splits):
        r = jnp.dot(hn, w_ref[:, c0:c1], preferred_element_type=F32)
        o_ref[...] = r.reshape(o_ref.shape).astype(o_ref.dtype)


def _norm_proj(x, g, w, outs, tm):
    B, S, D = x.shape
    splits = tuple((c0, c1) for c0, c1, _ in outs)
    out_shape, out_specs = [], []
    for c0, c1, dt in outs:
        n = c1 - c0
        out_shape.append(jax.ShapeDtypeStruct((B, S, n), dt))
        out_specs.append(pl.BlockSpec((1, tm, n), lambda b, i: (b, i, 0)))
    return pl.pallas_call(
        functools.partial(_norm_proj_kernel, splits=splits),
        out_shape=out_shape,
        grid=(B, S // tm),
        in_specs=[
            pl.BlockSpec((1, tm, D), lambda b, i: (b, i, 0)),
            pl.BlockSpec((1, D), lambda b, i: (0, 0)),
            pl.BlockSpec(w.shape, lambda b, i: (0, 0)),
        ],
        out_specs=out_specs,
        compiler_params=_cparams(("parallel", "parallel")),
        name="norm_proj",
    )(x, g.reshape(1, D), w)


def _s5_kernel(u_hbm, bmat_ref, ar_ref, ai_ref, cmat_ref, d_ref, wglu_ref, y_hbm,
               xr_ref, xi_ref, ubuf, ybuf, in_sems, out_sems, *bufs, lc, nb, nslab, sw):
    i = pl.program_id(0)
    n = pl.num_programs(0)
    slot = lax.rem(i, 2)
    W = ubuf.shape[-1]

    def in_copies(step, slot):
        t0 = pl.multiple_of(step * lc, lc)
        return [pltpu.make_async_copy(u_hbm.at[b, pl.ds(t0, lc), :], ubuf.at[slot, :, b, :], in_sems.at[slot])
                for b in range(nb)]

    def out_copies(step, slot):
        t0 = pl.multiple_of(step * lc, lc)
        return [pltpu.make_async_copy(ybuf.at[slot, :, b, :], y_hbm.at[b, pl.ds(t0, lc), :], out_sems.at[slot])
                for b in range(nb)]

    @pl.when(i == 0)
    def _():
        xr_ref[...] = jnp.zeros_like(xr_ref)
        xi_ref[...] = jnp.zeros_like(xi_ref)
        for c in in_copies(0, 0):
            c.start()

    @pl.when(i + 1 < n)
    def _():
        for c in in_copies(i + 1, 1 - slot):
            c.start()

    for c in in_copies(i, slot):
        c.wait()

    uf = ubuf[slot].reshape(lc * nb, W)
    u = uf.astype(BF16)
    xr_all = xr_ref[...]
    xi_all = xi_ref[...]

    def drive(k):
        bufs[k][...] = jnp.dot(u[:, LANES * k:LANES * (k + 1)], bmat_ref[k], preferred_element_type=F32)

    drive(0)
    ys, new_xr, new_xi = [], [], []
    for k in range(nslab):
        if k + 1 < nslab:
            drive(k + 1)
        st_cols = slice(sw * k, sw * (k + 1))
        ar = jnp.broadcast_to(ar_ref[:, st_cols], (nb, sw))
        ai = jnp.broadcast_to(ai_ref[:, st_cols], (nb, sw))
        xr, xi = xr_all[:, st_cols], xi_all[:, st_cols]
        for t in range(lc):
            rows = slice(t * nb, (t + 1) * nb)
            nxr = ar * xr - ai * xi + bufs[k][rows, :sw]
            nxi = ar * xi + ai * xr + bufs[k][rows, sw:]
            bufs[k][rows, :sw] = nxr
            bufs[k][rows, sw:] = nxi
            xr, xi = nxr, nxi
        new_xr.append(xr)
        new_xi.append(xi)
        ys.append(jnp.dot(bufs[k][...].astype(BF16), cmat_ref[k], preferred_element_type=F32))
    xr_ref[...] = jnp.concatenate(new_xr, axis=-1)
    xi_ref[...] = jnp.concatenate(new_xi, axis=-1)
    y = jnp.concatenate(ys, axis=-1)
    y = _gelu(y + d_ref[...] * uf)
    gl = jnp.dot(y.astype(BF16), wglu_ref[...], preferred_element_type=F32)

    @pl.when(i >= 2)
    def _():
        for c in out_copies(i - 2, slot):
            c.wait()

    ybuf[slot] = (y * _sigmoid(gl)).reshape(lc, nb, W)
    for c in out_copies(i, slot):
        c.start()

    @pl.when(i == n - 1)
    def _():
        for c in out_copies(i, slot):
            c.wait()

        @pl.when(n >= 2)
        def _():
            for c in out_copies(i - 1, 1 - slot):
                c.wait()


def _s5_params(lam_re, lam_im, log_dt, b_re, b_im, c_re, c_im):
    G, P = lam_re.shape
    Hc = b_re.shape[-1]
    dt = jnp.exp(log_dt.astype(F32))[:, None]
    mag = jnp.exp(lam_re * dt)
    ar = mag * jnp.cos(lam_im * dt)
    ai = mag * jnp.sin(lam_im * dt)
    den = lam_re * lam_re + lam_im * lam_im
    cr = ((ar - 1.0) * lam_re + ai * lam_im) / den
    ci = (ai * lam_re - (ar - 1.0) * lam_im) / den
    bb_re = cr[..., None] * b_re - ci[..., None] * b_im
    bb_im = cr[..., None] * b_im + ci[..., None] * b_re
    gs = S5_SLAB_GROUPS
    nslab = G // gs
    eye = jnp.eye(gs, dtype=F32)

    def bd_in(b):
        b = b.reshape(nslab, gs, P, Hc)
        return jnp.einsum('kgph,gj->kghjp', b, eye).reshape(nslab, gs * Hc, gs * P)

    def bd_out(c):
        c = c.reshape(nslab, gs, Hc, P)
        return jnp.einsum('kghp,gj->kgpjh', c, eye).reshape(nslab, gs * P, gs * Hc)

    bmat = jnp.concatenate([bd_in(bb_re), bd_in(bb_im)], axis=-1).astype(BF16)
    cmat = jnp.concatenate([bd_out(c_re), -bd_out(c_im)], axis=1).astype(BF16)
    return bmat, ar.reshape(1, G * P), ai.reshape(1, G * P), cmat


def _s5(u, bmat, ar, ai, cmat, d_skip, w_glu, lc):
    nb, S, W = u.shape
    assert nb % SUBLANES == 0
    nslab = bmat.shape[0]
    sw = bmat.shape[2] // 2
    rows = lc * nb
    return pl.pallas_call(
        functools.partial(_s5_kernel, lc=lc, nb=nb, nslab=nslab, sw=sw),
        out_shape=jax.ShapeDtypeStruct((nb, S, W), F32),
        grid=(S // lc,),
        in_specs=[
            pl.BlockSpec(memory_space=pl.ANY),
            pl.BlockSpec(bmat.shape, lambda i: (0, 0, 0)),
            pl.BlockSpec(ar.shape, lambda i: (0, 0)),
            pl.BlockSpec(ai.shape, lambda i: (0, 0)),
            pl.BlockSpec(cmat.shape, lambda i: (0, 0, 0)),
            pl.BlockSpec((1, W), lambda i: (0, 0)),
            pl.BlockSpec(w_glu.shape, lambda i: (0, 0)),
        ],
        out_specs=pl.BlockSpec(memory_space=pl.ANY),
        scratch_shapes=[pltpu.VMEM((nb, sw * nslab), F32), pltpu.VMEM((nb, sw * nslab), F32),
                        pltpu.VMEM((2, lc, nb, W), F32), pltpu.VMEM((2, lc, nb, W), F32),
                        pltpu.SemaphoreType.DMA((2,)), pltpu.SemaphoreType.DMA((2,))]
        + [pltpu.VMEM((rows, 2 * sw), F32) for _ in range(nslab)],
        compiler_params=_cparams(("arbitrary",)),
        name="s5",
    )(u, bmat, ar, ai, cmat, d_skip.reshape(1, W), w_glu)


def _mlstm_pre(x, g, tail, m_row, cw, cb, *, L, H, Dh):
    W = H * Dh
    halo = SUBLANES
    log_scale = -0.5 * math.log(Dh)

    row = lax.broadcasted_iota(jnp.int32, (L, L), 0)
    col = lax.broadcasted_iota(jnp.int32, (L, L), 1)
    conv = cb + cw[0:1, :] * x.astype(F32)
    corr = jnp.zeros((halo, 2 * W), F32)
    for j in range(1, CONV_WIDTH):
        shift = (row - col == j).astype(BF16)
        conv = conv + cw[j:j + 1, :] * jnp.dot(shift, x, preferred_element_type=F32)
        corr = corr + cw[j:j + 1, :] * tail[halo - j:2 * halo - j, :]
    conv = jnp.concatenate([conv[:halo] + corr, conv[halo:]], axis=0)
    new_tail = x[L - halo:, :].astype(F32)
    qk = (conv * _sigmoid(conv)).astype(BF16)
    log_i = g[:, :LANES]
    f_pre = g[:, LANES:]
    logf = jnp.minimum(f_pre, 0.0) - jnp.log(1.0 + jnp.exp(-jnp.abs(f_pre)))
    causal = row >= col
    bcum = jnp.dot(causal.astype(F32), logf, preferred_element_type=F32, precision=lax.Precision.HIGHEST)
    w_cols = log_i - bcum
    trow = lax.broadcasted_iota(jnp.int32, (L, LANES), 0)
    cmax = w_cols
    k = 1
    while k < L:
        cmax = jnp.maximum(cmax, jnp.where(trow >= k, pltpu.roll(cmax, k, axis=0), -jnp.inf))
        k *= 2
    m_inter = bcum + m_row
    m_t = jnp.maximum(bcum + cmax, m_inter)
    u_cols = bcum - m_t + log_scale
    wi_cols = jnp.exp(m_inter - m_t)
    em_cols = jnp.exp(-m_t)
    b_last = bcum[L - 1:L, :]
    m_loc = b_last + cmax[L - 1:L, :]
    m_new = jnp.maximum(b_last + m_row, m_loc)
    s_prev = jnp.exp(b_last + m_row - m_new)
    s_loc = jnp.exp(m_loc - m_new)
    w_rows = w_cols.T
    return dict(q=qk[:, :W], k=qk[:, W:], causal=causal, u_cols=u_cols, wi_cols=wi_cols, em_cols=em_cols,
                w_rows=w_rows, b_last=b_last, m_loc=m_loc, m_new=m_new, s_prev=s_prev, s_loc=s_loc,
                new_tail=new_tail, log_scale=log_scale)


def _mlstm_heads(pres, vbs, o_pres, cxs, ng, *, L, H, Dh):
    nt = (((1,), (1,)), ((), ()))
    pairs = [(b, h) for b in range(len(pres)) for h in range(H)]
    hs = [slice(h * Dh, (h + 1) * Dh) for h in range(H)]
    hl = [slice(h, h + 1) for h in range(H)]
    ones = jnp.ones((L, Dh), BF16)
    sq_ones = jnp.ones((Dh, Dh), BF16)
    vext = {(b, h): jnp.concatenate([vbs[b][:, hs[h]], ones], axis=1) for b, h in pairs}
    qk_t = {(b, h): lax.dot_general(pres[b]['q'][:, hs[h]], pres[b]['k'][:, hs[h]], nt,
                                    preferred_element_type=F32) for b, h in pairs}
    r2 = {(b, h): jnp.dot(pres[b]['q'][:, hs[h]], cxs[b][h].astype(BF16), preferred_element_type=F32)
          for b, h in pairs}
    s = {}
    for b, h in pairs:
        p = pres[b]
        decay = jnp.exp(jnp.where(p['causal'], p['u_cols'][:, hl[h]] + p['w_rows'][hl[h], :], -jnp.inf))
        s[b, h] = (qk_t[b, h] * decay).astype(BF16)
    r1 = {bh: jnp.dot(s[bh], vext[bh], preferred_element_type=F32) for bh in pairs}
    hh = {}
    for b, h in pairs:
        p = pres[b]
        wi = jnp.broadcast_to(p['wi_cols'][:, hl[h]], (L, Dh))
        num = r1[b, h][:, :Dh] + wi * r2[b, h][:, :Dh]
        den = r1[b, h][:, Dh:] + wi * r2[b, h][:, Dh:]
        hh[b, h] = num / jnp.maximum(jnp.abs(den), jnp.broadcast_to(p['em_cols'][:, hl[h]], (L, Dh)))
    msq = {bh: jnp.dot((hh[bh] * hh[bh]).astype(BF16), sq_ones, preferred_element_type=F32) * (1.0 / Dh)
           for bh in pairs}
    ys = {(b, h): (_sigmoid(o_pres[b][:, hs[h]].astype(F32)) * hh[b, h] * lax.rsqrt(msq[b, h] + EPS)
                   * ng[:, hs[h]]).astype(BF16) for b, h in pairs}
    kwt = {}
    for b, h in pairs:
        p = pres[b]
        wexp = jnp.exp(p['b_last'][:, hl[h]] + p['w_rows'][hl[h], :] - p['m_loc'][:, hl[h]] + p['log_scale'])
        kwt[b, h] = (p['k'][:, hs[h]].astype(F32).T * wexp).astype(BF16)
    cx_loc = {bh: jnp.dot(kwt[bh], vext[bh], preferred_element_type=F32) for bh in pairs}
    new_cx = {(b, h): pres[b]['s_prev'][:, hl[h]] * cxs[b][h] + pres[b]['s_loc'][:, hl[h]] * cx_loc[b, h]
              for b, h in pairs}
    return ys, new_cx


def _mlstm_kernel(qk_ref, v_ref, o_ref, gt_ref, cw_ref, cb_ref, gb_ref, ng_ref, y_ref,
                  tail_ref, c_ref, m_ref, *, L, H, Dh, BB):
    @pl.when(pl.program_id(1) == 0)
    def _():
        tail_ref[...] = jnp.zeros_like(tail_ref)
        c_ref[...] = jnp.zeros_like(c_ref)
        m_ref[...] = jnp.zeros_like(m_ref)

    tails = [tail_ref[bb] for bb in range(BB)]
    cxs = [[c_ref[bb, h] for h in range(H)] for bb in range(BB)]
    m_rows = [m_ref[bb, 0:1, :] for bb in range(BB)]
    pres = [_mlstm_pre(qk_ref[bb], gt_ref[bb] + gb_ref[...], tails[bb], m_rows[bb], cw_ref[...], cb_ref[...],
                       L=L, H=H, Dh=Dh) for bb in range(BB)]
    ys, new_cx = _mlstm_heads(pres, [v_ref[bb].astype(BF16) for bb in range(BB)], [o_ref[bb] for bb in range(BB)],
                              cxs, ng_ref[...], L=L, H=H, Dh=Dh)
    for bb in range(BB):
        for h in range(H):
            y_ref[bb, :, h * Dh:(h + 1) * Dh] = ys[bb, h]
            c_ref[bb, h] = new_cx[bb, h]
        tail_ref[bb, 0:SUBLANES, :] = pres[bb]['new_tail']
        m_ref[bb, 0:1, :] = pres[bb]['m_new']


def _mlstm(qk_pre, v, o_pre, gates, conv_w, conv_b, gate_b_pad, norm_g):
    B, S, W2 = qk_pre.shape
    W = W2 // 2
    H, L = MLSTM_HEADS, MLSTM_CHUNK
    Dh = W // H
    BB = _pick_tile(B, MLSTM_SEQS_PER_STEP)
    return pl.pallas_call(
        functools.partial(_mlstm_kernel, L=L, H=H, Dh=Dh, BB=BB),
        out_shape=jax.ShapeDtypeStruct((B, S, W), BF16),
        grid=(B // BB, S // L),
        in_specs=[
            pl.BlockSpec((BB, L, W2), lambda b, c: (b, c, 0)),
            pl.BlockSpec((BB, L, W), lambda b, c: (b, c, 0)),
            pl.BlockSpec((BB, L, W), lambda b, c: (b, c, 0)),
            pl.BlockSpec((BB, L, 2 * LANES), lambda b, c: (b, c, 0)),
            pl.BlockSpec((CONV_WIDTH, W2), lambda b, c: (0, 0)),
            pl.BlockSpec((1, W2), lambda b, c: (0, 0)),
            pl.BlockSpec((1, 2 * LANES), lambda b, c: (0, 0)),
            pl.BlockSpec((1, W), lambda b, c: (0, 0)),
        ],
        out_specs=pl.BlockSpec((BB, L, W), lambda b, c: (b, c, 0)),
        scratch_shapes=[
            pltpu.VMEM((BB, 2 * SUBLANES, W2), F32),
            pltpu.VMEM((BB, H, Dh, 2 * Dh), F32),
            pltpu.VMEM((BB, SUBLANES, LANES), F32),
        ],
        compiler_params=_cparams(("parallel", "arbitrary")),
        name="mlstm",
    )(qk_pre, v, o_pre, gates, conv_w, conv_b.reshape(1, W2), gate_b_pad, norm_g.reshape(1, W))


def _mix_out(res, a, b, wa_ref, wb_ref):
    acc = jnp.dot(a.astype(BF16), wa_ref[...], preferred_element_type=F32)
    return res + acc + jnp.dot(b.astype(BF16), wb_ref[...], preferred_element_type=F32)


def _proj_ffn_kernel(res_ref, a_ref, b_ref, wa_ref, wb_ref, g_ref, wg_ref, wu_ref, wd_ref, o_ref, hn_ref):
    f = pl.program_id(2)

    @pl.when(f == 0)
    def _():
        h = _mix_out(res_ref[0], a_ref[0], b_ref[0], wa_ref, wb_ref)
        hn_ref[...] = _rms_bf16(h, g_ref[...])
        o_ref[0] = h

    hn = hn_ref[...]
    a = jnp.dot(hn, wg_ref[...], preferred_element_type=F32)
    u = jnp.dot(hn, wu_ref[...], preferred_element_type=F32)
    hmid = (a * _sigmoid(a) * u).astype(BF16)
    o_ref[0] += jnp.dot(hmid, wd_ref[...], preferred_element_type=F32)


def _proj_ffn(res, ya, yb, w_out, g, wg, wu, wd, tm, fc):
    B, S, D = res.shape
    wa_n = ya.shape[2]
    wa, wb = w_out[:wa_n], w_out[wa_n:]
    F = wg.shape[1]
    return pl.pallas_call(
        _proj_ffn_kernel,
        out_shape=jax.ShapeDtypeStruct((B, S, D), F32),
        grid=(B, S // tm, F // fc),
        in_specs=[
            pl.BlockSpec((1, tm, D), lambda b, i, f: (b, i, 0)),
            pl.BlockSpec((1, tm, wa_n), lambda b, i, f: (b, i, 0)),
            pl.BlockSpec((1, tm, wb.shape[0]), lambda b, i, f: (b, i, 0)),
            pl.BlockSpec(wa.shape, lambda b, i, f: (0, 0)),
            pl.BlockSpec(wb.shape, lambda b, i, f: (0, 0)),
            pl.BlockSpec((1, D), lambda b, i, f: (0, 0)),
            pl.BlockSpec((D, fc), lambda b, i, f: (0, f)),
            pl.BlockSpec((D, fc), lambda b, i, f: (0, f)),
            pl.BlockSpec((fc, D), lambda b, i, f: (f, 0)),
        ],
        out_specs=pl.BlockSpec((1, tm, D), lambda b, i, f: (b, i, 0)),
        scratch_shapes=[pltpu.VMEM((tm, D), BF16)],
        compiler_params=_cparams(("parallel", "parallel", "arbitrary")),
        name="proj_ffn_swiglu",
    )(res, ya, yb, wa, wb, g.reshape(1, D), wg, wu, wd)


def _pick_tile(n, target):
    t = min(n, target)
    while n % t:
        t //= 2
    return t


def _mix_ab(h, norm_g, w_in, lam_re, lam_im, log_dt, b_re, b_im, c_re, c_im, d_skip, w_glu,
            conv_w, conv_b, gate_b, mlstm_norm_g):
    B, S, D = h.shape
    s5w = lam_re.shape[0] * S5_GROUP
    mw = mlstm_norm_g.shape[0]
    c_qk, c_v, c_o, c_if = s5w, s5w + 2 * mw, s5w + 3 * mw, s5w + 4 * mw
    nh = (w_in.shape[1] - c_if) // 2
    lane_pad = ((0, 0), (0, LANES - nh))
    w_pad = jnp.concatenate([w_in[:, :c_if], jnp.pad(w_in[:, c_if:c_if + nh], lane_pad),
                             jnp.pad(w_in[:, c_if + nh:], lane_pad)], axis=1).astype(BF16)
    tm = _pick_tile(S, 512)
    u_s5, qk_pre, v, o_pre, gates = _norm_proj(
        h, norm_g, w_pad,
        [(0, c_qk, F32), (c_qk, c_v, BF16), (c_v, c_o, BF16), (c_o, c_if, BF16), (c_if, c_if + 2 * LANES, F32)],
        tm)
    bmat, ar, ai, cmat = _s5_params(lam_re, lam_im, log_dt, b_re, b_im, c_re, c_im)
    y_a = _s5(u_s5, bmat, ar, ai, cmat, d_skip, w_glu.astype(BF16), lc=_pick_tile(S, 32))
    gate_b_pad = jnp.concatenate([jnp.pad(gate_b[:nh], (0, LANES - nh)),
                                  jnp.pad(gate_b[nh:], (0, LANES - nh))]).reshape(1, 2 * LANES)
    y_b = _mlstm(qk_pre, v, o_pre, gates, conv_w, conv_b, gate_b_pad, mlstm_norm_g)
    return y_a, y_b


def _gmlp_kernel(u_ref, v_ref, ng_ref, nb_ref, ws_ref, bs_ref, y_ref, *, L, G, Dg, nchunk):
    u = _gelu(u_ref[0].astype(F32))
    v = _gelu(v_ref[0].astype(F32))
    mu = jnp.mean(v, axis=-1, keepdims=True)
    vc = v - mu
    var = jnp.mean(vc * vc, axis=-1, keepdims=True)
    vn = (vc * lax.rsqrt(var + EPS) * ng_ref[...] + nb_ref[...]).astype(BF16)
    for c in range(nchunk):
        rows = slice(c * L, (c + 1) * L)
        for g in range(G):
            cols = slice(g * Dg, (g + 1) * Dg)
            s = jnp.dot(ws_ref[g], vn[rows, cols], preferred_element_type=F32) + bs_ref[:, g:g + 1]
            y_ref[0, rows, cols] = (u[rows, cols] * s).astype(y_ref.dtype)


def _gmlp(u_pre, v_pre, norm_g, norm_b, w_s, b_s, tm):
    B, S, W = u_pre.shape
    G, L = GMLP_GROUPS, GMLP_CHUNK
    ws = (w_s * jnp.tril(jnp.ones((L, L), w_s.dtype))).astype(BF16)
    bs = jnp.pad(b_s.T, ((0, 0), (0, LANES - G)))
    return pl.pallas_call(
        functools.partial(_gmlp_kernel, L=L, G=G, Dg=W // G, nchunk=tm // L),
        out_shape=jax.ShapeDtypeStruct((B, S, W), BF16),
        grid=(B, S // tm),
        in_specs=[
            pl.BlockSpec((1, tm, W), lambda b, i: (b, i, 0)),
            pl.BlockSpec((1, tm, W), lambda b, i: (b, i, 0)),
            pl.BlockSpec((1, W), lambda b, i: (0, 0)),
            pl.BlockSpec((1, W), lambda b, i: (0, 0)),
            pl.BlockSpec((G, L, L), lambda b, i: (0, 0, 0)),
            pl.BlockSpec((L, LANES), lambda b, i: (0, 0)),
        ],
        out_specs=pl.BlockSpec((1, tm, W), lambda b, i: (b, i, 0)),
        compiler_params=_cparams(("parallel", "parallel")),
        name="gmlp",
    )(u_pre, v_pre, norm_g.reshape(1, W), norm_b.reshape(1, W), ws, bs)


NEG_BIG = -1e30


def _moba_kernel(q_ref, k_ref, v_ref, t0_ref, t1_ref, bfar_ref, y_ref, vext_ref, *, BL, NB, Dh, topk):
    nt = (((1,), (1,)), ((), ()))
    scale = 1.0 / math.sqrt(Dh)
    vext_ref[:, :Dh] = v_ref[0]
    vext_ref[:, Dh:] = jnp.ones((NB * BL, Dh), BF16)
    bfar = bfar_ref[0][:, 0:1]
    kmean = jnp.concatenate(
        [jnp.mean(k_ref[0, n * BL:(n + 1) * BL, :].astype(F32), axis=0, keepdims=True) for n in range(NB)],
        axis=0)
    nrow = lax.broadcasted_iota(jnp.int32, (NB, BL), 0)
    eye = (lax.broadcasted_iota(jnp.int32, (BL, BL), 0) ==
           lax.broadcasted_iota(jnp.int32, (BL, BL), 1)).astype(BF16)

    def selection_mask(qi):
        qf = q_ref[0, qi * BL:(qi + 1) * BL, :].astype(F32)
        gate = lax.dot_general(kmean, qf, nt, preferred_element_type=F32, precision=lax.Precision.HIGHEST)
        rank = jnp.zeros((NB, BL), F32)
        for m in range(qi):
            gm = gate[m:m + 1, :]
            rank = rank + jnp.where(gm > gate, 1.0, jnp.where(gm == gate, jnp.where(nrow > m, 1.0, 0.0), 0.0))
        sel_rows = jnp.where(nrow < qi, jnp.where(rank < topk, 1.0, 0.0), 0.0)
        sel_pad = jnp.concatenate([sel_rows, jnp.zeros((LANES - NB, BL), F32)], axis=0).astype(BF16)
        sel_cols = lax.dot_general(eye, sel_pad, nt, preferred_element_type=F32)
        return (1.0 - sel_cols) * NEG_BIG

    def logits(qi):
        qs = (q_ref[0, qi * BL:(qi + 1) * BL, :].astype(F32) * scale).astype(BF16)
        raw = [lax.dot_general(qs, k_ref[0, kb * BL:(kb + 1) * BL, :], nt, preferred_element_type=F32)
               for kb in range(qi + 1)]
        neg_cols = selection_mask(qi) if qi > topk else None
        m_tile = None
        pieces = []
        for kb in range(qi + 1):
            s = raw[kb]
            if kb == qi:
                s = s + t0_ref[0]
            elif kb == qi - 1:
                s = s + t1_ref[0]
                if neg_cols is not None:
                    s = s + neg_cols[:, kb:kb + 1]
            else:
                s = s + (bfar if neg_cols is None else bfar + neg_cols[:, kb:kb + 1])
            pieces.append(s)
            for c in range(BL // LANES):
                piece = s[:, c * LANES:(c + 1) * LANES]
                m_tile = piece if m_tile is None else jnp.maximum(m_tile, piece)
        return pieces, m_tile

    nxt = logits(0)
    for qi in range(NB):
        pieces, m_tile = nxt
        if qi + 1 < NB:
            nxt = logits(qi + 1)
        m = jnp.max(m_tile, axis=-1, keepdims=True)
        p = jnp.concatenate([jnp.exp(s - m).astype(BF16) for s in pieces], axis=1)
        acc = jnp.dot(p, vext_ref[:(qi + 1) * BL, :], preferred_element_type=F32)
        y_ref[0, qi * BL:(qi + 1) * BL, :] = (acc[:, :Dh] / acc[:, Dh:]).astype(y_ref.dtype)


def _rel_bucket(n):
    max_exact = REL_BUCKETS // 2
    nf = jnp.maximum(n, 1).astype(F32)
    large = max_exact + (jnp.log(nf / max_exact) / math.log(REL_MAX_DIST / max_exact)
                         * (REL_BUCKETS - max_exact)).astype(jnp.int32)
    large = jnp.minimum(large, REL_BUCKETS - 1)
    return jnp.where(n < max_exact, n, large)


def _moba(qkv, rel_bias):
    B, S, W3 = qkv.shape
    H, BL = MOBA_HEADS, MOBA_BLOCK
    W = W3 // 3
    Dh = W // H
    NB = S // BL
    assert BL + 1 >= REL_MAX_DIST and Dh == LANES and S % BL == 0
    i = jnp.arange(BL, dtype=jnp.int32)[:, None]
    j = jnp.arange(BL, dtype=jnp.int32)[None, :]
    buckets = jnp.arange(REL_BUCKETS, dtype=jnp.int32)

    def bias_tile(dist):
        onehot = (_rel_bucket(dist)[:, :, None] == buckets).astype(F32)
        return jnp.einsum('ijb,bh->hij', onehot, rel_bias, precision=lax.Precision.HIGHEST)

    t0 = jnp.where(i >= j, bias_tile(jnp.maximum(i - j, 0)), NEG_BIG)
    t1 = bias_tile(BL + i - j)
    bfar = jnp.broadcast_to(rel_bias.T[:, REL_BUCKETS - 1][:, None, None], (H, 1, LANES))
    return pl.pallas_call(
        functools.partial(_moba_kernel, BL=BL, NB=NB, Dh=Dh, topk=min(MOBA_TOPK, NB)),
        out_shape=jax.ShapeDtypeStruct((B, S, W), BF16),
        grid=(B, H),
        in_specs=[
            pl.BlockSpec((1, S, Dh), lambda b, h: (b, 0, h)),
            pl.BlockSpec((1, S, Dh), lambda b, h: (b, 0, H + h)),
            pl.BlockSpec((1, S, Dh), lambda b, h: (b, 0, 2 * H + h)),
            pl.BlockSpec((1, BL, BL), lambda b, h: (h, 0, 0)),
            pl.BlockSpec((1, BL, BL), lambda b, h: (h, 0, 0)),
            pl.BlockSpec((1, 1, LANES), lambda b, h: (h, 0, 0)),
        ],
        out_specs=pl.BlockSpec((1, S, Dh), lambda b, h: (b, 0, h)),
        scratch_shapes=[pltpu.VMEM((S, 2 * Dh), BF16)],
        compiler_params=_cparams(("parallel", "parallel")),
        name="moba",
    )(qkv, qkv, qkv, t0, t1, bfar)


def _mix_cd(h, norm_g, w_in, gmlp_norm_g, gmlp_norm_b, gmlp_w_s, gmlp_b_s, rel_bias):
    B, S, D = h.shape
    gw = gmlp_norm_g.shape[0]
    tm = _pick_tile(S, 512)
    u_pre, v_pre, qkv = _norm_proj(
        h, norm_g, w_in.astype(BF16),
        [(0, gw, BF16), (gw, 2 * gw, BF16), (2 * gw, w_in.shape[1], BF16)], tm)
    y_c = _gmlp(u_pre, v_pre, gmlp_norm_g, gmlp_norm_b, gmlp_w_s, gmlp_b_s, tm)
    return y_c, _moba(qkv, rel_bias)


def _route_logits(hn, wr, rb, E):
    hn_hi = hn.astype(BF16)
    hn_lo = (hn - hn_hi.astype(F32)).astype(BF16)
    wr_hi, wr_lo = wr
    lg = (jnp.dot(hn_hi, wr_hi, preferred_element_type=F32) + jnp.dot(hn_hi, wr_lo, preferred_element_type=F32)
          + jnp.dot(hn_lo, wr_hi, preferred_element_type=F32))
    return lg.T[:E, :] + rb


def _route_assign(logits, carry, E):
    tm = logits.shape[1]
    rowi = lax.broadcasted_iota(jnp.int32, (E, tm), 0)
    v1 = jnp.max(logits, axis=0, keepdims=True)
    e1 = jnp.min(jnp.where(logits == v1, rowi, E), axis=0, keepdims=True)
    masked = jnp.where(rowi == e1, -jnp.inf, logits)
    v2 = jnp.max(masked, axis=0, keepdims=True)
    e2 = jnp.min(jnp.where(masked == v2, rowi, E), axis=0, keepdims=True)
    ex = jnp.exp(v2 - v1)
    g1 = 1.0 / (1.0 + ex)
    g2 = ex / (1.0 + ex)
    oh1 = rowi == e1
    oh2 = rowi == e2
    cnt = jnp.where(oh1, 1.0, jnp.where(oh2, 1.0, 0.0))
    before = (lax.broadcasted_iota(jnp.int32, (tm, tm), 0) <
              lax.broadcasted_iota(jnp.int32, (tm, tm), 1)).astype(BF16)
    excl = jnp.dot(cnt.astype(BF16), before, preferred_element_type=F32) + carry
    r1 = jnp.sum(jnp.where(oh1, excl, 0.0), axis=0, keepdims=True).astype(jnp.int32)
    r2 = jnp.sum(jnp.where(oh2, excl, 0.0), axis=0, keepdims=True).astype(jnp.int32)
    idx = jnp.where(rowi == 0, e1, jnp.where(rowi == 1, e2, jnp.where(rowi == 2, r1, jnp.where(rowi == 3, r2, 0))))
    gts = jnp.where(rowi == 0, g1, jnp.where(rowi == 1, g2, 0.0))
    return idx, gts, jnp.sum(cnt, axis=1, keepdims=True)


def _router_kernel(res_ref, a_ref, b_ref, wa_ref, wb_ref, g_ref, wr_ref, rb_ref,
                   h_ref, hn_ref, idx_ref, gts_ref, cnt_ref, carry_ref, *, E, sub):
    @pl.when(pl.program_id(0) == 0)
    def _():
        carry_ref[...] = jnp.zeros_like(carry_ref)

    nsub = res_ref.shape[0] // sub
    spans = [slice(j * sub, (j + 1) * sub) for j in range(nsub)]
    xs = [_mix_out(res_ref[sp, :], a_ref[sp, :], b_ref[sp, :], wa_ref, wb_ref) for sp in spans]
    hns = []
    for sp, x in zip(spans, xs):
        h_ref[sp, :] = x
        var = jnp.mean(x * x, axis=-1, keepdims=True)
        hns.append(x * lax.rsqrt(var + EPS) * g_ref[...])
        hn_ref[sp, :] = hns[-1]
    logits = [_route_logits(hn, (wr_ref[0], wr_ref[1]), rb_ref[:, 0:1], E) for hn in hns]
    carry = carry_ref[:, 0:1]
    for sp, lg in zip(spans, logits):
        idx, gts, n_new = _route_assign(lg, carry, E)
        idx_ref[:, sp] = idx
        gts_ref[:, sp] = gts
        carry = carry + n_new
    carry_ref[...] = jnp.broadcast_to(carry, carry_ref.shape)
    cnt_ref[...] = carry_ref[...]


def _proj_router(res2d, ya, yb, w_out, g, router_w, router_b, tm):
    T, D = res2d.shape
    E = router_w.shape[1]
    assert E == SUBLANES
    wa_n = ya.shape[1]
    wa, wb = w_out[:wa_n], w_out[wa_n:]
    wr = jnp.pad(router_w, ((0, 0), (0, LANES - E)))
    wr_hi = wr.astype(BF16)
    wr_pair = jnp.stack([wr_hi, (wr - wr_hi.astype(F32)).astype(BF16)])
    return pl.pallas_call(
        functools.partial(_router_kernel, E=E, sub=_pick_tile(tm, ROUTER_SUB_TILE)),
        out_shape=[
            jax.ShapeDtypeStruct((T, D), F32),
            jax.ShapeDtypeStruct((T, D), F32),
            jax.ShapeDtypeStruct((E, T), jnp.int32),
            jax.ShapeDtypeStruct((E, T), F32),
            jax.ShapeDtypeStruct((E, LANES), F32),
        ],
        grid=(T // tm,),
        in_specs=[
            pl.BlockSpec((tm, D), lambda i: (i, 0)),
            pl.BlockSpec((tm, wa_n), lambda i: (i, 0)),
            pl.BlockSpec((tm, yb.shape[1]), lambda i: (i, 0)),
            pl.BlockSpec(wa.shape, lambda i: (0, 0)),
            pl.BlockSpec(wb.shape, lambda i: (0, 0)),
            pl.BlockSpec((1, D), lambda i: (0, 0)),
            pl.BlockSpec((2, D, LANES), lambda i: (0, 0, 0)),
            pl.BlockSpec((E, LANES), lambda i: (0, 0)),
        ],
        out_specs=[
            pl.BlockSpec((tm, D), lambda i: (i, 0)),
            pl.BlockSpec((tm, D), lambda i: (i, 0)),
            pl.BlockSpec((E, tm), lambda i: (0, i)),
            pl.BlockSpec((E, tm), lambda i: (0, i)),
            pl.BlockSpec((E, LANES), lambda i: (0, 0)),
        ],
        scratch_shapes=[pltpu.VMEM((E, LANES), F32)],
        compiler_params=_cparams(("arbitrary",)),
        name="proj_moe_router",
    )(res2d, ya, yb, wa, wb, g.reshape(1, D), wr_pair, jnp.broadcast_to(router_b[:, None], (E, LANES)))


DMA_ISSUE_UNROLL = True


def _dispatch_kernel(pos1_ref, pos2_ref, ends_ref, h_ref, xs_hbm, zero_ref, sem, zsem, *, tm, tg, E):
    i = pl.program_id(0)

    @pl.when(i == 0)
    def _():
        zero_ref[...] = jnp.zeros_like(zero_ref)

        def zero_copy(e):
            start = ends_ref[e] - tg
            return pltpu.make_async_copy(zero_ref, xs_hbm.at[pl.ds(pl.multiple_of(start, tg), tg)], zsem)

        def nonempty(e):
            return ends_ref[e] > (ends_ref[e - 1] if e else 0)

        for e in range(E):
            @pl.when(nonempty(e))
            def _(e=e):
                zero_copy(e).start()
        for e in range(E):
            @pl.when(nonempty(e))
            def _(e=e):
                zero_copy(e).wait()

    def body(r, carry):
        t = i * tm + r
        src = h_ref.at[pl.ds(r, 1)]
        pltpu.make_async_copy(src, xs_hbm.at[pl.ds(pos1_ref[t], 1)], sem).start()
        pltpu.make_async_copy(src, xs_hbm.at[pl.ds(pos2_ref[t], 1)], sem).start()
        return carry
    lax.fori_loop(0, tm, body, 0, unroll=DMA_ISSUE_UNROLL)

    for _ in range(TOP_K):
        pltpu.make_async_copy(h_ref, xs_hbm.at[pl.ds(0, tm)], sem).wait()


def _dispatch(pos1, pos2, ends, h2d, n_rows, tm, tg):
    T, D = h2d.shape
    E = ends.shape[0]
    return pl.pallas_call(
        functools.partial(_dispatch_kernel, tm=tm, tg=tg, E=E),
        out_shape=jax.ShapeDtypeStruct((n_rows, D), h2d.dtype),
        grid_spec=pltpu.PrefetchScalarGridSpec(
            num_scalar_prefetch=3,
            grid=(T // tm,),
            in_specs=[pl.BlockSpec((tm, D), lambda i, p1, p2, en: (i, 0))],
            out_specs=pl.BlockSpec(memory_space=pl.ANY),
            scratch_shapes=[pltpu.VMEM((tg, D), h2d.dtype), pltpu.SemaphoreType.DMA, pltpu.SemaphoreType.DMA],
        ),
        compiler_params=_cparams(("arbitrary",)),
        name="moe_dispatch",
    )(pos1, pos2, ends, h2d)


def _gmm_kernel(te_ref, tv_ref, x_ref, wg_ref, wu_ref, wd_ref, o_ref, xn_ref, *, rb):
    i = pl.program_id(0)
    f = pl.program_id(1)
    tm = xn_ref.shape[0]
    valid = tv_ref[i]

    @pl.when(f == 0)
    def _():
        o_ref[...] = jnp.zeros_like(o_ref)

    @pl.when((valid > 0) & (f == 0))
    def _():
        xn_ref[...] = x_ref[...].astype(BF16)

    def swiglu_rows(rows):
        x = xn_ref[rows, :]
        a = jnp.dot(x, wg_ref[0].astype(BF16), preferred_element_type=F32)
        u = jnp.dot(x, wu_ref[0].astype(BF16), preferred_element_type=F32)
        hmid = (a * _sigmoid(a) * u).astype(BF16)
        o_ref[rows, :] += jnp.dot(hmid, wd_ref[0].astype(BF16), preferred_element_type=F32)

    @pl.when(valid == tm)
    def _():
        swiglu_rows(slice(None))

    for r in range(tm // rb):
        @pl.when((valid < tm) & (valid > r * rb))
        def _(r=r):
            swiglu_rows(slice(r * rb, (r + 1) * rb))


def _gmm(tile_expert, tile_valid, xs, wg, wu, wd, tm, fc):
    R, D = xs.shape
    F = wg.shape[2]
    nf = F // fc

    def fsel(i, f, tv):
        return jnp.where(tv[i] > 0, f, nf - 1)

    return pl.pallas_call(
        functools.partial(_gmm_kernel, rb=_pick_tile(tm, GMM_TAIL_ROW_BLOCK)),
        out_shape=jax.ShapeDtypeStruct((R, D), F32),
        grid_spec=pltpu.PrefetchScalarGridSpec(
            num_scalar_prefetch=2,
            grid=(R // tm, nf),
            in_specs=[
                pl.BlockSpec((tm, D), lambda i, f, te, tv: (i, 0)),
                pl.BlockSpec((1, D, fc), lambda i, f, te, tv: (te[i], 0, fsel(i, f, tv))),
                pl.BlockSpec((1, D, fc), lambda i, f, te, tv: (te[i], 0, fsel(i, f, tv))),
                pl.BlockSpec((1, fc, D), lambda i, f, te, tv: (te[i], fsel(i, f, tv), 0)),
            ],
            out_specs=pl.BlockSpec((tm, D), lambda i, f, te, tv: (i, 0)),
            scratch_shapes=[pltpu.VMEM((tm, D), BF16)],
        ),
        compiler_params=_cparams(("parallel", "arbitrary")),
        name="moe_gmm",
    )(tile_expert, tile_valid, xs, wg, wu, wd)


def _combine_copies(pos1_ref, pos2_ref, ys_hbm, ya_ref, yb_ref, sems, tile, slot, tm):
    def row_copies(r):
        t = tile * tm + r
        return (pltpu.make_async_copy(ys_hbm.at[pl.ds(pos1_ref[t], 1)], ya_ref.at[slot, pl.ds(r, 1)], sems.at[slot]),
                pltpu.make_async_copy(ys_hbm.at[pl.ds(pos2_ref[t], 1)], yb_ref.at[slot, pl.ds(r, 1)], sems.at[slot]))
    return row_copies


def _combine_kernel(pos1_ref, pos2_ref, h_ref, ga_ref, gb_ref, g_ref, ys_hbm, o_ref, ya_ref, yb_ref, sems, *, tm):
    i = pl.program_id(0)
    n = pl.num_programs(0)
    slot = lax.rem(i, 2)

    def start_tile(tile, slot):
        copies = _combine_copies(pos1_ref, pos2_ref, ys_hbm, ya_ref, yb_ref, sems, tile, slot, tm)

        def body(r, carry):
            a, b = copies(r)
            a.start()
            b.start()
            return carry
        lax.fori_loop(0, tm, body, 0, unroll=DMA_ISSUE_UNROLL)

    @pl.when(i == 0)
    def _():
        start_tile(0, 0)

    @pl.when(i + 1 < n)
    def _():
        start_tile(i + 1, 1 - slot)

    pltpu.make_async_copy(ys_hbm.at[pl.ds(0, tm)], ya_ref.at[slot], sems.at[slot]).wait()
    pltpu.make_async_copy(ys_hbm.at[pl.ds(0, tm)], yb_ref.at[slot], sems.at[slot]).wait()

    h = h_ref[...] + ga_ref[...] * ya_ref[slot] + gb_ref[...] * yb_ref[slot]
    var = jnp.mean(h * h, axis=-1, keepdims=True)
    o_ref[...] = h * lax.rsqrt(var + EPS) * g_ref[...]


def _combine(pos1, pos2, h2d, ga, gb, g, ys, tm):
    T, D = h2d.shape
    row = pl.BlockSpec((tm, D), lambda i, p1, p2: (i, 0))
    colv = pl.BlockSpec((tm, 1), lambda i, p1, p2: (i, 0))
    return pl.pallas_call(
        functools.partial(_combine_kernel, tm=tm),
        out_shape=jax.ShapeDtypeStruct((T, D), F32),
        grid_spec=pltpu.PrefetchScalarGridSpec(
            num_scalar_prefetch=2,
            grid=(T // tm,),
            in_specs=[row, colv, colv, pl.BlockSpec((1, D), lambda i, p1, p2: (0, 0)),
                      pl.BlockSpec(memory_space=pl.ANY)],
            out_specs=row,
            scratch_shapes=[pltpu.VMEM((2, tm, D), F32), pltpu.VMEM((2, tm, D), F32),
                            pltpu.SemaphoreType.DMA((2,))],
        ),
        compiler_params=_cparams(("arbitrary",)),
        name="moe_combine_norm",
    )(pos1, pos2, h2d, ga, gb, g.reshape(1, D), ys)


def _moe_final(res, y_c, y_d, w_out, norm_g, final_g, router_w, router_b, w_gate, w_up, w_down):
    B, S, D = res.shape
    T = B * S
    E = router_w.shape[1]
    h2d, hn2d, idx, gts, cnt = _proj_router(res.reshape(T, D), y_c.reshape(T, -1), y_d.reshape(T, -1),
                                            w_out.astype(BF16), norm_g, router_w, router_b, _pick_tile(T, 1024))
    e1, e2, r1, r2 = idx[0], idx[1], idx[2], idx[3]
    tm = MOE_ROW_TILE if TOP_K * T >= E * MOE_ROW_TILE else _pick_tile(T, 512)
    counts = cnt[:, 0].astype(jnp.int32)
    padded = ((counts + tm - 1) // tm) * tm
    ends = jnp.cumsum(padded)
    offs = ends - padded
    eids = jnp.arange(E, dtype=jnp.int32)[:, None]
    pos1 = jnp.sum(jnp.where(e1[None, :] == eids, offs[:, None], 0), axis=0) + r1
    pos2 = jnp.sum(jnp.where(e2[None, :] == eids, offs[:, None], 0), axis=0) + r2
    n_tiles = -(-TOP_K * T // tm) + E
    tile_start = jnp.arange(n_tiles, dtype=jnp.int32) * tm
    tile_expert = jnp.minimum(jnp.sum(tile_start[:, None] >= ends[None, :], axis=1), E - 1).astype(jnp.int32)
    tile_valid = jnp.clip((offs + counts)[tile_expert] - tile_start, 0, tm).astype(jnp.int32)
    xs = _dispatch(pos1, pos2, ends.astype(jnp.int32), hn2d, n_tiles * tm, _pick_tile(T, 1024), tm)
    ys = _gmm(tile_expert, tile_valid, xs, w_gate, w_up, w_down, tm, _pick_tile(w_gate.shape[2], 512))
    out = _combine(pos1, pos2, h2d, gts[0][:, None], gts[1][:, None], final_g, ys, _pick_tile(T, 256))
    return out.reshape(B, S, D)


def kernel(x, norm_mix_g, norm_ffn_g, norm_final_g, ab_w_in, s5_lambda_re, s5_lambda_im, s5_log_dt, s5_b_re, s5_b_im, s5_c_re, s5_c_im, s5_d, s5_w_glu, mlstm_conv_w, mlstm_conv_b, mlstm_gate_b, mlstm_norm_g, ab_w_out, ffn_w_gate, ffn_w_up, ffn_w_down, cd_w_in, gmlp_norm_g, gmlp_norm_b, gmlp_w_s, gmlp_b_s, rel_bias, cd_w_out, moe_router_w, moe_router_b, moe_w_gate, moe_w_up, moe_w_down):
    B, S, D = x.shape
    y_a, y_b = _mix_ab(x, norm_mix_g[0], ab_w_in[0], s5_lambda_re[0], s5_lambda_im[0], s5_log_dt[0],
                       s5_b_re[0], s5_b_im[0], s5_c_re[0], s5_c_im[0], s5_d[0], s5_w_glu[0],
                       mlstm_conv_w[0], mlstm_conv_b[0], mlstm_gate_b[0], mlstm_norm_g[0])
    h = _proj_ffn(x, y_a, y_b, ab_w_out[0].astype(BF16), norm_ffn_g[0], ffn_w_gate[0].astype(BF16),
                  ffn_w_up[0].astype(BF16), ffn_w_down[0].astype(BF16),
                  _pick_tile(S, 512), _pick_tile(ffn_w_gate.shape[2], 1408))
    y_c, y_d = _mix_cd(h, norm_mix_g[1], cd_w_in[0], gmlp_norm_g[0], gmlp_norm_b[0],
                       gmlp_w_s[0], gmlp_b_s[0], rel_bias)
    return _moe_final(h, y_c, y_d, cd_w_out[0], norm_ffn_g[1], norm_final_g, moe_router_w[0], moe_router_b[0],
                      moe_w_gate[0], moe_w_up[0], moe_w_down[0])
```

```python
import functools
import math

import jax
import jax.numpy as jnp
from jax import lax
from jax.experimental import pallas as pl
from jax.experimental.pallas import tpu as pltpu

F32 = jnp.float32
BF16 = jnp.bfloat16
EPS = 1e-5

LANES = 128
SUBLANES = 8
VMEM_LIMIT_BYTES = 56 * 1024 * 1024

S5_GROUP = 16
S5_SLAB_GROUPS = 8
MLSTM_HEADS = 4
MLSTM_CHUNK = 128
MLSTM_SEQS_PER_STEP = 2
CONV_WIDTH = 4
GMLP_GROUPS = 4
GMLP_CHUNK = 128
MOBA_HEADS = 4
MOBA_BLOCK = 256
MOBA_TOPK = 3
REL_BUCKETS = 32
REL_MAX_DIST = 128
TOP_K = 2

TOKEN_TILE = 512
FFN_FF_CHUNK = 1408
S5_TIME_CHUNK = 32
ROUTER_TILE = 1024
ROUTER_SUB_TILE = 512
DISPATCH_TILE = 1024
COMBINE_TILE = 256
MOE_FF_CHUNK = 512
GMM_TAIL_ROW_BLOCK = 256
MOE_ROW_TILE = 1536


def _cparams(sem):
    return pltpu.CompilerParams(dimension_semantics=sem, vmem_limit_bytes=VMEM_LIMIT_BYTES)


def _rms_bf16(x, g):
    var = jnp.mean(x * x, axis=-1, keepdims=True)
    return (x * lax.rsqrt(var + EPS) * g).astype(BF16)


def _gelu(x):
    return jax.nn.gelu(x, approximate=True)


def _sigmoid(x):
    return 1.0 / (1.0 + jnp.exp(-x))


def _norm_proj_kernel(x_ref, g_ref, w_ref, *out_refs, splits):
    hn = _rms_bf16(x_ref[0], g_ref[...])
    for o_ref, (c0, c1) in zip(out_refs, splits):
        r = jnp.dot(hn, w_ref[:, c0:c1], preferred_element_type=F32)
        o_ref[...] = r.reshape(o_ref.shape).astype(o_ref.dtype)


def _norm_proj(x, g, w, outs, tm):
    B, S, D = x.shape
    splits = tuple((c0, c1) for c0, c1, _ in outs)
    out_shape, out_specs = [], []
    for c0, c1, dt in outs:
        n = c1 - c0
        out_shape.append(jax.ShapeDtypeStruct((B, S, n), dt))
        out_specs.append(pl.BlockSpec((1, tm, n), lambda b, i: (b, i, 0)))
    return pl.pallas_call(
        functools.partial(_norm_proj_kernel, splits=splits),
        out_shape=out_shape,
        grid=(B, S // tm),
        in_specs=[
            pl.BlockSpec((1, tm, D), lambda b, i: (b, i, 0)),
            pl.BlockSpec((1, D), lambda b, i: (0, 0)),
            pl.BlockSpec(w.shape, lambda b, i: (0, 0)),
        ],
        out_specs=out_specs,
        compiler_params=_cparams(("parallel", "parallel")),
        name="norm_proj",
    )(x, g.reshape(1, D), w)


def _s5_kernel(u_hbm, bmat_ref, ar_ref, ai_ref, cmat_ref, d_ref, wglu_ref, y_hbm,
               xr_ref, xi_ref, ubuf, ybuf, in_sems, out_sems, *bufs, lc, nb, nslab, sw):
    i = pl.program_id(0)
    n = pl.num_programs(0)
    slot = lax.rem(i, 2)
    W = ubuf.shape[-1]

    def in_copies(step, slot):
        t0 = pl.multiple_of(step * lc, lc)
        return [pltpu.make_async_copy(u_hbm.at[b, pl.ds(t0, lc), :], ubuf.at[slot, :, b, :], in_sems.at[slot])
                for b in range(nb)]

    def out_copies(step, slot):
        t0 = pl.multiple_of(step * lc, lc)
        return [pltpu.make_async_copy(ybuf.at[slot, :, b, :], y_hbm.at[b, pl.ds(t0, lc), :], out_sems.at[slot])
                for b in range(nb)]

    @pl.when(i == 0)
    def _():
        xr_ref[...] = jnp.zeros_like(xr_ref)
        xi_ref[...] = jnp.zeros_like(xi_ref)
        for c in in_copies(0, 0):
            c.start()

    @pl.when(i + 1 < n)
    def _():
        for c in in_copies(i + 1, 1 - slot):
            c.start()

    for c in in_copies(i, slot):
        c.wait()

    uf = ubuf[slot].reshape(lc * nb, W)
    u = uf.astype(BF16)
    xr_all = xr_ref[...]
    xi_all = xi_ref[...]

    def drive(k):
        bufs[k][...] = jnp.dot(u[:, LANES * k:LANES * (k + 1)], bmat_ref[k], preferred_element_type=F32)

    drive(0)
    ys, new_xr, new_xi = [], [], []
    for k in range(nslab):
        if k + 1 < nslab:
            drive(k + 1)
        st_cols = slice(sw * k, sw * (k + 1))
        ar = jnp.broadcast_to(ar_ref[:, st_cols], (nb, sw))
        ai = jnp.broadcast_to(ai_ref[:, st_cols], (nb, sw))
        xr, xi = xr_all[:, st_cols], xi_all[:, st_cols]
        for t in range(lc):
            rows = slice(t * nb, (t + 1) * nb)
            nxr = ar * xr - ai * xi + bufs[k][rows, :sw]
            nxi = ar * xi + ai * xr + bufs[k][rows, sw:]
            bufs[k][rows, :sw] = nxr
            bufs[k][rows, sw:] = nxi
            xr, xi = nxr, nxi
        new_xr.append(xr)
        new_xi.append(xi)
        ys.append(jnp.dot(bufs[k][...].astype(BF16), cmat_ref[k], preferred_element_type=F32))
    xr_ref[...] = jnp.concatenate(new_xr, axis=-1)
    xi_ref[...] = jnp.concatenate(new_xi, axis=-1)
    y = jnp.concatenate(ys, axis=-1)
    y = _gelu(y + d_ref[...] * uf)
    gl = jnp.dot(y.astype(BF16), wglu_ref[...], preferred_element_type=F32)

    @pl.when(i >= 2)
    def _():
        for c in out_copies(i - 2, slot):
            c.wait()

    ybuf[slot] = (y * _sigmoid(gl)).reshape(lc, nb, W)
    for c in out_copies(i, slot):
        c.start()

    @pl.when(i == n - 1)
    def _():
        for c in out_copies(i, slot):
            c.wait()

        @pl.when(n >= 2)
        def _():
            for c in out_copies(i - 1, 1 - slot):
                c.wait()


def _s5_params(lam_re, lam_im, log_dt, b_re, b_im, c_re, c_im):
    G, P = lam_re.shape
    Hc = b_re.shape[-1]
    dt = jnp.exp(log_dt.astype(F32))[:, None]
    mag = jnp.exp(lam_re * dt)
    ar = mag * jnp.cos(lam_im * dt)
    ai = mag * jnp.sin(lam_im * dt)
    den = lam_re * lam_re + lam_im * lam_im
    cr = ((ar - 1.0) * lam_re + ai * lam_im) / den
    ci = (ai * lam_re - (ar - 1.0) * lam_im) / den
    bb_re = cr[..., None] * b_re - ci[..., None] * b_im
    bb_im = cr[..., None] * b_im + ci[..., None] * b_re
    gs = S5_SLAB_GROUPS
    nslab = G // gs
    eye = jnp.eye(gs, dtype=F32)

    def bd_in(b):
        b = b.reshape(nslab, gs, P, Hc)
        return jnp.einsum('kgph,gj->kghjp', b, eye).reshape(nslab, gs * Hc, gs * P)

    def bd_out(c):
        c = c.reshape(nslab, gs, Hc, P)
        return jnp.einsum('kghp,gj->kgpjh', c, eye).reshape(nslab, gs * P, gs * Hc)

    bmat = jnp.concatenate([bd_in(bb_re), bd_in(bb_im)], axis=-1).astype(BF16)
    cmat = jnp.concatenate([bd_out(c_re), -bd_out(c_im)], axis=1).astype(BF16)
    return bmat, ar.reshape(1, G * P), ai.reshape(1, G * P), cmat


def _s5(u, bmat, ar, ai, cmat, d_skip, w_glu, lc):
    nb, S, W = u.shape
    assert nb % SUBLANES == 0
    nslab = bmat.shape[0]
    sw = bmat.shape[2] // 2
    rows = lc * nb
    return pl.pallas_call(
        functools.partial(_s5_kernel, lc=lc, nb=nb, nslab=nslab, sw=sw),
        out_shape=jax.ShapeDtypeStruct((nb, S, W), F32),
        grid=(S // lc,),
        in_specs=[
            pl.BlockSpec(memory_space=pl.ANY),
            pl.BlockSpec(bmat.shape, lambda i: (0, 0, 0)),
            pl.BlockSpec(ar.shape, lambda i: (0, 0)),
            pl.BlockSpec(ai.shape, lambda i: (0, 0)),
            pl.BlockSpec(cmat.shape, lambda i: (0, 0, 0)),
            pl.BlockSpec((1, W), lambda i: (0, 0)),
            pl.BlockSpec(w_glu.shape, lambda i: (0, 0)),
        ],
        out_specs=pl.BlockSpec(memory_space=pl.ANY),
        scratch_shapes=[pltpu.VMEM((nb, sw * nslab), F32), pltpu.VMEM((nb, sw * nslab), F32),
                        pltpu.VMEM((2, lc, nb, W), F32), pltpu.VMEM((2, lc, nb, W), F32),
                        pltpu.SemaphoreType.DMA((2,)), pltpu.SemaphoreType.DMA((2,))]
        + [pltpu.VMEM((rows, 2 * sw), F32) for _ in range(nslab)],
        compiler_params=_cparams(("arbitrary",)),
        name="s5",
    )(u, bmat, ar, ai, cmat, d_skip.reshape(1, W), w_glu)


def _mlstm_pre(x, g, tail, m_row, cw, cb, *, L, H, Dh):
    W = H * Dh
    halo = SUBLANES
    log_scale = -0.5 * math.log(Dh)

    row = lax.broadcasted_iota(jnp.int32, (L, L), 0)
    col = lax.broadcasted_iota(jnp.int32, (L, L), 1)
    conv = cb + cw[0:1, :] * x.astype(F32)
    corr = jnp.zeros((halo, 2 * W), F32)
    for j in range(1, CONV_WIDTH):
        shift = (row - col == j).astype(BF16)
        conv = conv + cw[j:j + 1, :] * jnp.dot(shift, x, preferred_element_type=F32)
        corr = corr + cw[j:j + 1, :] * tail[halo - j:2 * halo - j, :]
    conv = jnp.concatenate([conv[:halo] + corr, conv[halo:]], axis=0)
    new_tail = x[L - halo:, :].astype(F32)
    qk = (conv * _sigmoid(conv)).astype(BF16)
    log_i = g[:, :LANES]
    f_pre = g[:, LANES:]
    logf = jnp.minimum(f_pre, 0.0) - jnp.log(1.0 + jnp.exp(-jnp.abs(f_pre)))
    causal = row >= col
    bcum = jnp.dot(causal.astype(F32), logf, preferred_element_type=F32, precision=lax.Precision.HIGHEST)
    w_cols = log_i - bcum
    trow = lax.broadcasted_iota(jnp.int32, (L, LANES), 0)
    cmax = w_cols
    k = 1
    while k < L:
        cmax = jnp.maximum(cmax, jnp.where(trow >= k, pltpu.roll(cmax, k, axis=0), -jnp.inf))
        k *= 2
    m_inter = bcum + m_row
    m_t = jnp.maximum(bcum + cmax, m_inter)
    u_cols = bcum - m_t + log_scale
    wi_cols = jnp.exp(m_inter - m_t)
    em_cols = jnp.exp(-m_t)
    b_last = bcum[L - 1:L, :]
    m_loc = b_last + cmax[L - 1:L, :]
    m_new = jnp.maximum(b_last + m_row, m_loc)
    s_prev = jnp.exp(b_last + m_row - m_new)
    s_loc = jnp.exp(m_loc - m_new)
    w_rows = w_cols.T
    return dict(q=qk[:, :W], k=qk[:, W:], causal=causal, u_cols=u_cols, wi_cols=wi_cols, em_cols=em_cols,
                w_rows=w_rows, b_last=b_last, m_loc=m_loc, m_new=m_new, s_prev=s_prev, s_loc=s_loc,
                new_tail=new_tail, log_scale=log_scale)


def _mlstm_heads(pres, vbs, o_pres, cxs, ng, *, L, H, Dh):
    nt = (((1,), (1,)), ((), ()))
    pairs = [(b, h) for b in range(len(pres)) for h in range(H)]
    hs = [slice(h * Dh, (h + 1) * Dh) for h in range(H)]
    hl = [slice(h, h + 1) for h in range(H)]
    ones = jnp.ones((L, Dh), BF16)
    sq_ones = jnp.ones((Dh, Dh), BF16)
    vext = {(b, h): jnp.concatenate([vbs[b][:, hs[h]], ones], axis=1) for b, h in pairs}
    qk_t = {(b, h): lax.dot_general(pres[b]['q'][:, hs[h]], pres[b]['k'][:, hs[h]], nt,
                                    preferred_element_type=F32) for b, h in pairs}
    r2 = {(b, h): jnp.dot(pres[b]['q'][:, hs[h]], cxs[b][h].astype(BF16), preferred_element_type=F32)
          for b, h in pairs}
    s = {}
    for b, h in pairs:
        p = pres[b]
        decay = jnp.exp(jnp.where(p['causal'], p['u_cols'][:, hl[h]] + p['w_rows'][hl[h], :], -jnp.inf))
        s[b, h] = (qk_t[b, h] * decay).astype(BF16)
    r1 = {bh: jnp.dot(s[bh], vext[bh], preferred_element_type=F32) for bh in pairs}
    hh = {}
    for b, h in pairs:
        p = pres[b]
        wi = jnp.broadcast_to(p['wi_cols'][:, hl[h]], (L, Dh))
        num = r1[b, h][:, :Dh] + wi * r2[b, h][:, :Dh]
        den = r1[b, h][:, Dh:] + wi * r2[b, h][:, Dh:]
        hh[b, h] = num / jnp.maximum(jnp.abs(den), jnp.broadcast_to(p['em_cols'][:, hl[h]], (L, Dh)))
    msq = {bh: jnp.dot((hh[bh] * hh[bh]).astype(BF16), sq_ones, preferred_element_type=F32) * (1.0 / Dh)
           for bh in pairs}
    ys = {(b, h): (_sigmoid(o_pres[b][:, hs[h]].astype(F32)) * hh[b, h] * lax.rsqrt(msq[b, h] + EPS)
                   * ng[:, hs[h]]).astype(BF16) for b, h in pairs}
    kwt = {}
    for b, h in pairs:
        p = pres[b]
        wexp = jnp.exp(p['b_last'][:, hl[h]] + p['w_rows'][hl[h], :] - p['m_loc'][:, hl[h]] + p['log_scale'])
        kwt[b, h] = (p['k'][:, hs[h]].astype(F32).T * wexp).astype(BF16)
    cx_loc = {bh: jnp.dot(kwt[bh], vext[bh], preferred_element_type=F32) for bh in pairs}
    new_cx = {(b, h): pres[b]['s_prev'][:, hl[h]] * cxs[b][h] + pres[b]['s_loc'][:, hl[h]] * cx_loc[b, h]
              for b, h in pairs}
    return ys, new_cx


def _mlstm_kernel(qk_ref, v_ref, o_ref, gt_ref, cw_ref, cb_ref, gb_ref, ng_ref, y_ref,
                  tail_ref, c_ref, m_ref, *, L, H, Dh, BB):
    @pl.when(pl.program_id(1) == 0)
    def _():
        tail_ref[...] = jnp.zeros_like(tail_ref)
        c_ref[...] = jnp.zeros_like(c_ref)
        m_ref[...] = jnp.zeros_like(m_ref)

    tails = [tail_ref[bb] for bb in range(BB)]
    cxs = [[c_ref[bb, h] for h in range(H)] for bb in range(BB)]
    m_rows = [m_ref[bb, 0:1, :] for bb in range(BB)]
    pres = [_mlstm_pre(qk_ref[bb], gt_ref[bb] + gb_ref[...], tails[bb], m_rows[bb], cw_ref[...], cb_ref[...],
                       L=L, H=H, Dh=Dh) for bb in range(BB)]
    ys, new_cx = _mlstm_heads(pres, [v_ref[bb].astype(BF16) for bb in range(BB)], [o_ref[bb] for bb in range(BB)],
                              cxs, ng_ref[...], L=L, H=H, Dh=Dh)
    for bb in range(BB):
        for h in range(H):
            y_ref[bb, :, h * Dh:(h + 1) * Dh] = ys[bb, h]
            c_ref[bb, h] = new_cx[bb, h]
        tail_ref[bb, 0:SUBLANES, :] = pres[bb]['new_tail']
        m_ref[bb, 0:1, :] = pres[bb]['m_new']


def _mlstm(qk_pre, v, o_pre, gates, conv_w, conv_b, gate_b_pad, norm_g):
    B, S, W2 = qk_pre.shape
    W = W2 // 2
    H, L = MLSTM_HEADS, MLSTM_CHUNK
    Dh = W // H
    BB = _pick_tile(B, MLSTM_SEQS_PER_STEP)
    return pl.pallas_call(
        functools.partial(_mlstm_kernel, L=L, H=H, Dh=Dh, BB=BB),
        out_shape=jax.ShapeDtypeStruct((B, S, W), BF16),
        grid=(B // BB, S // L),
        in_specs=[
            pl.BlockSpec((BB, L, W2), lambda b, c: (b, c, 0)),
            pl.BlockSpec((BB, L, W), lambda b, c: (b, c, 0)),
            pl.BlockSpec((BB, L, W), lambda b, c: (b, c, 0)),
            pl.BlockSpec((BB, L, 2 * LANES), lambda b, c: (b, c, 0)),
            pl.BlockSpec((CONV_WIDTH, W2), lambda b, c: (0, 0)),
            pl.BlockSpec((1, W2), lambda b, c: (0, 0)),
            pl.BlockSpec((1, 2 * LANES), lambda b, c: (0, 0)),
            pl.BlockSpec((1, W), lambda b, c: (0, 0)),
        ],
        out_specs=pl.BlockSpec((BB, L, W), lambda b, c: (b, c, 0)),
        scratch_shapes=[
            pltpu.VMEM((BB, 2 * SUBLANES, W2), F32),
            pltpu.VMEM((BB, H, Dh, 2 * Dh), F32),
            pltpu.VMEM((BB, SUBLANES, LANES), F32),
        ],
        compiler_params=_cparams(("parallel", "arbitrary")),
        name="mlstm",
    )(qk_pre, v, o_pre, gates, conv_w, conv_b.reshape(1, W2), gate_b_pad, norm_g.reshape(1, W))


def _mix_out(res, a, b, wa_ref, wb_ref):
    acc = jnp.dot(a.astype(BF16), wa_ref[...], preferred_element_type=F32)
    return res + acc + jnp.dot(b.astype(BF16), wb_ref[...], preferred_element_type=F32)


def _proj_ffn_kernel(res_ref, a_ref, b_ref, wa_ref, wb_ref, g_ref, wg_ref, wu_ref, wd_ref, o_ref, hn_ref):
    f = pl.program_id(2)

    @pl.when(f == 0)
    def _():
        h = _mix_out(res_ref[0], a_ref[0], b_ref[0], wa_ref, wb_ref)
        hn_ref[...] = _rms_bf16(h, g_ref[...])
        o_ref[0] = h

    hn = hn_ref[...]
    a = jnp.dot(hn, wg_ref[...], preferred_element_type=F32)
    u = jnp.dot(hn, wu_ref[...], preferred_element_type=F32)
    hmid = (a * _sigmoid(a) * u).astype(BF16)
    o_ref[0] += jnp.dot(hmid, wd_ref[...], preferred_element_type=F32)


def _proj_ffn(res, ya, yb, w_out, g, wg, wu, wd, tm, fc):
    B, S, D = res.shape
    wa_n = ya.shape[2]
    wa, wb = w_out[:wa_n], w_out[wa_n:]
    F = wg.shape[1]
    return pl.pallas_call(
        _proj_ffn_kernel,
        out_shape=jax.ShapeDtypeStruct((B, S, D), F32),
        grid=(B, S // tm, F // fc),
        in_specs=[
            pl.BlockSpec((1, tm, D), lambda b, i, f: (b, i, 0)),
            pl.BlockSpec((1, tm, wa_n), lambda b, i, f: (b, i, 0)),
            pl.BlockSpec((1, tm, wb.shape[0]), lambda b, i, f: (b, i, 0)),
            pl.BlockSpec(wa.shape, lambda b, i, f: (0, 0)),
            pl.BlockSpec(wb.shape, lambda b, i, f: (0, 0)),
            pl.BlockSpec((1, D), lambda b, i, f: (0, 0)),
            pl.BlockSpec((D, fc), lambda b, i, f: (0, f)),
            pl.BlockSpec((D, fc), lambda b, i, f: (0, f)),
            pl.BlockSpec((fc, D), lambda b, i, f: (f, 0)),
        ],
        out_specs=pl.BlockSpec((1, tm, D), lambda b, i, f: (b, i, 0)),
        scratch_shapes=[pltpu.VMEM((tm, D), BF16)],
        compiler_params=_cparams(("parallel", "parallel", "arbitrary")),
        name="proj_ffn_swiglu",
    )(res, ya, yb, wa, wb, g.reshape(1, D), wg, wu, wd)


def _pick_tile(n, target):
    t = min(n, target)
    while n % t:
        t //= 2
    return t


def _mix_ab(h, norm_g, w_in, lam_re, lam_im, log_dt, b_re, b_im, c_re, c_im, d_skip, w_glu,
            conv_w, conv_b, gate_b, mlstm_norm_g):
    B, S, D = h.shape
    s5w = lam_re.shape[0] * S5_GROUP
    mw = mlstm_norm_g.shape[0]
    c_qk, c_v, c_o, c_if = s5w, s5w + 2 * mw, s5w + 3 * mw, s5w + 4 * mw
    nh = (w_in.shape[1] - c_if) // 2
    lane_pad = ((0, 0), (0, LANES - nh))
    w_pad = jnp.concatenate([w_in[:, :c_if], jnp.pad(w_in[:, c_if:c_if + nh], lane_pad),
                             jnp.pad(w_in[:, c_if + nh:], lane_pad)], axis=1).astype(BF16)
    tm = _pick_tile(S, TOKEN_TILE)
    u_s5, qk_pre, v, o_pre, gates = _norm_proj(
        h, norm_g, w_pad,
        [(0, c_qk, F32), (c_qk, c_v, BF16), (c_v, c_o, BF16), (c_o, c_if, BF16), (c_if, c_if + 2 * LANES, F32)],
        tm)
    bmat, ar, ai, cmat = _s5_params(lam_re, lam_im, log_dt, b_re, b_im, c_re, c_im)
    y_a = _s5(u_s5, bmat, ar, ai, cmat, d_skip, w_glu.astype(BF16), lc=_pick_tile(S, S5_TIME_CHUNK))
    gate_b_pad = jnp.concatenate([jnp.pad(gate_b[:nh], (0, LANES - nh)),
                                  jnp.pad(gate_b[nh:], (0, LANES - nh))]).reshape(1, 2 * LANES)
    y_b = _mlstm(qk_pre, v, o_pre, gates, conv_w, conv_b, gate_b_pad, mlstm_norm_g)
    return y_a, y_b


def _gmlp_kernel(u_ref, v_ref, ng_ref, nb_ref, ws_ref, bs_ref, y_ref, *, L, G, Dg, nchunk):
    u = _gelu(u_ref[0].astype(F32))
    v = _gelu(v_ref[0].astype(F32))
    mu = jnp.mean(v, axis=-1, keepdims=True)
    vc = v - mu
    var = jnp.mean(vc * vc, axis=-1, keepdims=True)
    vn = (vc * lax.rsqrt(var + EPS) * ng_ref[...] + nb_ref[...]).astype(BF16)
    for c in range(nchunk):
        rows = slice(c * L, (c + 1) * L)
        for g in range(G):
            cols = slice(g * Dg, (g + 1) * Dg)
            s = jnp.dot(ws_ref[g], vn[rows, cols], preferred_element_type=F32) + bs_ref[:, g:g + 1]
            y_ref[0, rows, cols] = (u[rows, cols] * s).astype(y_ref.dtype)


def _gmlp(u_pre, v_pre, norm_g, norm_b, w_s, b_s, tm):
    B, S, W = u_pre.shape
    G, L = GMLP_GROUPS, GMLP_CHUNK
    ws = (w_s * jnp.tril(jnp.ones((L, L), w_s.dtype))).astype(BF16)
    bs = jnp.pad(b_s.T, ((0, 0), (0, LANES - G)))
    return pl.pallas_call(
        functools.partial(_gmlp_kernel, L=L, G=G, Dg=W // G, nchunk=tm // L),
        out_shape=jax.ShapeDtypeStruct((B, S, W), BF16),
        grid=(B, S // tm),
        in_specs=[
            pl.BlockSpec((1, tm, W), lambda b, i: (b, i, 0)),
            pl.BlockSpec((1, tm, W), lambda b, i: (b, i, 0)),
            pl.BlockSpec((1, W), lambda b, i: (0, 0)),
            pl.BlockSpec((1, W), lambda b, i: (0, 0)),
            pl.BlockSpec((G, L, L), lambda b, i: (0, 0, 0)),
            pl.BlockSpec((L, LANES), lambda b, i: (0, 0)),
        ],
        out_specs=pl.BlockSpec((1, tm, W), lambda b, i: (b, i, 0)),
        compiler_params=_cparams(("parallel", "parallel")),
        name="gmlp",
    )(u_pre, v_pre, norm_g.reshape(1, W), norm_b.reshape(1, W), ws, bs)


NEG_BIG = -1e30


def _moba_kernel(q_ref, k_ref, v_ref, t0_ref, t1_ref, bfar_ref, y_ref, vext_ref, *, BL, NB, Dh, topk):
    nt = (((1,), (1,)), ((), ()))
    scale = 1.0 / math.sqrt(Dh)
    vext_ref[:, :Dh] = v_ref[0]
    vext_ref[:, Dh:] = jnp.ones((NB * BL, Dh), BF16)
    bfar = bfar_ref[0][:, 0:1]
    kmean = jnp.concatenate(
        [jnp.mean(k_ref[0, n * BL:(n + 1) * BL, :].astype(F32), axis=0, keepdims=True) for n in range(NB)],
        axis=0)
    nrow = lax.broadcasted_iota(jnp.int32, (NB, BL), 0)
    eye = (lax.broadcasted_iota(jnp.int32, (BL, BL), 0) ==
           lax.broadcasted_iota(jnp.int32, (BL, BL), 1)).astype(BF16)

    def selection_mask(qi):
        qf = q_ref[0, qi * BL:(qi + 1) * BL, :].astype(F32)
        gate = lax.dot_general(kmean, qf, nt, preferred_element_type=F32, precision=lax.Precision.HIGHEST)
        rank = jnp.zeros((NB, BL), F32)
        for m in range(qi):
            gm = gate[m:m + 1, :]
            rank = rank + jnp.where(gm > gate, 1.0, jnp.where(gm == gate, jnp.where(nrow > m, 1.0, 0.0), 0.0))
        sel_rows = jnp.where(nrow < qi, jnp.where(rank < topk, 1.0, 0.0), 0.0)
        sel_pad = jnp.concatenate([sel_rows, jnp.zeros((LANES - NB, BL), F32)], axis=0).astype(BF16)
        sel_cols = lax.dot_general(eye, sel_pad, nt, preferred_element_type=F32)
        return (1.0 - sel_cols) * NEG_BIG

    def logits(qi):
        qs = (q_ref[0, qi * BL:(qi + 1) * BL, :].astype(F32) * scale).astype(BF16)
        raw = [lax.dot_general(qs, k_ref[0, kb * BL:(kb + 1) * BL, :], nt, preferred_element_type=F32)
               for kb in range(qi + 1)]
        neg_cols = selection_mask(qi) if qi > topk else None
        m_tile = None
        pieces = []
        for kb in range(qi + 1):
            s = raw[kb]
            if kb == qi:
                s = s + t0_ref[0]
            elif kb == qi - 1:
                s = s + t1_ref[0]
                if neg_cols is not None:
                    s = s + neg_cols[:, kb:kb + 1]
            else:
                s = s + (bfar if neg_cols is None else bfar + neg_cols[:, kb:kb + 1])
            pieces.append(s)
            for c in range(BL // LANES):
                piece = s[:, c * LANES:(c + 1) * LANES]
                m_tile = piece if m_tile is None else jnp.maximum(m_tile, piece)
        return pieces, m_tile

    nxt = logits(0)
    for qi in range(NB):
        pieces, m_tile = nxt
        if qi + 1 < NB:
            nxt = logits(qi + 1)
        m = jnp.max(m_tile, axis=-1, keepdims=True)
        p = jnp.concatenate([jnp.exp(s - m).astype(BF16) for s in pieces], axis=1)
        acc = jnp.dot(p, vext_ref[:(qi + 1) * BL, :], preferred_element_type=F32)
        y_ref[0, qi * BL:(qi + 1) * BL, :] = (acc[:, :Dh] / acc[:, Dh:]).astype(y_ref.dtype)


def _rel_bucket(n):
    max_exact = REL_BUCKETS // 2
    nf = jnp.maximum(n, 1).astype(F32)
    large = max_exact + (jnp.log(nf / max_exact) / math.log(REL_MAX_DIST / max_exact)
                         * (REL_BUCKETS - max_exact)).astype(jnp.int32)
    large = jnp.minimum(large, REL_BUCKETS - 1)
    return jnp.where(n < max_exact, n, large)


def _moba(qkv, rel_bias):
    B, S, W3 = qkv.shape
    H, BL = MOBA_HEADS, MOBA_BLOCK
    W = W3 // 3
    Dh = W // H
    NB = S // BL
    assert BL + 1 >= REL_MAX_DIST and Dh == LANES and S % BL == 0
    i = jnp.arange(BL, dtype=jnp.int32)[:, None]
    j = jnp.arange(BL, dtype=jnp.int32)[None, :]
    buckets = jnp.arange(REL_BUCKETS, dtype=jnp.int32)

    def bias_tile(dist):
        onehot = (_rel_bucket(dist)[:, :, None] == buckets).astype(F32)
        return jnp.einsum('ijb,bh->hij', onehot, rel_bias, precision=lax.Precision.HIGHEST)

    t0 = jnp.where(i >= j, bias_tile(jnp.maximum(i - j, 0)), NEG_BIG)
    t1 = bias_tile(BL + i - j)
    bfar = jnp.broadcast_to(rel_bias.T[:, REL_BUCKETS - 1][:, None, None], (H, 1, LANES))
    return pl.pallas_call(
        functools.partial(_moba_kernel, BL=BL, NB=NB, Dh=Dh, topk=min(MOBA_TOPK, NB)),
        out_shape=jax.ShapeDtypeStruct((B, S, W), BF16),
        grid=(B, H),
        in_specs=[
            pl.BlockSpec((1, S, Dh), lambda b, h: (b, 0, h)),
            pl.BlockSpec((1, S, Dh), lambda b, h: (b, 0, H + h)),
            pl.BlockSpec((1, S, Dh), lambda b, h: (b, 0, 2 * H + h)),
            pl.BlockSpec((1, BL, BL), lambda b, h: (h, 0, 0)),
            pl.BlockSpec((1, BL, BL), lambda b, h: (h, 0, 0)),
            pl.BlockSpec((1, 1, LANES), lambda b, h: (h, 0, 0)),
        ],
        out_specs=pl.BlockSpec((1, S, Dh), lambda b, h: (b, 0, h)),
        scratch_shapes=[pltpu.VMEM((S, 2 * Dh), BF16)],
        compiler_params=_cparams(("parallel", "parallel")),
        name="moba",
    )(qkv, qkv, qkv, t0, t1, bfar)


def _mix_cd(h, norm_g, w_in, gmlp_norm_g, gmlp_norm_b, gmlp_w_s, gmlp_b_s, rel_bias):
    B, S, D = h.shape
    gw = gmlp_norm_g.shape[0]
    tm = _pick_tile(S, TOKEN_TILE)
    u_pre, v_pre, qkv = _norm_proj(
        h, norm_g, w_in.astype(BF16),
        [(0, gw, BF16), (gw, 2 * gw, BF16), (2 * gw, w_in.shape[1], BF16)], tm)
    y_c = _gmlp(u_pre, v_pre, gmlp_norm_g, gmlp_norm_b, gmlp_w_s, gmlp_b_s, tm)
    return y_c, _moba(qkv, rel_bias)


def _route_logits(hn, wr, rb, E):
    hn_hi = hn.astype(BF16)
    hn_lo = (hn - hn_hi.astype(F32)).astype(BF16)
    wr_hi, wr_lo = wr
    lg = (jnp.dot(hn_hi, wr_hi, preferred_element_type=F32) + jnp.dot(hn_hi, wr_lo, preferred_element_type=F32)
          + jnp.dot(hn_lo, wr_hi, preferred_element_type=F32))
    return lg.T[:E, :] + rb


def _route_assign(logits, carry, E):
    tm = logits.shape[1]
    rowi = lax.broadcasted_iota(jnp.int32, (E, tm), 0)
    v1 = jnp.max(logits, axis=0, keepdims=True)
    e1 = jnp.min(jnp.where(logits == v1, rowi, E), axis=0, keepdims=True)
    masked = jnp.where(rowi == e1, -jnp.inf, logits)
    v2 = jnp.max(masked, axis=0, keepdims=True)
    e2 = jnp.min(jnp.where(masked == v2, rowi, E), axis=0, keepdims=True)
    ex = jnp.exp(v2 - v1)
    g1 = 1.0 / (1.0 + ex)
    g2 = ex / (1.0 + ex)
    oh1 = rowi == e1
    oh2 = rowi == e2
    cnt = jnp.where(oh1, 1.0, jnp.where(oh2, 1.0, 0.0))
    before = (lax.broadcasted_iota(jnp.int32, (tm, tm), 0) <
              lax.broadcasted_iota(jnp.int32, (tm, tm), 1)).astype(BF16)
    excl = jnp.dot(cnt.astype(BF16), before, preferred_element_type=F32) + carry
    r1 = jnp.sum(jnp.where(oh1, excl, 0.0), axis=0, keepdims=True).astype(jnp.int32)
    r2 = jnp.sum(jnp.where(oh2, excl, 0.0), axis=0, keepdims=True).astype(jnp.int32)
    idx = jnp.where(rowi == 0, e1, jnp.where(rowi == 1, e2, jnp.where(rowi == 2, r1, jnp.where(rowi == 3, r2, 0))))
    gts = jnp.where(rowi == 0, g1, jnp.where(rowi == 1, g2, 0.0))
    return idx, gts, jnp.sum(cnt, axis=1, keepdims=True)


def _router_kernel(res_ref, a_ref, b_ref, wa_ref, wb_ref, g_ref, wr_ref, rb_ref,
                   h_ref, hn_ref, idx_ref, gts_ref, cnt_ref, carry_ref, *, E, sub):
    @pl.when(pl.program_id(0) == 0)
    def _():
        carry_ref[...] = jnp.zeros_like(carry_ref)

    nsub = res_ref.shape[0] // sub
    spans = [slice(j * sub, (j + 1) * sub) for j in range(nsub)]
    xs = [_mix_out(res_ref[sp, :], a_ref[sp, :], b_ref[sp, :], wa_ref, wb_ref) for sp in spans]
    hns = []
    for sp, x in zip(spans, xs):
        h_ref[sp, :] = x
        var = jnp.mean(x * x, axis=-1, keepdims=True)
        hns.append(x * lax.rsqrt(var + EPS) * g_ref[...])
        hn_ref[sp, :] = hns[-1]
    logits = [_route_logits(hn, (wr_ref[0], wr_ref[1]), rb_ref[:, 0:1], E) for hn in hns]
    carry = carry_ref[:, 0:1]
    for sp, lg in zip(spans, logits):
        idx, gts, n_new = _route_assign(lg, carry, E)
        idx_ref[:, sp] = idx
        gts_ref[:, sp] = gts
        carry = carry + n_new
    carry_ref[...] = jnp.broadcast_to(carry, carry_ref.shape)
    cnt_ref[...] = carry_ref[...]


def _proj_router(res2d, ya, yb, w_out, g, router_w, router_b, tm):
    T, D = res2d.shape
    E = router_w.shape[1]
    assert E == SUBLANES
    wa_n = ya.shape[1]
    wa, wb = w_out[:wa_n], w_out[wa_n:]
    wr = jnp.pad(router_w, ((0, 0), (0, LANES - E)))
    wr_hi = wr.astype(BF16)
    wr_pair = jnp.stack([wr_hi, (wr - wr_hi.astype(F32)).astype(BF16)])
    return pl.pallas_call(
        functools.partial(_router_kernel, E=E, sub=_pick_tile(tm, ROUTER_SUB_TILE)),
        out_shape=[
            jax.ShapeDtypeStruct((T, D), F32),
            jax.ShapeDtypeStruct((T, D), F32),
            jax.ShapeDtypeStruct((E, T), jnp.int32),
            jax.ShapeDtypeStruct((E, T), F32),
            jax.ShapeDtypeStruct((E, LANES), F32),
        ],
        grid=(T // tm,),
        in_specs=[
            pl.BlockSpec((tm, D), lambda i: (i, 0)),
            pl.BlockSpec((tm, wa_n), lambda i: (i, 0)),
            pl.BlockSpec((tm, yb.shape[1]), lambda i: (i, 0)),
            pl.BlockSpec(wa.shape, lambda i: (0, 0)),
            pl.BlockSpec(wb.shape, lambda i: (0, 0)),
            pl.BlockSpec((1, D), lambda i: (0, 0)),
            pl.BlockSpec((2, D, LANES), lambda i: (0, 0, 0)),
            pl.BlockSpec((E, LANES), lambda i: (0, 0)),
        ],
        out_specs=[
            pl.BlockSpec((tm, D), lambda i: (i, 0)),
            pl.BlockSpec((tm, D), lambda i: (i, 0)),
            pl.BlockSpec((E, tm), lambda i: (0, i)),
            pl.BlockSpec((E, tm), lambda i: (0, i)),
            pl.BlockSpec((E, LANES), lambda i: (0, 0)),
        ],
        scratch_shapes=[pltpu.VMEM((E, LANES), F32)],
        compiler_params=_cparams(("arbitrary",)),
        name="proj_moe_router",
    )(res2d, ya, yb, wa, wb, g.reshape(1, D), wr_pair, jnp.broadcast_to(router_b[:, None], (E, LANES)))


DMA_ISSUE_UNROLL = True


def _dispatch_kernel(pos1_ref, pos2_ref, ends_ref, h_ref, xs_hbm, zero_ref, sem, zsem, *, tm, tg, E):
    i = pl.program_id(0)

    @pl.when(i == 0)
    def _():
        zero_ref[...] = jnp.zeros_like(zero_ref)

        def zero_copy(e):
            start = ends_ref[e] - tg
            return pltpu.make_async_copy(zero_ref, xs_hbm.at[pl.ds(pl.multiple_of(start, tg), tg)], zsem)

        def nonempty(e):
            return ends_ref[e] > (ends_ref[e - 1] if e else 0)

        for e in range(E):
            @pl.when(nonempty(e))
            def _(e=e):
                zero_copy(e).start()
        for e in range(E):
            @pl.when(nonempty(e))
            def _(e=e):
                zero_copy(e).wait()

    def body(r, carry):
        t = i * tm + r
        src = h_ref.at[pl.ds(r, 1)]
        pltpu.make_async_copy(src, xs_hbm.at[pl.ds(pos1_ref[t], 1)], sem).start()
        pltpu.make_async_copy(src, xs_hbm.at[pl.ds(pos2_ref[t], 1)], sem).start()
        return carry
    lax.fori_loop(0, tm, body, 0, unroll=DMA_ISSUE_UNROLL)

    for _ in range(TOP_K):
        pltpu.make_async_copy(h_ref, xs_hbm.at[pl.ds(0, tm)], sem).wait()


def _dispatch(pos1, pos2, ends, h2d, n_rows, tm, tg):
    T, D = h2d.shape
    E = ends.shape[0]
    return pl.pallas_call(
        functools.partial(_dispatch_kernel, tm=tm, tg=tg, E=E),
        out_shape=jax.ShapeDtypeStruct((n_rows, D), h2d.dtype),
        grid_spec=pltpu.PrefetchScalarGridSpec(
            num_scalar_prefetch=3,
            grid=(T // tm,),
            in_specs=[pl.BlockSpec((tm, D), lambda i, p1, p2, en: (i, 0))],
            out_specs=pl.BlockSpec(memory_space=pl.ANY),
            scratch_shapes=[pltpu.VMEM((tg, D), h2d.dtype), pltpu.SemaphoreType.DMA, pltpu.SemaphoreType.DMA],
        ),
        compiler_params=_cparams(("arbitrary",)),
        name="moe_dispatch",
    )(pos1, pos2, ends, h2d)


def _gmm_kernel(te_ref, tv_ref, x_ref, wg_ref, wu_ref, wd_ref, o_ref, xn_ref, *, rb):
    i = pl.program_id(0)
    f = pl.program_id(1)
    tm = xn_ref.shape[0]
    valid = tv_ref[i]

    @pl.when(f == 0)
    def _():
        o_ref[...] = jnp.zeros_like(o_ref)

    @pl.when((valid > 0) & (f == 0))
    def _():
        xn_ref[...] = x_ref[...].astype(BF16)

    def swiglu_rows(rows):
        x = xn_ref[rows, :]
        a = jnp.dot(x, wg_ref[0].astype(BF16), preferred_element_type=F32)
        u = jnp.dot(x, wu_ref[0].astype(BF16), preferred_element_type=F32)
        hmid = (a * _sigmoid(a) * u).astype(BF16)
        o_ref[rows, :] += jnp.dot(hmid, wd_ref[0].astype(BF16), preferred_element_type=F32)

    @pl.when(valid == tm)
    def _():
        swiglu_rows(slice(None))

    for r in range(tm // rb):
        @pl.when((valid < tm) & (valid > r * rb))
        def _(r=r):
            swiglu_rows(slice(r * rb, (r + 1) * rb))


def _gmm(tile_expert, tile_valid, xs, wg, wu, wd, tm, fc):
    R, D = xs.shape
    F = wg.shape[2]
    nf = F // fc

    def fsel(i, f, tv):
        return jnp.where(tv[i] > 0, f, nf - 1)

    return pl.pallas_call(
        functools.partial(_gmm_kernel, rb=_pick_tile(tm, GMM_TAIL_ROW_BLOCK)),
        out_shape=jax.ShapeDtypeStruct((R, D), F32),
        grid_spec=pltpu.PrefetchScalarGridSpec(
            num_scalar_prefetch=2,
            grid=(R // tm, nf),
            in_specs=[
                pl.BlockSpec((tm, D), lambda i, f, te, tv: (i, 0)),
                pl.BlockSpec((1, D, fc), lambda i, f, te, tv: (te[i], 0, fsel(i, f, tv))),
                pl.BlockSpec((1, D, fc), lambda i, f, te, tv: (te[i], 0, fsel(i, f, tv))),
                pl.BlockSpec((1, fc, D), lambda i, f, te, tv: (te[i], fsel(i, f, tv), 0)),
            ],
            out_specs=pl.BlockSpec((tm, D), lambda i, f, te, tv: (i, 0)),
            scratch_shapes=[pltpu.VMEM((tm, D), BF16)],
        ),
        compiler_params=_cparams(("parallel", "arbitrary")),
        name="moe_gmm",
    )(tile_expert, tile_valid, xs, wg, wu, wd)


def _combine_copies(pos1_ref, pos2_ref, ys_hbm, ya_ref, yb_ref, sems, tile, slot, tm):
    def row_copies(r):
        t = tile * tm + r
        return (pltpu.make_async_copy(ys_hbm.at[pl.ds(pos1_ref[t], 1)], ya_ref.at[slot, pl.ds(r, 1)], sems.at[slot]),
                pltpu.make_async_copy(ys_hbm.at[pl.ds(pos2_ref[t], 1)], yb_ref.at[slot, pl.ds(r, 1)], sems.at[slot]))
    return row_copies


def _combine_kernel(pos1_ref, pos2_ref, h_ref, ga_ref, gb_ref, g_ref, ys_hbm, o_ref, ya_ref, yb_ref, sems, *, tm):
    i = pl.program_id(0)
    n = pl.num_programs(0)
    slot = lax.rem(i, 2)

    def start_tile(tile, slot):
        copies = _combine_copies(pos1_ref, pos2_ref, ys_hbm, ya_ref, yb_ref, sems, tile, slot, tm)

        def body(r, carry):
            a, b = copies(r)
            a.start()
            b.start()
            return carry
        lax.fori_loop(0, tm, body, 0, unroll=DMA_ISSUE_UNROLL)

    @pl.when(i == 0)
    def _():
        start_tile(0, 0)

    @pl.when(i + 1 < n)
    def _():
        start_tile(i + 1, 1 - slot)

    pltpu.make_async_copy(ys_hbm.at[pl.ds(0, tm)], ya_ref.at[slot], sems.at[slot]).wait()
    pltpu.make_async_copy(ys_hbm.at[pl.ds(0, tm)], yb_ref.at[slot], sems.at[slot]).wait()

    h = h_ref[...] + ga_ref[...] * ya_ref[slot] + gb_ref[...] * yb_ref[slot]
    var = jnp.mean(h * h, axis=-1, keepdims=True)
    o_ref[...] = h * lax.rsqrt(var + EPS) * g_ref[...]


def _combine(pos1, pos2, h2d, ga, gb, g, ys, tm):
    T, D = h2d.shape
    row = pl.BlockSpec((tm, D), lambda i, p1, p2: (i, 0))
    colv = pl.BlockSpec((tm, 1), lambda i, p1, p2: (i, 0))
    return pl.pallas_call(
        functools.partial(_combine_kernel, tm=tm),
        out_shape=jax.ShapeDtypeStruct((T, D), F32),
        grid_spec=pltpu.PrefetchScalarGridSpec(
            num_scalar_prefetch=2,
            grid=(T // tm,),
            in_specs=[row, colv, colv, pl.BlockSpec((1, D), lambda i, p1, p2: (0, 0)),
                      pl.BlockSpec(memory_space=pl.ANY)],
            out_specs=row,
            scratch_shapes=[pltpu.VMEM((2, tm, D), F32), pltpu.VMEM((2, tm, D), F32),
                            pltpu.SemaphoreType.DMA((2,))],
        ),
        compiler_params=_cparams(("arbitrary",)),
        name="moe_combine_norm",
    )(pos1, pos2, h2d, ga, gb, g.reshape(1, D), ys)


def _moe_final(res, y_c, y_d, w_out, norm_g, final_g, router_w, router_b, w_gate, w_up, w_down):
    B, S, D = res.shape
    T = B * S
    E = router_w.shape[1]
    h2d, hn2d, idx, gts, cnt = _proj_router(res.reshape(T, D), y_c.reshape(T, -1), y_d.reshape(T, -1),
                                            w_out.astype(BF16), norm_g, router_w, router_b, _pick_tile(T, ROUTER_TILE))
    e1, e2, r1, r2 = idx[0], idx[1], idx[2], idx[3]
    tm = MOE_ROW_TILE if TOP_K * T >= E * MOE_ROW_TILE else _pick_tile(T, TOKEN_TILE)
    counts = cnt[:, 0].astype(jnp.int32)
    padded = ((counts + tm - 1) // tm) * tm
    ends = jnp.cumsum(padded)
    offs = ends - padded
    eids = jnp.arange(E, dtype=jnp.int32)[:, None]
    pos1 = jnp.sum(jnp.where(e1[None, :] == eids, offs[:, None], 0), axis=0) + r1
    pos2 = jnp.sum(jnp.where(e2[None, :] == eids, offs[:, None], 0), axis=0) + r2
    n_tiles = -(-TOP_K * T // tm) + E
    tile_start = jnp.arange(n_tiles, dtype=jnp.int32) * tm
    tile_expert = jnp.minimum(jnp.sum(tile_start[:, None] >= ends[None, :], axis=1), E - 1).astype(jnp.int32)
    tile_valid = jnp.clip((offs + counts)[tile_expert] - tile_start, 0, tm).astype(jnp.int32)
    xs = _dispatch(pos1, pos2, ends.astype(jnp.int32), hn2d, n_tiles * tm, _pick_tile(T, DISPATCH_TILE), tm)
    ys = _gmm(tile_expert, tile_valid, xs, w_gate, w_up, w_down, tm, _pick_tile(w_gate.shape[2], MOE_FF_CHUNK))
    out = _combine(pos1, pos2, h2d, gts[0][:, None], gts[1][:, None], final_g, ys, _pick_tile(T, COMBINE_TILE))
    return out.reshape(B, S, D)


def kernel(x, norm_mix_g, norm_ffn_g, norm_final_g, ab_w_in, s5_lambda_re, s5_lambda_im, s5_log_dt, s5_b_re, s5_b_im, s5_c_re, s5_c_im, s5_d, s5_w_glu, mlstm_conv_w, mlstm_conv_b, mlstm_gate_b, mlstm_norm_g, ab_w_out, ffn_w_gate, ffn_w_up, ffn_w_down, cd_w_in, gmlp_norm_g, gmlp_norm_b, gmlp_w_s, gmlp_b_s, rel_bias, cd_w_out, moe_router_w, moe_router_b, moe_w_gate, moe_w_up, moe_w_down):
    B, S, D = x.shape
    y_a, y_b = _mix_ab(x, norm_mix_g[0], ab_w_in[0], s5_lambda_re[0], s5_lambda_im[0], s5_log_dt[0],
                       s5_b_re[0], s5_b_im[0], s5_c_re[0], s5_c_im[0], s5_d[0], s5_w_glu[0],
                       mlstm_conv_w[0], mlstm_conv_b[0], mlstm_gate_b[0], mlstm_norm_g[0])
    h = _proj_ffn(x, y_a, y_b, ab_w_out[0].astype(BF16), norm_ffn_g[0], ffn_w_gate[0].astype(BF16),
                  ffn_w_up[0].astype(BF16), ffn_w_down[0].astype(BF16),
                  _pick_tile(S, TOKEN_TILE), _pick_tile(ffn_w_gate.shape[2], FFN_FF_CHUNK))
    y_c, y_d = _mix_cd(h, norm_mix_g[1], cd_w_in[0], gmlp_norm_g[0], gmlp_norm_b[0],
                       gmlp_w_s[0], gmlp_b_s[0], rel_bias)
    return _moe_final(h, y_c, y_d, cd_w_out[0], norm_ffn_g[1], norm_final_g, moe_router_w[0], moe_router_b[0],
                      moe_w_gate[0], moe_w_up[0], moe_w_down[0])
```

```python
import functools
import math

import jax
import jax.numpy as jnp
from jax import lax
from jax.experimental import pallas as pl
from jax.experimental.pallas import tpu as pltpu

F32 = jnp.float32
BF16 = jnp.bfloat16
EPS = 1e-5

LANES = 128
SUBLANES = 8
VMEM_LIMIT_BYTES = 56 * 1024 * 1024

S5_GROUP = 16
S5_SLAB_GROUPS = 8
MLSTM_HEADS = 4
MLSTM_CHUNK = 128
MLSTM_SEQS_PER_STEP = 2
CONV_WIDTH = 4
GMLP_GROUPS = 4
GMLP_CHUNK = 128
MOBA_HEADS = 4
MOBA_BLOCK = 256
MOBA_TOPK = 3
REL_BUCKETS = 32
REL_MAX_DIST = 128
TOP_K = 2

TOKEN_TILE = 512
FFN_FF_CHUNK = 256
S5_TIME_CHUNK = 32
ROUTER_TILE = 1024
ROUTER_SUB_TILE = 512
DISPATCH_TILE = 1024
COMBINE_TILE = 256
MOE_FF_CHUNK = 512
GMM_TAIL_ROW_BLOCK = 256
MOE_ROW_TILE = 1536


def _cparams(sem):
    return pltpu.CompilerParams(dimension_semantics=sem, vmem_limit_bytes=VMEM_LIMIT_BYTES)


def _rms_bf16(x, g):
    var = jnp.mean(x * x, axis=-1, keepdims=True)
    return (x * lax.rsqrt(var + EPS) * g).astype(BF16)


def _gelu(x):
    return jax.nn.gelu(x, approximate=True)


def _sigmoid(x):
    return 1.0 / (1.0 + jnp.exp(-x))


def _norm_proj_kernel(x_ref, g_ref, w_ref, *out_refs, splits):
    hn = _rms_bf16(x_ref[0], g_ref[...])
    for o_ref, (c0, c1) in zip(out_refs, splits):
        r = jnp.dot(hn, w_ref[:, c0:c1], preferred_element_type=F32)
        o_ref[...] = r.reshape(o_ref.shape).astype(o_ref.dtype)


def _norm_proj(x, g, w, outs, tm):
    B, S, D = x.shape
    splits = tuple((c0, c1) for c0, c1, _ in outs)
    out_shape, out_specs = [], []
    for c0, c1, dt in outs:
        n = c1 - c0
        out_shape.append(jax.ShapeDtypeStruct((B, S, n), dt))
        out_specs.append(pl.BlockSpec((1, tm, n), lambda b, i: (b, i, 0)))
    return pl.pallas_call(
        functools.partial(_norm_proj_kernel, splits=splits),
        out_shape=out_shape,
        grid=(B, S // tm),
        in_specs=[
            pl.BlockSpec((1, tm, D), lambda b, i: (b, i, 0)),
            pl.BlockSpec((1, D), lambda b, i: (0, 0)),
            pl.BlockSpec(w.shape, lambda b, i: (0, 0)),
        ],
        out_specs=out_specs,
        compiler_params=_cparams(("parallel", "parallel")),
        name="norm_proj",
    )(x, g.reshape(1, D), w)


def _s5_kernel(u_hbm, bmat_ref, ar_ref, ai_ref, cmat_ref, d_ref, wglu_ref, y_hbm,
               xr_ref, xi_ref, ubuf, ybuf, in_sems, out_sems, *bufs, lc, nb, nslab, sw):
    i = pl.program_id(0)
    n = pl.num_programs(0)
    slot = lax.rem(i, 2)
    W = ubuf.shape[-1]

    def in_copies(step, slot):
        t0 = pl.multiple_of(step * lc, lc)
        return [pltpu.make_async_copy(u_hbm.at[b, pl.ds(t0, lc), :], ubuf.at[slot, :, b, :], in_sems.at[slot])
                for b in range(nb)]

    def out_copies(step, slot):
        t0 = pl.multiple_of(step * lc, lc)
        return [pltpu.make_async_copy(ybuf.at[slot, :, b, :], y_hbm.at[b, pl.ds(t0, lc), :], out_sems.at[slot])
                for b in range(nb)]

    @pl.when(i == 0)
    def _():
        xr_ref[...] = jnp.zeros_like(xr_ref)
        xi_ref[...] = jnp.zeros_like(xi_ref)
        for c in in_copies(0, 0):
            c.start()

    @pl.when(i + 1 < n)
    def _():
        for c in in_copies(i + 1, 1 - slot):
            c.start()

    for c in in_copies(i, slot):
        c.wait()

    uf = ubuf[slot].reshape(lc * nb, W)
    u = uf.astype(BF16)
    xr_all = xr_ref[...]
    xi_all = xi_ref[...]

    def drive(k):
        bufs[k][...] = jnp.dot(u[:, LANES * k:LANES * (k + 1)], bmat_ref[k], preferred_element_type=F32)

    drive(0)
    ys, new_xr, new_xi = [], [], []
    for k in range(nslab):
        if k + 1 < nslab:
            drive(k + 1)
        st_cols = slice(sw * k, sw * (k + 1))
        ar = jnp.broadcast_to(ar_ref[:, st_cols], (nb, sw))
        ai = jnp.broadcast_to(ai_ref[:, st_cols], (nb, sw))
        xr, xi = xr_all[:, st_cols], xi_all[:, st_cols]
        for t in range(lc):
            rows = slice(t * nb, (t + 1) * nb)
            nxr = ar * xr - ai * xi + bufs[k][rows, :sw]
            nxi = ar * xi + ai * xr + bufs[k][rows, sw:]
            bufs[k][rows, :sw] = nxr
            bufs[k][rows, sw:] = nxi
            xr, xi = nxr, nxi
        new_xr.append(xr)
        new_xi.append(xi)
        ys.append(jnp.dot(bufs[k][...].astype(BF16), cmat_ref[k], preferred_element_type=F32))
    xr_ref[...] = jnp.concatenate(new_xr, axis=-1)
    xi_ref[...] = jnp.concatenate(new_xi, axis=-1)
    y = jnp.concatenate(ys, axis=-1)
    y = _gelu(y + d_ref[...] * uf)
    gl = jnp.dot(y.astype(BF16), wglu_ref[...], preferred_element_type=F32)

    @pl.when(i >= 2)
    def _():
        for c in out_copies(i - 2, slot):
            c.wait()

    ybuf[slot] = (y * _sigmoid(gl)).reshape(lc, nb, W)
    for c in out_copies(i, slot):
        c.start()

    @pl.when(i == n - 1)
    def _():
        for c in out_copies(i, slot):
            c.wait()

        @pl.when(n >= 2)
        def _():
            for c in out_copies(i - 1, 1 - slot):
                c.wait()


def _s5_params(lam_re, lam_im, log_dt, b_re, b_im, c_re, c_im):
    G, P = lam_re.shape
    Hc = b_re.shape[-1]
    dt = jnp.exp(log_dt.astype(F32))[:, None]
    mag = jnp.exp(lam_re * dt)
    ar = mag * jnp.cos(lam_im * dt)
    ai = mag * jnp.sin(lam_im * dt)
    den = lam_re * lam_re + lam_im * lam_im
    cr = ((ar - 1.0) * lam_re + ai * lam_im) / den
    ci = (ai * lam_re - (ar - 1.0) * lam_im) / den
    bb_re = cr[..., None] * b_re - ci[..., None] * b_im
    bb_im = cr[..., None] * b_im + ci[..., None] * b_re
    gs = S5_SLAB_GROUPS
    nslab = G // gs
    eye = jnp.eye(gs, dtype=F32)

    def bd_in(b):
        b = b.reshape(nslab, gs, P, Hc)
        return jnp.einsum('kgph,gj->kghjp', b, eye).reshape(nslab, gs * Hc, gs * P)

    def bd_out(c):
        c = c.reshape(nslab, gs, Hc, P)
        return jnp.einsum('kghp,gj->kgpjh', c, eye).reshape(nslab, gs * P, gs * Hc)

    bmat = jnp.concatenate([bd_in(bb_re), bd_in(bb_im)], axis=-1).astype(BF16)
    cmat = jnp.concatenate([bd_out(c_re), -bd_out(c_im)], axis=1).astype(BF16)
    return bmat, ar.reshape(1, G * P), ai.reshape(1, G * P), cmat


def _s5(u, bmat, ar, ai, cmat, d_skip, w_glu, lc):
    nb, S, W = u.shape
    assert nb % SUBLANES == 0
    nslab = bmat.shape[0]
    sw = bmat.shape[2] // 2
    rows = lc * nb
    return pl.pallas_call(
        functools.partial(_s5_kernel, lc=lc, nb=nb, nslab=nslab, sw=sw),
        out_shape=jax.ShapeDtypeStruct((nb, S, W), F32),
        grid=(S // lc,),
        in_specs=[
            pl.BlockSpec(memory_space=pl.ANY),
            pl.BlockSpec(bmat.shape, lambda i: (0, 0, 0)),
            pl.BlockSpec(ar.shape, lambda i: (0, 0)),
            pl.BlockSpec(ai.shape, lambda i: (0, 0)),
            pl.BlockSpec(cmat.shape, lambda i: (0, 0, 0)),
            pl.BlockSpec((1, W), lambda i: (0, 0)),
            pl.BlockSpec(w_glu.shape, lambda i: (0, 0)),
        ],
        out_specs=pl.BlockSpec(memory_space=pl.ANY),
        scratch_shapes=[pltpu.VMEM((nb, sw * nslab), F32), pltpu.VMEM((nb, sw * nslab), F32),
                        pltpu.VMEM((2, lc, nb, W), F32), pltpu.VMEM((2, lc, nb, W), F32),
                        pltpu.SemaphoreType.DMA((2,)), pltpu.SemaphoreType.DMA((2,))]
        + [pltpu.VMEM((rows, 2 * sw), F32) for _ in range(nslab)],
        compiler_params=_cparams(("arbitrary",)),
        name="s5",
    )(u, bmat, ar, ai, cmat, d_skip.reshape(1, W), w_glu)


def _mlstm_pre(x, g, tail, m_row, cw, cb, *, L, H, Dh):
    W = H * Dh
    halo = SUBLANES
    log_scale = -0.5 * math.log(Dh)

    row = lax.broadcasted_iota(jnp.int32, (L, L), 0)
    col = lax.broadcasted_iota(jnp.int32, (L, L), 1)
    conv = cb + cw[0:1, :] * x.astype(F32)
    corr = jnp.zeros((halo, 2 * W), F32)
    for j in range(1, CONV_WIDTH):
        shift = (row - col == j).astype(BF16)
        conv = conv + cw[j:j + 1, :] * jnp.dot(shift, x, preferred_element_type=F32)
        corr = corr + cw[j:j + 1, :] * tail[halo - j:2 * halo - j, :]
    conv = jnp.concatenate([conv[:halo] + corr, conv[halo:]], axis=0)
    new_tail = x[L - halo:, :].astype(F32)
    qk = (conv * _sigmoid(conv)).astype(BF16)
    log_i = g[:, :LANES]
    f_pre = g[:, LANES:]
    logf = jnp.minimum(f_pre, 0.0) - jnp.log(1.0 + jnp.exp(-jnp.abs(f_pre)))
    causal = row >= col
    bcum = jnp.dot(causal.astype(F32), logf, preferred_element_type=F32, precision=lax.Precision.HIGHEST)
    w_cols = log_i - bcum
    trow = lax.broadcasted_iota(jnp.int32, (L, LANES), 0)
    cmax = w_cols
    k = 1
    while k < L:
        cmax = jnp.maximum(cmax, jnp.where(trow >= k, pltpu.roll(cmax, k, axis=0), -jnp.inf))
        k *= 2
    m_inter = bcum + m_row
    m_t = jnp.maximum(bcum + cmax, m_inter)
    u_cols = bcum - m_t + log_scale
    wi_cols = jnp.exp(m_inter - m_t)
    em_cols = jnp.exp(-m_t)
    b_last = bcum[L - 1:L, :]
    m_loc = b_last + cmax[L - 1:L, :]
    m_new = jnp.maximum(b_last + m_row, m_loc)
    s_prev = jnp.exp(b_last + m_row - m_new)
    s_loc = jnp.exp(m_loc - m_new)
    w_rows = w_cols.T
    return dict(q=qk[:, :W], k=qk[:, W:], causal=causal, u_cols=u_cols, wi_cols=wi_cols, em_cols=em_cols,
                w_rows=w_rows, b_last=b_last, m_loc=m_loc, m_new=m_new, s_prev=s_prev, s_loc=s_loc,
                new_tail=new_tail, log_scale=log_scale)


def _mlstm_heads(pres, vbs, o_pres, cxs, ng, *, L, H, Dh):
    nt = (((1,), (1,)), ((), ()))
    pairs = [(b, h) for b in range(len(pres)) for h in range(H)]
    hs = [slice(h * Dh, (h + 1) * Dh) for h in range(H)]
    hl = [slice(h, h + 1) for h in range(H)]
    ones = jnp.ones((L, Dh), BF16)
    sq_ones = jnp.ones((Dh, Dh), BF16)
    vext = {(b, h): jnp.concatenate([vbs[b][:, hs[h]], ones], axis=1) for b, h in pairs}
    qk_t = {(b, h): lax.dot_general(pres[b]['q'][:, hs[h]], pres[b]['k'][:, hs[h]], nt,
                                    preferred_element_type=F32) for b, h in pairs}
    r2 = {(b, h): jnp.dot(pres[b]['q'][:, hs[h]], cxs[b][h].astype(BF16), preferred_element_type=F32)
          for b, h in pairs}
    s = {}
    for b, h in pairs:
        p = pres[b]
        decay = jnp.exp(jnp.where(p['causal'], p['u_cols'][:, hl[h]] + p['w_rows'][hl[h], :], -jnp.inf))
        s[b, h] = (qk_t[b, h] * decay).astype(BF16)
    r1 = {bh: jnp.dot(s[bh], vext[bh], preferred_element_type=F32) for bh in pairs}
    hh = {}
    for b, h in pairs:
        p = pres[b]
        wi = jnp.broadcast_to(p['wi_cols'][:, hl[h]], (L, Dh))
        num = r1[b, h][:, :Dh] + wi * r2[b, h][:, :Dh]
        den = r1[b, h][:, Dh:] + wi * r2[b, h][:, Dh:]
        hh[b, h] = num / jnp.maximum(jnp.abs(den), jnp.broadcast_to(p['em_cols'][:, hl[h]], (L, Dh)))
    msq = {bh: jnp.dot((hh[bh] * hh[bh]).astype(BF16), sq_ones, preferred_element_type=F32) * (1.0 / Dh)
           for bh in pairs}
    ys = {(b, h): (_sigmoid(o_pres[b][:, hs[h]].astype(F32)) * hh[b, h] * lax.rsqrt(msq[b, h] + EPS)
                   * ng[:, hs[h]]).astype(BF16) for b, h in pairs}
    kwt = {}
    for b, h in pairs:
        p = pres[b]
        wexp = jnp.exp(p['b_last'][:, hl[h]] + p['w_rows'][hl[h], :] - p['m_loc'][:, hl[h]] + p['log_scale'])
        kwt[b, h] = (p['k'][:, hs[h]].astype(F32).T * wexp).astype(BF16)
    cx_loc = {bh: jnp.dot(kwt[bh], vext[bh], preferred_element_type=F32) for bh in pairs}
    new_cx = {(b, h): pres[b]['s_prev'][:, hl[h]] * cxs[b][h] + pres[b]['s_loc'][:, hl[h]] * cx_loc[b, h]
              for b, h in pairs}
    return ys, new_cx


def _mlstm_kernel(qk_ref, v_ref, o_ref, gt_ref, cw_ref, cb_ref, gb_ref, ng_ref, y_ref,
                  tail_ref, c_ref, m_ref, *, L, H, Dh, BB):
    @pl.when(pl.program_id(1) == 0)
    def _():
        tail_ref[...] = jnp.zeros_like(tail_ref)
        c_ref[...] = jnp.zeros_like(c_ref)
        m_ref[...] = jnp.zeros_like(m_ref)

    tails = [tail_ref[bb] for bb in range(BB)]
    cxs = [[c_ref[bb, h] for h in range(H)] for bb in range(BB)]
    m_rows = [m_ref[bb, 0:1, :] for bb in range(BB)]
    pres = [_mlstm_pre(qk_ref[bb], gt_ref[bb] + gb_ref[...], tails[bb], m_rows[bb], cw_ref[...], cb_ref[...],
                       L=L, H=H, Dh=Dh) for bb in range(BB)]
    ys, new_cx = _mlstm_heads(pres, [v_ref[bb].astype(BF16) for bb in range(BB)], [o_ref[bb] for bb in range(BB)],
                              cxs, ng_ref[...], L=L, H=H, Dh=Dh)
    for bb in range(BB):
        for h in range(H):
            y_ref[bb, :, h * Dh:(h + 1) * Dh] = ys[bb, h]
            c_ref[bb, h] = new_cx[bb, h]
        tail_ref[bb, 0:SUBLANES, :] = pres[bb]['new_tail']
        m_ref[bb, 0:1, :] = pres[bb]['m_new']


def _mlstm(qk_pre, v, o_pre, gates, conv_w, conv_b, gate_b_pad, norm_g):
    B, S, W2 = qk_pre.shape
    W = W2 // 2
    H, L = MLSTM_HEADS, MLSTM_CHUNK
    Dh = W // H
    BB = _pick_tile(B, MLSTM_SEQS_PER_STEP)
    return pl.pallas_call(
        functools.partial(_mlstm_kernel, L=L, H=H, Dh=Dh, BB=BB),
        out_shape=jax.ShapeDtypeStruct((B, S, W), BF16),
        grid=(B // BB, S // L),
        in_specs=[
            pl.BlockSpec((BB, L, W2), lambda b, c: (b, c, 0)),
            pl.BlockSpec((BB, L, W), lambda b, c: (b, c, 0)),
            pl.BlockSpec((BB, L, W), lambda b, c: (b, c, 0)),
            pl.BlockSpec((BB, L, 2 * LANES), lambda b, c: (b, c, 0)),
            pl.BlockSpec((CONV_WIDTH, W2), lambda b, c: (0, 0)),
            pl.BlockSpec((1, W2), lambda b, c: (0, 0)),
            pl.BlockSpec((1, 2 * LANES), lambda b, c: (0, 0)),
            pl.BlockSpec((1, W), lambda b, c: (0, 0)),
        ],
        out_specs=pl.BlockSpec((BB, L, W), lambda b, c: (b, c, 0)),
        scratch_shapes=[
            pltpu.VMEM((BB, 2 * SUBLANES, W2), F32),
            pltpu.VMEM((BB, H, Dh, 2 * Dh), F32),
            pltpu.VMEM((BB, SUBLANES, LANES), F32),
        ],
        compiler_params=_cparams(("parallel", "arbitrary")),
        name="mlstm",
    )(qk_pre, v, o_pre, gates, conv_w, conv_b.reshape(1, W2), gate_b_pad, norm_g.reshape(1, W))


def _mix_out(res, a, b, wa_ref, wb_ref):
    acc = jnp.dot(a.astype(BF16), wa_ref[...], preferred_element_type=F32)
    return res + acc + jnp.dot(b.astype(BF16), wb_ref[...], preferred_element_type=F32)


def _proj_ffn_kernel(res_ref, a_ref, b_ref, wa_ref, wb_ref, g_ref, wg_hbm, wu_hbm, wd_hbm, o_ref,
                     wg_ref, wu_ref, wd_ref, sem):
    @pl.when((pl.program_id(0) == 0) & (pl.program_id(1) == 0))
    def _():
        copies = [pltpu.make_async_copy(src, dst, sem.at[j])
                  for j, (src, dst) in enumerate(((wg_hbm, wg_ref), (wu_hbm, wu_ref), (wd_hbm, wd_ref)))]
        for c in copies:
            c.start()
        for c in copies:
            c.wait()

    h = _mix_out(res_ref[0], a_ref[0], b_ref[0], wa_ref, wb_ref)
    hn = _rms_bf16(h, g_ref[...])
    o_ref[0] = h
    for j in range(wg_ref.shape[0]):
        a = jnp.dot(hn, wg_ref[j], preferred_element_type=F32)
        u = jnp.dot(hn, wu_ref[j], preferred_element_type=F32)
        hmid = (a * _sigmoid(a) * u).astype(BF16)
        o_ref[0] += jnp.dot(hmid, wd_ref[j], preferred_element_type=F32)


def _proj_ffn(res, ya, yb, w_out, g, wg, wu, wd, tm, fc):
    B, S, D = res.shape
    wa_n = ya.shape[2]
    wa, wb = w_out[:wa_n], w_out[wa_n:]
    F = wg.shape[1]
    nf = F // fc
    wg3 = wg.reshape(D, nf, fc).transpose(1, 0, 2)
    wu3 = wu.reshape(D, nf, fc).transpose(1, 0, 2)
    wd3 = wd.reshape(nf, fc, D)
    return pl.pallas_call(
        _proj_ffn_kernel,
        out_shape=jax.ShapeDtypeStruct((B, S, D), F32),
        grid=(B, S // tm),
        in_specs=[
            pl.BlockSpec((1, tm, D), lambda b, i: (b, i, 0)),
            pl.BlockSpec((1, tm, wa_n), lambda b, i: (b, i, 0)),
            pl.BlockSpec((1, tm, wb.shape[0]), lambda b, i: (b, i, 0)),
            pl.BlockSpec(wa.shape, lambda b, i: (0, 0)),
            pl.BlockSpec(wb.shape, lambda b, i: (0, 0)),
            pl.BlockSpec((1, D), lambda b, i: (0, 0)),
            pl.BlockSpec(memory_space=pl.ANY),
            pl.BlockSpec(memory_space=pl.ANY),
            pl.BlockSpec(memory_space=pl.ANY),
        ],
        out_specs=pl.BlockSpec((1, tm, D), lambda b, i: (b, i, 0)),
        scratch_shapes=[pltpu.VMEM(wg3.shape, wg3.dtype), pltpu.VMEM(wu3.shape, wu3.dtype),
                        pltpu.VMEM(wd3.shape, wd3.dtype), pltpu.SemaphoreType.DMA((3,))],
        compiler_params=_cparams(("arbitrary", "arbitrary")),
        name="proj_ffn_swiglu",
    )(res, ya, yb, wa, wb, g.reshape(1, D), wg3, wu3, wd3)


def _pick_tile(n, target):
    t = min(n, target)
    while n % t:
        t //= 2
    return t


def _mix_ab(h, norm_g, w_in, lam_re, lam_im, log_dt, b_re, b_im, c_re, c_im, d_skip, w_glu,
            conv_w, conv_b, gate_b, mlstm_norm_g):
    B, S, D = h.shape
    s5w = lam_re.shape[0] * S5_GROUP
    mw = mlstm_norm_g.shape[0]
    c_qk, c_v, c_o, c_if = s5w, s5w + 2 * mw, s5w + 3 * mw, s5w + 4 * mw
    nh = (w_in.shape[1] - c_if) // 2
    lane_pad = ((0, 0), (0, LANES - nh))
    w_pad = jnp.concatenate([w_in[:, :c_if], jnp.pad(w_in[:, c_if:c_if + nh], lane_pad),
                             jnp.pad(w_in[:, c_if + nh:], lane_pad)], axis=1).astype(BF16)
    tm = _pick_tile(S, TOKEN_TILE)
    u_s5, qk_pre, v, o_pre, gates = _norm_proj(
        h, norm_g, w_pad,
        [(0, c_qk, F32), (c_qk, c_v, BF16), (c_v, c_o, BF16), (c_o, c_if, BF16), (c_if, c_if + 2 * LANES, F32)],
        tm)
    bmat, ar, ai, cmat = _s5_params(lam_re, lam_im, log_dt, b_re, b_im, c_re, c_im)
    y_a = _s5(u_s5, bmat, ar, ai, cmat, d_skip, w_glu.astype(BF16), lc=_pick_tile(S, S5_TIME_CHUNK))
    gate_b_pad = jnp.concatenate([jnp.pad(gate_b[:nh], (0, LANES - nh)),
                                  jnp.pad(gate_b[nh:], (0, LANES - nh))]).reshape(1, 2 * LANES)
    y_b = _mlstm(qk_pre, v, o_pre, gates, conv_w, conv_b, gate_b_pad, mlstm_norm_g)
    return y_a, y_b


def _gmlp_kernel(u_ref, v_ref, ng_ref, nb_ref, ws_ref, bs_ref, y_ref, *, L, G, Dg, nchunk):
    u = _gelu(u_ref[0].astype(F32))
    v = _gelu(v_ref[0].astype(F32))
    mu = jnp.mean(v, axis=-1, keepdims=True)
    vc = v - mu
    var = jnp.mean(vc * vc, axis=-1, keepdims=True)
    vn = (vc * lax.rsqrt(var + EPS) * ng_ref[...] + nb_ref[...]).astype(BF16)
    for c in range(nchunk):
        rows = slice(c * L, (c + 1) * L)
        for g in range(G):
            cols = slice(g * Dg, (g + 1) * Dg)
            s = jnp.dot(ws_ref[g], vn[rows, cols], preferred_element_type=F32) + bs_ref[:, g:g + 1]
            y_ref[0, rows, cols] = (u[rows, cols] * s).astype(y_ref.dtype)


def _gmlp(u_pre, v_pre, norm_g, norm_b, w_s, b_s, tm):
    B, S, W = u_pre.shape
    G, L = GMLP_GROUPS, GMLP_CHUNK
    ws = (w_s * jnp.tril(jnp.ones((L, L), w_s.dtype))).astype(BF16)
    bs = jnp.pad(b_s.T, ((0, 0), (0, LANES - G)))
    return pl.pallas_call(
        functools.partial(_gmlp_kernel, L=L, G=G, Dg=W // G, nchunk=tm // L),
        out_shape=jax.ShapeDtypeStruct((B, S, W), BF16),
        grid=(B, S // tm),
        in_specs=[
            pl.BlockSpec((1, tm, W), lambda b, i: (b, i, 0)),
            pl.BlockSpec((1, tm, W), lambda b, i: (b, i, 0)),
            pl.BlockSpec((1, W), lambda b, i: (0, 0)),
            pl.BlockSpec((1, W), lambda b, i: (0, 0)),
            pl.BlockSpec((G, L, L), lambda b, i: (0, 0, 0)),
            pl.BlockSpec((L, LANES), lambda b, i: (0, 0)),
        ],
        out_specs=pl.BlockSpec((1, tm, W), lambda b, i: (b, i, 0)),
        compiler_params=_cparams(("parallel", "parallel")),
        name="gmlp",
    )(u_pre, v_pre, norm_g.reshape(1, W), norm_b.reshape(1, W), ws, bs)


NEG_BIG = -1e30


def _moba_kernel(q_ref, k_ref, v_ref, t0_ref, t1_ref, bfar_ref, y_ref, vext_ref, *, BL, NB, Dh, topk):
    nt = (((1,), (1,)), ((), ()))
    scale = 1.0 / math.sqrt(Dh)
    vext_ref[:, :Dh] = v_ref[0]
    vext_ref[:, Dh:] = jnp.ones((NB * BL, Dh), BF16)
    bfar = bfar_ref[0][:, 0:1]
    kmean = jnp.concatenate(
        [jnp.mean(k_ref[0, n * BL:(n + 1) * BL, :].astype(F32), axis=0, keepdims=True) for n in range(NB)],
        axis=0)
    nrow = lax.broadcasted_iota(jnp.int32, (NB, BL), 0)
    eye = (lax.broadcasted_iota(jnp.int32, (BL, BL), 0) ==
           lax.broadcasted_iota(jnp.int32, (BL, BL), 1)).astype(BF16)

    def selection_mask(qi):
        qf = q_ref[0, qi * BL:(qi + 1) * BL, :].astype(F32)
        gate = lax.dot_general(kmean, qf, nt, preferred_element_type=F32, precision=lax.Precision.HIGHEST)
        rank = jnp.zeros((NB, BL), F32)
        for m in range(qi):
            gm = gate[m:m + 1, :]
            rank = rank + jnp.where(gm > gate, 1.0, jnp.where(gm == gate, jnp.where(nrow > m, 1.0, 0.0), 0.0))
        sel_rows = jnp.where(nrow < qi, jnp.where(rank < topk, 1.0, 0.0), 0.0)
        sel_pad = jnp.concatenate([sel_rows, jnp.zeros((LANES - NB, BL), F32)], axis=0).astype(BF16)
        sel_cols = lax.dot_general(eye, sel_pad, nt, preferred_element_type=F32)
        return (1.0 - sel_cols) * NEG_BIG

    def logits(qi):
        qs = (q_ref[0, qi * BL:(qi + 1) * BL, :].astype(F32) * scale).astype(BF16)
        raw = [lax.dot_general(qs, k_ref[0, kb * BL:(kb + 1) * BL, :], nt, preferred_element_type=F32)
               for kb in range(qi + 1)]
        neg_cols = selection_mask(qi) if qi > topk else None
        m_tile = None
        pieces = []
        for kb in range(qi + 1):
            s = raw[kb]
            if kb == qi:
                s = s + t0_ref[0]
            elif kb == qi - 1:
                s = s + t1_ref[0]
                if neg_cols is not None:
                    s = s + neg_cols[:, kb:kb + 1]
            else:
                s = s + (bfar if neg_cols is None else bfar + neg_cols[:, kb:kb + 1])
            pieces.append(s)
            for c in range(BL // LANES):
                piece = s[:, c * LANES:(c + 1) * LANES]
                m_tile = piece if m_tile is None else jnp.maximum(m_tile, piece)
        return pieces, m_tile

    nxt = logits(0)
    for qi in range(NB):
        pieces, m_tile = nxt
        if qi + 1 < NB:
            nxt = logits(qi + 1)
        m = jnp.max(m_tile, axis=-1, keepdims=True)
        p = jnp.concatenate([jnp.exp(s - m).astype(BF16) for s in pieces], axis=1)
        acc = jnp.dot(p, vext_ref[:(qi + 1) * BL, :], preferred_element_type=F32)
        y_ref[0, qi * BL:(qi + 1) * BL, :] = (acc[:, :Dh] / acc[:, Dh:]).astype(y_ref.dtype)


def _rel_bucket(n):
    max_exact = REL_BUCKETS // 2
    nf = jnp.maximum(n, 1).astype(F32)
    large = max_exact + (jnp.log(nf / max_exact) / math.log(REL_MAX_DIST / max_exact)
                         * (REL_BUCKETS - max_exact)).astype(jnp.int32)
    large = jnp.minimum(large, REL_BUCKETS - 1)
    return jnp.where(n < max_exact, n, large)


def _moba(qkv, rel_bias):
    B, S, W3 = qkv.shape
    H, BL = MOBA_HEADS, MOBA_BLOCK
    W = W3 // 3
    Dh = W // H
    NB = S // BL
    assert BL + 1 >= REL_MAX_DIST and Dh == LANES and S % BL == 0
    i = jnp.arange(BL, dtype=jnp.int32)[:, None]
    j = jnp.arange(BL, dtype=jnp.int32)[None, :]
    buckets = jnp.arange(REL_BUCKETS, dtype=jnp.int32)

    def bias_tile(dist):
        onehot = (_rel_bucket(dist)[:, :, None] == buckets).astype(F32)
        return jnp.einsum('ijb,bh->hij', onehot, rel_bias, precision=lax.Precision.HIGHEST)

    t0 = jnp.where(i >= j, bias_tile(jnp.maximum(i - j, 0)), NEG_BIG)
    t1 = bias_tile(BL + i - j)
    bfar = jnp.broadcast_to(rel_bias.T[:, REL_BUCKETS - 1][:, None, None], (H, 1, LANES))
    return pl.pallas_call(
        functools.partial(_moba_kernel, BL=BL, NB=NB, Dh=Dh, topk=min(MOBA_TOPK, NB)),
        out_shape=jax.ShapeDtypeStruct((B, S, W), BF16),
        grid=(B, H),
        in_specs=[
            pl.BlockSpec((1, S, Dh), lambda b, h: (b, 0, h)),
            pl.BlockSpec((1, S, Dh), lambda b, h: (b, 0, H + h)),
            pl.BlockSpec((1, S, Dh), lambda b, h: (b, 0, 2 * H + h)),
            pl.BlockSpec((1, BL, BL), lambda b, h: (h, 0, 0)),
            pl.BlockSpec((1, BL, BL), lambda b, h: (h, 0, 0)),
            pl.BlockSpec((1, 1, LANES), lambda b, h: (h, 0, 0)),
        ],
        out_specs=pl.BlockSpec((1, S, Dh), lambda b, h: (b, 0, h)),
        scratch_shapes=[pltpu.VMEM((S, 2 * Dh), BF16)],
        compiler_params=_cparams(("parallel", "parallel")),
        name="moba",
    )(qkv, qkv, qkv, t0, t1, bfar)


def _mix_cd(h, norm_g, w_in, gmlp_norm_g, gmlp_norm_b, gmlp_w_s, gmlp_b_s, rel_bias):
    B, S, D = h.shape
    gw = gmlp_norm_g.shape[0]
    tm = _pick_tile(S, TOKEN_TILE)
    u_pre, v_pre, qkv = _norm_proj(
        h, norm_g, w_in.astype(BF16),
        [(0, gw, BF16), (gw, 2 * gw, BF16), (2 * gw, w_in.shape[1], BF16)], tm)
    y_c = _gmlp(u_pre, v_pre, gmlp_norm_g, gmlp_norm_b, gmlp_w_s, gmlp_b_s, tm)
    return y_c, _moba(qkv, rel_bias)


def _route_logits(hn, wr, rb, E):
    hn_hi = hn.astype(BF16)
    hn_lo = (hn - hn_hi.astype(F32)).astype(BF16)
    wr_hi, wr_lo = wr
    lg = (jnp.dot(hn_hi, wr_hi, preferred_element_type=F32) + jnp.dot(hn_hi, wr_lo, preferred_element_type=F32)
          + jnp.dot(hn_lo, wr_hi, preferred_element_type=F32))
    return lg.T[:E, :] + rb


def _route_assign(logits, carry, E):
    tm = logits.shape[1]
    rowi = lax.broadcasted_iota(jnp.int32, (E, tm), 0)
    v1 = jnp.max(logits, axis=0, keepdims=True)
    e1 = jnp.min(jnp.where(logits == v1, rowi, E), axis=0, keepdims=True)
    masked = jnp.where(rowi == e1, -jnp.inf, logits)
    v2 = jnp.max(masked, axis=0, keepdims=True)
    e2 = jnp.min(jnp.where(masked == v2, rowi, E), axis=0, keepdims=True)
    ex = jnp.exp(v2 - v1)
    g1 = 1.0 / (1.0 + ex)
    g2 = ex / (1.0 + ex)
    oh1 = rowi == e1
    oh2 = rowi == e2
    cnt = jnp.where(oh1, 1.0, jnp.where(oh2, 1.0, 0.0))
    before = (lax.broadcasted_iota(jnp.int32, (tm, tm), 0) <
              lax.broadcasted_iota(jnp.int32, (tm, tm), 1)).astype(BF16)
    excl = jnp.dot(cnt.astype(BF16), before, preferred_element_type=F32) + carry
    r1 = jnp.sum(jnp.where(oh1, excl, 0.0), axis=0, keepdims=True).astype(jnp.int32)
    r2 = jnp.sum(jnp.where(oh2, excl, 0.0), axis=0, keepdims=True).astype(jnp.int32)
    idx = jnp.where(rowi == 0, e1, jnp.where(rowi == 1, e2, jnp.where(rowi == 2, r1, jnp.where(rowi == 3, r2, 0))))
    gts = jnp.where(rowi == 0, g1, jnp.where(rowi == 1, g2, 0.0))
    return idx, gts, jnp.sum(cnt, axis=1, keepdims=True)


def _router_kernel(res_ref, a_ref, b_ref, wa_ref, wb_ref, g_ref, wr_ref, rb_ref,
                   h_ref, hn_ref, idx_ref, gts_ref, cnt_ref, carry_ref, *, E, sub):
    @pl.when(pl.program_id(0) == 0)
    def _():
        carry_ref[...] = jnp.zeros_like(carry_ref)

    nsub = res_ref.shape[0] // sub
    spans = [slice(j * sub, (j + 1) * sub) for j in range(nsub)]
    xs = [_mix_out(res_ref[sp, :], a_ref[sp, :], b_ref[sp, :], wa_ref, wb_ref) for sp in spans]
    hns = []
    for sp, x in zip(spans, xs):
        h_ref[sp, :] = x
        var = jnp.mean(x * x, axis=-1, keepdims=True)
        hns.append(x * lax.rsqrt(var + EPS) * g_ref[...])
        hn_ref[sp, :] = hns[-1]
    logits = [_route_logits(hn, (wr_ref[0], wr_ref[1]), rb_ref[:, 0:1], E) for hn in hns]
    carry = carry_ref[:, 0:1]
    for sp, lg in zip(spans, logits):
        idx, gts, n_new = _route_assign(lg, carry, E)
        idx_ref[:, sp] = idx
        gts_ref[:, sp] = gts
        carry = carry + n_new
    carry_ref[...] = jnp.broadcast_to(carry, carry_ref.shape)
    cnt_ref[...] = carry_ref[...]


def _proj_router(res2d, ya, yb, w_out, g, router_w, router_b, tm):
    T, D = res2d.shape
    E = router_w.shape[1]
    assert E == SUBLANES
    wa_n = ya.shape[1]
    wa, wb = w_out[:wa_n], w_out[wa_n:]
    wr = jnp.pad(router_w, ((0, 0), (0, LANES - E)))
    wr_hi = wr.astype(BF16)
    wr_pair = jnp.stack([wr_hi, (wr - wr_hi.astype(F32)).astype(BF16)])
    return pl.pallas_call(
        functools.partial(_router_kernel, E=E, sub=_pick_tile(tm, ROUTER_SUB_TILE)),
        out_shape=[
            jax.ShapeDtypeStruct((T, D), F32),
            jax.ShapeDtypeStruct((T, D), F32),
            jax.ShapeDtypeStruct((E, T), jnp.int32),
            jax.ShapeDtypeStruct((E, T), F32),
            jax.ShapeDtypeStruct((E, LANES), F32),
        ],
        grid=(T // tm,),
        in_specs=[
            pl.BlockSpec((tm, D), lambda i: (i, 0)),
            pl.BlockSpec((tm, wa_n), lambda i: (i, 0)),
            pl.BlockSpec((tm, yb.shape[1]), lambda i: (i, 0)),
            pl.BlockSpec(wa.shape, lambda i: (0, 0)),
            pl.BlockSpec(wb.shape, lambda i: (0, 0)),
            pl.BlockSpec((1, D), lambda i: (0, 0)),
            pl.BlockSpec((2, D, LANES), lambda i: (0, 0, 0)),
            pl.BlockSpec((E, LANES), lambda i: (0, 0)),
        ],
        out_specs=[
            pl.BlockSpec((tm, D), lambda i: (i, 0)),
            pl.BlockSpec((tm, D), lambda i: (i, 0)),
            pl.BlockSpec((E, tm), lambda i: (0, i)),
            pl.BlockSpec((E, tm), lambda i: (0, i)),
            pl.BlockSpec((E, LANES), lambda i: (0, 0)),
        ],
        scratch_shapes=[pltpu.VMEM((E, LANES), F32)],
        compiler_params=_cparams(("arbitrary",)),
        name="proj_moe_router",
    )(res2d, ya, yb, wa, wb, g.reshape(1, D), wr_pair, jnp.broadcast_to(router_b[:, None], (E, LANES)))


DMA_ISSUE_UNROLL = True


def _dispatch_kernel(pos1_ref, pos2_ref, ends_ref, h_ref, xs_hbm, zero_ref, sem, zsem, *, tm, tg, E):
    i = pl.program_id(0)

    @pl.when(i == 0)
    def _():
        zero_ref[...] = jnp.zeros_like(zero_ref)

        def zero_copy(e):
            start = ends_ref[e] - tg
            return pltpu.make_async_copy(zero_ref, xs_hbm.at[pl.ds(pl.multiple_of(start, tg), tg)], zsem)

        def nonempty(e):
            return ends_ref[e] > (ends_ref[e - 1] if e else 0)

        for e in range(E):
            @pl.when(nonempty(e))
            def _(e=e):
                zero_copy(e).start()
        for e in range(E):
            @pl.when(nonempty(e))
            def _(e=e):
                zero_copy(e).wait()

    def body(r, carry):
        t = i * tm + r
        src = h_ref.at[pl.ds(r, 1)]
        pltpu.make_async_copy(src, xs_hbm.at[pl.ds(pos1_ref[t], 1)], sem).start()
        pltpu.make_async_copy(src, xs_hbm.at[pl.ds(pos2_ref[t], 1)], sem).start()
        return carry
    lax.fori_loop(0, tm, body, 0, unroll=DMA_ISSUE_UNROLL)

    for _ in range(TOP_K):
        pltpu.make_async_copy(h_ref, xs_hbm.at[pl.ds(0, tm)], sem).wait()


def _dispatch(pos1, pos2, ends, h2d, n_rows, tm, tg):
    T, D = h2d.shape
    E = ends.shape[0]
    return pl.pallas_call(
        functools.partial(_dispatch_kernel, tm=tm, tg=tg, E=E),
        out_shape=jax.ShapeDtypeStruct((n_rows, D), h2d.dtype),
        grid_spec=pltpu.PrefetchScalarGridSpec(
            num_scalar_prefetch=3,
            grid=(T // tm,),
            in_specs=[pl.BlockSpec((tm, D), lambda i, p1, p2, en: (i, 0))],
            out_specs=pl.BlockSpec(memory_space=pl.ANY),
            scratch_shapes=[pltpu.VMEM((tg, D), h2d.dtype), pltpu.SemaphoreType.DMA, pltpu.SemaphoreType.DMA],
        ),
        compiler_params=_cparams(("arbitrary",)),
        name="moe_dispatch",
    )(pos1, pos2, ends, h2d)


def _gmm_kernel(te_ref, tv_ref, x_ref, wg_ref, wu_ref, wd_ref, o_ref, xn_ref, *, rb):
    i = pl.program_id(0)
    f = pl.program_id(1)
    tm = xn_ref.shape[0]
    valid = tv_ref[i]

    @pl.when(f == 0)
    def _():
        o_ref[...] = jnp.zeros_like(o_ref)

    @pl.when((valid > 0) & (f == 0))
    def _():
        xn_ref[...] = x_ref[...].astype(BF16)

    def swiglu_rows(rows):
        x = xn_ref[rows, :]
        a = jnp.dot(x, wg_ref[0].astype(BF16), preferred_element_type=F32)
        u = jnp.dot(x, wu_ref[0].astype(BF16), preferred_element_type=F32)
        hmid = (a * _sigmoid(a) * u).astype(BF16)
        o_ref[rows, :] += jnp.dot(hmid, wd_ref[0].astype(BF16), preferred_element_type=F32)

    @pl.when(valid == tm)
    def _():
        swiglu_rows(slice(None))

    for r in range(tm // rb):
        @pl.when((valid < tm) & (valid > r * rb))
        def _(r=r):
            swiglu_rows(slice(r * rb, (r + 1) * rb))


def _gmm(tile_expert, tile_valid, xs, wg, wu, wd, tm, fc):
    R, D = xs.shape
    F = wg.shape[2]
    nf = F // fc

    def fsel(i, f, tv):
        return jnp.where(tv[i] > 0, f, nf - 1)

    return pl.pallas_call(
        functools.partial(_gmm_kernel, rb=_pick_tile(tm, GMM_TAIL_ROW_BLOCK)),
        out_shape=jax.ShapeDtypeStruct((R, D), F32),
        grid_spec=pltpu.PrefetchScalarGridSpec(
            num_scalar_prefetch=2,
            grid=(R // tm, nf),
            in_specs=[
                pl.BlockSpec((tm, D), lambda i, f, te, tv: (i, 0)),
                pl.BlockSpec((1, D, fc), lambda i, f, te, tv: (te[i], 0, fsel(i, f, tv))),
                pl.BlockSpec((1, D, fc), lambda i, f, te, tv: (te[i], 0, fsel(i, f, tv))),
                pl.BlockSpec((1, fc, D), lambda i, f, te, tv: (te[i], fsel(i, f, tv), 0)),
            ],
            out_specs=pl.BlockSpec((tm, D), lambda i, f, te, tv: (i, 0)),
            scratch_shapes=[pltpu.VMEM((tm, D), BF16)],
        ),
        compiler_params=_cparams(("parallel", "arbitrary")),
        name="moe_gmm",
    )(tile_expert, tile_valid, xs, wg, wu, wd)


def _combine_copies(pos1_ref, pos2_ref, ys_hbm, ya_ref, yb_ref, sems, tile, slot, tm):
    def row_copies(r):
        t = tile * tm + r
        return (pltpu.make_async_copy(ys_hbm.at[pl.ds(pos1_ref[t], 1)], ya_ref.at[slot, pl.ds(r, 1)], sems.at[slot]),
                pltpu.make_async_copy(ys_hbm.at[pl.ds(pos2_ref[t], 1)], yb_ref.at[slot, pl.ds(r, 1)], sems.at[slot]))
    return row_copies


def _combine_kernel(pos1_ref, pos2_ref, h_ref, ga_ref, gb_ref, g_ref, ys_hbm, o_ref, ya_ref, yb_ref, sems, *, tm):
    i = pl.program_id(0)
    n = pl.num_programs(0)
    slot = lax.rem(i, 2)

    def start_tile(tile, slot):
        copies = _combine_copies(pos1_ref, pos2_ref, ys_hbm, ya_ref, yb_ref, sems, tile, slot, tm)

        def body(r, carry):
            a, b = copies(r)
            a.start()
            b.start()
            return carry
        lax.fori_loop(0, tm, body, 0, unroll=DMA_ISSUE_UNROLL)

    @pl.when(i == 0)
    def _():
        start_tile(0, 0)

    @pl.when(i + 1 < n)
    def _():
        start_tile(i + 1, 1 - slot)

    pltpu.make_async_copy(ys_hbm.at[pl.ds(0, tm)], ya_ref.at[slot], sems.at[slot]).wait()
    pltpu.make_async_copy(ys_hbm.at[pl.ds(0, tm)], yb_ref.at[slot], sems.at[slot]).wait()

    h = h_ref[...] + ga_ref[...] * ya_ref[slot] + gb_ref[...] * yb_ref[slot]
    var = jnp.mean(h * h, axis=-1, keepdims=True)
    o_ref[...] = h * lax.rsqrt(var + EPS) * g_ref[...]


def _combine(pos1, pos2, h2d, ga, gb, g, ys, tm):
    T, D = h2d.shape
    row = pl.BlockSpec((tm, D), lambda i, p1, p2: (i, 0))
    colv = pl.BlockSpec((tm, 1), lambda i, p1, p2: (i, 0))
    return pl.pallas_call(
        functools.partial(_combine_kernel, tm=tm),
        out_shape=jax.ShapeDtypeStruct((T, D), F32),
        grid_spec=pltpu.PrefetchScalarGridSpec(
            num_scalar_prefetch=2,
            grid=(T // tm,),
            in_specs=[row, colv, colv, pl.BlockSpec((1, D), lambda i, p1, p2: (0, 0)),
                      pl.BlockSpec(memory_space=pl.ANY)],
            out_specs=row,
            scratch_shapes=[pltpu.VMEM((2, tm, D), F32), pltpu.VMEM((2, tm, D), F32),
                            pltpu.SemaphoreType.DMA((2,))],
        ),
        compiler_params=_cparams(("arbitrary",)),
        name="moe_combine_norm",
    )(pos1, pos2, h2d, ga, gb, g.reshape(1, D), ys)


def _moe_final(res, y_c, y_d, w_out, norm_g, final_g, router_w, router_b, w_gate, w_up, w_down):
    B, S, D = res.shape
    T = B * S
    E = router_w.shape[1]
    h2d, hn2d, idx, gts, cnt = _proj_router(res.reshape(T, D), y_c.reshape(T, -1), y_d.reshape(T, -1),
                                            w_out.astype(BF16), norm_g, router_w, router_b, _pick_tile(T, ROUTER_TILE))
    e1, e2, r1, r2 = idx[0], idx[1], idx[2], idx[3]
    tm = MOE_ROW_TILE if TOP_K * T >= E * MOE_ROW_TILE else _pick_tile(T, TOKEN_TILE)
    counts = cnt[:, 0].astype(jnp.int32)
    padded = ((counts + tm - 1) // tm) * tm
    ends = jnp.cumsum(padded)
    offs = ends - padded
    eids = jnp.arange(E, dtype=jnp.int32)[:, None]
    pos1 = jnp.sum(jnp.where(e1[None, :] == eids, offs[:, None], 0), axis=0) + r1
    pos2 = jnp.sum(jnp.where(e2[None, :] == eids, offs[:, None], 0), axis=0) + r2
    n_tiles = -(-TOP_K * T // tm) + E
    tile_start = jnp.arange(n_tiles, dtype=jnp.int32) * tm
    tile_expert = jnp.minimum(jnp.sum(tile_start[:, None] >= ends[None, :], axis=1), E - 1).astype(jnp.int32)
    tile_valid = jnp.clip((offs + counts)[tile_expert] - tile_start, 0, tm).astype(jnp.int32)
    xs = _dispatch(pos1, pos2, ends.astype(jnp.int32), hn2d, n_tiles * tm, _pick_tile(T, DISPATCH_TILE), tm)
    ys = _gmm(tile_expert, tile_valid, xs, w_gate, w_up, w_down, tm, _pick_tile(w_gate.shape[2], MOE_FF_CHUNK))
    out = _combine(pos1, pos2, h2d, gts[0][:, None], gts[1][:, None], final_g, ys, _pick_tile(T, COMBINE_TILE))
    return out.reshape(B, S, D)


def kernel(x, norm_mix_g, norm_ffn_g, norm_final_g, ab_w_in, s5_lambda_re, s5_lambda_im, s5_log_dt, s5_b_re, s5_b_im, s5_c_re, s5_c_im, s5_d, s5_w_glu, mlstm_conv_w, mlstm_conv_b, mlstm_gate_b, mlstm_norm_g, ab_w_out, ffn_w_gate, ffn_w_up, ffn_w_down, cd_w_in, gmlp_norm_g, gmlp_norm_b, gmlp_w_s, gmlp_b_s, rel_bias, cd_w_out, moe_router_w, moe_router_b, moe_w_gate, moe_w_up, moe_w_down):
    B, S, D = x.shape
    y_a, y_b = _mix_ab(x, norm_mix_g[0], ab_w_in[0], s5_lambda_re[0], s5_lambda_im[0], s5_log_dt[0],
                       s5_b_re[0], s5_b_im[0], s5_c_re[0], s5_c_im[0], s5_d[0], s5_w_glu[0],
                       mlstm_conv_w[0], mlstm_conv_b[0], mlstm_gate_b[0], mlstm_norm_g[0])
    h = _proj_ffn(x, y_a, y_b, ab_w_out[0].astype(BF16), norm_ffn_g[0], ffn_w_gate[0].astype(BF16),
                  ffn_w_up[0].astype(BF16), ffn_w_down[0].astype(BF16),
                  _pick_tile(S, TOKEN_TILE), _pick_tile(ffn_w_gate.shape[2], FFN_FF_CHUNK))
    y_c, y_d = _mix_cd(h, norm_mix_g[1], cd_w_in[0], gmlp_norm_g[0], gmlp_norm_b[0],
                       gmlp_w_s[0], gmlp_b_s[0], rel_bias)
    return _moe_final(h, y_c, y_d, cd_w_out[0], norm_ffn_g[1], norm_final_g, moe_router_w[0], moe_router_b[0],
                      moe_w_gate[0], moe_w_up[0], moe_w_down[0])
```

```python
import functools
import math

import jax
import jax.numpy as jnp
from jax import lax
from jax.experimental import pallas as pl
from jax.experimental.pallas import tpu as pltpu

F32 = jnp.float32
BF16 = jnp.bfloat16
EPS = 1e-5

LANES = 128
SUBLANES = 8
VMEM_LIMIT_BYTES = 56 * 1024 * 1024

S5_GROUP = 16
S5_SLAB_GROUPS = 8
MLSTM_HEADS = 4
MLSTM_CHUNK = 128
MLSTM_SEQS_PER_STEP = 2
CONV_WIDTH = 4
GMLP_GROUPS = 4
GMLP_CHUNK = 128
MOBA_HEADS = 4
MOBA_BLOCK = 256
MOBA_TOPK = 3
REL_BUCKETS = 32
REL_MAX_DIST = 128
TOP_K = 2

TOKEN_TILE = 1024
FFN_FF_CHUNK = 256
S5_TIME_CHUNK = 32
ROUTER_TILE = 1024
ROUTER_SUB_TILE = 512
DISPATCH_TILE = 1024
COMBINE_TILE = 256
MOE_FF_CHUNK = 512
GMM_TAIL_ROW_BLOCK = 256
MOE_ROW_TILE = 1536


def _cparams(sem):
    return pltpu.CompilerParams(dimension_semantics=sem, vmem_limit_bytes=VMEM_LIMIT_BYTES)


def _rms_bf16(x, g):
    var = jnp.mean(x * x, axis=-1, keepdims=True)
    return (x * lax.rsqrt(var + EPS) * g).astype(BF16)


def _gelu(x):
    return jax.nn.gelu(x, approximate=True)


def _sigmoid(x):
    return 1.0 / (1.0 + jnp.exp(-x))


def _norm_proj_kernel(x_ref, g_ref, w_ref, *out_refs, splits):
    hn = _rms_bf16(x_ref[0], g_ref[...])
    for o_ref, (c0, c1) in zip(out_refs, splits):
        r = jnp.dot(hn, w_ref[:, c0:c1], preferred_element_type=F32)
        o_ref[...] = r.reshape(o_ref.shape).astype(o_ref.dtype)


def _norm_proj(x, g, w, outs, tm):
    B, S, D = x.shape
    splits = tuple((c0, c1) for c0, c1, _ in outs)
    out_shape, out_specs = [], []
    for c0, c1, dt in outs:
        n = c1 - c0
        out_shape.append(jax.ShapeDtypeStruct((B, S, n), dt))
        out_specs.append(pl.BlockSpec((1, tm, n), lambda b, i: (b, i, 0)))
    return pl.pallas_call(
        functools.partial(_norm_proj_kernel, splits=splits),
        out_shape=out_shape,
        grid=(B, S // tm),
        in_specs=[
            pl.BlockSpec((1, tm, D), lambda b, i: (b, i, 0)),
            pl.BlockSpec((1, D), lambda b, i: (0, 0)),
            pl.BlockSpec(w.shape, lambda b, i: (0, 0)),
        ],
        out_specs=out_specs,
        compiler_params=_cparams(("parallel", "parallel")),
        name="norm_proj",
    )(x, g.reshape(1, D), w)


def _s5_kernel(u_hbm, bmat_ref, ar_ref, ai_ref, cmat_ref, d_ref, wglu_ref, y_hbm,
               xr_ref, xi_ref, ubuf, ybuf, in_sems, out_sems, *bufs, lc, nb, nslab, sw):
    i = pl.program_id(0)
    n = pl.num_programs(0)
    slot = lax.rem(i, 2)
    W = ubuf.shape[-1]

    def in_copies(step, slot):
        t0 = pl.multiple_of(step * lc, lc)
        return [pltpu.make_async_copy(u_hbm.at[b, pl.ds(t0, lc), :], ubuf.at[slot, :, b, :], in_sems.at[slot])
                for b in range(nb)]

    def out_copies(step, slot):
        t0 = pl.multiple_of(step * lc, lc)
        return [pltpu.make_async_copy(ybuf.at[slot, :, b, :], y_hbm.at[b, pl.ds(t0, lc), :], out_sems.at[slot])
                for b in range(nb)]

    @pl.when(i == 0)
    def _():
        xr_ref[...] = jnp.zeros_like(xr_ref)
        xi_ref[...] = jnp.zeros_like(xi_ref)
        for c in in_copies(0, 0):
            c.start()

    @pl.when(i + 1 < n)
    def _():
        for c in in_copies(i + 1, 1 - slot):
            c.start()

    for c in in_copies(i, slot):
        c.wait()

    uf = ubuf[slot].reshape(lc * nb, W)
    u = uf.astype(BF16)
    xr_all = xr_ref[...]
    xi_all = xi_ref[...]

    def drive(k):
        bufs[k][...] = jnp.dot(u[:, LANES * k:LANES * (k + 1)], bmat_ref[k], preferred_element_type=F32)

    drive(0)
    ys, new_xr, new_xi = [], [], []
    for k in range(nslab):
        if k + 1 < nslab:
            drive(k + 1)
        st_cols = slice(sw * k, sw * (k + 1))
        ar = jnp.broadcast_to(ar_ref[:, st_cols], (nb, sw))
        ai = jnp.broadcast_to(ai_ref[:, st_cols], (nb, sw))
        xr, xi = xr_all[:, st_cols], xi_all[:, st_cols]
        for t in range(lc):
            rows = slice(t * nb, (t + 1) * nb)
            nxr = ar * xr - ai * xi + bufs[k][rows, :sw]
            nxi = ar * xi + ai * xr + bufs[k][rows, sw:]
            bufs[k][rows, :sw] = nxr
            bufs[k][rows, sw:] = nxi
            xr, xi = nxr, nxi
        new_xr.append(xr)
        new_xi.append(xi)
        ys.append(jnp.dot(bufs[k][...].astype(BF16), cmat_ref[k], preferred_element_type=F32))
    xr_ref[...] = jnp.concatenate(new_xr, axis=-1)
    xi_ref[...] = jnp.concatenate(new_xi, axis=-1)
    y = jnp.concatenate(ys, axis=-1)
    y = _gelu(y + d_ref[...] * uf)
    gl = jnp.dot(y.astype(BF16), wglu_ref[...], preferred_element_type=F32)

    @pl.when(i >= 2)
    def _():
        for c in out_copies(i - 2, slot):
            c.wait()

    ybuf[slot] = (y * _sigmoid(gl)).reshape(lc, nb, W)
    for c in out_copies(i, slot):
        c.start()

    @pl.when(i == n - 1)
    def _():
        for c in out_copies(i, slot):
            c.wait()

        @pl.when(n >= 2)
        def _():
            for c in out_copies(i - 1, 1 - slot):
                c.wait()


def _s5_params(lam_re, lam_im, log_dt, b_re, b_im, c_re, c_im):
    G, P = lam_re.shape
    Hc = b_re.shape[-1]
    dt = jnp.exp(log_dt.astype(F32))[:, None]
    mag = jnp.exp(lam_re * dt)
    ar = mag * jnp.cos(lam_im * dt)
    ai = mag * jnp.sin(lam_im * dt)
    den = lam_re * lam_re + lam_im * lam_im
    cr = ((ar - 1.0) * lam_re + ai * lam_im) / den
    ci = (ai * lam_re - (ar - 1.0) * lam_im) / den
    bb_re = cr[..., None] * b_re - ci[..., None] * b_im
    bb_im = cr[..., None] * b_im + ci[..., None] * b_re
    gs = S5_SLAB_GROUPS
    nslab = G // gs
    eye = jnp.eye(gs, dtype=F32)

    def bd_in(b):
        b = b.reshape(nslab, gs, P, Hc)
        return jnp.einsum('kgph,gj->kghjp', b, eye).reshape(nslab, gs * Hc, gs * P)

    def bd_out(c):
        c = c.reshape(nslab, gs, Hc, P)
        return jnp.einsum('kghp,gj->kgpjh', c, eye).reshape(nslab, gs * P, gs * Hc)

    bmat = jnp.concatenate([bd_in(bb_re), bd_in(bb_im)], axis=-1).astype(BF16)
    cmat = jnp.concatenate([bd_out(c_re), -bd_out(c_im)], axis=1).astype(BF16)
    return bmat, ar.reshape(1, G * P), ai.reshape(1, G * P), cmat


def _s5(u, bmat, ar, ai, cmat, d_skip, w_glu, lc):
    nb, S, W = u.shape
    assert nb % SUBLANES == 0
    nslab = bmat.shape[0]
    sw = bmat.shape[2] // 2
    rows = lc * nb
    return pl.pallas_call(
        functools.partial(_s5_kernel, lc=lc, nb=nb, nslab=nslab, sw=sw),
        out_shape=jax.ShapeDtypeStruct((nb, S, W), F32),
        grid=(S // lc,),
        in_specs=[
            pl.BlockSpec(memory_space=pl.ANY),
            pl.BlockSpec(bmat.shape, lambda i: (0, 0, 0)),
            pl.BlockSpec(ar.shape, lambda i: (0, 0)),
            pl.BlockSpec(ai.shape, lambda i: (0, 0)),
            pl.BlockSpec(cmat.shape, lambda i: (0, 0, 0)),
            pl.BlockSpec((1, W), lambda i: (0, 0)),
            pl.BlockSpec(w_glu.shape, lambda i: (0, 0)),
        ],
        out_specs=pl.BlockSpec(memory_space=pl.ANY),
        scratch_shapes=[pltpu.VMEM((nb, sw * nslab), F32), pltpu.VMEM((nb, sw * nslab), F32),
                        pltpu.VMEM((2, lc, nb, W), F32), pltpu.VMEM((2, lc, nb, W), F32),
                        pltpu.SemaphoreType.DMA((2,)), pltpu.SemaphoreType.DMA((2,))]
        + [pltpu.VMEM((rows, 2 * sw), F32) for _ in range(nslab)],
        compiler_params=_cparams(("arbitrary",)),
        name="s5",
    )(u, bmat, ar, ai, cmat, d_skip.reshape(1, W), w_glu)


def _mlstm_pre(x, g, tail, m_row, cw, cb, *, L, H, Dh):
    W = H * Dh
    halo = SUBLANES
    log_scale = -0.5 * math.log(Dh)

    row = lax.broadcasted_iota(jnp.int32, (L, L), 0)
    col = lax.broadcasted_iota(jnp.int32, (L, L), 1)
    conv = cb + cw[0:1, :] * x.astype(F32)
    corr = jnp.zeros((halo, 2 * W), F32)
    for j in range(1, CONV_WIDTH):
        shift = (row - col == j).astype(BF16)
        conv = conv + cw[j:j + 1, :] * jnp.dot(shift, x, preferred_element_type=F32)
        corr = corr + cw[j:j + 1, :] * tail[halo - j:2 * halo - j, :]
    conv = jnp.concatenate([conv[:halo] + corr, conv[halo:]], axis=0)
    new_tail = x[L - halo:, :].astype(F32)
    qk = (conv * _sigmoid(conv)).astype(BF16)
    log_i = g[:, :LANES]
    f_pre = g[:, LANES:]
    logf = jnp.minimum(f_pre, 0.0) - jnp.log(1.0 + jnp.exp(-jnp.abs(f_pre)))
    causal = row >= col
    bcum = jnp.dot(causal.astype(F32), logf, preferred_element_type=F32, precision=lax.Precision.HIGHEST)
    w_cols = log_i - bcum
    trow = lax.broadcasted_iota(jnp.int32, (L, LANES), 0)
    cmax = w_cols
    k = 1
    while k < L:
        cmax = jnp.maximum(cmax, jnp.where(trow >= k, pltpu.roll(cmax, k, axis=0), -jnp.inf))
        k *= 2
    m_inter = bcum + m_row
    m_t = jnp.maximum(bcum + cmax, m_inter)
    u_cols = bcum - m_t + log_scale
    wi_cols = jnp.exp(m_inter - m_t)
    em_cols = jnp.exp(-m_t)
    b_last = bcum[L - 1:L, :]
    m_loc = b_last + cmax[L - 1:L, :]
    m_new = jnp.maximum(b_last + m_row, m_loc)
    s_prev = jnp.exp(b_last + m_row - m_new)
    s_loc = jnp.exp(m_loc - m_new)
    w_rows = w_cols.T
    return dict(q=qk[:, :W], k=qk[:, W:], causal=causal, u_cols=u_cols, wi_cols=wi_cols, em_cols=em_cols,
                w_rows=w_rows, b_last=b_last, m_loc=m_loc, m_new=m_new, s_prev=s_prev, s_loc=s_loc,
                new_tail=new_tail, log_scale=log_scale)


def _mlstm_heads(pres, vbs, o_pres, cxs, ng, *, L, H, Dh):
    nt = (((1,), (1,)), ((), ()))
    pairs = [(b, h) for b in range(len(pres)) for h in range(H)]
    hs = [slice(h * Dh, (h + 1) * Dh) for h in range(H)]
    hl = [slice(h, h + 1) for h in range(H)]
    ones = jnp.ones((L, Dh), BF16)
    sq_ones = jnp.ones((Dh, Dh), BF16)
    vext = {(b, h): jnp.concatenate([vbs[b][:, hs[h]], ones], axis=1) for b, h in pairs}
    qk_t = {(b, h): lax.dot_general(pres[b]['q'][:, hs[h]], pres[b]['k'][:, hs[h]], nt,
                                    preferred_element_type=F32) for b, h in pairs}
    r2 = {(b, h): jnp.dot(pres[b]['q'][:, hs[h]], cxs[b][h].astype(BF16), preferred_element_type=F32)
          for b, h in pairs}
    s = {}
    for b, h in pairs:
        p = pres[b]
        decay = jnp.exp(jnp.where(p['causal'], p['u_cols'][:, hl[h]] + p['w_rows'][hl[h], :], -jnp.inf))
        s[b, h] = (qk_t[b, h] * decay).astype(BF16)
    r1 = {bh: jnp.dot(s[bh], vext[bh], preferred_element_type=F32) for bh in pairs}
    hh = {}
    for b, h in pairs:
        p = pres[b]
        wi = jnp.broadcast_to(p['wi_cols'][:, hl[h]], (L, Dh))
        num = r1[b, h][:, :Dh] + wi * r2[b, h][:, :Dh]
        den = r1[b, h][:, Dh:] + wi * r2[b, h][:, Dh:]
        hh[b, h] = num / jnp.maximum(jnp.abs(den), jnp.broadcast_to(p['em_cols'][:, hl[h]], (L, Dh)))
    msq = {bh: jnp.dot((hh[bh] * hh[bh]).astype(BF16), sq_ones, preferred_element_type=F32) * (1.0 / Dh)
           for bh in pairs}
    ys = {(b, h): (_sigmoid(o_pres[b][:, hs[h]].astype(F32)) * hh[b, h] * lax.rsqrt(msq[b, h] + EPS)
                   * ng[:, hs[h]]).astype(BF16) for b, h in pairs}
    kwt = {}
    for b, h in pairs:
        p = pres[b]
        wexp = jnp.exp(p['b_last'][:, hl[h]] + p['w_rows'][hl[h], :] - p['m_loc'][:, hl[h]] + p['log_scale'])
        kwt[b, h] = (p['k'][:, hs[h]].astype(F32).T * wexp).astype(BF16)
    cx_loc = {bh: jnp.dot(kwt[bh], vext[bh], preferred_element_type=F32) for bh in pairs}
    new_cx = {(b, h): pres[b]['s_prev'][:, hl[h]] * cxs[b][h] + pres[b]['s_loc'][:, hl[h]] * cx_loc[b, h]
              for b, h in pairs}
    return ys, new_cx


def _mlstm_kernel(qk_ref, v_ref, o_ref, gt_ref, cw_ref, cb_ref, gb_ref, ng_ref, y_ref,
                  tail_ref, c_ref, m_ref, *, L, H, Dh, BB):
    @pl.when(pl.program_id(1) == 0)
    def _():
        tail_ref[...] = jnp.zeros_like(tail_ref)
        c_ref[...] = jnp.zeros_like(c_ref)
        m_ref[...] = jnp.zeros_like(m_ref)

    tails = [tail_ref[bb] for bb in range(BB)]
    cxs = [[c_ref[bb, h] for h in range(H)] for bb in range(BB)]
    m_rows = [m_ref[bb, 0:1, :] for bb in range(BB)]
    pres = [_mlstm_pre(qk_ref[bb], gt_ref[bb] + gb_ref[...], tails[bb], m_rows[bb], cw_ref[...], cb_ref[...],
                       L=L, H=H, Dh=Dh) for bb in range(BB)]
    ys, new_cx = _mlstm_heads(pres, [v_ref[bb].astype(BF16) for bb in range(BB)], [o_ref[bb] for bb in range(BB)],
                              cxs, ng_ref[...], L=L, H=H, Dh=Dh)
    for bb in range(BB):
        for h in range(H):
            y_ref[bb, :, h * Dh:(h + 1) * Dh] = ys[bb, h]
            c_ref[bb, h] = new_cx[bb, h]
        tail_ref[bb, 0:SUBLANES, :] = pres[bb]['new_tail']
        m_ref[bb, 0:1, :] = pres[bb]['m_new']


def _mlstm(qk_pre, v, o_pre, gates, conv_w, conv_b, gate_b_pad, norm_g):
    B, S, W2 = qk_pre.shape
    W = W2 // 2
    H, L = MLSTM_HEADS, MLSTM_CHUNK
    Dh = W // H
    BB = _pick_tile(B, MLSTM_SEQS_PER_STEP)
    return pl.pallas_call(
        functools.partial(_mlstm_kernel, L=L, H=H, Dh=Dh, BB=BB),
        out_shape=jax.ShapeDtypeStruct((B, S, W), BF16),
        grid=(B // BB, S // L),
        in_specs=[
            pl.BlockSpec((BB, L, W2), lambda b, c: (b, c, 0)),
            pl.BlockSpec((BB, L, W), lambda b, c: (b, c, 0)),
            pl.BlockSpec((BB, L, W), lambda b, c: (b, c, 0)),
            pl.BlockSpec((BB, L, 2 * LANES), lambda b, c: (b, c, 0)),
            pl.BlockSpec((CONV_WIDTH, W2), lambda b, c: (0, 0)),
            pl.BlockSpec((1, W2), lambda b, c: (0, 0)),
            pl.BlockSpec((1, 2 * LANES), lambda b, c: (0, 0)),
            pl.BlockSpec((1, W), lambda b, c: (0, 0)),
        ],
        out_specs=pl.BlockSpec((BB, L, W), lambda b, c: (b, c, 0)),
        scratch_shapes=[
            pltpu.VMEM((BB, 2 * SUBLANES, W2), F32),
            pltpu.VMEM((BB, H, Dh, 2 * Dh), F32),
            pltpu.VMEM((BB, SUBLANES, LANES), F32),
        ],
        compiler_params=_cparams(("parallel", "arbitrary")),
        name="mlstm",
    )(qk_pre, v, o_pre, gates, conv_w, conv_b.reshape(1, W2), gate_b_pad, norm_g.reshape(1, W))


def _mix_out(res, a, b, wa_ref, wb_ref):
    acc = jnp.dot(a.astype(BF16), wa_ref[...], preferred_element_type=F32)
    return res + acc + jnp.dot(b.astype(BF16), wb_ref[...], preferred_element_type=F32)


def _proj_ffn_kernel(res_ref, a_ref, b_ref, wa_ref, wb_ref, g_ref, wg_hbm, wu_hbm, wd_hbm, o_ref,
                     wg_ref, wu_ref, wd_ref, sem):
    @pl.when((pl.program_id(0) == 0) & (pl.program_id(1) == 0))
    def _():
        copies = [pltpu.make_async_copy(src, dst, sem.at[j])
                  for j, (src, dst) in enumerate(((wg_hbm, wg_ref), (wu_hbm, wu_ref), (wd_hbm, wd_ref)))]
        for c in copies:
            c.start()
        for c in copies:
            c.wait()

    h = _mix_out(res_ref[0], a_ref[0], b_ref[0], wa_ref, wb_ref)
    hn = _rms_bf16(h, g_ref[...])
    o_ref[0] = h
    for j in range(wg_ref.shape[0]):
        a = jnp.dot(hn, wg_ref[j], preferred_element_type=F32)
        u = jnp.dot(hn, wu_ref[j], preferred_element_type=F32)
        hmid = (a * _sigmoid(a) * u).astype(BF16)
        o_ref[0] += jnp.dot(hmid, wd_ref[j], preferred_element_type=F32)


def _proj_ffn(res, ya, yb, w_out, g, wg, wu, wd, tm, fc):
    B, S, D = res.shape
    wa_n = ya.shape[2]
    wa, wb = w_out[:wa_n], w_out[wa_n:]
    F = wg.shape[1]
    nf = F // fc
    wg3 = wg.reshape(D, nf, fc).transpose(1, 0, 2)
    wu3 = wu.reshape(D, nf, fc).transpose(1, 0, 2)
    wd3 = wd.reshape(nf, fc, D)
    return pl.pallas_call(
        _proj_ffn_kernel,
        out_shape=jax.ShapeDtypeStruct((B, S, D), F32),
        grid=(B, S // tm),
        in_specs=[
            pl.BlockSpec((1, tm, D), lambda b, i: (b, i, 0)),
            pl.BlockSpec((1, tm, wa_n), lambda b, i: (b, i, 0)),
            pl.BlockSpec((1, tm, wb.shape[0]), lambda b, i: (b, i, 0)),
            pl.BlockSpec(wa.shape, lambda b, i: (0, 0)),
            pl.BlockSpec(wb.shape, lambda b, i: (0, 0)),
            pl.BlockSpec((1, D), lambda b, i: (0, 0)),
            pl.BlockSpec(memory_space=pl.ANY),
            pl.BlockSpec(memory_space=pl.ANY),
            pl.BlockSpec(memory_space=pl.ANY),
        ],
        out_specs=pl.BlockSpec((1, tm, D), lambda b, i: (b, i, 0)),
        scratch_shapes=[pltpu.VMEM(wg3.shape, wg3.dtype), pltpu.VMEM(wu3.shape, wu3.dtype),
                        pltpu.VMEM(wd3.shape, wd3.dtype), pltpu.SemaphoreType.DMA((3,))],
        compiler_params=_cparams(("arbitrary", "arbitrary")),
        name="proj_ffn_swiglu",
    )(res, ya, yb, wa, wb, g.reshape(1, D), wg3, wu3, wd3)


def _pick_tile(n, target):
    t = min(n, target)
    while n % t:
        t //= 2
    return t


def _mix_ab(h, norm_g, w_in, lam_re, lam_im, log_dt, b_re, b_im, c_re, c_im, d_skip, w_glu,
            conv_w, conv_b, gate_b, mlstm_norm_g):
    B, S, D = h.shape
    s5w = lam_re.shape[0] * S5_GROUP
    mw = mlstm_norm_g.shape[0]
    c_qk, c_v, c_o, c_if = s5w, s5w + 2 * mw, s5w + 3 * mw, s5w + 4 * mw
    nh = (w_in.shape[1] - c_if) // 2
    lane_pad = ((0, 0), (0, LANES - nh))
    w_pad = jnp.concatenate([w_in[:, :c_if], jnp.pad(w_in[:, c_if:c_if + nh], lane_pad),
                             jnp.pad(w_in[:, c_if + nh:], lane_pad)], axis=1).astype(BF16)
    tm = _pick_tile(S, TOKEN_TILE)
    u_s5, qk_pre, v, o_pre, gates = _norm_proj(
        h, norm_g, w_pad,
        [(0, c_qk, F32), (c_qk, c_v, BF16), (c_v, c_o, BF16), (c_o, c_if, BF16), (c_if, c_if + 2 * LANES, F32)],
        tm)
    bmat, ar, ai, cmat = _s5_params(lam_re, lam_im, log_dt, b_re, b_im, c_re, c_im)
    y_a = _s5(u_s5, bmat, ar, ai, cmat, d_skip, w_glu.astype(BF16), lc=_pick_tile(S, S5_TIME_CHUNK))
    gate_b_pad = jnp.concatenate([jnp.pad(gate_b[:nh], (0, LANES - nh)),
                                  jnp.pad(gate_b[nh:], (0, LANES - nh))]).reshape(1, 2 * LANES)
    y_b = _mlstm(qk_pre, v, o_pre, gates, conv_w, conv_b, gate_b_pad, mlstm_norm_g)
    return y_a, y_b


def _gmlp_kernel(u_ref, v_ref, ng_ref, nb_ref, ws_ref, bs_ref, y_ref, *, L, G, Dg, nchunk):
    u = _gelu(u_ref[0].astype(F32))
    v = _gelu(v_ref[0].astype(F32))
    mu = jnp.mean(v, axis=-1, keepdims=True)
    vc = v - mu
    var = jnp.mean(vc * vc, axis=-1, keepdims=True)
    vn = (vc * lax.rsqrt(var + EPS) * ng_ref[...] + nb_ref[...]).astype(BF16)
    for c in range(nchunk):
        rows = slice(c * L, (c + 1) * L)
        for g in range(G):
            cols = slice(g * Dg, (g + 1) * Dg)
            s = jnp.dot(ws_ref[g], vn[rows, cols], preferred_element_type=F32) + bs_ref[:, g:g + 1]
            y_ref[0, rows, cols] = (u[rows, cols] * s).astype(y_ref.dtype)


def _gmlp(u_pre, v_pre, norm_g, norm_b, w_s, b_s, tm):
    B, S, W = u_pre.shape
    G, L = GMLP_GROUPS, GMLP_CHUNK
    ws = (w_s * jnp.tril(jnp.ones((L, L), w_s.dtype))).astype(BF16)
    bs = jnp.pad(b_s.T, ((0, 0), (0, LANES - G)))
    return pl.pallas_call(
        functools.partial(_gmlp_kernel, L=L, G=G, Dg=W // G, nchunk=tm // L),
        out_shape=jax.ShapeDtypeStruct((B, S, W), BF16),
        grid=(B, S // tm),
        in_specs=[
            pl.BlockSpec((1, tm, W), lambda b, i: (b, i, 0)),
            pl.BlockSpec((1, tm, W), lambda b, i: (b, i, 0)),
            pl.BlockSpec((1, W), lambda b, i: (0, 0)),
            pl.BlockSpec((1, W), lambda b, i: (0, 0)),
            pl.BlockSpec((G, L, L), lambda b, i: (0, 0, 0)),
            pl.BlockSpec((L, LANES), lambda b, i: (0, 0)),
        ],
        out_specs=pl.BlockSpec((1, tm, W), lambda b, i: (b, i, 0)),
        compiler_params=_cparams(("parallel", "parallel")),
        name="gmlp",
    )(u_pre, v_pre, norm_g.reshape(1, W), norm_b.reshape(1, W), ws, bs)


NEG_BIG = -1e30


def _moba_kernel(q_ref, k_ref, v_ref, t0_ref, t1_ref, bfar_ref, y_ref, vext_ref, *, BL, NB, Dh, topk):
    nt = (((1,), (1,)), ((), ()))
    scale = 1.0 / math.sqrt(Dh)
    vext_ref[:, :Dh] = v_ref[0]
    vext_ref[:, Dh:] = jnp.ones((NB * BL, Dh), BF16)
    bfar = bfar_ref[0][:, 0:1]
    kmean = jnp.concatenate(
        [jnp.mean(k_ref[0, n * BL:(n + 1) * BL, :].astype(F32), axis=0, keepdims=True) for n in range(NB)],
        axis=0)
    nrow = lax.broadcasted_iota(jnp.int32, (NB, BL), 0)
    eye = (lax.broadcasted_iota(jnp.int32, (BL, BL), 0) ==
           lax.broadcasted_iota(jnp.int32, (BL, BL), 1)).astype(BF16)

    def selection_mask(qi):
        qf = q_ref[0, qi * BL:(qi + 1) * BL, :].astype(F32)
        gate = lax.dot_general(kmean, qf, nt, preferred_element_type=F32, precision=lax.Precision.HIGHEST)
        rank = jnp.zeros((NB, BL), F32)
        for m in range(qi):
            gm = gate[m:m + 1, :]
            rank = rank + jnp.where(gm > gate, 1.0, jnp.where(gm == gate, jnp.where(nrow > m, 1.0, 0.0), 0.0))
        sel_rows = jnp.where(nrow < qi, jnp.where(rank < topk, 1.0, 0.0), 0.0)
        sel_pad = jnp.concatenate([sel_rows, jnp.zeros((LANES - NB, BL), F32)], axis=0).astype(BF16)
        sel_cols = lax.dot_general(eye, sel_pad, nt, preferred_element_type=F32)
        return (1.0 - sel_cols) * NEG_BIG

    def logits(qi):
        qs = (q_ref[0, qi * BL:(qi + 1) * BL, :].astype(F32) * scale).astype(BF16)
        raw = [lax.dot_general(qs, k_ref[0, kb * BL:(kb + 1) * BL, :], nt, preferred_element_type=F32)
               for kb in range(qi + 1)]
        neg_cols = selection_mask(qi) if qi > topk else None
        m_tile = None
        pieces = []
        for kb in range(qi + 1):
            s = raw[kb]
            if kb == qi:
                s = s + t0_ref[0]
            elif kb == qi - 1:
                s = s + t1_ref[0]
                if neg_cols is not None:
                    s = s + neg_cols[:, kb:kb + 1]
            else:
                s = s + (bfar if neg_cols is None else bfar + neg_cols[:, kb:kb + 1])
            pieces.append(s)
            for c in range(BL // LANES):
                piece = s[:, c * LANES:(c + 1) * LANES]
                m_tile = piece if m_tile is None else jnp.maximum(m_tile, piece)
        return pieces, m_tile

    nxt = logits(0)
    for qi in range(NB):
        pieces, m_tile = nxt
        if qi + 1 < NB:
            nxt = logits(qi + 1)
        m = jnp.max(m_tile, axis=-1, keepdims=True)
        p = jnp.concatenate([jnp.exp(s - m).astype(BF16) for s in pieces], axis=1)
        acc = jnp.dot(p, vext_ref[:(qi + 1) * BL, :], preferred_element_type=F32)
        y_ref[0, qi * BL:(qi + 1) * BL, :] = (acc[:, :Dh] / acc[:, Dh:]).astype(y_ref.dtype)


def _rel_bucket(n):
    max_exact = REL_BUCKETS // 2
    nf = jnp.maximum(n, 1).astype(F32)
    large = max_exact + (jnp.log(nf / max_exact) / math.log(REL_MAX_DIST / max_exact)
                         * (REL_BUCKETS - max_exact)).astype(jnp.int32)
    large = jnp.minimum(large, REL_BUCKETS - 1)
    return jnp.where(n < max_exact, n, large)


def _moba(qkv, rel_bias):
    B, S, W3 = qkv.shape
    H, BL = MOBA_HEADS, MOBA_BLOCK
    W = W3 // 3
    Dh = W // H
    NB = S // BL
    assert BL + 1 >= REL_MAX_DIST and Dh == LANES and S % BL == 0
    i = jnp.arange(BL, dtype=jnp.int32)[:, None]
    j = jnp.arange(BL, dtype=jnp.int32)[None, :]
    buckets = jnp.arange(REL_BUCKETS, dtype=jnp.int32)

    def bias_tile(dist):
        onehot = (_rel_bucket(dist)[:, :, None] == buckets).astype(F32)
        return jnp.einsum('ijb,bh->hij', onehot, rel_bias, precision=lax.Precision.HIGHEST)

    t0 = jnp.where(i >= j, bias_tile(jnp.maximum(i - j, 0)), NEG_BIG)
    t1 = bias_tile(BL + i - j)
    bfar = jnp.broadcast_to(rel_bias.T[:, REL_BUCKETS - 1][:, None, None], (H, 1, LANES))
    return pl.pallas_call(
        functools.partial(_moba_kernel, BL=BL, NB=NB, Dh=Dh, topk=min(MOBA_TOPK, NB)),
        out_shape=jax.ShapeDtypeStruct((B, S, W), BF16),
        grid=(B, H),
        in_specs=[
            pl.BlockSpec((1, S, Dh), lambda b, h: (b, 0, h)),
            pl.BlockSpec((1, S, Dh), lambda b, h: (b, 0, H + h)),
            pl.BlockSpec((1, S, Dh), lambda b, h: (b, 0, 2 * H + h)),
            pl.BlockSpec((1, BL, BL), lambda b, h: (h, 0, 0)),
            pl.BlockSpec((1, BL, BL), lambda b, h: (h, 0, 0)),
            pl.BlockSpec((1, 1, LANES), lambda b, h: (h, 0, 0)),
        ],
        out_specs=pl.BlockSpec((1, S, Dh), lambda b, h: (b, 0, h)),
        scratch_shapes=[pltpu.VMEM((S, 2 * Dh), BF16)],
        compiler_params=_cparams(("parallel", "parallel")),
        name="moba",
    )(qkv, qkv, qkv, t0, t1, bfar)


def _mix_cd(h, norm_g, w_in, gmlp_norm_g, gmlp_norm_b, gmlp_w_s, gmlp_b_s, rel_bias):
    B, S, D = h.shape
    gw = gmlp_norm_g.shape[0]
    tm = _pick_tile(S, TOKEN_TILE)
    u_pre, v_pre, qkv = _norm_proj(
        h, norm_g, w_in.astype(BF16),
        [(0, gw, BF16), (gw, 2 * gw, BF16), (2 * gw, w_in.shape[1], BF16)], tm)
    y_c = _gmlp(u_pre, v_pre, gmlp_norm_g, gmlp_norm_b, gmlp_w_s, gmlp_b_s, tm)
    return y_c, _moba(qkv, rel_bias)


def _route_logits(hn, wr, rb, E):
    hn_hi = hn.astype(BF16)
    hn_lo = (hn - hn_hi.astype(F32)).astype(BF16)
    wr_hi, wr_lo = wr
    lg = (jnp.dot(hn_hi, wr_hi, preferred_element_type=F32) + jnp.dot(hn_hi, wr_lo, preferred_element_type=F32)
          + jnp.dot(hn_lo, wr_hi, preferred_element_type=F32))
    return lg.T[:E, :] + rb


def _route_assign(logits, carry, E):
    tm = logits.shape[1]
    rowi = lax.broadcasted_iota(jnp.int32, (E, tm), 0)
    v1 = jnp.max(logits, axis=0, keepdims=True)
    e1 = jnp.min(jnp.where(logits == v1, rowi, E), axis=0, keepdims=True)
    masked = jnp.where(rowi == e1, -jnp.inf, logits)
    v2 = jnp.max(masked, axis=0, keepdims=True)
    e2 = jnp.min(jnp.where(masked == v2, rowi, E), axis=0, keepdims=True)
    ex = jnp.exp(v2 - v1)
    g1 = 1.0 / (1.0 + ex)
    g2 = ex / (1.0 + ex)
    oh1 = rowi == e1
    oh2 = rowi == e2
    cnt = jnp.where(oh1, 1.0, jnp.where(oh2, 1.0, 0.0))
    before = (lax.broadcasted_iota(jnp.int32, (tm, tm), 0) <
              lax.broadcasted_iota(jnp.int32, (tm, tm), 1)).astype(BF16)
    excl = jnp.dot(cnt.astype(BF16), before, preferred_element_type=F32) + carry
    r1 = jnp.sum(jnp.where(oh1, excl, 0.0), axis=0, keepdims=True).astype(jnp.int32)
    r2 = jnp.sum(jnp.where(oh2, excl, 0.0), axis=0, keepdims=True).astype(jnp.int32)
    idx = jnp.where(rowi == 0, e1, jnp.where(rowi == 1, e2, jnp.where(rowi == 2, r1, jnp.where(rowi == 3, r2, 0))))
    gts = jnp.where(rowi == 0, g1, jnp.where(rowi == 1, g2, 0.0))
    return idx, gts, jnp.sum(cnt, axis=1, keepdims=True)


def _router_kernel(res_ref, a_ref, b_ref, wa_ref, wb_ref, g_ref, wr_ref, rb_ref,
                   h_ref, hn_ref, idx_ref, gts_ref, cnt_ref, carry_ref, *, E, sub):
    @pl.when(pl.program_id(0) == 0)
    def _():
        carry_ref[...] = jnp.zeros_like(carry_ref)

    nsub = res_ref.shape[0] // sub
    spans = [slice(j * sub, (j + 1) * sub) for j in range(nsub)]
    xs = [_mix_out(res_ref[sp, :], a_ref[sp, :], b_ref[sp, :], wa_ref, wb_ref) for sp in spans]
    hns = []
    for sp, x in zip(spans, xs):
        h_ref[sp, :] = x
        var = jnp.mean(x * x, axis=-1, keepdims=True)
        hns.append(x * lax.rsqrt(var + EPS) * g_ref[...])
        hn_ref[sp, :] = hns[-1]
    logits = [_route_logits(hn, (wr_ref[0], wr_ref[1]), rb_ref[:, 0:1], E) for hn in hns]
    carry = carry_ref[:, 0:1]
    for sp, lg in zip(spans, logits):
        idx, gts, n_new = _route_assign(lg, carry, E)
        idx_ref[:, sp] = idx
        gts_ref[:, sp] = gts
        carry = carry + n_new
    carry_ref[...] = jnp.broadcast_to(carry, carry_ref.shape)
    cnt_ref[...] = carry_ref[...]


def _proj_router(res2d, ya, yb, w_out, g, router_w, router_b, tm):
    T, D = res2d.shape
    E = router_w.shape[1]
    assert E == SUBLANES
    wa_n = ya.shape[1]
    wa, wb = w_out[:wa_n], w_out[wa_n:]
    wr = jnp.pad(router_w, ((0, 0), (0, LANES - E)))
    wr_hi = wr.astype(BF16)
    wr_pair = jnp.stack([wr_hi, (wr - wr_hi.astype(F32)).astype(BF16)])
    return pl.pallas_call(
        functools.partial(_router_kernel, E=E, sub=_pick_tile(tm, ROUTER_SUB_TILE)),
        out_shape=[
            jax.ShapeDtypeStruct((T, D), F32),
            jax.ShapeDtypeStruct((T, D), F32),
            jax.ShapeDtypeStruct((E, T), jnp.int32),
            jax.ShapeDtypeStruct((E, T), F32),
            jax.ShapeDtypeStruct((E, LANES), F32),
        ],
        grid=(T // tm,),
        in_specs=[
            pl.BlockSpec((tm, D), lambda i: (i, 0)),
            pl.BlockSpec((tm, wa_n), lambda i: (i, 0)),
            pl.BlockSpec((tm, yb.shape[1]), lambda i: (i, 0)),
            pl.BlockSpec(wa.shape, lambda i: (0, 0)),
            pl.BlockSpec(wb.shape, lambda i: (0, 0)),
            pl.BlockSpec((1, D), lambda i: (0, 0)),
            pl.BlockSpec((2, D, LANES), lambda i: (0, 0, 0)),
            pl.BlockSpec((E, LANES), lambda i: (0, 0)),
        ],
        out_specs=[
            pl.BlockSpec((tm, D), lambda i: (i, 0)),
            pl.BlockSpec((tm, D), lambda i: (i, 0)),
            pl.BlockSpec((E, tm), lambda i: (0, i)),
            pl.BlockSpec((E, tm), lambda i: (0, i)),
            pl.BlockSpec((E, LANES), lambda i: (0, 0)),
        ],
        scratch_shapes=[pltpu.VMEM((E, LANES), F32)],
        compiler_params=_cparams(("arbitrary",)),
        name="proj_moe_router",
    )(res2d, ya, yb, wa, wb, g.reshape(1, D), wr_pair, jnp.broadcast_to(router_b[:, None], (E, LANES)))


DMA_ISSUE_UNROLL = True


def _dispatch_kernel(pos1_ref, pos2_ref, ends_ref, h_ref, xs_hbm, zero_ref, sem, zsem, *, tm, tg, E):
    i = pl.program_id(0)

    @pl.when(i == 0)
    def _():
        zero_ref[...] = jnp.zeros_like(zero_ref)

        def zero_copy(e):
            start = ends_ref[e] - tg
            return pltpu.make_async_copy(zero_ref, xs_hbm.at[pl.ds(pl.multiple_of(start, tg), tg)], zsem)

        def nonempty(e):
            return ends_ref[e] > (ends_ref[e - 1] if e else 0)

        for e in range(E):
            @pl.when(nonempty(e))
            def _(e=e):
                zero_copy(e).start()
        for e in range(E):
            @pl.when(nonempty(e))
            def _(e=e):
                zero_copy(e).wait()

    def body(r, carry):
        t = i * tm + r
        src = h_ref.at[pl.ds(r, 1)]
        pltpu.make_async_copy(src, xs_hbm.at[pl.ds(pos1_ref[t], 1)], sem).start()
        pltpu.make_async_copy(src, xs_hbm.at[pl.ds(pos2_ref[t], 1)], sem).start()
        return carry
    lax.fori_loop(0, tm, body, 0, unroll=DMA_ISSUE_UNROLL)

    for _ in range(TOP_K):
        pltpu.make_async_copy(h_ref, xs_hbm.at[pl.ds(0, tm)], sem).wait()


def _dispatch(pos1, pos2, ends, h2d, n_rows, tm, tg):
    T, D = h2d.shape
    E = ends.shape[0]
    return pl.pallas_call(
        functools.partial(_dispatch_kernel, tm=tm, tg=tg, E=E),
        out_shape=jax.ShapeDtypeStruct((n_rows, D), h2d.dtype),
        grid_spec=pltpu.PrefetchScalarGridSpec(
            num_scalar_prefetch=3,
            grid=(T // tm,),
            in_specs=[pl.BlockSpec((tm, D), lambda i, p1, p2, en: (i, 0))],
            out_specs=pl.BlockSpec(memory_space=pl.ANY),
            scratch_shapes=[pltpu.VMEM((tg, D), h2d.dtype), pltpu.SemaphoreType.DMA, pltpu.SemaphoreType.DMA],
        ),
        compiler_params=_cparams(("arbitrary",)),
        name="moe_dispatch",
    )(pos1, pos2, ends, h2d)


def _gmm_kernel(te_ref, tv_ref, x_ref, wg_ref, wu_ref, wd_ref, o_ref, xn_ref, *, rb):
    i = pl.program_id(0)
    f = pl.program_id(1)
    tm = xn_ref.shape[0]
    valid = tv_ref[i]

    @pl.when(f == 0)
    def _():
        o_ref[...] = jnp.zeros_like(o_ref)

    @pl.when((valid > 0) & (f == 0))
    def _():
        xn_ref[...] = x_ref[...].astype(BF16)

    def swiglu_rows(rows):
        x = xn_ref[rows, :]
        a = jnp.dot(x, wg_ref[0].astype(BF16), preferred_element_type=F32)
        u = jnp.dot(x, wu_ref[0].astype(BF16), preferred_element_type=F32)
        hmid = (a * _sigmoid(a) * u).astype(BF16)
        o_ref[rows, :] += jnp.dot(hmid, wd_ref[0].astype(BF16), preferred_element_type=F32)

    @pl.when(valid == tm)
    def _():
        swiglu_rows(slice(None))

    for r in range(tm // rb):
        @pl.when((valid < tm) & (valid > r * rb))
        def _(r=r):
            swiglu_rows(slice(r * rb, (r + 1) * rb))


def _gmm(tile_expert, tile_valid, xs, wg, wu, wd, tm, fc):
    R, D = xs.shape
    F = wg.shape[2]
    nf = F // fc

    def fsel(i, f, tv):
        return jnp.where(tv[i] > 0, f, nf - 1)

    return pl.pallas_call(
        functools.partial(_gmm_kernel, rb=_pick_tile(tm, GMM_TAIL_ROW_BLOCK)),
        out_shape=jax.ShapeDtypeStruct((R, D), F32),
        grid_spec=pltpu.PrefetchScalarGridSpec(
            num_scalar_prefetch=2,
            grid=(R // tm, nf),
            in_specs=[
                pl.BlockSpec((tm, D), lambda i, f, te, tv: (i, 0)),
                pl.BlockSpec((1, D, fc), lambda i, f, te, tv: (te[i], 0, fsel(i, f, tv))),
                pl.BlockSpec((1, D, fc), lambda i, f, te, tv: (te[i], 0, fsel(i, f, tv))),
                pl.BlockSpec((1, fc, D), lambda i, f, te, tv: (te[i], fsel(i, f, tv), 0)),
            ],
            out_specs=pl.BlockSpec((tm, D), lambda i, f, te, tv: (i, 0)),
            scratch_shapes=[pltpu.VMEM((tm, D), BF16)],
        ),
        compiler_params=_cparams(("parallel", "arbitrary")),
        name="moe_gmm",
    )(tile_expert, tile_valid, xs, wg, wu, wd)


def _combine_copies(pos1_ref, pos2_ref, ys_hbm, ya_ref, yb_ref, sems, tile, slot, tm):
    def row_copies(r):
        t = tile * tm + r
        return (pltpu.make_async_copy(ys_hbm.at[pl.ds(pos1_ref[t], 1)], ya_ref.at[slot, pl.ds(r, 1)], sems.at[slot]),
                pltpu.make_async_copy(ys_hbm.at[pl.ds(pos2_ref[t], 1)], yb_ref.at[slot, pl.ds(r, 1)], sems.at[slot]))
    return row_copies


def _combine_kernel(pos1_ref, pos2_ref, h_ref, ga_ref, gb_ref, g_ref, ys_hbm, o_ref, ya_ref, yb_ref, sems, *, tm):
    i = pl.program_id(0)
    n = pl.num_programs(0)
    slot = lax.rem(i, 2)

    def start_tile(tile, slot):
        copies = _combine_copies(pos1_ref, pos2_ref, ys_hbm, ya_ref, yb_ref, sems, tile, slot, tm)

        def body(r, carry):
            a, b = copies(r)
            a.start()
            b.start()
            return carry
        lax.fori_loop(0, tm, body, 0, unroll=DMA_ISSUE_UNROLL)

    @pl.when(i == 0)
    def _():
        start_tile(0, 0)

    @pl.when(i + 1 < n)
    def _():
        start_tile(i + 1, 1 - slot)

    pltpu.make_async_copy(ys_hbm.at[pl.ds(0, tm)], ya_ref.at[slot], sems.at[slot]).wait()
    pltpu.make_async_copy(ys_hbm.at[pl.ds(0, tm)], yb_ref.at[slot], sems.at[slot]).wait()

    h = h_ref[...] + ga_ref[...] * ya_ref[slot] + gb_ref[...] * yb_ref[slot]
    var = jnp.mean(h * h, axis=-1, keepdims=True)
    o_ref[...] = h * lax.rsqrt(var + EPS) * g_ref[...]


def _combine(pos1, pos2, h2d, ga, gb, g, ys, tm):
    T, D = h2d.shape
    row = pl.BlockSpec((tm, D), lambda i, p1, p2: (i, 0))
    colv = pl.BlockSpec((tm, 1), lambda i, p1, p2: (i, 0))
    return pl.pallas_call(
        functools.partial(_combine_kernel, tm=tm),
        out_shape=jax.ShapeDtypeStruct((T, D), F32),
        grid_spec=pltpu.PrefetchScalarGridSpec(
            num_scalar_prefetch=2,
            grid=(T // tm,),
            in_specs=[row, colv, colv, pl.BlockSpec((1, D), lambda i, p1, p2: (0, 0)),
                      pl.BlockSpec(memory_space=pl.ANY)],
            out_specs=row,
            scratch_shapes=[pltpu.VMEM((2, tm, D), F32), pltpu.VMEM((2, tm, D), F32),
                            pltpu.SemaphoreType.DMA((2,))],
        ),
        compiler_params=_cparams(("arbitrary",)),
        name="moe_combine_norm",
    )(pos1, pos2, h2d, ga, gb, g.reshape(1, D), ys)


def _moe_final(res, y_c, y_d, w_out, norm_g, final_g, router_w, router_b, w_gate, w_up, w_down):
    B, S, D = res.shape
    T = B * S
    E = router_w.shape[1]
    h2d, hn2d, idx, gts, cnt = _proj_router(res.reshape(T, D), y_c.reshape(T, -1), y_d.reshape(T, -1),
                                            w_out.astype(BF16), norm_g, router_w, router_b, _pick_tile(T, ROUTER_TILE))
    e1, e2, r1, r2 = idx[0], idx[1], idx[2], idx[3]
    tm = MOE_ROW_TILE if TOP_K * T >= E * MOE_ROW_TILE else _pick_tile(T, TOKEN_TILE)
    counts = cnt[:, 0].astype(jnp.int32)
    padded = ((counts + tm - 1) // tm) * tm
    ends = jnp.cumsum(padded)
    offs = ends - padded
    eids = jnp.arange(E, dtype=jnp.int32)[:, None]
    pos1 = jnp.sum(jnp.where(e1[None, :] == eids, offs[:, None], 0), axis=0) + r1
    pos2 = jnp.sum(jnp.where(e2[None, :] == eids, offs[:, None], 0), axis=0) + r2
    n_tiles = -(-TOP_K * T // tm) + E
    tile_start = jnp.arange(n_tiles, dtype=jnp.int32) * tm
    tile_expert = jnp.minimum(jnp.sum(tile_start[:, None] >= ends[None, :], axis=1), E - 1).astype(jnp.int32)
    tile_valid = jnp.clip((offs + counts)[tile_expert] - tile_start, 0, tm).astype(jnp.int32)
    xs = _dispatch(pos1, pos2, ends.astype(jnp.int32), hn2d, n_tiles * tm, _pick_tile(T, DISPATCH_TILE), tm)
    ys = _gmm(tile_expert, tile_valid, xs, w_gate, w_up, w_down, tm, _pick_tile(w_gate.shape[2], MOE_FF_CHUNK))
    out = _combine(pos1, pos2, h2d, gts[0][:, None], gts[1][:, None], final_g, ys, _pick_tile(T, COMBINE_TILE))
    return out.reshape(B, S, D)


def kernel(x, norm_mix_g, norm_ffn_g, norm_final_g, ab_w_in, s5_lambda_re, s5_lambda_im, s5_log_dt, s5_b_re, s5_b_im, s5_c_re, s5_c_im, s5_d, s5_w_glu, mlstm_conv_w, mlstm_conv_b, mlstm_gate_b, mlstm_norm_g, ab_w_out, ffn_w_gate, ffn_w_up, ffn_w_down, cd_w_in, gmlp_norm_g, gmlp_norm_b, gmlp_w_s, gmlp_b_s, rel_bias, cd_w_out, moe_router_w, moe_router_b, moe_w_gate, moe_w_up, moe_w_down):
    B, S, D = x.shape
    y_a, y_b = _mix_ab(x, norm_mix_g[0], ab_w_in[0], s5_lambda_re[0], s5_lambda_im[0], s5_log_dt[0],
                       s5_b_re[0], s5_b_im[0], s5_c_re[0], s5_c_im[0], s5_d[0], s5_w_glu[0],
                       mlstm_conv_w[0], mlstm_conv_b[0], mlstm_gate_b[0], mlstm_norm_g[0])
    h = _proj_ffn(x, y_a, y_b, ab_w_out[0].astype(BF16), norm_ffn_g[0], ffn_w_gate[0].astype(BF16),
                  ffn_w_up[0].astype(BF16), ffn_w_down[0].astype(BF16),
                  _pick_tile(S, TOKEN_TILE), _pick_tile(ffn_w_gate.shape[2], FFN_FF_CHUNK))
    y_c, y_d = _mix_cd(h, norm_mix_g[1], cd_w_in[0], gmlp_norm_g[0], gmlp_norm_b[0],
                       gmlp_w_s[0], gmlp_b_s[0], rel_bias)
    return _moe_final(h, y_c, y_d, cd_w_out[0], norm_ffn_g[1], norm_final_g, moe_router_w[0], moe_router_b[0],
                      moe_w_gate[0], moe_w_up[0], moe_w_down[0])
```

```python
import functools
import math

import jax
import jax.numpy as jnp
from jax import lax
from jax.experimental import pallas as pl
from jax.experimental.pallas import tpu as pltpu

F32 = jnp.float32
BF16 = jnp.bfloat16
EPS = 1e-5

LANES = 128
SUBLANES = 8
VMEM_LIMIT_BYTES = 56 * 1024 * 1024

S5_GROUP = 16
S5_SLAB_GROUPS = 8
MLSTM_HEADS = 4
MLSTM_CHUNK = 128
MLSTM_SEQS_PER_STEP = 2
CONV_WIDTH = 4
GMLP_GROUPS = 4
GMLP_CHUNK = 128
MOBA_HEADS = 4
MOBA_BLOCK = 256
MOBA_TOPK = 3
REL_BUCKETS = 32
REL_MAX_DIST = 128
TOP_K = 2

TOKEN_TILE = 1024
FFN_FF_CHUNK = 256
S5_TIME_CHUNK = 32
ROUTER_TILE = 1024
ROUTER_SUB_TILE = 512
DISPATCH_TILE = 1024
COMBINE_TILE = 256
MOE_FF_CHUNK = 512
GMM_TAIL_ROW_BLOCK = 256
MOE_ROW_TILE = 1536


def _cparams(sem):
    return pltpu.CompilerParams(dimension_semantics=sem, vmem_limit_bytes=VMEM_LIMIT_BYTES)


def _rms_bf16(x, g):
    var = jnp.mean(x * x, axis=-1, keepdims=True)
    return (x * lax.rsqrt(var + EPS) * g).astype(BF16)


def _gelu(x):
    return jax.nn.gelu(x, approximate=True)


def _sigmoid(x):
    return 1.0 / (1.0 + jnp.exp(-x))


def _norm_proj_kernel(x_ref, g_ref, w_ref, *out_refs, splits):
    hn = _rms_bf16(x_ref[0], g_ref[...])
    for o_ref, (c0, c1) in zip(out_refs, splits):
        r = jnp.dot(hn, w_ref[:, c0:c1], preferred_element_type=F32)
        o_ref[...] = r.reshape(o_ref.shape).astype(o_ref.dtype)


def _norm_proj(x, g, w, outs, tm):
    B, S, D = x.shape
    splits = tuple((c0, c1) for c0, c1, _ in outs)
    out_shape, out_specs = [], []
    for c0, c1, dt in outs:
        n = c1 - c0
        out_shape.append(jax.ShapeDtypeStruct((B, S, n), dt))
        out_specs.append(pl.BlockSpec((1, tm, n), lambda b, i: (b, i, 0)))
    return pl.pallas_call(
        functools.partial(_norm_proj_kernel, splits=splits),
        out_shape=out_shape,
        grid=(B, S // tm),
        in_specs=[
            pl.BlockSpec((1, tm, D), lambda b, i: (b, i, 0)),
            pl.BlockSpec((1, D), lambda b, i: (0, 0)),
            pl.BlockSpec(w.shape, lambda b, i: (0, 0)),
        ],
        out_specs=out_specs,
        compiler_params=_cparams(("parallel", "parallel")),
        name="norm_proj",
    )(x, g.reshape(1, D), w)


def _s5_kernel(u_hbm, bmat_ref, ar_ref, ai_ref, cmat_ref, d_ref, wglu_ref, y_hbm,
               xr_ref, xi_ref, ubuf, ybuf, in_sems, out_sems, *bufs, lc, nb, nslab, sw):
    i = pl.program_id(0)
    n = pl.num_programs(0)
    slot = lax.rem(i, 2)
    W = ubuf.shape[-1]

    def in_copies(step, slot):
        t0 = pl.multiple_of(step * lc, lc)
        return [pltpu.make_async_copy(u_hbm.at[b, pl.ds(t0, lc), :], ubuf.at[slot, :, b, :], in_sems.at[slot])
                for b in range(nb)]

    def out_copies(step, slot):
        t0 = pl.multiple_of(step * lc, lc)
        return [pltpu.make_async_copy(ybuf.at[slot, :, b, :], y_hbm.at[b, pl.ds(t0, lc), :], out_sems.at[slot])
                for b in range(nb)]

    @pl.when(i == 0)
    def _():
        xr_ref[...] = jnp.zeros_like(xr_ref)
        xi_ref[...] = jnp.zeros_like(xi_ref)
        for c in in_copies(0, 0):
            c.start()

    @pl.when(i + 1 < n)
    def _():
        for c in in_copies(i + 1, 1 - slot):
            c.start()

    for c in in_copies(i, slot):
        c.wait()

    uf = ubuf[slot].reshape(lc * nb, W)
    u = uf.astype(BF16)
    xr_all = xr_ref[...]
    xi_all = xi_ref[...]

    def drive(k):
        bufs[k][...] = jnp.dot(u[:, LANES * k:LANES * (k + 1)], bmat_ref[k], preferred_element_type=F32)

    drive(0)
    ys, new_xr, new_xi = [], [], []
    for k in range(nslab):
        if k + 1 < nslab:
            drive(k + 1)
        st_cols = slice(sw * k, sw * (k + 1))
        ar = jnp.broadcast_to(ar_ref[:, st_cols], (nb, sw))
        ai = jnp.broadcast_to(ai_ref[:, st_cols], (nb, sw))
        xr, xi = xr_all[:, st_cols], xi_all[:, st_cols]
        for t in range(lc):
            rows = slice(t * nb, (t + 1) * nb)
            nxr = ar * xr - ai * xi + bufs[k][rows, :sw]
            nxi = ar * xi + ai * xr + bufs[k][rows, sw:]
            bufs[k][rows, :sw] = nxr
            bufs[k][rows, sw:] = nxi
            xr, xi = nxr, nxi
        new_xr.append(xr)
        new_xi.append(xi)
        ys.append(jnp.dot(bufs[k][...].astype(BF16), cmat_ref[k], preferred_element_type=F32))
    xr_ref[...] = jnp.concatenate(new_xr, axis=-1)
    xi_ref[...] = jnp.concatenate(new_xi, axis=-1)
    y = jnp.concatenate(ys, axis=-1)
    y = _gelu(y + d_ref[...] * uf)
    gl = jnp.dot(y.astype(BF16), wglu_ref[...], preferred_element_type=F32)

    @pl.when(i >= 2)
    def _():
        for c in out_copies(i - 2, slot):
            c.wait()

    ybuf[slot] = (y * _sigmoid(gl)).reshape(lc, nb, W)
    for c in out_copies(i, slot):
        c.start()

    @pl.when(i == n - 1)
    def _():
        for c in out_copies(i, slot):
            c.wait()

        @pl.when(n >= 2)
        def _():
            for c in out_copies(i - 1, 1 - slot):
                c.wait()


def _s5_params(lam_re, lam_im, log_dt, b_re, b_im, c_re, c_im):
    G, P = lam_re.shape
    Hc = b_re.shape[-1]
    dt = jnp.exp(log_dt.astype(F32))[:, None]
    mag = jnp.exp(lam_re * dt)
    ar = mag * jnp.cos(lam_im * dt)
    ai = mag * jnp.sin(lam_im * dt)
    den = lam_re * lam_re + lam_im * lam_im
    cr = ((ar - 1.0) * lam_re + ai * lam_im) / den
    ci = (ai * lam_re - (ar - 1.0) * lam_im) / den
    bb_re = cr[..., None] * b_re - ci[..., None] * b_im
    bb_im = cr[..., None] * b_im + ci[..., None] * b_re
    gs = S5_SLAB_GROUPS
    nslab = G // gs
    eye = jnp.eye(gs, dtype=F32)

    def bd_in(b):
        b = b.reshape(nslab, gs, P, Hc)
        return jnp.einsum('kgph,gj->kghjp', b, eye).reshape(nslab, gs * Hc, gs * P)

    def bd_out(c):
        c = c.reshape(nslab, gs, Hc, P)
        return jnp.einsum('kghp,gj->kgpjh', c, eye).reshape(nslab, gs * P, gs * Hc)

    bmat = jnp.concatenate([bd_in(bb_re), bd_in(bb_im)], axis=-1).astype(BF16)
    cmat = jnp.concatenate([bd_out(c_re), -bd_out(c_im)], axis=1).astype(BF16)
    return bmat, ar.reshape(1, G * P), ai.reshape(1, G * P), cmat


def _s5(u, bmat, ar, ai, cmat, d_skip, w_glu, lc):
    nb, S, W = u.shape
    assert nb % SUBLANES == 0
    nslab = bmat.shape[0]
    sw = bmat.shape[2] // 2
    rows = lc * nb
    return pl.pallas_call(
        functools.partial(_s5_kernel, lc=lc, nb=nb, nslab=nslab, sw=sw),
        out_shape=jax.ShapeDtypeStruct((nb, S, W), F32),
        grid=(S // lc,),
        in_specs=[
            pl.BlockSpec(memory_space=pl.ANY),
            pl.BlockSpec(bmat.shape, lambda i: (0, 0, 0)),
            pl.BlockSpec(ar.shape, lambda i: (0, 0)),
            pl.BlockSpec(ai.shape, lambda i: (0, 0)),
            pl.BlockSpec(cmat.shape, lambda i: (0, 0, 0)),
            pl.BlockSpec((1, W), lambda i: (0, 0)),
            pl.BlockSpec(w_glu.shape, lambda i: (0, 0)),
        ],
        out_specs=pl.BlockSpec(memory_space=pl.ANY),
        scratch_shapes=[pltpu.VMEM((nb, sw * nslab), F32), pltpu.VMEM((nb, sw * nslab), F32),
                        pltpu.VMEM((2, lc, nb, W), F32), pltpu.VMEM((2, lc, nb, W), F32),
                        pltpu.SemaphoreType.DMA((2,)), pltpu.SemaphoreType.DMA((2,))]
        + [pltpu.VMEM((rows, 2 * sw), F32) for _ in range(nslab)],
        compiler_params=_cparams(("arbitrary",)),
        name="s5",
    )(u, bmat, ar, ai, cmat, d_skip.reshape(1, W), w_glu)


def _mlstm_pre(x, g, tail, m_row, cw, cb, *, L, H, Dh):
    W = H * Dh
    halo = SUBLANES
    log_scale = -0.5 * math.log(Dh)

    row = lax.broadcasted_iota(jnp.int32, (L, L), 0)
    col = lax.broadcasted_iota(jnp.int32, (L, L), 1)
    conv = cb + cw[0:1, :] * x.astype(F32)
    corr = jnp.zeros((halo, 2 * W), F32)
    for j in range(1, CONV_WIDTH):
        shift = (row - col == j).astype(BF16)
        conv = conv + cw[j:j + 1, :] * jnp.dot(shift, x, preferred_element_type=F32)
        corr = corr + cw[j:j + 1, :] * tail[halo - j:2 * halo - j, :]
    conv = jnp.concatenate([conv[:halo] + corr, conv[halo:]], axis=0)
    new_tail = x[L - halo:, :].astype(F32)
    qk = (conv * _sigmoid(conv)).astype(BF16)
    log_i = g[:, :LANES]
    f_pre = g[:, LANES:]
    logf = jnp.minimum(f_pre, 0.0) - jnp.log(1.0 + jnp.exp(-jnp.abs(f_pre)))
    causal = row >= col
    bcum = jnp.dot(causal.astype(F32), logf, preferred_element_type=F32, precision=lax.Precision.HIGHEST)
    w_cols = log_i - bcum
    trow = lax.broadcasted_iota(jnp.int32, (L, LANES), 0)
    cmax = w_cols
    k = 1
    while k < L:
        cmax = jnp.maximum(cmax, jnp.where(trow >= k, pltpu.roll(cmax, k, axis=0), -jnp.inf))
        k *= 2
    m_inter = bcum + m_row
    m_t = jnp.maximum(bcum + cmax, m_inter)
    u_cols = bcum - m_t + log_scale
    wi_cols = jnp.exp(m_inter - m_t)
    em_cols = jnp.exp(-m_t)
    b_last = bcum[L - 1:L, :]
    m_loc = b_last + cmax[L - 1:L, :]
    m_new = jnp.maximum(b_last + m_row, m_loc)
    s_prev = jnp.exp(b_last + m_row - m_new)
    s_loc = jnp.exp(m_loc - m_new)
    w_rows = w_cols.T
    return dict(q=qk[:, :W], k=qk[:, W:], causal=causal, u_cols=u_cols, wi_cols=wi_cols, em_cols=em_cols,
                w_rows=w_rows, b_last=b_last, m_loc=m_loc, m_new=m_new, s_prev=s_prev, s_loc=s_loc,
                new_tail=new_tail, log_scale=log_scale)


def _mlstm_heads(pres, vbs, o_pres, cxs, ng, *, L, H, Dh):
    nt = (((1,), (1,)), ((), ()))
    pairs = [(b, h) for b in range(len(pres)) for h in range(H)]
    hs = [slice(h * Dh, (h + 1) * Dh) for h in range(H)]
    hl = [slice(h, h + 1) for h in range(H)]
    ones = jnp.ones((L, Dh), BF16)
    sq_ones = jnp.ones((Dh, Dh), BF16)
    vext = {(b, h): jnp.concatenate([vbs[b][:, hs[h]], ones], axis=1) for b, h in pairs}
    qk_t = {(b, h): lax.dot_general(pres[b]['q'][:, hs[h]], pres[b]['k'][:, hs[h]], nt,
                                    preferred_element_type=F32) for b, h in pairs}
    r2 = {(b, h): jnp.dot(pres[b]['q'][:, hs[h]], cxs[b][h].astype(BF16), preferred_element_type=F32)
          for b, h in pairs}
    s = {}
    for b, h in pairs:
        p = pres[b]
        decay = jnp.exp(jnp.where(p['causal'], p['u_cols'][:, hl[h]] + p['w_rows'][hl[h], :], -jnp.inf))
        s[b, h] = (qk_t[b, h] * decay).astype(BF16)
    r1 = {bh: jnp.dot(s[bh], vext[bh], preferred_element_type=F32) for bh in pairs}
    hh = {}
    for b, h in pairs:
        p = pres[b]
        wi = jnp.broadcast_to(p['wi_cols'][:, hl[h]], (L, Dh))
        num = r1[b, h][:, :Dh] + wi * r2[b, h][:, :Dh]
        den = r1[b, h][:, Dh:] + wi * r2[b, h][:, Dh:]
        hh[b, h] = num / jnp.maximum(jnp.abs(den), jnp.broadcast_to(p['em_cols'][:, hl[h]], (L, Dh)))
    msq = {bh: jnp.dot((hh[bh] * hh[bh]).astype(BF16), sq_ones, preferred_element_type=F32) * (1.0 / Dh)
           for bh in pairs}
    ys = {(b, h): (_sigmoid(o_pres[b][:, hs[h]].astype(F32)) * hh[b, h] * lax.rsqrt(msq[b, h] + EPS)
                   * ng[:, hs[h]]).astype(BF16) for b, h in pairs}
    kwt = {}
    for b, h in pairs:
        p = pres[b]
        wexp = jnp.exp(p['b_last'][:, hl[h]] + p['w_rows'][hl[h], :] - p['m_loc'][:, hl[h]] + p['log_scale'])
        kwt[b, h] = (p['k'][:, hs[h]].astype(F32).T * wexp).astype(BF16)
    cx_loc = {bh: jnp.dot(kwt[bh], vext[bh], preferred_element_type=F32) for bh in pairs}
    new_cx = {(b, h): pres[b]['s_prev'][:, hl[h]] * cxs[b][h] + pres[b]['s_loc'][:, hl[h]] * cx_loc[b, h]
              for b, h in pairs}
    return ys, new_cx


def _mlstm_kernel(qk_ref, v_ref, o_ref, gt_ref, cw_ref, cb_ref, gb_ref, ng_ref, y_ref,
                  tail_ref, c_ref, m_ref, *, L, H, Dh, BB):
    @pl.when(pl.program_id(1) == 0)
    def _():
        tail_ref[...] = jnp.zeros_like(tail_ref)
        c_ref[...] = jnp.zeros_like(c_ref)
        m_ref[...] = jnp.zeros_like(m_ref)

    tails = [tail_ref[bb] for bb in range(BB)]
    cxs = [[c_ref[bb, h] for h in range(H)] for bb in range(BB)]
    m_rows = [m_ref[bb, 0:1, :] for bb in range(BB)]
    pres = [_mlstm_pre(qk_ref[bb], gt_ref[bb] + gb_ref[...], tails[bb], m_rows[bb], cw_ref[...], cb_ref[...],
                       L=L, H=H, Dh=Dh) for bb in range(BB)]
    ys, new_cx = _mlstm_heads(pres, [v_ref[bb].astype(BF16) for bb in range(BB)], [o_ref[bb] for bb in range(BB)],
                              cxs, ng_ref[...], L=L, H=H, Dh=Dh)
    for bb in range(BB):
        for h in range(H):
            y_ref[bb, :, h * Dh:(h + 1) * Dh] = ys[bb, h]
            c_ref[bb, h] = new_cx[bb, h]
        tail_ref[bb, 0:SUBLANES, :] = pres[bb]['new_tail']
        m_ref[bb, 0:1, :] = pres[bb]['m_new']


def _mlstm(qk_pre, v, o_pre, gates, conv_w, conv_b, gate_b_pad, norm_g):
    B, S, W2 = qk_pre.shape
    W = W2 // 2
    H, L = MLSTM_HEADS, MLSTM_CHUNK
    Dh = W // H
    BB = _pick_tile(B, MLSTM_SEQS_PER_STEP)
    return pl.pallas_call(
        functools.partial(_mlstm_kernel, L=L, H=H, Dh=Dh, BB=BB),
        out_shape=jax.ShapeDtypeStruct((B, S, W), BF16),
        grid=(B // BB, S // L),
        in_specs=[
            pl.BlockSpec((BB, L, W2), lambda b, c: (b, c, 0)),
            pl.BlockSpec((BB, L, W), lambda b, c: (b, c, 0)),
            pl.BlockSpec((BB, L, W), lambda b, c: (b, c, 0)),
            pl.BlockSpec((BB, L, 2 * LANES), lambda b, c: (b, c, 0)),
            pl.BlockSpec((CONV_WIDTH, W2), lambda b, c: (0, 0)),
            pl.BlockSpec((1, W2), lambda b, c: (0, 0)),
            pl.BlockSpec((1, 2 * LANES), lambda b, c: (0, 0)),
            pl.BlockSpec((1, W), lambda b, c: (0, 0)),
        ],
        out_specs=pl.BlockSpec((BB, L, W), lambda b, c: (b, c, 0)),
        scratch_shapes=[
            pltpu.VMEM((BB, 2 * SUBLANES, W2), F32),
            pltpu.VMEM((BB, H, Dh, 2 * Dh), F32),
            pltpu.VMEM((BB, SUBLANES, LANES), F32),
        ],
        compiler_params=_cparams(("parallel", "arbitrary")),
        name="mlstm",
    )(qk_pre, v, o_pre, gates, conv_w, conv_b.reshape(1, W2), gate_b_pad, norm_g.reshape(1, W))


def _mix_out(res, a, b, wa_ref, wb_ref):
    acc = jnp.dot(a.astype(BF16), wa_ref[...], preferred_element_type=F32)
    return res + acc + jnp.dot(b.astype(BF16), wb_ref[...], preferred_element_type=F32)


def _proj_ffn_kernel(res_ref, a_ref, b_ref, wa_ref, wb_ref, g_ref, wg_hbm, wu_hbm, wd_hbm, o_ref,
                     wg_ref, wu_ref, wd_ref, sem):
    @pl.when((pl.program_id(0) == 0) & (pl.program_id(1) == 0))
    def _():
        copies = [pltpu.make_async_copy(src, dst, sem.at[j])
                  for j, (src, dst) in enumerate(((wg_hbm, wg_ref), (wu_hbm, wu_ref), (wd_hbm, wd_ref)))]
        for c in copies:
            c.start()
        for c in copies:
            c.wait()

    h = _mix_out(res_ref[0], a_ref[0], b_ref[0], wa_ref, wb_ref)
    hn = _rms_bf16(h, g_ref[...])
    o_ref[0] = h
    for j in range(wg_ref.shape[0]):
        a = jnp.dot(hn, wg_ref[j], preferred_element_type=F32)
        u = jnp.dot(hn, wu_ref[j], preferred_element_type=F32)
        hmid = (a * _sigmoid(a) * u).astype(BF16)
        o_ref[0] += jnp.dot(hmid, wd_ref[j], preferred_element_type=F32)


def _proj_ffn(res, ya, yb, w_out, g, wg, wu, wd, tm, fc):
    B, S, D = res.shape
    wa_n = ya.shape[2]
    wa, wb = w_out[:wa_n], w_out[wa_n:]
    F = wg.shape[1]
    nf = F // fc
    wg3 = wg.reshape(D, nf, fc).transpose(1, 0, 2)
    wu3 = wu.reshape(D, nf, fc).transpose(1, 0, 2)
    wd3 = wd.reshape(nf, fc, D)
    return pl.pallas_call(
        _proj_ffn_kernel,
        out_shape=jax.ShapeDtypeStruct((B, S, D), F32),
        grid=(B, S // tm),
        in_specs=[
            pl.BlockSpec((1, tm, D), lambda b, i: (b, i, 0)),
            pl.BlockSpec((1, tm, wa_n), lambda b, i: (b, i, 0)),
            pl.BlockSpec((1, tm, wb.shape[0]), lambda b, i: (b, i, 0)),
            pl.BlockSpec(wa.shape, lambda b, i: (0, 0)),
            pl.BlockSpec(wb.shape, lambda b, i: (0, 0)),
            pl.BlockSpec((1, D), lambda b, i: (0, 0)),
            pl.BlockSpec(memory_space=pl.ANY),
            pl.BlockSpec(memory_space=pl.ANY),
            pl.BlockSpec(memory_space=pl.ANY),
        ],
        out_specs=pl.BlockSpec((1, tm, D), lambda b, i: (b, i, 0)),
        scratch_shapes=[pltpu.VMEM(wg3.shape, wg3.dtype), pltpu.VMEM(wu3.shape, wu3.dtype),
                        pltpu.VMEM(wd3.shape, wd3.dtype), pltpu.SemaphoreType.DMA((3,))],
        compiler_params=_cparams(("arbitrary", "arbitrary")),
        name="proj_ffn_swiglu",
    )(res, ya, yb, wa, wb, g.reshape(1, D), wg3, wu3, wd3)


def _pick_tile(n, target):
    t = min(n, target)
    while n % t:
        t //= 2
    return t


def _mix_ab(h, norm_g, w_in, lam_re, lam_im, log_dt, b_re, b_im, c_re, c_im, d_skip, w_glu,
            conv_w, conv_b, gate_b, mlstm_norm_g):
    B, S, D = h.shape
    s5w = lam_re.shape[0] * S5_GROUP
    mw = mlstm_norm_g.shape[0]
    c_qk, c_v, c_o, c_if = s5w, s5w + 2 * mw, s5w + 3 * mw, s5w + 4 * mw
    nh = (w_in.shape[1] - c_if) // 2
    lane_pad = ((0, 0), (0, LANES - nh))
    w_pad = jnp.concatenate([w_in[:, :c_if], jnp.pad(w_in[:, c_if:c_if + nh], lane_pad),
                             jnp.pad(w_in[:, c_if + nh:], lane_pad)], axis=1).astype(BF16)
    tm = _pick_tile(S, TOKEN_TILE)
    u_s5, qk_pre, v, o_pre, gates = _norm_proj(
        h, norm_g, w_pad,
        [(0, c_qk, F32), (c_qk, c_v, BF16), (c_v, c_o, BF16), (c_o, c_if, BF16), (c_if, c_if + 2 * LANES, F32)],
        tm)
    bmat, ar, ai, cmat = _s5_params(lam_re, lam_im, log_dt, b_re, b_im, c_re, c_im)
    y_a = _s5(u_s5, bmat, ar, ai, cmat, d_skip, w_glu.astype(BF16), lc=_pick_tile(S, S5_TIME_CHUNK))
    gate_b_pad = jnp.concatenate([jnp.pad(gate_b[:nh], (0, LANES - nh)),
                                  jnp.pad(gate_b[nh:], (0, LANES - nh))]).reshape(1, 2 * LANES)
    y_b = _mlstm(qk_pre, v, o_pre, gates, conv_w, conv_b, gate_b_pad, mlstm_norm_g)
    return y_a, y_b


def _gmlp_kernel(u_ref, v_ref, ng_ref, nb_ref, ws_ref, bs_ref, y_ref, *, L, G, Dg, nchunk):
    u = _gelu(u_ref[0].astype(F32))
    v = _gelu(v_ref[0].astype(F32))
    mu = jnp.mean(v, axis=-1, keepdims=True)
    vc = v - mu
    var = jnp.mean(vc * vc, axis=-1, keepdims=True)
    vn = (vc * lax.rsqrt(var + EPS) * ng_ref[...] + nb_ref[...]).astype(BF16)
    for c in range(nchunk):
        rows = slice(c * L, (c + 1) * L)
        for g in range(G):
            cols = slice(g * Dg, (g + 1) * Dg)
            s = jnp.dot(ws_ref[g], vn[rows, cols], preferred_element_type=F32) + bs_ref[:, g:g + 1]
            y_ref[0, rows, cols] = (u[rows, cols] * s).astype(y_ref.dtype)


def _gmlp(u_pre, v_pre, norm_g, norm_b, w_s, b_s, tm):
    B, S, W = u_pre.shape
    G, L = GMLP_GROUPS, GMLP_CHUNK
    ws = (w_s * jnp.tril(jnp.ones((L, L), w_s.dtype))).astype(BF16)
    bs = jnp.pad(b_s.T, ((0, 0), (0, LANES - G)))
    return pl.pallas_call(
        functools.partial(_gmlp_kernel, L=L, G=G, Dg=W // G, nchunk=tm // L),
        out_shape=jax.ShapeDtypeStruct((B, S, W), BF16),
        grid=(B, S // tm),
        in_specs=[
            pl.BlockSpec((1, tm, W), lambda b, i: (b, i, 0)),
            pl.BlockSpec((1, tm, W), lambda b, i: (b, i, 0)),
            pl.BlockSpec((1, W), lambda b, i: (0, 0)),
            pl.BlockSpec((1, W), lambda b, i: (0, 0)),
            pl.BlockSpec((G, L, L), lambda b, i: (0, 0, 0)),
            pl.BlockSpec((L, LANES), lambda b, i: (0, 0)),
        ],
        out_specs=pl.BlockSpec((1, tm, W), lambda b, i: (b, i, 0)),
        compiler_params=_cparams(("parallel", "parallel")),
        name="gmlp",
    )(u_pre, v_pre, norm_g.reshape(1, W), norm_b.reshape(1, W), ws, bs)


NEG_BIG = -1e30


def _moba_kernel(q_ref, k_ref, v_ref, t0_ref, t1_ref, bfar_ref, y_ref, vext_ref, *, BL, NB, Dh, topk):
    nt = (((1,), (1,)), ((), ()))
    scale = 1.0 / math.sqrt(Dh)
    vext_ref[:, :Dh] = v_ref[0]
    vext_ref[:, Dh:] = jnp.ones((NB * BL, Dh), BF16)
    bfar = bfar_ref[0][:, 0:1]
    kmean = jnp.concatenate(
        [jnp.mean(k_ref[0, n * BL:(n + 1) * BL, :].astype(F32), axis=0, keepdims=True) for n in range(NB)],
        axis=0)
    nrow = lax.broadcasted_iota(jnp.int32, (NB, BL), 0)
    eye = (lax.broadcasted_iota(jnp.int32, (BL, BL), 0) ==
           lax.broadcasted_iota(jnp.int32, (BL, BL), 1)).astype(BF16)

    def selection_mask(qi):
        qf = q_ref[0, qi * BL:(qi + 1) * BL, :].astype(F32)
        gate = lax.dot_general(kmean, qf, nt, preferred_element_type=F32, precision=lax.Precision.HIGHEST)
        rank = jnp.zeros((NB, BL), F32)
        for m in range(qi):
            gm = gate[m:m + 1, :]
            rank = rank + jnp.where(gm > gate, 1.0, jnp.where(gm == gate, jnp.where(nrow > m, 1.0, 0.0), 0.0))
        sel_rows = jnp.where(nrow < qi, jnp.where(rank < topk, 1.0, 0.0), 0.0)
        sel_pad = jnp.concatenate([sel_rows, jnp.zeros((LANES - NB, BL), F32)], axis=0).astype(BF16)
        sel_cols = lax.dot_general(eye, sel_pad, nt, preferred_element_type=F32)
        return (1.0 - sel_cols) * NEG_BIG

    def logits(qi):
        qs = (q_ref[0, qi * BL:(qi + 1) * BL, :].astype(F32) * scale).astype(BF16)
        raw = [lax.dot_general(qs, k_ref[0, kb * BL:(kb + 1) * BL, :], nt, preferred_element_type=F32)
               for kb in range(qi + 1)]
        neg_cols = selection_mask(qi) if qi > topk else None
        m_tile = None
        pieces = []
        for kb in range(qi + 1):
            s = raw[kb]
            if kb == qi:
                s = s + t0_ref[0]
            elif kb == qi - 1:
                s = s + t1_ref[0]
                if neg_cols is not None:
                    s = s + neg_cols[:, kb:kb + 1]
            else:
                s = s + (bfar if neg_cols is None else bfar + neg_cols[:, kb:kb + 1])
            pieces.append(s)
            for c in range(BL // LANES):
                piece = s[:, c * LANES:(c + 1) * LANES]
                m_tile = piece if m_tile is None else jnp.maximum(m_tile, piece)
        return pieces, m_tile

    nxt = logits(0)
    for qi in range(NB):
        pieces, m_tile = nxt
        if qi + 1 < NB:
            nxt = logits(qi + 1)
        m = jnp.max(m_tile, axis=-1, keepdims=True)
        p = jnp.concatenate([jnp.exp(s - m).astype(BF16) for s in pieces], axis=1)
        acc = jnp.dot(p, vext_ref[:(qi + 1) * BL, :], preferred_element_type=F32)
        y_ref[0, qi * BL:(qi + 1) * BL, :] = (acc[:, :Dh] / acc[:, Dh:]).astype(y_ref.dtype)


def _rel_bucket(n):
    max_exact = REL_BUCKETS // 2
    nf = jnp.maximum(n, 1).astype(F32)
    large = max_exact + (jnp.log(nf / max_exact) / math.log(REL_MAX_DIST / max_exact)
                         * (REL_BUCKETS - max_exact)).astype(jnp.int32)
    large = jnp.minimum(large, REL_BUCKETS - 1)
    return jnp.where(n < max_exact, n, large)


def _moba(qkv, rel_bias):
    B, S, W3 = qkv.shape
    H, BL = MOBA_HEADS, MOBA_BLOCK
    W = W3 // 3
    Dh = W // H
    NB = S // BL
    assert BL + 1 >= REL_MAX_DIST and Dh == LANES and S % BL == 0
    i = jnp.arange(BL, dtype=jnp.int32)[:, None]
    j = jnp.arange(BL, dtype=jnp.int32)[None, :]
    buckets = jnp.arange(REL_BUCKETS, dtype=jnp.int32)

    def bias_tile(dist):
        onehot = (_rel_bucket(dist)[:, :, None] == buckets).astype(F32)
        return jnp.einsum('ijb,bh->hij', onehot, rel_bias, precision=lax.Precision.HIGHEST)

    t0 = jnp.where(i >= j, bias_tile(jnp.maximum(i - j, 0)), NEG_BIG)
    t1 = bias_tile(BL + i - j)
    bfar = jnp.broadcast_to(rel_bias.T[:, REL_BUCKETS - 1][:, None, None], (H, 1, LANES))
    return pl.pallas_call(
        functools.partial(_moba_kernel, BL=BL, NB=NB, Dh=Dh, topk=min(MOBA_TOPK, NB)),
        out_shape=jax.ShapeDtypeStruct((B, S, W), BF16),
        grid=(B, H),
        in_specs=[
            pl.BlockSpec((1, S, Dh), lambda b, h: (b, 0, h)),
            pl.BlockSpec((1, S, Dh), lambda b, h: (b, 0, H + h)),
            pl.BlockSpec((1, S, Dh), lambda b, h: (b, 0, 2 * H + h)),
            pl.BlockSpec((1, BL, BL), lambda b, h: (h, 0, 0)),
            pl.BlockSpec((1, BL, BL), lambda b, h: (h, 0, 0)),
            pl.BlockSpec((1, 1, LANES), lambda b, h: (h, 0, 0)),
        ],
        out_specs=pl.BlockSpec((1, S, Dh), lambda b, h: (b, 0, h)),
        scratch_shapes=[pltpu.VMEM((S, 2 * Dh), BF16)],
        compiler_params=_cparams(("parallel", "parallel")),
        name="moba",
    )(qkv, qkv, qkv, t0, t1, bfar)


def _mix_cd(h, norm_g, w_in, gmlp_norm_g, gmlp_norm_b, gmlp_w_s, gmlp_b_s, rel_bias):
    B, S, D = h.shape
    gw = gmlp_norm_g.shape[0]
    tm = _pick_tile(S, TOKEN_TILE)
    u_pre, v_pre, qkv = _norm_proj(
        h, norm_g, w_in.astype(BF16),
        [(0, gw, BF16), (gw, 2 * gw, BF16), (2 * gw, w_in.shape[1], BF16)], tm)
    y_c = _gmlp(u_pre, v_pre, gmlp_norm_g, gmlp_norm_b, gmlp_w_s, gmlp_b_s, tm)
    return y_c, _moba(qkv, rel_bias)


def _route_logits(hn, wr, rb, E):
    hn_hi = hn.astype(BF16)
    hn_lo = (hn - hn_hi.astype(F32)).astype(BF16)
    wr_hi, wr_lo = wr
    lg = (jnp.dot(hn_hi, wr_hi, preferred_element_type=F32) + jnp.dot(hn_hi, wr_lo, preferred_element_type=F32)
          + jnp.dot(hn_lo, wr_hi, preferred_element_type=F32))
    return lg.T[:E, :] + rb


def _route_assign(logits, carry, E):
    tm = logits.shape[1]
    rowi = lax.broadcasted_iota(jnp.int32, (E, tm), 0)
    v1 = jnp.max(logits, axis=0, keepdims=True)
    e1 = jnp.min(jnp.where(logits == v1, rowi, E), axis=0, keepdims=True)
    masked = jnp.where(rowi == e1, -jnp.inf, logits)
    v2 = jnp.max(masked, axis=0, keepdims=True)
    e2 = jnp.min(jnp.where(masked == v2, rowi, E), axis=0, keepdims=True)
    ex = jnp.exp(v2 - v1)
    g1 = 1.0 / (1.0 + ex)
    g2 = ex / (1.0 + ex)
    oh1 = rowi == e1
    oh2 = rowi == e2
    cnt = jnp.where(oh1, 1.0, jnp.where(oh2, 1.0, 0.0))
    before = (lax.broadcasted_iota(jnp.int32, (tm, tm), 0) <
              lax.broadcasted_iota(jnp.int32, (tm, tm), 1)).astype(BF16)
    excl = jnp.dot(cnt.astype(BF16), before, preferred_element_type=F32) + carry
    r1 = jnp.sum(jnp.where(oh1, excl, 0.0), axis=0, keepdims=True).astype(jnp.int32)
    r2 = jnp.sum(jnp.where(oh2, excl, 0.0), axis=0, keepdims=True).astype(jnp.int32)
    idx = jnp.where(rowi == 0, e1, jnp.where(rowi == 1, e2, jnp.where(rowi == 2, r1, jnp.where(rowi == 3, r2, 0))))
    gts = jnp.where(rowi == 0, g1, jnp.where(rowi == 1, g2, 0.0))
    return idx, gts, jnp.sum(cnt, axis=1, keepdims=True)


def _router_kernel(res_ref, a_ref, b_ref, wa_ref, wb_ref, g_ref, wr_ref, rb_ref,
                   h_ref, hn_ref, idx_ref, gts_ref, cnt_ref, carry_ref, *, E, sub):
    @pl.when(pl.program_id(0) == 0)
    def _():
        carry_ref[...] = jnp.zeros_like(carry_ref)

    nsub = res_ref.shape[0] // sub
    spans = [slice(j * sub, (j + 1) * sub) for j in range(nsub)]
    xs = [_mix_out(res_ref[sp, :], a_ref[sp, :], b_ref[sp, :], wa_ref, wb_ref) for sp in spans]
    hns = []
    for sp, x in zip(spans, xs):
        h_ref[sp, :] = x
        var = jnp.mean(x * x, axis=-1, keepdims=True)
        hns.append(x * lax.rsqrt(var + EPS) * g_ref[...])
        hn_ref[sp, :] = hns[-1]
    logits = [_route_logits(hn, (wr_ref[0], wr_ref[1]), rb_ref[:, 0:1], E) for hn in hns]
    carry = carry_ref[:, 0:1]
    for sp, lg in zip(spans, logits):
        idx, gts, n_new = _route_assign(lg, carry, E)
        idx_ref[:, sp] = idx
        gts_ref[:, sp] = gts
        carry = carry + n_new
    carry_ref[...] = jnp.broadcast_to(carry, carry_ref.shape)
    cnt_ref[...] = carry_ref[...]


def _proj_router(res2d, ya, yb, w_out, g, router_w, router_b, tm):
    T, D = res2d.shape
    E = router_w.shape[1]
    assert E == SUBLANES
    wa_n = ya.shape[1]
    wa, wb = w_out[:wa_n], w_out[wa_n:]
    wr = jnp.pad(router_w, ((0, 0), (0, LANES - E)))
    wr_hi = wr.astype(BF16)
    wr_pair = jnp.stack([wr_hi, (wr - wr_hi.astype(F32)).astype(BF16)])
    return pl.pallas_call(
        functools.partial(_router_kernel, E=E, sub=_pick_tile(tm, ROUTER_SUB_TILE)),
        out_shape=[
            jax.ShapeDtypeStruct((T, D), F32),
            jax.ShapeDtypeStruct((T, D), F32),
            jax.ShapeDtypeStruct((E, T), jnp.int32),
            jax.ShapeDtypeStruct((E, T), F32),
            jax.ShapeDtypeStruct((E, LANES), F32),
        ],
        grid=(T // tm,),
        in_specs=[
            pl.BlockSpec((tm, D), lambda i: (i, 0)),
            pl.BlockSpec((tm, wa_n), lambda i: (i, 0)),
            pl.BlockSpec((tm, yb.shape[1]), lambda i: (i, 0)),
            pl.BlockSpec(wa.shape, lambda i: (0, 0)),
            pl.BlockSpec(wb.shape, lambda i: (0, 0)),
            pl.BlockSpec((1, D), lambda i: (0, 0)),
            pl.BlockSpec((2, D, LANES), lambda i: (0, 0, 0)),
            pl.BlockSpec((E, LANES), lambda i: (0, 0)),
        ],
        out_specs=[
            pl.BlockSpec((tm, D), lambda i: (i, 0)),
            pl.BlockSpec((tm, D), lambda i: (i, 0)),
            pl.BlockSpec((E, tm), lambda i: (0, i)),
            pl.BlockSpec((E, tm), lambda i: (0, i)),
            pl.BlockSpec((E, LANES), lambda i: (0, 0)),
        ],
        scratch_shapes=[pltpu.VMEM((E, LANES), F32)],
        compiler_params=_cparams(("arbitrary",)),
        name="proj_moe_router",
    )(res2d, ya, yb, wa, wb, g.reshape(1, D), wr_pair, jnp.broadcast_to(router_b[:, None], (E, LANES)))


DMA_ISSUE_UNROLL = True


def _dispatch_kernel(pos1_ref, pos2_ref, ends_ref, h_ref, xs_hbm, zero_ref, sem, zsem, *, tm, tg, E):
    i = pl.program_id(0)

    @pl.when(i == 0)
    def _():
        zero_ref[...] = jnp.zeros_like(zero_ref)

        def zero_copy(e):
            start = ends_ref[e] - tg
            return pltpu.make_async_copy(zero_ref, xs_hbm.at[pl.ds(pl.multiple_of(start, tg), tg)], zsem)

        def nonempty(e):
            return ends_ref[e] > (ends_ref[e - 1] if e else 0)

        for e in range(E):
            @pl.when(nonempty(e))
            def _(e=e):
                zero_copy(e).start()
        for e in range(E):
            @pl.when(nonempty(e))
            def _(e=e):
                zero_copy(e).wait()

    def body(r, carry):
        t = i * tm + r
        src = h_ref.at[pl.ds(r, 1)]
        pltpu.make_async_copy(src, xs_hbm.at[pl.ds(pos1_ref[t], 1)], sem).start(priority=0)
        pltpu.make_async_copy(src, xs_hbm.at[pl.ds(pos2_ref[t], 1)], sem).start(priority=1)
        return carry
    lax.fori_loop(0, tm, body, 0, unroll=DMA_ISSUE_UNROLL)

    for _ in range(TOP_K):
        pltpu.make_async_copy(h_ref, xs_hbm.at[pl.ds(0, tm)], sem).wait()


def _dispatch(pos1, pos2, ends, h2d, n_rows, tm, tg):
    T, D = h2d.shape
    E = ends.shape[0]
    return pl.pallas_call(
        functools.partial(_dispatch_kernel, tm=tm, tg=tg, E=E),
        out_shape=jax.ShapeDtypeStruct((n_rows, D), h2d.dtype),
        grid_spec=pltpu.PrefetchScalarGridSpec(
            num_scalar_prefetch=3,
            grid=(T // tm,),
            in_specs=[pl.BlockSpec((tm, D), lambda i, p1, p2, en: (i, 0))],
            out_specs=pl.BlockSpec(memory_space=pl.ANY),
            scratch_shapes=[pltpu.VMEM((tg, D), h2d.dtype), pltpu.SemaphoreType.DMA, pltpu.SemaphoreType.DMA],
        ),
        compiler_params=_cparams(("arbitrary",)),
        name="moe_dispatch",
    )(pos1, pos2, ends, h2d)


def _gmm_kernel(te_ref, tv_ref, x_ref, wg_ref, wu_ref, wd_ref, o_ref, xn_ref, *, rb):
    i = pl.program_id(0)
    f = pl.program_id(1)
    tm = xn_ref.shape[0]
    valid = tv_ref[i]

    @pl.when(f == 0)
    def _():
        o_ref[...] = jnp.zeros_like(o_ref)

    @pl.when((valid > 0) & (f == 0))
    def _():
        xn_ref[...] = x_ref[...].astype(BF16)

    def swiglu_rows(rows):
        x = xn_ref[rows, :]
        a = jnp.dot(x, wg_ref[0].astype(BF16), preferred_element_type=F32)
        u = jnp.dot(x, wu_ref[0].astype(BF16), preferred_element_type=F32)
        hmid = (a * _sigmoid(a) * u).astype(BF16)
        o_ref[rows, :] += jnp.dot(hmid, wd_ref[0].astype(BF16), preferred_element_type=F32)

    @pl.when(valid == tm)
    def _():
        swiglu_rows(slice(None))

    for r in range(tm // rb):
        @pl.when((valid < tm) & (valid > r * rb))
        def _(r=r):
            swiglu_rows(slice(r * rb, (r + 1) * rb))


def _gmm(tile_expert, tile_valid, xs, wg, wu, wd, tm, fc):
    R, D = xs.shape
    F = wg.shape[2]
    nf = F // fc

    def fsel(i, f, tv):
        return jnp.where(tv[i] > 0, f, nf - 1)

    return pl.pallas_call(
        functools.partial(_gmm_kernel, rb=_pick_tile(tm, GMM_TAIL_ROW_BLOCK)),
        out_shape=jax.ShapeDtypeStruct((R, D), F32),
        grid_spec=pltpu.PrefetchScalarGridSpec(
            num_scalar_prefetch=2,
            grid=(R // tm, nf),
            in_specs=[
                pl.BlockSpec((tm, D), lambda i, f, te, tv: (i, 0)),
                pl.BlockSpec((1, D, fc), lambda i, f, te, tv: (te[i], 0, fsel(i, f, tv))),
                pl.BlockSpec((1, D, fc), lambda i, f, te, tv: (te[i], 0, fsel(i, f, tv))),
                pl.BlockSpec((1, fc, D), lambda i, f, te, tv: (te[i], fsel(i, f, tv), 0)),
            ],
            out_specs=pl.BlockSpec((tm, D), lambda i, f, te, tv: (i, 0)),
            scratch_shapes=[pltpu.VMEM((tm, D), BF16)],
        ),
        compiler_params=_cparams(("parallel", "arbitrary")),
        name="moe_gmm",
    )(tile_expert, tile_valid, xs, wg, wu, wd)


def _combine_copies(pos1_ref, pos2_ref, ys_hbm, ya_ref, yb_ref, sems, tile, slot, tm):
    def row_copies(r):
        t = tile * tm + r
        return (pltpu.make_async_copy(ys_hbm.at[pl.ds(pos1_ref[t], 1)], ya_ref.at[slot, pl.ds(r, 1)], sems.at[slot]),
                pltpu.make_async_copy(ys_hbm.at[pl.ds(pos2_ref[t], 1)], yb_ref.at[slot, pl.ds(r, 1)], sems.at[slot]))
    return row_copies


def _combine_kernel(pos1_ref, pos2_ref, h_ref, ga_ref, gb_ref, g_ref, ys_hbm, o_ref, ya_ref, yb_ref, sems, *, tm):
    i = pl.program_id(0)
    n = pl.num_programs(0)
    slot = lax.rem(i, 2)

    def start_tile(tile, slot):
        copies = _combine_copies(pos1_ref, pos2_ref, ys_hbm, ya_ref, yb_ref, sems, tile, slot, tm)

        def body(r, carry):
            a, b = copies(r)
            a.start(priority=0)
            b.start(priority=1)
            return carry
        lax.fori_loop(0, tm, body, 0, unroll=DMA_ISSUE_UNROLL)

    @pl.when(i == 0)
    def _():
        start_tile(0, 0)

    @pl.when(i + 1 < n)
    def _():
        start_tile(i + 1, 1 - slot)

    pltpu.make_async_copy(ys_hbm.at[pl.ds(0, tm)], ya_ref.at[slot], sems.at[slot]).wait()
    pltpu.make_async_copy(ys_hbm.at[pl.ds(0, tm)], yb_ref.at[slot], sems.at[slot]).wait()

    h = h_ref[...] + ga_ref[...] * ya_ref[slot] + gb_ref[...] * yb_ref[slot]
    var = jnp.mean(h * h, axis=-1, keepdims=True)
    o_ref[...] = h * lax.rsqrt(var + EPS) * g_ref[...]


def _combine(pos1, pos2, h2d, ga, gb, g, ys, tm):
    T, D = h2d.shape
    row = pl.BlockSpec((tm, D), lambda i, p1, p2: (i, 0))
    colv = pl.BlockSpec((tm, 1), lambda i, p1, p2: (i, 0))
    return pl.pallas_call(
        functools.partial(_combine_kernel, tm=tm),
        out_shape=jax.ShapeDtypeStruct((T, D), F32),
        grid_spec=pltpu.PrefetchScalarGridSpec(
            num_scalar_prefetch=2,
            grid=(T // tm,),
            in_specs=[row, colv, colv, pl.BlockSpec((1, D), lambda i, p1, p2: (0, 0)),
                      pl.BlockSpec(memory_space=pl.ANY)],
            out_specs=row,
            scratch_shapes=[pltpu.VMEM((2, tm, D), F32), pltpu.VMEM((2, tm, D), F32),
                            pltpu.SemaphoreType.DMA((2,))],
        ),
        compiler_params=_cparams(("arbitrary",)),
        name="moe_combine_norm",
    )(pos1, pos2, h2d, ga, gb, g.reshape(1, D), ys)


def _moe_final(res, y_c, y_d, w_out, norm_g, final_g, router_w, router_b, w_gate, w_up, w_down):
    B, S, D = res.shape
    T = B * S
    E = router_w.shape[1]
    h2d, hn2d, idx, gts, cnt = _proj_router(res.reshape(T, D), y_c.reshape(T, -1), y_d.reshape(T, -1),
                                            w_out.astype(BF16), norm_g, router_w, router_b, _pick_tile(T, ROUTER_TILE))
    e1, e2, r1, r2 = idx[0], idx[1], idx[2], idx[3]
    tm = MOE_ROW_TILE if TOP_K * T >= E * MOE_ROW_TILE else _pick_tile(T, TOKEN_TILE)
    counts = cnt[:, 0].astype(jnp.int32)
    padded = ((counts + tm - 1) // tm) * tm
    ends = jnp.cumsum(padded)
    offs = ends - padded
    eids = jnp.arange(E, dtype=jnp.int32)[:, None]
    pos1 = jnp.sum(jnp.where(e1[None, :] == eids, offs[:, None], 0), axis=0) + r1
    pos2 = jnp.sum(jnp.where(e2[None, :] == eids, offs[:, None], 0), axis=0) + r2
    n_tiles = -(-TOP_K * T // tm) + E
    tile_start = jnp.arange(n_tiles, dtype=jnp.int32) * tm
    tile_expert = jnp.minimum(jnp.sum(tile_start[:, None] >= ends[None, :], axis=1), E - 1).astype(jnp.int32)
    tile_valid = jnp.clip((offs + counts)[tile_expert] - tile_start, 0, tm).astype(jnp.int32)
    xs = _dispatch(pos1, pos2, ends.astype(jnp.int32), hn2d, n_tiles * tm, _pick_tile(T, DISPATCH_TILE), tm)
    ys = _gmm(tile_expert, tile_valid, xs, w_gate, w_up, w_down, tm, _pick_tile(w_gate.shape[2], MOE_FF_CHUNK))
    out = _combine(pos1, pos2, h2d, gts[0][:, None], gts[1][:, None], final_g, ys, _pick_tile(T, COMBINE_TILE))
    return out.reshape(B, S, D)


def kernel(x, norm_mix_g, norm_ffn_g, norm_final_g, ab_w_in, s5_lambda_re, s5_lambda_im, s5_log_dt, s5_b_re, s5_b_im, s5_c_re, s5_c_im, s5_d, s5_w_glu, mlstm_conv_w, mlstm_conv_b, mlstm_gate_b, mlstm_norm_g, ab_w_out, ffn_w_gate, ffn_w_up, ffn_w_down, cd_w_in, gmlp_norm_g, gmlp_norm_b, gmlp_w_s, gmlp_b_s, rel_bias, cd_w_out, moe_router_w, moe_router_b, moe_w_gate, moe_w_up, moe_w_down):
    B, S, D = x.shape
    y_a, y_b = _mix_ab(x, norm_mix_g[0], ab_w_in[0], s5_lambda_re[0], s5_lambda_im[0], s5_log_dt[0],
                       s5_b_re[0], s5_b_im[0], s5_c_re[0], s5_c_im[0], s5_d[0], s5_w_glu[0],
                       mlstm_conv_w[0], mlstm_conv_b[0], mlstm_gate_b[0], mlstm_norm_g[0])
    h = _proj_ffn(x, y_a, y_b, ab_w_out[0].astype(BF16), norm_ffn_g[0], ffn_w_gate[0].astype(BF16),
                  ffn_w_up[0].astype(BF16), ffn_w_down[0].astype(BF16),
                  _pick_tile(S, TOKEN_TILE), _pick_tile(ffn_w_gate.shape[2], FFN_FF_CHUNK))
    y_c, y_d = _mix_cd(h, norm_mix_g[1], cd_w_in[0], gmlp_norm_g[0], gmlp_norm_b[0],
                       gmlp_w_s[0], gmlp_b_s[0], rel_bias)
    return _moe_final(h, y_c, y_d, cd_w_out[0], norm_ffn_g[1], norm_final_g, moe_router_w[0], moe_router_b[0],
                      moe_w_gate[0], moe_w_up[0], moe_w_down[0])
```
